```python
import math
import jax, jax.numpy as jnp
from jax import lax
import numpy as np

D_MODEL = 1024
BATCH = 4
SEQ = 8192
DEPTH = 2

N_HEADS = 16
HEAD_DIM = D_MODEL // N_HEADS
HD = N_HEADS * HEAD_DIM
ROT_DIM = HEAD_DIM // 4
ROPE_THETA = 500000.0
Q_BLOCK = 128
N_KV_GROUPS = 2
HEADS_PER_GROUP = N_HEADS // N_KV_GROUPS
CMP_LEN = 32
CMP_STRIDE = 16
CMP_HIDDEN = 256
SLC_LEN = 64
SLC_TOPK = 16
WIN = 512
D_FF = 2816
CONV_W = 3
N_A_LAYERS = (DEPTH + 1) // 2
N_B_LAYERS = DEPTH // 2
RMS_EPS = 1e-6
NEG = -1e30
FORCED_SCORE = 1e6
FORGET_BIAS = 3.0

kernel_name = "yoco_fox_nsa_convffn"


def rms_norm(x, g):
    xf = x.astype(jnp.float32)
    y = xf * lax.rsqrt(jnp.mean(xf * xf, axis=-1, keepdims=True) + RMS_EPS)
    return (y * g.astype(jnp.float32)).astype(x.dtype)


def partial_rope(x, pos):
    half = ROT_DIM // 2
    inv = ROPE_THETA ** (-jnp.arange(half, dtype=jnp.float32) * (2.0 / ROT_DIM))
    ang = pos.astype(jnp.float32)[..., None] * inv
    cos = jnp.cos(ang)[..., None, :]
    sin = jnp.sin(ang)[..., None, :]
    xf = x.astype(jnp.float32)
    x1 = xf[..., :half]
    x2 = xf[..., half:ROT_DIM]
    out = jnp.concatenate([x1 * cos - x2 * sin, x2 * cos + x1 * sin, xf[..., ROT_DIM:]], axis=-1)
    return out.astype(x.dtype)


def forgetting_attention(xn, w_in, b_f, q_gain, k_gain, w_out):
    B, T, _ = xn.shape
    proj = xn @ w_in
    q = rms_norm(proj[..., :HD].reshape(B, T, N_HEADS, HEAD_DIM), q_gain)
    k = rms_norm(proj[..., HD:2 * HD].reshape(B, T, N_HEADS, HEAD_DIM), k_gain)
    v = proj[..., 2 * HD:3 * HD].reshape(B, T, N_HEADS, HEAD_DIM)
    log_f = jax.nn.log_sigmoid((proj[..., 3 * HD:] + b_f).astype(jnp.float32))
    c = jnp.cumsum(log_f, axis=1).transpose(0, 2, 1)
    q = q.transpose(0, 2, 1, 3)
    k = k.transpose(0, 2, 1, 3)
    v = v.transpose(0, 2, 1, 3)
    scale = HEAD_DIM ** -0.5
    kpos = jnp.arange(T)

    def block(i):
        q0 = i * Q_BLOCK
        qb = lax.dynamic_slice_in_dim(q, q0, Q_BLOCK, axis=2)
        cb = lax.dynamic_slice_in_dim(c, q0, Q_BLOCK, axis=2)
        s = jnp.einsum('bhqd,bhkd->bhqk', qb, k, preferred_element_type=jnp.float32) * scale
        s = s + cb[..., :, None] - c[..., None, :]
        qpos = q0 + jnp.arange(Q_BLOCK)
        s = jnp.where(kpos[None, :] <= qpos[:, None], s, -jnp.inf)
        p = jax.nn.softmax(s, axis=-1)
        return jnp.einsum('bhqk,bhkd->bqhd', p.astype(v.dtype), v)

    o = lax.map(block, jnp.arange(T // Q_BLOCK))
    o = jnp.moveaxis(o, 0, 1).reshape(B, T, HD)
    return o @ w_out


def conv_ffn(xn, w_up, conv_w, conv_b, w_down):
    T = xn.shape[1]
    u = xn @ w_up
    u_pad = jnp.pad(u, ((0, 0), (CONV_W - 1, 0), (0, 0)))
    c = sum((conv_w[j] * u_pad[:, j:j + T] for j in range(CONV_W)), conv_b)
    gate, val = jnp.split(c, 2, axis=-1)
    return (jax.nn.silu(gate) * val) @ w_down


def shared_kv(h, positions, kv_norm, kv_w, kc_pe, vc_pe, kc_w1, kc_w2, vc_w1, vc_w2, kc_gain, ks_gain, kw_gain):
    B, T, _ = h.shape
    hn = rms_norm(h, kv_norm)
    parts = (hn @ kv_w).reshape(B, T, 6, N_KV_GROUPS, HEAD_DIM)
    kc_raw, vc_raw = parts[:, :, 0], parts[:, :, 1]
    ks, vs = parts[:, :, 2], parts[:, :, 3]
    kw, vw = parts[:, :, 4], parts[:, :, 5]
    n_cmp = (T - CMP_LEN) // CMP_STRIDE + 1
    starts = jnp.arange(n_cmp) * CMP_STRIDE
    idx = starts[:, None] + jnp.arange(CMP_LEN)[None, :]

    def compress(raw, pe, w1, w2):
        blk = raw[:, idx] + pe[None, None, :, None, :]
        blk = jnp.moveaxis(blk, 3, 2).reshape(B, n_cmp, N_KV_GROUPS, CMP_LEN * HEAD_DIM)
        return jax.nn.gelu(blk @ w1) @ w2

    kc = compress(kc_raw, kc_pe, kc_w1, kc_w2)
    vc = compress(vc_raw, vc_pe, vc_w1, vc_w2)
    ends = starts + CMP_LEN - 1
    kc = partial_rope(rms_norm(kc, kc_gain), positions[:, ends])
    ks = partial_rope(rms_norm(ks, ks_gain), positions)
    kw = partial_rope(rms_norm(kw, kw_gain), positions)
    return kc, vc, ks, vs, kw, vw


def native_sparse_attention(xn, positions, kc, vc, ks, vs, kw, vw, w_in, b_gate, q_gain, w_out):
    B, T, _ = xn.shape
    G, HG, dh = N_KV_GROUPS, HEADS_PER_GROUP, HEAD_DIM
    proj = xn @ w_in
    q = partial_rope(rms_norm(proj[..., :HD].reshape(B, T, N_HEADS, dh), q_gain), positions)
    q = q.reshape(B, T, G, HG, dh).transpose(0, 2, 3, 1, 4)
    gates = jax.nn.sigmoid((proj[..., HD:] + b_gate).astype(jnp.float32)).reshape(B, T, 3, G, HG)
    n_cmp = kc.shape[1]
    n_slc = T // SLC_LEN
    top_k = min(SLC_TOPK, n_slc)
    kc_g = kc.transpose(0, 2, 1, 3)
    vc_g = vc.transpose(0, 2, 1, 3)
    ks_blk = ks.transpose(0, 2, 1, 3).reshape(B, G, n_slc, SLC_LEN * dh)
    vs_blk = vs.transpose(0, 2, 1, 3).reshape(B, G, n_slc, SLC_LEN * dh)
    pad = ((0, 0), (0, 0), (WIN, 0), (0, 0))
    kw_pad = jnp.pad(kw.transpose(0, 2, 1, 3), pad)
    vw_pad = jnp.pad(vw.transpose(0, 2, 1, 3), pad)
    cmp_start = jnp.arange(n_cmp) * CMP_STRIDE
    cmp_end = cmp_start + CMP_LEN - 1
    slc_start = jnp.arange(n_slc) * SLC_LEN
    overlap = jnp.maximum(jnp.minimum(cmp_start[:, None] + CMP_LEN, slc_start[None, :] + SLC_LEN)
                          - jnp.maximum(cmp_start[:, None], slc_start[None, :]), 0).astype(jnp.float32) / CMP_LEN
    scale = dh ** -0.5
    bidx = jnp.arange(B)[:, None, None]
    gidx = jnp.arange(G)[None, :, None]
    jb = jnp.arange(n_slc)

    def block(i):
        q0 = i * Q_BLOCK
        qpos = q0 + jnp.arange(Q_BLOCK)
        qb = lax.dynamic_slice_in_dim(q, q0, Q_BLOCK, axis=3)
        s_c = jnp.einsum('bghqd,bgkd->bghqk', qb, kc_g, preferred_element_type=jnp.float32) * scale
        m_c = cmp_end[None, :] <= qpos[:, None]
        s_c = jnp.where(m_c, s_c, NEG)
        e_c = jnp.exp(s_c - jnp.max(s_c, axis=-1, keepdims=True)) * m_c
        p_c = e_c / jnp.maximum(jnp.sum(e_c, axis=-1, keepdims=True), 1.0)
        o_c = jnp.einsum('bghqk,bgkd->bqghd', p_c.astype(vc_g.dtype), vc_g)
        imp = jnp.einsum('bghqk,kn->bgqn', p_c, overlap)
        cur = qpos // SLC_LEN
        forced = (jb[None, :] == 0) | (jb[None, :] == cur[:, None]) | (jb[None, :] == cur[:, None] - 1)
        eligible = slc_start[None, :] <= qpos[:, None]
        score = jnp.where(eligible, jnp.where(forced, FORCED_SCORE, imp), NEG)
        _, sel = lax.top_k(score, top_k)
        flat = sel.reshape(B, G, Q_BLOCK * top_k)
        k_sel = ks_blk[bidx, gidx, flat].reshape(B, G, Q_BLOCK, top_k * SLC_LEN, dh)
        v_sel = vs_blk[bidx, gidx, flat].reshape(B, G, Q_BLOCK, top_k * SLC_LEN, dh)
        tok = (sel[..., None] * SLC_LEN + jnp.arange(SLC_LEN)).reshape(B, G, Q_BLOCK, top_k * SLC_LEN)
        m_s = tok <= qpos[None, None, :, None]
        s_s = jnp.einsum('bghqd,bgqkd->bghqk', qb, k_sel, preferred_element_type=jnp.float32) * scale
        p_s = jax.nn.softmax(jnp.where(m_s[:, :, None], s_s, -jnp.inf), axis=-1)
        o_s = jnp.einsum('bghqk,bgqkd->bqghd', p_s.astype(v_sel.dtype), v_sel)
        k_win = lax.dynamic_slice_in_dim(kw_pad, q0, WIN + Q_BLOCK, axis=2)
        v_win = lax.dynamic_slice_in_dim(vw_pad, q0, WIN + Q_BLOCK, axis=2)
        kpos = q0 - WIN + jnp.arange(WIN + Q_BLOCK)
        dist = qpos[:, None] - kpos[None, :]
        m_w = (dist >= 0) & (dist < WIN) & (kpos[None, :] >= 0)
        s_w = jnp.einsum('bghqd,bgkd->bghqk', qb, k_win, preferred_element_type=jnp.float32) * scale
        p_w = jax.nn.softmax(jnp.where(m_w, s_w, -jnp.inf), axis=-1)
        o_w = jnp.einsum('bghqk,bgkd->bqghd', p_w.astype(v_win.dtype), v_win)
        g = lax.dynamic_slice_in_dim(gates, q0, Q_BLOCK, axis=1)[..., None]
        o = g[:, :, 0] * o_c + g[:, :, 1] * o_s + g[:, :, 2] * o_w
        return o.astype(xn.dtype)

    o = lax.map(block, jnp.arange(T // Q_BLOCK))
    o = jnp.moveaxis(o, 0, 1).reshape(B, T, HD)
    return o @ w_out


def setup_inputs(seed: int = 0) -> dict:
    key = jax.random.key(seed)
    ks = jax.random.split(key, 32)
    f32 = jnp.float32
    nrm = lambda k, shape, s: jax.random.normal(k, shape, f32) * s
    gain = lambda k, shape: 1.0 + 0.02 * jax.random.normal(k, shape, f32)
    D, F, G = D_MODEL, D_FF, N_KV_GROUPS
    return {
        "x": jax.random.normal(ks[0], (BATCH, SEQ, D), f32),
        "positions": jnp.broadcast_to(jnp.arange(SEQ, dtype=jnp.int32), (BATCH, SEQ)),
        "a_norm": gain(ks[1], (N_A_LAYERS, D)),
        "a_w_in": nrm(ks[2], (N_A_LAYERS, D, 3 * HD + N_HEADS), D ** -0.5),
        "a_b_f": FORGET_BIAS + 0.1 * jax.random.normal(ks[3], (N_A_LAYERS, N_HEADS), f32),
        "a_q_gain": gain(ks[4], (N_A_LAYERS, HEAD_DIM)),
        "a_k_gain": gain(ks[5], (N_A_LAYERS, HEAD_DIM)),
        "a_w_out": nrm(ks[6], (N_A_LAYERS, HD, D), HD ** -0.5),
        "kv_norm": gain(ks[7], (D,)),
        "kv_w": nrm(ks[8], (D, 6 * G * HEAD_DIM), D ** -0.5),
        "kc_pe": nrm(ks[9], (CMP_LEN, HEAD_DIM), 0.02),
        "vc_pe": nrm(ks[10], (CMP_LEN, HEAD_DIM), 0.02),
        "kc_w1": nrm(ks[11], (CMP_LEN * HEAD_DIM, CMP_HIDDEN), (CMP_LEN * HEAD_DIM) ** -0.5),
        "kc_w2": nrm(ks[12], (CMP_HIDDEN, HEAD_DIM), CMP_HIDDEN ** -0.5),
        "vc_w1": nrm(ks[13], (CMP_LEN * HEAD_DIM, CMP_HIDDEN), (CMP_LEN * HEAD_DIM) ** -0.5),
        "vc_w2": nrm(ks[14], (CMP_HIDDEN, HEAD_DIM), CMP_HIDDEN ** -0.5),
        "kc_gain": gain(ks[15], (HEAD_DIM,)),
        "ks_gain": gain(ks[16], (HEAD_DIM,)),
        "kw_gain": gain(ks[17], (HEAD_DIM,)),
        "b_norm": gain(ks[18], (N_B_LAYERS, D)),
        "b_w_in": nrm(ks[19], (N_B_LAYERS, D, HD + 3 * N_HEADS), D ** -0.5),
        "b_b_gate": nrm(ks[20], (N_B_LAYERS, 3 * N_HEADS), 0.01),
        "b_q_gain": gain(ks[21], (N_B_LAYERS, HEAD_DIM)),
        "b_w_out": nrm(ks[22], (N_B_LAYERS, HD, D), HD ** -0.5),
        "f_norm": gain(ks[23], (DEPTH, D)),
        "f_w_up": nrm(ks[24], (DEPTH, D, 2 * F), D ** -0.5),
        "f_conv_w": nrm(ks[25], (DEPTH, CONV_W, 2 * F), CONV_W ** -0.5),
        "f_conv_b": nrm(ks[26], (DEPTH, 2 * F), 0.01),
        "f_w_down": nrm(ks[27], (DEPTH, F, D), F ** -0.5),
    }


def reference(x, positions, a_norm, a_w_in, a_b_f, a_q_gain, a_k_gain, a_w_out, kv_norm, kv_w, kc_pe, vc_pe, kc_w1, kc_w2, vc_w1, vc_w2, kc_gain, ks_gain, kw_gain, b_norm, b_w_in, b_b_gate, b_q_gain, b_w_out, f_norm, f_w_up, f_conv_w, f_conv_b, f_w_down):
    h = x
    kv = None
    for layer in range(DEPTH):
        if layer < N_A_LAYERS:
            i = layer
            h = h + forgetting_attention(rms_norm(h, a_norm[i]), a_w_in[i], a_b_f[i], a_q_gain[i], a_k_gain[i], a_w_out[i])
        else:
            i = layer - N_A_LAYERS
            kc, vc, ks, vs, kw, vw = kv
            h = h + native_sparse_attention(rms_norm(h, b_norm[i]), positions, kc, vc, ks, vs, kw, vw,
                                            b_w_in[i], b_b_gate[i], b_q_gain[i], b_w_out[i])
        h = h + conv_ffn(rms_norm(h, f_norm[layer]), f_w_up[layer], f_conv_w[layer], f_conv_b[layer], f_w_down[layer])
        if layer == N_A_LAYERS - 1:
            kv = shared_kv(h, positions, kv_norm, kv_w, kc_pe, vc_pe, kc_w1, kc_w2, vc_w1, vc_w2, kc_gain, ks_gain, kw_gain)
    return h
```

```python
import functools

import jax
import jax.numpy as jnp
from jax import lax
from jax.experimental import pallas as pl
from jax.experimental.pallas import tpu as pltpu

F32 = jnp.float32
BF16 = jnp.bfloat16

LANES = 128
HEAD_DIM = 64
N_HEADS = 16
N_PAIRS = N_HEADS // 2
N_KV_GROUPS = 2
HEADS_PER_GROUP = N_HEADS // N_KV_GROUPS
ROT_DIM = HEAD_DIM // 4
ROPE_THETA = 500000.0
CMP_LEN = 32
CMP_STRIDE = 16
SLC_LEN = 64
SLC_TOPK = 16
WIN = 512
CONV_W = 3
RMS_EPS = 1e-6
NEG = -1e30
FORCED_SCORE = 1e6
SCALE = HEAD_DIM ** -0.5

VMEM_LIMIT = 48 * 1024 * 1024


def _cparams(sem):
    return pltpu.CompilerParams(dimension_semantics=sem, vmem_limit_bytes=VMEM_LIMIT)


def _iota(shape, axis):
    return lax.broadcasted_iota(jnp.int32, shape, axis)


def _row_rms(x, g):
    ms = jnp.mean(x * x, axis=-1, keepdims=True)
    return x * lax.rsqrt(ms + RMS_EPS) * g


def _pair_rms(y, gain):
    lo = _iota(y.shape, 1) < HEAD_DIM
    y2 = y * y
    s_lo = jnp.sum(jnp.where(lo, y2, 0.0), axis=-1, keepdims=True)
    s_hi = jnp.sum(jnp.where(lo, 0.0, y2), axis=-1, keepdims=True)
    ms = jnp.where(lo, s_lo, s_hi) * (1.0 / HEAD_DIM)
    return y * lax.rsqrt(ms + RMS_EPS) * gain


def _pair_rope(y, cos, sin):
    lane = _iota(y.shape, 1) & (HEAD_DIM - 1)
    partner = jnp.where(lane < ROT_DIM // 2,
                        pltpu.roll(y, LANES - ROT_DIM // 2, 1),
                        pltpu.roll(y, ROT_DIM // 2, 1))
    return y * cos + partner * sin


def _dot(a, b):
    return jnp.dot(a, b, preferred_element_type=F32)


def _dot_nt(a, b):
    return lax.dot_general(a, b, (((1,), (1,)), ((), ())), preferred_element_type=F32)


def _rope_table_kernel(pos_ref, inv_ref, sign_ref, c_ref, s_ref):
    ang = pos_ref[0].astype(F32) * inv_ref[...]
    c_ref[0] = jnp.cos(ang)
    s_ref[0] = jnp.sin(ang) * sign_ref[...]


def _rope_tables(pos, inv_lane, sign_lane):
    B, T = pos.shape
    tm = min(T, 512)
    spec = pl.BlockSpec((1, tm, LANES), lambda b, t: (b, t, 0))
    vec = pl.BlockSpec((1, LANES), lambda b, t: (0, 0))
    return pl.pallas_call(
        _rope_table_kernel,
        grid=(B, T // tm),
        in_specs=[pl.BlockSpec((1, tm, 1), lambda b, t: (b, t, 0)), vec, vec],
        out_specs=[spec, spec],
        out_shape=[jax.ShapeDtypeStruct((B, T, LANES), F32)] * 2,
        compiler_params=_cparams(("parallel", "parallel")),
        name="rope_tables",
    )(pos.reshape(B, T, 1), inv_lane, sign_lane)


def _fox_inproj_kernel(x_ref, g_ref, w_ref, bf_ref, qg_ref, kg_ref,
                       q_ref, k_ref, v_ref, nc_ref, carry_sc, *, tm, hd):
    ti = pl.program_id(1)
    xn = _row_rms(x_ref[0], g_ref[...]).astype(BF16)
    for j in range(hd // 256):
        for part, (ref, gain, mul) in enumerate(((q_ref, qg_ref, SCALE), (k_ref, kg_ref, 1.0), (v_ref, None, 1.0))):
            c0 = part * hd + 256 * j
            y = _dot(xn, w_ref[:, c0:c0 + 256])
            for hh in range(2):
                blk = y[:, LANES * hh:LANES * (hh + 1)]
                if gain is not None:
                    blk = _pair_rms(blk, gain[...]) * mul
                ref[0, :, 256 * j + LANES * hh:256 * j + LANES * (hh + 1)] = blk.astype(BF16)
    z = _dot(xn, w_ref[:, 3 * hd:3 * hd + LANES]) + bf_ref[...]
    lf = jnp.minimum(z, 0.0) - jnp.log1p(jnp.exp(-jnp.abs(z)))
    row = _iota(lf.shape, 0)
    sh = 1
    while sh < tm:
        lf = lf + jnp.where(row >= sh, pltpu.roll(lf, sh, 0), 0.0)
        sh *= 2

    @pl.when(ti == 0)
    def _():
        carry_sc[...] = jnp.zeros_like(carry_sc)

    c = lf + carry_sc[0:1, :]
    carry_sc[...] = jnp.broadcast_to(c[tm - 1:tm, :], carry_sc.shape)
    nc_ref[0] = -c


def _fox_inproj(x, g, w, bf, qg, kg, tm=512):
    B, T, D = x.shape
    hd = N_HEADS * HEAD_DIM
    act = pl.BlockSpec((1, tm, hd), lambda b, t: (b, t, 0))
    vec = lambda n: pl.BlockSpec((1, n), lambda b, t: (0, 0))
    return pl.pallas_call(
        functools.partial(_fox_inproj_kernel, tm=tm, hd=hd),
        grid=(B, T // tm),
        in_specs=[pl.BlockSpec((1, tm, D), lambda b, t: (b, t, 0)), vec(D),
                  pl.BlockSpec(w.shape, lambda b, t: (0, 0)), vec(LANES), vec(LANES), vec(LANES)],
        out_specs=[act, act, act, pl.BlockSpec((1, tm, LANES), lambda b, t: (b, t, 0))],
        out_shape=[jax.ShapeDtypeStruct((B, T, hd), BF16)] * 3 + [jax.ShapeDtypeStruct((B, T, LANES), F32)],
        scratch_shapes=[pltpu.VMEM((8, LANES), F32)],
        compiler_params=_cparams(("arbitrary", "arbitrary")),
        name="fox_inproj",
    )(x, g, w, bf, qg, kg)


def _online_softmax_step(s, v, m_ref, l_ref, acc_ref):
    m_old = m_ref[...]
    m_new = jnp.maximum(m_old, jnp.max(s, axis=-1, keepdims=True))
    alpha = jnp.exp(m_old - m_new)
    p = jnp.exp(s - m_new)
    l_ref[...] = alpha * l_ref[...] + jnp.sum(p, axis=-1, keepdims=True)
    acc_ref[...] = alpha * acc_ref[...] + _dot(p.astype(BF16), v)
    m_ref[...] = m_new


def _fox_attn_kernel(q_ref, k_ref, v_ref, cb_ref, o_ref, m_sc, l_sc, acc_sc, *, tq, tk):
    qi = pl.program_id(2)
    ki = pl.program_id(3)
    last = (qi * tq + tq - 1) // tk

    @pl.when(ki == 0)
    def _():
        m_sc[...] = jnp.full_like(m_sc, NEG)
        l_sc[...] = jnp.zeros_like(l_sc)
        acc_sc[...] = jnp.zeros_like(acc_sc)

    def step(diag):
        q = q_ref[0]
        k = k_ref[0]
        v = v_ref[0]
        lo = _iota(q.shape, 1) < HEAD_DIM
        zero = jnp.zeros_like(q)
        for hh in range(2):
            qh = jnp.where(lo, q, zero) if hh == 0 else jnp.where(lo, zero, q)
            s = _dot_nt(qh, k) + cb_ref[0, 0, hh:hh + 1, :]
            if diag:
                qpos = qi * tq + _iota(s.shape, 0)
                kpos = ki * tk + _iota(s.shape, 1)
                s = jnp.where(kpos <= qpos, s, NEG)
            _online_softmax_step(s, v, m_sc.at[hh], l_sc.at[hh], acc_sc.at[hh])

    @pl.when(ki < last)
    def _():
        step(False)

    @pl.when(ki == last)
    def _():
        step(True)
        lo = _iota((tq, LANES), 1) < HEAD_DIM
        o = jnp.where(lo, acc_sc[0] / l_sc[0], acc_sc[1] / l_sc[1])
        o_ref[0] = o.astype(BF16)


def _fox_attn(q, k, v, cb, tq=512, tk=512):
    B, T, hd = q.shape
    last = lambda qi: (qi * tq + tq - 1) // tk
    kv_spec = pl.BlockSpec((1, tk, LANES), lambda b, p, qi, ki: (b, jnp.minimum(ki, last(qi)), p))
    return pl.pallas_call(
        functools.partial(_fox_attn_kernel, tq=tq, tk=tk),
        grid=(B, N_PAIRS, T // tq, T // tk),
        in_specs=[pl.BlockSpec((1, tq, LANES), lambda b, p, qi, ki: (b, qi, p)), kv_spec, kv_spec,
                  pl.BlockSpec((1, 1, 2, tk), lambda b, p, qi, ki: (b, p, 0, jnp.minimum(ki, last(qi))))],
        out_specs=pl.BlockSpec((1, tq, LANES), lambda b, p, qi, ki: (b, qi, p)),
        out_shape=jax.ShapeDtypeStruct((B, T, hd), BF16),
        scratch_shapes=[pltpu.VMEM((2, tq, 1), F32), pltpu.VMEM((2, tq, 1), F32), pltpu.VMEM((2, tq, LANES), F32)],
        compiler_params=_cparams(("parallel", "parallel", "parallel", "arbitrary")),
        name="fox_attn",
    )(q, k, v, cb)


def _outproj_kernel(h_ref, o_ref, w_ref, out_ref):
    out_ref[...] = h_ref[...] + _dot(o_ref[...], w_ref[...])


def _outproj(h, o, w, tm=512):
    B, T, D = h.shape
    n = B * T
    row = lambda c: pl.BlockSpec((tm, c), lambda i: (i, 0))
    out = pl.pallas_call(
        _outproj_kernel,
        grid=(n // tm,),
        in_specs=[row(D), row(o.shape[-1]), pl.BlockSpec(w.shape, lambda i: (0, 0))],
        out_specs=row(D),
        out_shape=jax.ShapeDtypeStruct((n, D), F32),
        compiler_params=_cparams(("parallel",)),
        name="outproj",
    )(h.reshape(n, D), o.reshape(n, -1), w)
    return out.reshape(B, T, D)


def _nsa_outproj_kernel(h_ref, oc_ref, os_ref, ow_ref, g_ref, w_ref, out_ref, o_sc):
    g = g_ref[...]
    lo = _iota((g.shape[0], LANES), 1) < HEAD_DIM
    for j in range(N_PAIRS):
        acc = None
        for br, ref in enumerate((oc_ref, os_ref, ow_ref)):
            c0 = br * N_HEADS + 2 * j
            gg = jnp.where(lo, g[:, c0:c0 + 1], g[:, c0 + 1:c0 + 2])
            term = gg * ref[:, LANES * j:LANES * (j + 1)].astype(F32)
            acc = term if acc is None else acc + term
        o_sc[:, LANES * j:LANES * (j + 1)] = acc.astype(BF16)
    out_ref[...] = h_ref[...] + _dot(o_sc[...], w_ref[...])


def _nsa_outproj(h, oc, osel, ow, gates, w, tm=512):
    B, T, D = h.shape
    n = B * T
    hd = oc.shape[-1]
    row = lambda c: pl.BlockSpec((tm, c), lambda i: (i, 0))
    out = pl.pallas_call(
        _nsa_outproj_kernel,
        grid=(n // tm,),
        in_specs=[row(D), row(hd), row(hd), row(hd), row(LANES), pl.BlockSpec(w.shape, lambda i: (0, 0))],
        out_specs=row(D),
        out_shape=jax.ShapeDtypeStruct((n, D), F32),
        scratch_shapes=[pltpu.VMEM((tm, hd), BF16)],
        compiler_params=_cparams(("parallel",)),
        name="nsa_outproj",
    )(h.reshape(n, D), oc.reshape(n, hd), osel.reshape(n, hd), ow.reshape(n, hd), gates.reshape(n, LANES), w)
    return out.reshape(B, T, D)


def _ffn_kernel(x_ref, g_ref, wg_ref, wv_ref, cwg_ref, cwv_ref, cbg_ref, cbv_ref, wd_ref,
                out_ref, xn_sc, acc_sc, carry_sc, *, tt, tf, nf):
    ti = pl.program_id(1)
    f = pl.program_id(2)

    @pl.when(f == 0)
    def _():
        xn_sc[...] = _row_rms(x_ref[0], g_ref[...]).astype(BF16)
        acc_sc[...] = jnp.zeros_like(acc_sc)

    @pl.when(ti == 0)
    def _():
        carry_sc[f] = jnp.zeros(carry_sc.shape[1:], F32)

    xn = xn_sc[...]
    prev = carry_sc[f]
    row = _iota((tt, tf), 0)

    def conv(u, cw_ref, cb_ref, prev8):
        um1 = jnp.where(row == 0, prev8[7:8, :], pltpu.roll(u, 1, 0))
        um2 = jnp.where(row == 0, prev8[6:7, :], jnp.where(row == 1, prev8[7:8, :], pltpu.roll(u, 2, 0)))
        cw = cw_ref[...]
        return cb_ref[...] + cw[0:1, :] * um2 + cw[1:2, :] * um1 + cw[2:3, :] * u

    ug = _dot(xn, wg_ref[...])
    uv = _dot(xn, wv_ref[...])
    cg = conv(ug, cwg_ref, cbg_ref, prev[:, :tf])
    cv = conv(uv, cwv_ref, cbv_ref, prev[:, tf:])
    carry_sc[f] = jnp.concatenate([ug[tt - 8:, :], uv[tt - 8:, :]], axis=1)
    a = cg * jax.nn.sigmoid(cg) * cv
    acc_sc[...] += _dot(a.astype(BF16), wd_ref[...])

    @pl.when(f == nf - 1)
    def _():
        out_ref[0] = x_ref[0] + acc_sc[...]


def _conv_ffn(h, g, w_up, conv_w, conv_b, w_down, tt=512, tf=256):
    B, T, D = h.shape
    d_ff = w_down.shape[0]
    nf = d_ff // tf
    act = pl.BlockSpec((1, tt, D), lambda b, t, f: (b, t, 0))
    return pl.pallas_call(
        functools.partial(_ffn_kernel, tt=tt, tf=tf, nf=nf),
        grid=(B, T // tt, nf),
        in_specs=[act, pl.BlockSpec((1, D), lambda b, t, f: (0, 0)),
                  pl.BlockSpec((D, tf), lambda b, t, f: (0, f)),
                  pl.BlockSpec((D, tf), lambda b, t, f: (0, f + nf)),
                  pl.BlockSpec((CONV_W, tf), lambda b, t, f: (0, f)),
                  pl.BlockSpec((CONV_W, tf), lambda b, t, f: (0, f + nf)),
                  pl.BlockSpec((1, tf), lambda b, t, f: (0, f)),
                  pl.BlockSpec((1, tf), lambda b, t, f: (0, f + nf)),
                  pl.BlockSpec((tf, D), lambda b, t, f: (f, 0))],
        out_specs=act,
        out_shape=jax.ShapeDtypeStruct((B, T, D), F32),
        scratch_shapes=[pltpu.VMEM((tt, D), BF16), pltpu.VMEM((tt, D), F32), pltpu.VMEM((nf, 8, 2 * tf), F32)],
        compiler_params=_cparams(("arbitrary", "arbitrary", "arbitrary")),
        name="conv_ffn",
    )(h, g, w_up, w_up, conv_w, conv_w, conv_b, conv_b, w_down)


def _kvproj_kernel(h_ref, g_ref, w_ref, ksg_ref, kwg_ref, c_ref, s_ref,
                   craw_ref, ks_ref, vs_ref, kw_ref, vw_ref):
    xn = _row_rms(h_ref[0], g_ref[...]).astype(BF16)
    cos = c_ref[0]
    sin = s_ref[0]
    craw_ref[0] = _dot(xn, w_ref[:, 0:256])
    for idx, (ref, gain) in enumerate(((ks_ref, ksg_ref), (vs_ref, None), (kw_ref, kwg_ref), (vw_ref, None))):
        y = _dot(xn, w_ref[:, 256 * (idx + 1):256 * (idx + 2)])
        for grp in range(N_KV_GROUPS):
            blk = y[:, LANES * grp:LANES * (grp + 1)]
            if gain is not None:
                blk = _pair_rope(_pair_rms(blk, gain[...]), cos, sin)
            ref[0, grp] = blk.astype(BF16)


def _kvproj(h, g, w, ksg, kwg, cos, sin, tm=512):
    B, T, D = h.shape
    vec = lambda n: pl.BlockSpec((1, n), lambda b, t: (0, 0))
    tab = pl.BlockSpec((1, tm, LANES), lambda b, t: (b, t, 0))
    dup = pl.BlockSpec((1, N_KV_GROUPS, tm, LANES), lambda b, t: (b, 0, t, 0))
    dup_shape = jax.ShapeDtypeStruct((B, N_KV_GROUPS, T, LANES), BF16)
    return pl.pallas_call(
        _kvproj_kernel,
        grid=(B, T // tm),
        in_specs=[pl.BlockSpec((1, tm, D), lambda b, t: (b, t, 0)), vec(D),
                  pl.BlockSpec(w.shape, lambda b, t: (0, 0)), vec(LANES), vec(LANES), tab, tab],
        out_specs=[pl.BlockSpec((1, tm, 256), lambda b, t: (b, t, 0)), dup, dup, dup, dup],
        out_shape=[jax.ShapeDtypeStruct((B, T, 256), F32)] + [dup_shape] * 4,
        compiler_params=_cparams(("parallel", "parallel")),
        name="kvproj",
    )(h, g, w, ksg, kwg, cos, sin)


def _compress_kernel(r_ref, w1_ref, pe_ref, w2_ref, gain_ref, c_ref, s_ref, out_ref, *, n_cmp):
    kv = pl.program_id(1)
    r = r_ref[0, 0, 0].astype(BF16)
    half = r.shape[1]
    a = _dot(r, w1_ref[0, :half, :])
    b = _dot(r, w1_ref[0, half:, :])
    peb = _dot(pe_ref[0], w1_ref[0])[0:1, :]
    rows = r.shape[0]
    hid = a + pltpu.roll(b, rows - 1, 0) + peb
    y = _dot(jax.nn.gelu(hid).astype(BF16), w2_ref[0])
    yk = _pair_rope(_pair_rms(y, gain_ref[...]), c_ref[0], s_ref[0])
    y = jnp.where(kv == 0, yk, y)
    out_ref[0, 0, 0] = jnp.where(_iota(y.shape, 0) < n_cmp, y, 0.0).astype(BF16)


def _compress(r, w1, pe, w2, gain, cos, sin, n_cmp):
    B, _, G, rows, width = r.shape
    return pl.pallas_call(
        functools.partial(_compress_kernel, n_cmp=n_cmp),
        grid=(B, 2, G),
        in_specs=[pl.BlockSpec((1, 1, 1, rows, width), lambda b, kv, g: (b, kv, g, 0, 0)),
                  pl.BlockSpec((1,) + w1.shape[1:], lambda b, kv, g: (kv, 0, 0)),
                  pl.BlockSpec((1,) + pe.shape[1:], lambda b, kv, g: (kv, 0, 0)),
                  pl.BlockSpec((1,) + w2.shape[1:], lambda b, kv, g: (kv, 0, 0)),
                  pl.BlockSpec((1, LANES), lambda b, kv, g: (0, 0)),
                  pl.BlockSpec((1, rows, LANES), lambda b, kv, g: (b, 0, 0)),
                  pl.BlockSpec((1, rows, LANES), lambda b, kv, g: (b, 0, 0))],
        out_specs=pl.BlockSpec((1, 1, 1, rows, LANES), lambda b, kv, g: (b, kv, g, 0, 0)),
        out_shape=jax.ShapeDtypeStruct((B, 2, G, rows, LANES), BF16),
        compiler_params=_cparams(("parallel", "parallel", "parallel")),
        name="compress",
    )(r, w1, pe, w2, gain, cos, sin)


def _nsa_inproj_kernel(h_ref, g_ref, w_ref, bg_ref, qg_ref, c_ref, s_ref, q_ref, gate_ref, *, hd):
    xn = _row_rms(h_ref[0], g_ref[...]).astype(BF16)
    cos = c_ref[0]
    sin = s_ref[0]
    for j in range(hd // 256):
        y = _dot(xn, w_ref[:, 256 * j:256 * (j + 1)])
        for hh in range(2):
            blk = _pair_rope(_pair_rms(y[:, LANES * hh:LANES * (hh + 1)], qg_ref[...]), cos, sin) * SCALE
            q_ref[0, :, 256 * j + LANES * hh:256 * j + LANES * (hh + 1)] = blk.astype(BF16)
    z = _dot(xn, w_ref[:, hd:hd + LANES]) + bg_ref[...]
    gate_ref[0] = jax.nn.sigmoid(z)


def _nsa_inproj(h, g, w, bg, qg, cos, sin, tm=512):
    B, T, D = h.shape
    hd = N_HEADS * HEAD_DIM
    vec = lambda n: pl.BlockSpec((1, n), lambda b, t: (0, 0))
    tab = pl.BlockSpec((1, tm, LANES), lambda b, t: (b, t, 0))
    return pl.pallas_call(
        functools.partial(_nsa_inproj_kernel, hd=hd),
        grid=(B, T // tm),
        in_specs=[pl.BlockSpec((1, tm, D), lambda b, t: (b, t, 0)), vec(D),
                  pl.BlockSpec(w.shape, lambda b, t: (0, 0)), vec(LANES), vec(LANES), tab, tab],
        out_specs=[pl.BlockSpec((1, tm, hd), lambda b, t: (b, t, 0)), tab],
        out_shape=[jax.ShapeDtypeStruct((B, T, hd), BF16), jax.ShapeDtypeStruct((B, T, LANES), F32)],
        compiler_params=_cparams(("parallel", "parallel")),
        name="nsa_inproj",
    )(h, g, w, bg, qg, cos, sin)


def _stack_heads(q, dst_ref, tq, width):
    lo = _iota((tq, LANES), 1) < HEAD_DIM
    zero = jnp.zeros((tq, LANES), q.dtype)
    for j in range(HEADS_PER_GROUP // 2):
        x = q[:, LANES * j:LANES * (j + 1)]
        dst_ref[(2 * j) * tq:(2 * j + 1) * tq, 0:LANES] = jnp.where(lo, x, zero)
        dst_ref[(2 * j + 1) * tq:(2 * j + 2) * tq, 0:LANES] = jnp.where(lo, zero, x)


def _unstack_heads(o, o_ref, tq):
    lo = _iota((tq, LANES), 1) < HEAD_DIM
    for j in range(HEADS_PER_GROUP // 2):
        even = o[(2 * j) * tq:(2 * j + 1) * tq, :]
        odd = o[(2 * j + 1) * tq:(2 * j + 2) * tq, :]
        o_ref[0, :, LANES * j:LANES * (j + 1)] = jnp.where(lo, even, odd).astype(o_ref.dtype)


def _nsa_cmp_kernel(q_ref, kc_ref, vc_ref, ov_ref, oc_ref, sel_ref, qs_sc, *, tq, n_slc, top_k):
    qi = pl.program_id(2)
    _stack_heads(q_ref[0], qs_sc, tq, LANES)
    kc = kc_ref[0, 0, 0]
    vc = vc_ref[0, 0, 0]
    n_rows = kc.shape[0]
    qpos = qi * tq + _iota((tq, n_rows), 0)
    vis = _iota((tq, n_rows), 1) * CMP_STRIDE + (CMP_LEN - 1) <= qpos
    lo = _iota((tq, LANES), 1) < HEAD_DIM
    p_sum = jnp.zeros((tq, n_rows), F32)
    even = None
    for h in range(HEADS_PER_GROUP):
        s = jnp.where(vis, _dot_nt(qs_sc[h * tq:(h + 1) * tq, :], kc), NEG)
        e = jnp.where(vis, jnp.exp(s - jnp.max(s, axis=-1, keepdims=True)), 0.0)
        p = e / jnp.maximum(jnp.sum(e, axis=-1, keepdims=True), 1.0)
        o = _dot(p.astype(BF16), vc)
        p_sum = p_sum + p
        if h % 2 == 0:
            even = o
        else:
            oc_ref[0, :, LANES * (h // 2):LANES * (h // 2 + 1)] = jnp.where(lo, even, o).astype(BF16)
    p_hi = p_sum.astype(BF16)
    p_lo = (p_sum - p_hi.astype(F32)).astype(BF16)
    imp = _dot(p_hi, ov_ref[...]) + _dot(p_lo, ov_ref[...])
    jb = _iota((tq, LANES), 1)
    qp = qi * tq + _iota((tq, LANES), 0)
    cur = qp // SLC_LEN
    forced = (jb == 0) | (jb == cur) | (jb == cur - 1)
    score = jnp.where(jb * SLC_LEN <= qp, jnp.where(forced, FORCED_SCORE, imp), NEG)
    score = jnp.where(jb < n_slc, score, -jnp.inf)
    bias = jnp.full((tq, LANES), NEG, F32)
    for _ in range(top_k):
        mx = jnp.max(score, axis=-1, keepdims=True)
        idx = jnp.min(jnp.where(score == mx, jb, LANES), axis=-1, keepdims=True)
        pick = jb == idx
        bias = jnp.where(pick, 0.0, bias)
        score = jnp.where(pick, -jnp.inf, score)
    sel_ref[0, 0] = bias.astype(BF16)


def _nsa_cmp(q, kcvc, overlap, n_slc, tq=128):
    B, T, hd = q.shape
    G = N_KV_GROUPS
    rows = kcvc.shape[3]
    gw = hd // G
    top_k = min(SLC_TOPK, n_slc)
    return pl.pallas_call(
        functools.partial(_nsa_cmp_kernel, tq=tq, n_slc=n_slc, top_k=top_k),
        grid=(B, G, T // tq),
        in_specs=[pl.BlockSpec((1, tq, gw), lambda b, g, qi: (b, qi, g)),
                  pl.BlockSpec((1, 1, 1, rows, LANES), lambda b, g, qi: (b, 0, g, 0, 0)),
                  pl.BlockSpec((1, 1, 1, rows, LANES), lambda b, g, qi: (b, 1, g, 0, 0)),
                  pl.BlockSpec(overlap.shape, lambda b, g, qi: (0, 0))],
        out_specs=[pl.BlockSpec((1, tq, gw), lambda b, g, qi: (b, qi, g)),
                   pl.BlockSpec((1, 1, tq, LANES), lambda b, g, qi: (b, g, qi, 0))],
        out_shape=[jax.ShapeDtypeStruct((B, T, hd), BF16), jax.ShapeDtypeStruct((B, G, T, LANES), BF16)],
        scratch_shapes=[pltpu.VMEM((HEADS_PER_GROUP * tq, LANES), BF16)],
        compiler_params=_cparams(("parallel", "parallel", "parallel")),
        name="nsa_cmp",
    )(q, kcvc, kcvc, overlap)


def _nsa_slc_kernel(q_ref, sel_ref, k_ref, v_ref, o_ref, qa_sc, m_sc, l_sc, acc_sc, *, tq, tk):
    qi = pl.program_id(2)
    ki = pl.program_id(3)
    last = (qi * tq + tq - 1) // tk

    @pl.when(ki == 0)
    def _():
        _stack_heads(q_ref[0], qa_sc, tq, 2 * LANES)
        sel = sel_ref[0, 0]
        for h in range(HEADS_PER_GROUP):
            qa_sc[h * tq:(h + 1) * tq, LANES:2 * LANES] = sel
        m_sc[...] = jnp.full_like(m_sc, NEG)
        l_sc[...] = jnp.zeros_like(l_sc)
        acc_sc[...] = jnp.zeros_like(acc_sc)

    def step(diag):
        k = k_ref[0, 0]
        blk = (ki * tk + _iota((tk, LANES), 0)) // SLC_LEN
        onehot = jnp.where(blk == _iota((tk, LANES), 1), 1.0, 0.0).astype(BF16)
        s = _dot_nt(qa_sc[...], jnp.concatenate([k, onehot], axis=1))
        if diag:
            qpos = qi * tq + (_iota(s.shape, 0) & (tq - 1))
            kpos = ki * tk + _iota(s.shape, 1)
            s = jnp.where(kpos <= qpos, s, NEG)
        _online_softmax_step(s, v_ref[0, 0], m_sc, l_sc, acc_sc)

    @pl.when(ki < last)
    def _():
        step(False)

    @pl.when(ki == last)
    def _():
        step(True)
        _unstack_heads(acc_sc[...] / l_sc[...], o_ref, tq)


def _nsa_slc(q, sel, ks, vs, tq=128, tk=512):
    B, T, hd = q.shape
    G = N_KV_GROUPS
    gw = hd // G
    rows = HEADS_PER_GROUP * tq
    last = lambda qi: (qi * tq + tq - 1) // tk
    kv_spec = pl.BlockSpec((1, 1, tk, LANES), lambda b, g, qi, ki: (b, g, jnp.minimum(ki, last(qi)), 0))
    return pl.pallas_call(
        functools.partial(_nsa_slc_kernel, tq=tq, tk=tk),
        grid=(B, G, T // tq, T // tk),
        in_specs=[pl.BlockSpec((1, tq, gw), lambda b, g, qi, ki: (b, qi, g)),
                  pl.BlockSpec((1, 1, tq, LANES), lambda b, g, qi, ki: (b, g, qi, 0)), kv_spec, kv_spec],
        out_specs=pl.BlockSpec((1, tq, gw), lambda b, g, qi, ki: (b, qi, g)),
        out_shape=jax.ShapeDtypeStruct((B, T, hd), BF16),
        scratch_shapes=[pltpu.VMEM((rows, 2 * LANES), BF16), pltpu.VMEM((rows, 1), F32),
                        pltpu.VMEM((rows, 1), F32), pltpu.VMEM((rows, LANES), F32)],
        compiler_params=_cparams(("parallel", "parallel", "parallel", "arbitrary")),
        name="nsa_slc",
    )(q, sel, ks, vs)


def _nsa_win_kernel(q_ref, k_ref, v_ref, o_ref, qs_sc, m_sc, l_sc, acc_sc, *, tq, n_steps):
    qi = pl.program_id(2)
    j = pl.program_id(3)
    tile = qi - (n_steps - 1) + j

    @pl.when(j == 0)
    def _():
        _stack_heads(q_ref[0], qs_sc, tq, LANES)
        m_sc[...] = jnp.full_like(m_sc, NEG)
        l_sc[...] = jnp.zeros_like(l_sc)
        acc_sc[...] = jnp.zeros_like(acc_sc)

    def step(kind):
        s = _dot_nt(qs_sc[...], k_ref[0, 0])
        r = _iota(s.shape, 0) & (tq - 1)
        c = _iota(s.shape, 1)
        if kind == "oldest":
            s = jnp.where(c > r, s, NEG)
        elif kind == "diag":
            s = jnp.where(c <= r, s, NEG)
        _online_softmax_step(s, v_ref[0, 0], m_sc, l_sc, acc_sc)

    @pl.when((j == 0) & (tile >= 0))
    def _():
        step("oldest")

    @pl.when((j > 0) & (j < n_steps - 1) & (tile >= 0))
    def _():
        step("full")

    @pl.when(j == n_steps - 1)
    def _():
        step("diag")
        _unstack_heads(acc_sc[...] / l_sc[...], o_ref, tq)


def _nsa_win(q, kw, vw, tq=256):
    B, T, hd = q.shape
    G = N_KV_GROUPS
    gw = hd // G
    rows = HEADS_PER_GROUP * tq
    n_steps = WIN // tq + 1
    kv_spec = pl.BlockSpec((1, 1, tq, LANES),
                           lambda b, g, qi, j: (b, g, jnp.maximum(qi - (n_steps - 1) + j, 0), 0))
    return pl.pallas_call(
        functools.partial(_nsa_win_kernel, tq=tq, n_steps=n_steps),
        grid=(B, G, T // tq, n_steps),
        in_specs=[pl.BlockSpec((1, tq, gw), lambda b, g, qi, j: (b, qi, g)), kv_spec, kv_spec],
        out_specs=pl.BlockSpec((1, tq, gw), lambda b, g, qi, j: (b, qi, g)),
        out_shape=jax.ShapeDtypeStruct((B, T, hd), BF16),
        scratch_shapes=[pltpu.VMEM((rows, LANES), BF16), pltpu.VMEM((rows, 1), F32),
                        pltpu.VMEM((rows, 1), F32), pltpu.VMEM((rows, LANES), F32)],
        compiler_params=_cparams(("parallel", "parallel", "parallel", "arbitrary")),
        name="nsa_win",
    )(q, kw, vw)


def _pad_cols(a, n):
    return jnp.pad(a, ((0, 0), (0, n - a.shape[1])))


def _lane_vec(v):
    return jnp.tile(v.astype(F32), 2).reshape(1, LANES)


def kernel(x, positions, a_norm, a_w_in, a_b_f, a_q_gain, a_k_gain, a_w_out, kv_norm, kv_w, kc_pe, vc_pe, kc_w1, kc_w2, vc_w1, vc_w2, kc_gain, ks_gain, kw_gain, b_norm, b_w_in, b_b_gate, b_q_gain, b_w_out, f_norm, f_w_up, f_conv_w, f_conv_b, f_w_down):
    B, T, D = x.shape
    hd = N_HEADS * HEAD_DIM
    G = N_KV_GROUPS
    n_a = a_norm.shape[0]
    n_b = b_norm.shape[0]
    depth = n_a + n_b
    n_slc = T // SLC_LEN
    n_cmp = (T - CMP_LEN) // CMP_STRIDE + 1
    assert T % 1024 == 0 and n_slc <= LANES and hd == 1024 and D == 1024

    half = ROT_DIM // 2
    inv = ROPE_THETA ** (-jnp.arange(half, dtype=F32) * (2.0 / ROT_DIM))
    head_inv = jnp.concatenate([inv, inv, jnp.zeros((HEAD_DIM - ROT_DIM,), F32)])
    head_sign = jnp.concatenate([-jnp.ones((half,), F32), jnp.ones((half,), F32), jnp.zeros((HEAD_DIM - ROT_DIM,), F32)])
    inv_lane = jnp.tile(head_inv, 2).reshape(1, LANES)
    sign_lane = jnp.tile(head_sign, 2).reshape(1, LANES)
    cos_t, sin_t = _rope_tables(positions, inv_lane, sign_lane)
    end_pos = positions[:, CMP_LEN - 1::CMP_STRIDE]
    end_pos = jnp.pad(end_pos, ((0, 0), (0, T // CMP_STRIDE - n_cmp)))
    cos_c, sin_c = _rope_tables(end_pos, inv_lane, sign_lane)

    cs = jnp.arange(T // CMP_STRIDE) * CMP_STRIDE
    ss = jnp.arange(LANES) * SLC_LEN
    overlap = (jnp.maximum(jnp.minimum(cs[:, None] + CMP_LEN, ss[None, :] + SLC_LEN)
                           - jnp.maximum(cs[:, None], ss[None, :]), 0).astype(F32) / CMP_LEN).astype(BF16)

    h = x
    kv = None
    for layer in range(depth):
        if layer < n_a:
            i = layer
            w = jnp.concatenate([a_w_in[i][:, :3 * hd], _pad_cols(a_w_in[i][:, 3 * hd:], LANES)], axis=1).astype(BF16)
            bf = _pad_cols(a_b_f[i].reshape(1, -1), LANES)
            q, k, v, negc = _fox_inproj(h, a_norm[i].reshape(1, D), w, bf, _lane_vec(a_q_gain[i]), _lane_vec(a_k_gain[i]))
            cb = negc[:, :, :N_HEADS].transpose(0, 2, 1).reshape(B, N_PAIRS, 2, T)
            o = _fox_attn(q, k, v, cb)
            h = _outproj(h, o, a_w_out[i].astype(BF16))
        else:
            i = layer - n_a
            kcvc, ks, vs, kw, vw = kv
            w = jnp.concatenate([b_w_in[i][:, :hd], _pad_cols(b_w_in[i][:, hd:], LANES)], axis=1).astype(BF16)
            bg = _pad_cols(b_b_gate[i].reshape(1, -1), LANES)
            q, gates = _nsa_inproj(h, b_norm[i].reshape(1, D), w, bg, _lane_vec(b_q_gain[i]), cos_t, sin_t)
            o_c, sel = _nsa_cmp(q, kcvc, overlap, n_slc)
            o_s = _nsa_slc(q, sel, ks, vs)
            o_w = _nsa_win(q, kw, vw)
            h = _nsa_outproj(h, o_c, o_s, o_w, gates, b_w_out[i].astype(BF16))
        h = _conv_ffn(h, f_norm[layer].reshape(1, D), f_w_up[layer].astype(BF16), f_conv_w[layer],
                      f_conv_b[layer].reshape(1, -1), f_w_down[layer].astype(BF16))
        if layer == n_a - 1:
            w6 = kv_w.reshape(D, 6, G, HEAD_DIM)
            raw_cols = w6[:, 0:2].reshape(D, 2 * G * HEAD_DIM)
            dup_cols = jnp.concatenate([w6[:, 2:], w6[:, 2:]], axis=-1).reshape(D, 4 * G * LANES)
            wkv = jnp.concatenate([raw_cols, dup_cols], axis=1).astype(BF16)
            craw, ks, vs, kw, vw = _kvproj(h, kv_norm.reshape(1, D), wkv, _lane_vec(ks_gain), _lane_vec(kw_gain), cos_t, sin_t)
            r = craw.reshape(B, T, 2, G, HEAD_DIM).transpose(0, 2, 3, 1, 4).reshape(B, 2, G, T // CMP_STRIDE, CMP_STRIDE * HEAD_DIM)
            w1 = jnp.stack([kc_w1, vc_w1]).astype(BF16)
            pe = jnp.stack([kc_pe.reshape(1, -1), vc_pe.reshape(1, -1)])
            pe = jnp.pad(pe, ((0, 0), (0, 7), (0, 0))).astype(BF16)
            w2 = jnp.stack([jnp.concatenate([kc_w2, kc_w2], axis=1), jnp.concatenate([vc_w2, vc_w2], axis=1)]).astype(BF16)
            kcvc = _compress(r, w1, pe, w2, _lane_vec(kc_gain), cos_c, sin_c, n_cmp)
            kv = (kcvc, ks, vs, kw, vw)
    return h
```

```python
import functools

import numpy as np
import jax
import jax.numpy as jnp
from jax import lax
from jax.experimental import pallas as pl
from jax.experimental.pallas import tpu as pltpu

F32 = jnp.float32
BF16 = jnp.bfloat16

LANES = 128
SUBLANES = 8
HEAD_DIM = 64
N_HEADS = 16
N_PAIRS = N_HEADS // 2
N_KV_GROUPS = 2
HEADS_PER_GROUP = N_HEADS // N_KV_GROUPS
ROT_DIM = HEAD_DIM // 4
ROPE_THETA = 500000.0
CMP_LEN = 32
CMP_STRIDE = 16
SLC_LEN = 64
SLC_TOPK = 16
WIN = 512
CONV_W = 3
RMS_EPS = 1e-6
NEG = -1e30
FORCED_SCORE = 1e6
LOG2E = 1.4426950408889634
Q_SCALE = HEAD_DIM ** -0.5 * LOG2E
N_BIAS_PARTS = 3

ROW_TILE = 512
FOX_TQ = 512
FOX_TK = ROW_TILE
SLC_TQ = 128
SLC_TK = ROW_TILE
WIN_TQ = 256
CMP_TQ = 128

VMEM_LIMIT = 48 * 1024 * 1024


def _cparams(sem):
    return pltpu.CompilerParams(dimension_semantics=sem, vmem_limit_bytes=VMEM_LIMIT)


def _iota(shape, axis):
    return lax.broadcasted_iota(jnp.int32, shape, axis)


def _row_rms(x, g):
    ms = jnp.mean(x * x, axis=-1, keepdims=True)
    return x * lax.rsqrt(ms + RMS_EPS) * g


def _pair_rms(y, gain):
    lo = _iota(y.shape, 1) < HEAD_DIM
    y2 = y * y
    s_lo = jnp.sum(jnp.where(lo, y2, 0.0), axis=-1, keepdims=True)
    s_hi = jnp.sum(jnp.where(lo, 0.0, y2), axis=-1, keepdims=True)
    ms = jnp.where(lo, s_lo, s_hi) * (1.0 / HEAD_DIM)
    return y * lax.rsqrt(ms + RMS_EPS) * gain


def _pair_rope(y, cos, sin):
    lane = _iota(y.shape, 1) & (HEAD_DIM - 1)
    partner = jnp.where(lane < ROT_DIM // 2,
                        pltpu.roll(y, LANES - ROT_DIM // 2, 1),
                        pltpu.roll(y, ROT_DIM // 2, 1))
    return y * cos + partner * sin


def _dot(a, b):
    return jnp.dot(a, b, preferred_element_type=F32)


def _dot_nt(a, b):
    return lax.dot_general(a, b, (((1,), (1,)), ((), ())), preferred_element_type=F32)


def _softmax_update_t(s, vt, m_ref, l_ref, acc_ref):
    tk, n = s.shape
    m_old = m_ref[...]
    m_tile = jnp.max(jnp.max(s.reshape(tk // SUBLANES, SUBLANES, n), axis=0), axis=0, keepdims=True)
    m_new = jnp.maximum(m_old, m_tile)
    alpha = jnp.exp2(m_old - m_new)
    p = jnp.exp2(s - m_new)
    l_ref[...] = alpha * l_ref[...] + jnp.sum(p.reshape(tk // SUBLANES, SUBLANES, n), axis=0)
    acc_ref[...] = alpha * acc_ref[...] + _dot(vt, p.astype(BF16))
    m_ref[...] = m_new


def _softmax_init(m_ref, l_ref, acc_ref):
    m_ref[...] = jnp.full_like(m_ref, NEG)
    l_ref[...] = jnp.zeros_like(l_ref)
    acc_ref[...] = jnp.zeros_like(acc_ref)


def _softmax_result(l_ref, acc_ref):
    return acc_ref[...] * (1.0 / jnp.sum(l_ref[...], axis=0, keepdims=True))


def _rope_table_kernel(pos_ref, inv_ref, sign_ref, c_ref, s_ref):
    ang = pos_ref[0].astype(F32) * inv_ref[...]
    c_ref[0] = jnp.cos(ang)
    s_ref[0] = jnp.sin(ang) * sign_ref[...]


def _rope_tables(pos, inv_lane, sign_lane):
    B, T = pos.shape
    tm = min(T, ROW_TILE)
    spec = pl.BlockSpec((1, tm, LANES), lambda b, t: (b, t, 0))
    vec = pl.BlockSpec((1, LANES), lambda b, t: (0, 0))
    return pl.pallas_call(
        _rope_table_kernel,
        grid=(B, T // tm),
        in_specs=[pl.BlockSpec((1, tm, 1), lambda b, t: (b, t, 0)), vec, vec],
        out_specs=[spec, spec],
        out_shape=[jax.ShapeDtypeStruct((B, T, LANES), F32)] * 2,
        compiler_params=_cparams(("parallel", "parallel")),
        name="rope_tables",
    )(pos.reshape(B, T, 1), inv_lane, sign_lane)


def _fox_inproj_kernel(x_ref, g_ref, w_ref, wvt_ref, bf_ref, qg_ref, kg_ref, sel_ref,
                       q_ref, k_ref, vt_ref, cf_ref, carry_sc, *, tm, hd):
    ti = pl.program_id(1)
    xn = _row_rms(x_ref[0], g_ref[...]).astype(BF16)
    for j in range(hd // 256):
        for part, (ref, gain, mul) in enumerate(((q_ref, qg_ref, Q_SCALE), (k_ref, kg_ref, 1.0))):
            c0 = part * hd + 256 * j
            y = _dot(xn, w_ref[:, c0:c0 + 256])
            for hh in range(2):
                blk = _pair_rms(y[:, LANES * hh:LANES * (hh + 1)], gain[...]) * mul
                ref[0, :, 256 * j + LANES * hh:256 * j + LANES * (hh + 1)] = blk.astype(BF16)
        yt = _dot_nt(wvt_ref[256 * j:256 * (j + 1), :], xn)
        for hh in range(2):
            vt_ref[0, 2 * j + hh, 0] = yt[LANES * hh:LANES * (hh + 1), :].astype(BF16)
    z = _dot(xn, w_ref[:, 2 * hd:2 * hd + LANES]) + bf_ref[...]
    lf = jnp.minimum(z, 0.0) - jnp.log1p(jnp.exp(-jnp.abs(z)))
    row = _iota(lf.shape, 0)
    sh = 1
    while sh < tm:
        lf = lf + jnp.where(row >= sh, pltpu.roll(lf, sh, 0), 0.0)
        sh *= 2

    @pl.when(ti == 0)
    def _():
        carry_sc[...] = jnp.zeros_like(carry_sc)

    c = lf + carry_sc[0:1, :]
    carry_sc[...] = jnp.broadcast_to(c[tm - 1:tm, :], carry_sc.shape)
    rest = c * (-LOG2E)
    pieces = []
    for _ in range(N_BIAS_PARTS):
        piece = rest.astype(BF16)
        pieces.append(piece)
        rest = rest - piece.astype(F32)
    cf_ref[0] = _dot(jnp.concatenate(pieces, axis=1), sel_ref[...]).astype(BF16)


def _fox_inproj(x, g, w, wvt, bf, qg, kg, sel, tm=ROW_TILE):
    B, T, D = x.shape
    hd = N_HEADS * HEAD_DIM
    act = pl.BlockSpec((1, tm, hd), lambda b, t: (b, t, 0))
    vec = lambda n: pl.BlockSpec((1, n), lambda b, t: (0, 0))
    full = lambda a: pl.BlockSpec(a.shape, lambda b, t: (0,) * a.ndim)
    return pl.pallas_call(
        functools.partial(_fox_inproj_kernel, tm=tm, hd=hd),
        grid=(B, T // tm),
        in_specs=[pl.BlockSpec((1, tm, D), lambda b, t: (b, t, 0)), vec(D), full(w), full(wvt),
                  vec(LANES), vec(LANES), vec(LANES), full(sel)],
        out_specs=[act, act, pl.BlockSpec((1, N_PAIRS, 1, LANES, tm), lambda b, t: (b, 0, t, 0, 0)), act],
        out_shape=[jax.ShapeDtypeStruct((B, T, hd), BF16), jax.ShapeDtypeStruct((B, T, hd), BF16),
                   jax.ShapeDtypeStruct((B, N_PAIRS, T // tm, LANES, tm), BF16),
                   jax.ShapeDtypeStruct((B, T, hd), BF16)],
        scratch_shapes=[pltpu.VMEM((SUBLANES, LANES), F32)],
        compiler_params=_cparams(("arbitrary", "arbitrary")),
        name="fox_inproj",
    )(x, g, w, wvt, bf, qg, kg, sel)


def _fox_attn_kernel(q_ref, k_ref, cf_ref, vt_ref, o_ref, qa_sc, m_sc, l_sc, acc_sc, *, tq, tk):
    qi = pl.program_id(2)
    q = q_ref[0]
    lane = _iota(q.shape, 1)
    lo = lane < HEAD_DIM
    zero = jnp.zeros_like(q)
    one = jnp.ones_like(q)
    for hh in range(2):
        qa_sc[hh, :, 0:LANES] = jnp.where(lo, q, zero) if hh == 0 else jnp.where(lo, zero, q)
        feat = jnp.where(lane < N_BIAS_PARTS * (hh + 1), 1.0, 0.0)
        qa_sc[hh, :, LANES:2 * LANES] = jnp.where(lane >= N_BIAS_PARTS * hh, feat, 0.0).astype(BF16)
        _softmax_init(m_sc.at[hh], l_sc.at[hh], acc_sc.at[hh])

    def step(ki, diag):
        start = pl.multiple_of(ki * tk, tk)
        ka = jnp.concatenate([k_ref[0, pl.ds(start, tk), :], cf_ref[0, pl.ds(start, tk), :]], axis=1)
        vt = vt_ref[0, 0, ki]
        for hh in range(2):
            s = _dot_nt(ka, qa_sc[hh])
            if diag:
                s = jnp.where(_iota(s.shape, 0) <= _iota(s.shape, 1), s, NEG)
            _softmax_update_t(s, vt[HEAD_DIM * hh:HEAD_DIM * (hh + 1), :], m_sc.at[hh], l_sc.at[hh], acc_sc.at[hh])

    def body(ki, carry):
        step(ki, False)
        return carry

    lax.fori_loop(0, qi, body, 0)
    step(qi, True)
    o_t = jnp.concatenate([_softmax_result(l_sc.at[hh], acc_sc.at[hh]) for hh in range(2)], axis=0)
    o_ref[0] = o_t.T.astype(BF16)


def _fox_attn(q, k, cf, vt, tq=FOX_TQ, tk=FOX_TK):
    B, T, hd = q.shape
    assert tq == tk
    nk = T // tk
    seq = pl.BlockSpec((1, T, LANES), lambda b, p, qi: (b, 0, p))
    return pl.pallas_call(
        functools.partial(_fox_attn_kernel, tq=tq, tk=tk),
        grid=(B, N_PAIRS, T // tq),
        in_specs=[pl.BlockSpec((1, tq, LANES), lambda b, p, qi: (b, qi, p)), seq, seq,
                  pl.BlockSpec((1, 1, nk, LANES, tk), lambda b, p, qi: (b, p, 0, 0, 0))],
        out_specs=pl.BlockSpec((1, tq, LANES), lambda b, p, qi: (b, qi, p)),
        out_shape=jax.ShapeDtypeStruct((B, T, hd), BF16),
        scratch_shapes=[pltpu.VMEM((2, tq, 2 * LANES), BF16), pltpu.VMEM((2, 1, tq), F32),
                        pltpu.VMEM((2, SUBLANES, tq), F32), pltpu.VMEM((2, HEAD_DIM, tq), F32)],
        compiler_params=_cparams(("parallel", "parallel", "arbitrary")),
        name="fox_attn",
    )(q, k, cf, vt)


def _outproj_kernel(h_ref, o_ref, w_ref, out_ref):
    out_ref[...] = h_ref[...] + _dot(o_ref[...], w_ref[...])


def _outproj(h, o, w, tm=ROW_TILE):
    B, T, D = h.shape
    n = B * T
    row = lambda c: pl.BlockSpec((tm, c), lambda i: (i, 0))
    out = pl.pallas_call(
        _outproj_kernel,
        grid=(n // tm,),
        in_specs=[row(D), row(o.shape[-1]), pl.BlockSpec(w.shape, lambda i: (0, 0))],
        out_specs=row(D),
        out_shape=jax.ShapeDtypeStruct((n, D), F32),
        compiler_params=_cparams(("parallel",)),
        name="outproj",
    )(h.reshape(n, D), o.reshape(n, -1), w)
    return out.reshape(B, T, D)


def _nsa_outproj_kernel(h_ref, oc_ref, os_ref, ow_ref, g_ref, w_ref, out_ref, o_sc):
    g = g_ref[...]
    lo = _iota((g.shape[0], LANES), 1) < HEAD_DIM
    for j in range(N_PAIRS):
        acc = None
        for br, ref in enumerate((oc_ref, os_ref, ow_ref)):
            c0 = br * N_HEADS + 2 * j
            gg = jnp.where(lo, g[:, c0:c0 + 1], g[:, c0 + 1:c0 + 2])
            term = gg * ref[:, LANES * j:LANES * (j + 1)].astype(F32)
            acc = term if acc is None else acc + term
        o_sc[:, LANES * j:LANES * (j + 1)] = acc.astype(BF16)
    out_ref[...] = h_ref[...] + _dot(o_sc[...], w_ref[...])


def _nsa_outproj(h, oc, osel, ow, gates, w, tm=ROW_TILE):
    B, T, D = h.shape
    n = B * T
    hd = oc.shape[-1]
    row = lambda c: pl.BlockSpec((tm, c), lambda i: (i, 0))
    out = pl.pallas_call(
        _nsa_outproj_kernel,
        grid=(n // tm,),
        in_specs=[row(D), row(hd), row(hd), row(hd), row(LANES), pl.BlockSpec(w.shape, lambda i: (0, 0))],
        out_specs=row(D),
        out_shape=jax.ShapeDtypeStruct((n, D), F32),
        scratch_shapes=[pltpu.VMEM((tm, hd), BF16)],
        compiler_params=_cparams(("parallel",)),
        name="nsa_outproj",
    )(h.reshape(n, D), oc.reshape(n, hd), osel.reshape(n, hd), ow.reshape(n, hd), gates.reshape(n, LANES), w)
    return out.reshape(B, T, D)


def _ffn_kernel(x_ref, g_ref, wg_ref, wv_ref, cwg_ref, cwv_ref, cbg_ref, cbv_ref, wd_ref,
                out_ref, xn_sc, acc_sc, carry_sc, *, tt, tf, nf):
    ti = pl.program_id(1)
    f = pl.program_id(2)

    @pl.when(f == 0)
    def _():
        xn_sc[...] = _row_rms(x_ref[0], g_ref[...]).astype(BF16)
        acc_sc[...] = jnp.zeros_like(acc_sc)

    @pl.when(ti == 0)
    def _():
        carry_sc[f] = jnp.zeros(carry_sc.shape[1:], F32)

    xn = xn_sc[...]
    prev = carry_sc[f]
    row = _iota((tt, tf), 0)

    def conv(u, cw_ref, cb_ref, prev8):
        um1 = jnp.where(row == 0, prev8[7:8, :], pltpu.roll(u, 1, 0))
        um2 = jnp.where(row == 0, prev8[6:7, :], jnp.where(row == 1, prev8[7:8, :], pltpu.roll(u, 2, 0)))
        cw = cw_ref[...]
        return cb_ref[...] + cw[0:1, :] * um2 + cw[1:2, :] * um1 + cw[2:3, :] * u

    ug = _dot(xn, wg_ref[...])
    uv = _dot(xn, wv_ref[...])
    cg = conv(ug, cwg_ref, cbg_ref, prev[:, :tf])
    cv = conv(uv, cwv_ref, cbv_ref, prev[:, tf:])
    carry_sc[f] = jnp.concatenate([ug[tt - 8:, :], uv[tt - 8:, :]], axis=1)
    a = cg * jax.nn.sigmoid(cg) * cv
    acc_sc[...] += _dot(a.astype(BF16), wd_ref[...])

    @pl.when(f == nf - 1)
    def _():
        out_ref[0] = x_ref[0] + acc_sc[...]


def _conv_ffn(h, g, w_up, conv_w, conv_b, w_down, tt=ROW_TILE, tf=256):
    B, T, D = h.shape
    d_ff = w_down.shape[0]
    nf = d_ff // tf
    act = pl.BlockSpec((1, tt, D), lambda b, t, f: (b, t, 0))
    return pl.pallas_call(
        functools.partial(_ffn_kernel, tt=tt, tf=tf, nf=nf),
        grid=(B, T // tt, nf),
        in_specs=[act, pl.BlockSpec((1, D), lambda b, t, f: (0, 0)),
                  pl.BlockSpec((D, tf), lambda b, t, f: (0, f)),
                  pl.BlockSpec((D, tf), lambda b, t, f: (0, f + nf)),
                  pl.BlockSpec((CONV_W, tf), lambda b, t, f: (0, f)),
                  pl.BlockSpec((CONV_W, tf), lambda b, t, f: (0, f + nf)),
                  pl.BlockSpec((1, tf), lambda b, t, f: (0, f)),
                  pl.BlockSpec((1, tf), lambda b, t, f: (0, f + nf)),
                  pl.BlockSpec((tf, D), lambda b, t, f: (f, 0))],
        out_specs=act,
        out_shape=jax.ShapeDtypeStruct((B, T, D), F32),
        scratch_shapes=[pltpu.VMEM((tt, D), BF16), pltpu.VMEM((tt, D), F32), pltpu.VMEM((nf, 8, 2 * tf), F32)],
        compiler_params=_cparams(("arbitrary", "arbitrary", "arbitrary")),
        name="conv_ffn",
    )(h, g, w_up, w_up, conv_w, conv_w, conv_b, conv_b, w_down)


def _kvproj_kernel(h_ref, g_ref, w_ref, wvt_ref, ksg_ref, kwg_ref, c_ref, s_ref,
                   craw_ref, ks_ref, kw_ref, vst_ref, vwt_ref, *, tm):
    xn = _row_rms(h_ref[0], g_ref[...]).astype(BF16)
    cos = c_ref[0]
    sin = s_ref[0]
    craw_ref[0] = _dot(xn, w_ref[:, 0:256])
    for idx, (ref, gain) in enumerate(((ks_ref, ksg_ref), (kw_ref, kwg_ref))):
        y = _dot(xn, w_ref[:, 256 * (idx + 1):256 * (idx + 2)])
        for grp in range(N_KV_GROUPS):
            blk = _pair_rope(_pair_rms(y[:, LANES * grp:LANES * (grp + 1)], gain[...]), cos, sin)
            ref[0, grp] = blk.astype(BF16)
    yt = _dot_nt(wvt_ref[...], xn).astype(BF16)
    for grp in range(N_KV_GROUPS):
        vst_ref[0, grp, 0] = yt[HEAD_DIM * grp:HEAD_DIM * (grp + 1), :]
        r0 = HEAD_DIM * (N_KV_GROUPS + grp)
        for c in range(tm // WIN_TQ):
            vwt_ref[0, grp, c] = yt[r0:r0 + HEAD_DIM, WIN_TQ * c:WIN_TQ * (c + 1)]


def _kvproj(h, g, w, wvt, ksg, kwg, cos, sin, tm=ROW_TILE):
    B, T, D = h.shape
    G = N_KV_GROUPS
    vec = lambda n: pl.BlockSpec((1, n), lambda b, t: (0, 0))
    full = lambda a: pl.BlockSpec(a.shape, lambda b, t: (0,) * a.ndim)
    tab = pl.BlockSpec((1, tm, LANES), lambda b, t: (b, t, 0))
    dup = pl.BlockSpec((1, G, tm, LANES), lambda b, t: (b, 0, t, 0))
    dup_shape = jax.ShapeDtypeStruct((B, G, T, LANES), BF16)
    nw = tm // WIN_TQ
    return pl.pallas_call(
        functools.partial(_kvproj_kernel, tm=tm),
        grid=(B, T // tm),
        in_specs=[pl.BlockSpec((1, tm, D), lambda b, t: (b, t, 0)), vec(D), full(w), full(wvt),
                  vec(LANES), vec(LANES), tab, tab],
        out_specs=[pl.BlockSpec((1, tm, 256), lambda b, t: (b, t, 0)), dup, dup,
                   pl.BlockSpec((1, G, 1, HEAD_DIM, tm), lambda b, t: (b, 0, t, 0, 0)),
                   pl.BlockSpec((1, G, nw, HEAD_DIM, WIN_TQ), lambda b, t: (b, 0, t, 0, 0))],
        out_shape=[jax.ShapeDtypeStruct((B, T, 256), F32), dup_shape, dup_shape,
                   jax.ShapeDtypeStruct((B, G, T // tm, HEAD_DIM, tm), BF16),
                   jax.ShapeDtypeStruct((B, G, T // WIN_TQ, HEAD_DIM, WIN_TQ), BF16)],
        compiler_params=_cparams(("parallel", "parallel")),
        name="kvproj",
    )(h, g, w, wvt, ksg, kwg, cos, sin)


def _compress_kernel(r_ref, w1_ref, pe_ref, w2_ref, w2t_ref, gain_ref, c_ref, s_ref, kc_ref, vct_ref, *, n_cmp):
    r = r_ref[0, 0, 0].astype(BF16)
    half = r.shape[1]
    a = _dot(r, w1_ref[0, :half, :])
    b = _dot(r, w1_ref[0, half:, :])
    peb = _dot(pe_ref[0], w1_ref[0])[0:1, :]
    rows = r.shape[0]
    hid = a + pltpu.roll(b, rows - 1, 0) + peb
    act = jax.nn.gelu(hid).astype(BF16)
    y = _dot(act, w2_ref[0])
    yk = _pair_rope(_pair_rms(y, gain_ref[...]), c_ref[0], s_ref[0])
    kc_ref[0, 0, 0] = jnp.where(_iota(y.shape, 0) < n_cmp, yk, 0.0).astype(BF16)
    yt = _dot_nt(w2t_ref[0], act)
    vct_ref[0, 0, 0] = jnp.where(_iota(yt.shape, 1) < n_cmp, yt, 0.0).astype(BF16)


def _compress(r, w1, pe, w2, w2t, gain, cos, sin, n_cmp):
    B, _, G, rows, width = r.shape
    per_kv = lambda a: pl.BlockSpec((1,) + a.shape[1:], lambda b, kv, g: (kv,) + (0,) * (a.ndim - 1))
    return pl.pallas_call(
        functools.partial(_compress_kernel, n_cmp=n_cmp),
        grid=(B, 2, G),
        in_specs=[pl.BlockSpec((1, 1, 1, rows, width), lambda b, kv, g: (b, kv, g, 0, 0)),
                  per_kv(w1), per_kv(pe), per_kv(w2), per_kv(w2t),
                  pl.BlockSpec((1, LANES), lambda b, kv, g: (0, 0)),
                  pl.BlockSpec((1, rows, LANES), lambda b, kv, g: (b, 0, 0)),
                  pl.BlockSpec((1, rows, LANES), lambda b, kv, g: (b, 0, 0))],
        out_specs=[pl.BlockSpec((1, 1, 1, rows, LANES), lambda b, kv, g: (b, kv, g, 0, 0)),
                   pl.BlockSpec((1, 1, 1, HEAD_DIM, rows), lambda b, kv, g: (b, kv, g, 0, 0))],
        out_shape=[jax.ShapeDtypeStruct((B, 2, G, rows, LANES), BF16),
                   jax.ShapeDtypeStruct((B, 2, G, HEAD_DIM, rows), BF16)],
        compiler_params=_cparams(("parallel", "parallel", "parallel")),
        name="compress",
    )(r, w1, pe, w2, w2t, gain, cos, sin)


def _nsa_inproj_kernel(h_ref, g_ref, w_ref, bg_ref, qg_ref, c_ref, s_ref, q_ref, gate_ref, *, hd):
    xn = _row_rms(h_ref[0], g_ref[...]).astype(BF16)
    cos = c_ref[0]
    sin = s_ref[0]
    for j in range(hd // 256):
        y = _dot(xn, w_ref[:, 256 * j:256 * (j + 1)])
        for hh in range(2):
            blk = _pair_rope(_pair_rms(y[:, LANES * hh:LANES * (hh + 1)], qg_ref[...]), cos, sin) * Q_SCALE
            q_ref[0, :, 256 * j + LANES * hh:256 * j + LANES * (hh + 1)] = blk.astype(BF16)
    z = _dot(xn, w_ref[:, hd:hd + LANES]) + bg_ref[...]
    gate_ref[0] = jax.nn.sigmoid(z)


def _nsa_inproj(h, g, w, bg, qg, cos, sin, tm=ROW_TILE):
    B, T, D = h.shape
    hd = N_HEADS * HEAD_DIM
    vec = lambda n: pl.BlockSpec((1, n), lambda b, t: (0, 0))
    tab = pl.BlockSpec((1, tm, LANES), lambda b, t: (b, t, 0))
    return pl.pallas_call(
        functools.partial(_nsa_inproj_kernel, hd=hd),
        grid=(B, T // tm),
        in_specs=[pl.BlockSpec((1, tm, D), lambda b, t: (b, t, 0)), vec(D),
                  pl.BlockSpec(w.shape, lambda b, t: (0, 0)), vec(LANES), vec(LANES), tab, tab],
        out_specs=[pl.BlockSpec((1, tm, hd), lambda b, t: (b, t, 0)), tab],
        out_shape=[jax.ShapeDtypeStruct((B, T, hd), BF16), jax.ShapeDtypeStruct((B, T, LANES), F32)],
        compiler_params=_cparams(("parallel", "parallel")),
        name="nsa_inproj",
    )(h, g, w, bg, qg, cos, sin)


def _stack_heads(q, dst_ref, tq):
    lo = _iota((tq, LANES), 1) < HEAD_DIM
    zero = jnp.zeros((tq, LANES), q.dtype)
    for j in range(HEADS_PER_GROUP // 2):
        x = q[:, LANES * j:LANES * (j + 1)]
        dst_ref[(2 * j) * tq:(2 * j + 1) * tq, 0:LANES] = jnp.where(lo, x, zero)
        dst_ref[(2 * j + 1) * tq:(2 * j + 2) * tq, 0:LANES] = jnp.where(lo, zero, x)


def _unstack_heads_t(o_t, o_ref, tq):
    for j in range(HEADS_PER_GROUP // 2):
        pair = jnp.concatenate([o_t[:, (2 * j) * tq:(2 * j + 1) * tq], o_t[:, (2 * j + 1) * tq:(2 * j + 2) * tq]], axis=0)
        o_ref[0, :, LANES * j:LANES * (j + 1)] = pair.T.astype(o_ref.dtype)


def _nsa_cmp_kernel(q_ref, kc_ref, vct_ref, ovt_ref, oc_ref, sel_ref, qs_sc, *, tq, n_slc, top_k):
    qi = pl.program_id(2)
    _stack_heads(q_ref[0], qs_sc, tq)
    kc = kc_ref[0, 0, 0]
    n_rows = kc.shape[0]
    s = _dot_nt(kc, qs_sc[...])
    qpos = qi * tq + (_iota(s.shape, 1) & (tq - 1))
    vis = _iota(s.shape, 0) * CMP_STRIDE + (CMP_LEN - 1) <= qpos
    s = jnp.where(vis, s, NEG)
    e = jnp.where(vis, jnp.exp2(s - jnp.max(s, axis=0, keepdims=True)), 0.0)
    p = e * (1.0 / jnp.maximum(jnp.sum(e, axis=0, keepdims=True), 1.0))
    _unstack_heads_t(_dot(vct_ref[0, 0, 0], p.astype(BF16)), oc_ref, tq)
    p_sum = p[:, 0:tq]
    for h in range(1, HEADS_PER_GROUP):
        p_sum = p_sum + p[:, h * tq:(h + 1) * tq]
    p_hi = p_sum.astype(BF16)
    p_lo = (p_sum - p_hi.astype(F32)).astype(BF16)
    imp = _dot(ovt_ref[...], p_hi) + _dot(ovt_ref[...], p_lo)
    jb = _iota(imp.shape, 0)
    qp = qi * tq + _iota(imp.shape, 1)
    cur = qp // SLC_LEN
    forced = (jb == 0) | (jb == cur) | (jb == cur - 1)
    score = jnp.where(jb * SLC_LEN <= qp, jnp.where(forced, FORCED_SCORE, imp), NEG)
    score = jnp.where(jb < n_slc, score, -jnp.inf)
    bias = jnp.full(imp.shape, NEG, F32)
    for _ in range(top_k):
        mx = jnp.max(score, axis=0, keepdims=True)
        idx = jnp.min(jnp.where(score == mx, jb, LANES), axis=0, keepdims=True)
        pick = jb == idx
        bias = jnp.where(pick, 0.0, bias)
        score = jnp.where(pick, -jnp.inf, score)
    sel_ref[0, 0] = bias.T.astype(BF16)


def _nsa_cmp(q, kc_all, vct_all, overlap_t, n_slc, tq=CMP_TQ):
    B, T, hd = q.shape
    G = N_KV_GROUPS
    rows = kc_all.shape[3]
    gw = hd // G
    top_k = min(SLC_TOPK, n_slc)
    return pl.pallas_call(
        functools.partial(_nsa_cmp_kernel, tq=tq, n_slc=n_slc, top_k=top_k),
        grid=(B, G, T // tq),
        in_specs=[pl.BlockSpec((1, tq, gw), lambda b, g, qi: (b, qi, g)),
                  pl.BlockSpec((1, 1, 1, rows, LANES), lambda b, g, qi: (b, 0, g, 0, 0)),
                  pl.BlockSpec((1, 1, 1, HEAD_DIM, rows), lambda b, g, qi: (b, 1, g, 0, 0)),
                  pl.BlockSpec(overlap_t.shape, lambda b, g, qi: (0, 0))],
        out_specs=[pl.BlockSpec((1, tq, gw), lambda b, g, qi: (b, qi, g)),
                   pl.BlockSpec((1, 1, tq, LANES), lambda b, g, qi: (b, g, qi, 0))],
        out_shape=[jax.ShapeDtypeStruct((B, T, hd), BF16), jax.ShapeDtypeStruct((B, G, T, LANES), BF16)],
        scratch_shapes=[pltpu.VMEM((HEADS_PER_GROUP * tq, LANES), BF16)],
        compiler_params=_cparams(("parallel", "parallel", "parallel")),
        name="nsa_cmp",
    )(q, kc_all, vct_all, overlap_t)


def _nsa_slc_kernel(q_ref, sel_ref, k_ref, vt_ref, o_ref, qa_sc, m_sc, l_sc, acc_sc, *, tq, tk):
    qi = pl.program_id(2)
    last = (qi * tq) // tk
    _stack_heads(q_ref[0], qa_sc, tq)
    sel = sel_ref[0, 0]
    for h in range(HEADS_PER_GROUP):
        qa_sc[h * tq:(h + 1) * tq, LANES:2 * LANES] = sel
    _softmax_init(m_sc, l_sc, acc_sc)

    def step(ki, diag):
        start = pl.multiple_of(ki * tk, tk)
        blk = (start + _iota((tk, LANES), 0)) // SLC_LEN
        onehot = jnp.where(blk == _iota((tk, LANES), 1), 1.0, 0.0).astype(BF16)
        ka = jnp.concatenate([k_ref[0, 0, pl.ds(start, tk), :], onehot], axis=1)
        s = _dot_nt(ka, qa_sc[...])
        if diag:
            kpos = start + _iota(s.shape, 0)
            qpos = qi * tq + (_iota(s.shape, 1) & (tq - 1))
            s = jnp.where(kpos <= qpos, s, NEG)
        _softmax_update_t(s, vt_ref[0, 0, ki], m_sc, l_sc, acc_sc)

    def body(ki, carry):
        step(ki, False)
        return carry

    lax.fori_loop(0, last, body, 0)
    step(last, True)
    _unstack_heads_t(_softmax_result(l_sc, acc_sc), o_ref, tq)


def _nsa_slc(q, sel, ks, vst, tq=SLC_TQ, tk=SLC_TK):
    B, T, hd = q.shape
    G = N_KV_GROUPS
    gw = hd // G
    cols = HEADS_PER_GROUP * tq
    assert tk % tq == 0
    return pl.pallas_call(
        functools.partial(_nsa_slc_kernel, tq=tq, tk=tk),
        grid=(B, G, T // tq),
        in_specs=[pl.BlockSpec((1, tq, gw), lambda b, g, qi: (b, qi, g)),
                  pl.BlockSpec((1, 1, tq, LANES), lambda b, g, qi: (b, g, qi, 0)),
                  pl.BlockSpec((1, 1, T, LANES), lambda b, g, qi: (b, g, 0, 0)),
                  pl.BlockSpec((1, 1, T // tk, HEAD_DIM, tk), lambda b, g, qi: (b, g, 0, 0, 0))],
        out_specs=pl.BlockSpec((1, tq, gw), lambda b, g, qi: (b, qi, g)),
        out_shape=jax.ShapeDtypeStruct((B, T, hd), BF16),
        scratch_shapes=[pltpu.VMEM((cols, 2 * LANES), BF16), pltpu.VMEM((1, cols), F32),
                        pltpu.VMEM((SUBLANES, cols), F32), pltpu.VMEM((HEAD_DIM, cols), F32)],
        compiler_params=_cparams(("parallel", "parallel", "arbitrary")),
        name="nsa_slc",
    )(q, sel, ks, vst)


def _nsa_win_kernel(q_ref, k_ref, vt_ref, o_ref, qs_sc, m_sc, l_sc, acc_sc, *, tq, n_back):
    qi = pl.program_id(2)
    _stack_heads(q_ref[0], qs_sc, tq)
    _softmax_init(m_sc, l_sc, acc_sc)

    def step(ti, kind):
        start = pl.multiple_of(ti * tq, tq)
        s = _dot_nt(k_ref[0, 0, pl.ds(start, tq), :], qs_sc[...])
        r = _iota(s.shape, 0)
        c = _iota(s.shape, 1) & (tq - 1)
        if kind == "oldest":
            s = jnp.where(r > c, s, NEG)
        elif kind == "diag":
            s = jnp.where(r <= c, s, NEG)
        _softmax_update_t(s, vt_ref[0, 0, ti], m_sc, l_sc, acc_sc)

    for back in range(n_back, 0, -1):
        @pl.when(qi >= back)
        def _(back=back):
            step(qi - back, "oldest" if back == n_back else "full")

    step(qi, "diag")
    _unstack_heads_t(_softmax_result(l_sc, acc_sc), o_ref, tq)


def _nsa_win(q, kw, vwt, tq=WIN_TQ):
    B, T, hd = q.shape
    G = N_KV_GROUPS
    gw = hd // G
    cols = HEADS_PER_GROUP * tq
    return pl.pallas_call(
        functools.partial(_nsa_win_kernel, tq=tq, n_back=WIN // tq),
        grid=(B, G, T // tq),
        in_specs=[pl.BlockSpec((1, tq, gw), lambda b, g, qi: (b, qi, g)),
                  pl.BlockSpec((1, 1, T, LANES), lambda b, g, qi: (b, g, 0, 0)),
                  pl.BlockSpec((1, 1, T // tq, HEAD_DIM, tq), lambda b, g, qi: (b, g, 0, 0, 0))],
        out_specs=pl.BlockSpec((1, tq, gw), lambda b, g, qi: (b, qi, g)),
        out_shape=jax.ShapeDtypeStruct((B, T, hd), BF16),
        scratch_shapes=[pltpu.VMEM((cols, LANES), BF16), pltpu.VMEM((1, cols), F32),
                        pltpu.VMEM((SUBLANES, cols), F32), pltpu.VMEM((HEAD_DIM, cols), F32)],
        compiler_params=_cparams(("parallel", "parallel", "arbitrary")),
        name="nsa_win",
    )(q, kw, vwt)


def _pad_cols(a, n):
    return jnp.pad(a, ((0, 0), (0, n - a.shape[1])))


def _lane_vec(v):
    return jnp.tile(v.astype(F32), 2).reshape(1, LANES)


def _bias_feature_selector():
    sel = np.zeros((N_BIAS_PARTS * LANES, N_HEADS * HEAD_DIM), np.float32)
    for part in range(N_BIAS_PARTS):
        for h in range(N_HEADS):
            sel[part * LANES + h, (h // 2) * LANES + (h % 2) * N_BIAS_PARTS + part] = 1.0
    return jnp.asarray(sel, BF16)


def kernel(x, positions, a_norm, a_w_in, a_b_f, a_q_gain, a_k_gain, a_w_out, kv_norm, kv_w, kc_pe, vc_pe, kc_w1, kc_w2, vc_w1, vc_w2, kc_gain, ks_gain, kw_gain, b_norm, b_w_in, b_b_gate, b_q_gain, b_w_out, f_norm, f_w_up, f_conv_w, f_conv_b, f_w_down):
    B, T, D = x.shape
    hd = N_HEADS * HEAD_DIM
    G = N_KV_GROUPS
    n_a = a_norm.shape[0]
    n_b = b_norm.shape[0]
    depth = n_a + n_b
    n_slc = T // SLC_LEN
    n_cmp = (T - CMP_LEN) // CMP_STRIDE + 1
    assert T % 1024 == 0 and n_slc <= LANES and hd == 1024 and D == 1024

    half = ROT_DIM // 2
    inv = ROPE_THETA ** (-jnp.arange(half, dtype=F32) * (2.0 / ROT_DIM))
    head_inv = jnp.concatenate([inv, inv, jnp.zeros((HEAD_DIM - ROT_DIM,), F32)])
    head_sign = jnp.concatenate([-jnp.ones((half,), F32), jnp.ones((half,), F32), jnp.zeros((HEAD_DIM - ROT_DIM,), F32)])
    inv_lane = jnp.tile(head_inv, 2).reshape(1, LANES)
    sign_lane = jnp.tile(head_sign, 2).reshape(1, LANES)
    cos_t, sin_t = _rope_tables(positions, inv_lane, sign_lane)
    end_pos = positions[:, CMP_LEN - 1::CMP_STRIDE]
    end_pos = jnp.pad(end_pos, ((0, 0), (0, T // CMP_STRIDE - n_cmp)))
    cos_c, sin_c = _rope_tables(end_pos, inv_lane, sign_lane)

    cs = jnp.arange(T // CMP_STRIDE) * CMP_STRIDE
    ss = jnp.arange(LANES) * SLC_LEN
    overlap_t = (jnp.maximum(jnp.minimum(cs[None, :] + CMP_LEN, ss[:, None] + SLC_LEN)
                             - jnp.maximum(cs[None, :], ss[:, None]), 0).astype(F32) / CMP_LEN).astype(BF16)

    h = x
    kv = None
    for layer in range(depth):
        if layer < n_a:
            i = layer
            w = jnp.concatenate([a_w_in[i][:, :2 * hd], _pad_cols(a_w_in[i][:, 3 * hd:], LANES)], axis=1).astype(BF16)
            wvt = a_w_in[i][:, 2 * hd:3 * hd].T.astype(BF16)
            bf = _pad_cols(a_b_f[i].reshape(1, -1), LANES)
            q, k, vt, cf = _fox_inproj(h, a_norm[i].reshape(1, D), w, wvt, bf, _lane_vec(a_q_gain[i]),
                                       _lane_vec(a_k_gain[i]), _bias_feature_selector())
            o = _fox_attn(q, k, cf, vt)
            h = _outproj(h, o, a_w_out[i].astype(BF16))
        else:
            i = layer - n_a
            kc_all, vct_all, ks, kw, vst, vwt = kv
            w = jnp.concatenate([b_w_in[i][:, :hd], _pad_cols(b_w_in[i][:, hd:], LANES)], axis=1).astype(BF16)
            bg = _pad_cols(b_b_gate[i].reshape(1, -1), LANES)
            q, gates = _nsa_inproj(h, b_norm[i].reshape(1, D), w, bg, _lane_vec(b_q_gain[i]), cos_t, sin_t)
            o_c, sel = _nsa_cmp(q, kc_all, vct_all, overlap_t, n_slc)
            o_s = _nsa_slc(q, sel, ks, vst)
            o_w = _nsa_win(q, kw, vwt)
            h = _nsa_outproj(h, o_c, o_s, o_w, gates, b_w_out[i].astype(BF16))
        h = _conv_ffn(h, f_norm[layer].reshape(1, D), f_w_up[layer].astype(BF16), f_conv_w[layer],
                      f_conv_b[layer].reshape(1, -1), f_w_down[layer].astype(BF16))
        if layer == n_a - 1:
            w6 = kv_w.reshape(D, 6, G, HEAD_DIM)
            raw_cols = w6[:, 0:2].reshape(D, 2 * G * HEAD_DIM)
            wk = w6[:, (2, 4)]
            dup_cols = jnp.concatenate([wk, wk], axis=-1).reshape(D, 2 * G * LANES)
            wkv = jnp.concatenate([raw_cols, dup_cols], axis=1).astype(BF16)
            wvt = w6[:, (3, 5)].reshape(D, 2 * G * HEAD_DIM).T.astype(BF16)
            craw, ks, kw, vst, vwt = _kvproj(h, kv_norm.reshape(1, D), wkv, wvt, _lane_vec(ks_gain),
                                             _lane_vec(kw_gain), cos_t, sin_t)
            r = craw.reshape(B, T, 2, G, HEAD_DIM).transpose(0, 2, 3, 1, 4).reshape(B, 2, G, T // CMP_STRIDE, CMP_STRIDE * HEAD_DIM)
            w1 = jnp.stack([kc_w1, vc_w1]).astype(BF16)
            pe = jnp.stack([kc_pe.reshape(1, -1), vc_pe.reshape(1, -1)])
            pe = jnp.pad(pe, ((0, 0), (0, 7), (0, 0))).astype(BF16)
            w2 = jnp.stack([jnp.concatenate([kc_w2, kc_w2], axis=1), jnp.concatenate([vc_w2, vc_w2], axis=1)]).astype(BF16)
            w2t = jnp.stack([kc_w2.T, vc_w2.T]).astype(BF16)
            kc_all, vct_all = _compress(r, w1, pe, w2, w2t, _lane_vec(kc_gain), cos_c, sin_c, n_cmp)
            kv = (kc_all, vct_all, ks, kw, vst, vwt)
    return h
```

```python
import functools

import numpy as np
import jax
import jax.numpy as jnp
from jax import lax
from jax.experimental import pallas as pl
from jax.experimental.pallas import tpu as pltpu

F32 = jnp.float32
BF16 = jnp.bfloat16

LANES = 128
SUBLANES = 8
HEAD_DIM = 64
N_HEADS = 16
N_PAIRS = N_HEADS // 2
N_KV_GROUPS = 2
HEADS_PER_GROUP = N_HEADS // N_KV_GROUPS
ROT_DIM = HEAD_DIM // 4
ROPE_THETA = 500000.0
CMP_LEN = 32
CMP_STRIDE = 16
SLC_LEN = 64
SLC_TOPK = 16
WIN = 512
CONV_W = 3
RMS_EPS = 1e-6
NEG = -1e30
FORCED_SCORE = 1e6
LOG2E = 1.4426950408889634
Q_SCALE = HEAD_DIM ** -0.5 * LOG2E
N_BIAS_PARTS = 3

ROW_TILE = 512
FOX_TQ = 512
FOX_TK = ROW_TILE
SLC_TQ = 128
SLC_TK = ROW_TILE
WIN_TQ = 256
CMP_TQ = 128

VMEM_LIMIT = 48 * 1024 * 1024


def _cparams(sem):
    return pltpu.CompilerParams(dimension_semantics=sem, vmem_limit_bytes=VMEM_LIMIT)


def _iota(shape, axis):
    return lax.broadcasted_iota(jnp.int32, shape, axis)


def _row_rms(x, g):
    ms = jnp.mean(x * x, axis=-1, keepdims=True)
    return x * lax.rsqrt(ms + RMS_EPS) * g


def _pair_rms(y, gain):
    lo = _iota(y.shape, 1) < HEAD_DIM
    y2 = y * y
    s_lo = jnp.sum(jnp.where(lo, y2, 0.0), axis=-1, keepdims=True)
    s_hi = jnp.sum(jnp.where(lo, 0.0, y2), axis=-1, keepdims=True)
    ms = jnp.where(lo, s_lo, s_hi) * (1.0 / HEAD_DIM)
    return y * lax.rsqrt(ms + RMS_EPS) * gain


def _pair_rope(y, cos, sin):
    lane = _iota(y.shape, 1) & (HEAD_DIM - 1)
    partner = jnp.where(lane < ROT_DIM // 2,
                        pltpu.roll(y, LANES - ROT_DIM // 2, 1),
                        pltpu.roll(y, ROT_DIM // 2, 1))
    return y * cos + partner * sin


def _dot(a, b):
    return jnp.dot(a, b, preferred_element_type=F32)


def _dot_nt(a, b):
    return lax.dot_general(a, b, (((1,), (1,)), ((), ())), preferred_element_type=F32)


def _softmax_update_t(s, vt, m_ref, l_ref, acc_ref):
    tk, n = s.shape
    m_old = m_ref[...]
    m_tile = jnp.max(jnp.max(s.reshape(tk // SUBLANES, SUBLANES, n), axis=0), axis=0, keepdims=True)
    m_new = jnp.maximum(m_old, m_tile)
    alpha = jnp.exp2(m_old - m_new)
    p = jnp.exp2(s - m_new)
    l_ref[...] = alpha * l_ref[...] + jnp.sum(p.reshape(tk // SUBLANES, SUBLANES, n), axis=0)
    acc_ref[...] = alpha * acc_ref[...] + _dot(vt, p.astype(BF16))
    m_ref[...] = m_new


def _softmax_init(m_ref, l_ref, acc_ref):
    m_ref[...] = jnp.full_like(m_ref, NEG)
    l_ref[...] = jnp.zeros_like(l_ref)
    acc_ref[...] = jnp.zeros_like(acc_ref)


def _softmax_result(l_ref, acc_ref):
    return acc_ref[...] * (1.0 / jnp.sum(l_ref[...], axis=0, keepdims=True))


def _rope_table_kernel(pos_ref, inv_ref, sign_ref, c_ref, s_ref):
    ang = pos_ref[0].astype(F32) * inv_ref[...]
    c_ref[0] = jnp.cos(ang)
    s_ref[0] = jnp.sin(ang) * sign_ref[...]


def _rope_tables(pos, inv_lane, sign_lane):
    B, T = pos.shape
    tm = min(T, ROW_TILE)
    spec = pl.BlockSpec((1, tm, LANES), lambda b, t: (b, t, 0))
    vec = pl.BlockSpec((1, LANES), lambda b, t: (0, 0))
    return pl.pallas_call(
        _rope_table_kernel,
        grid=(B, T // tm),
        in_specs=[pl.BlockSpec((1, tm, 1), lambda b, t: (b, t, 0)), vec, vec],
        out_specs=[spec, spec],
        out_shape=[jax.ShapeDtypeStruct((B, T, LANES), F32)] * 2,
        compiler_params=_cparams(("parallel", "parallel")),
        name="rope_tables",
    )(pos.reshape(B, T, 1), inv_lane, sign_lane)


def _fox_inproj_kernel(x_ref, g_ref, w_ref, wvt_ref, bf_ref, qg_ref, kg_ref, sel_ref,
                       q_ref, k_ref, vt_ref, cf_ref, carry_sc, *, tm, hd):
    ti = pl.program_id(1)
    xn = _row_rms(x_ref[0], g_ref[...]).astype(BF16)
    for j in range(hd // 256):
        for part, (ref, gain, mul) in enumerate(((q_ref, qg_ref, Q_SCALE), (k_ref, kg_ref, 1.0))):
            c0 = part * hd + 256 * j
            y = _dot(xn, w_ref[:, c0:c0 + 256])
            for hh in range(2):
                blk = _pair_rms(y[:, LANES * hh:LANES * (hh + 1)], gain[...]) * mul
                ref[0, :, 256 * j + LANES * hh:256 * j + LANES * (hh + 1)] = blk.astype(BF16)
        yt = _dot_nt(wvt_ref[256 * j:256 * (j + 1), :], xn)
        for hh in range(2):
            vt_ref[0, 2 * j + hh, 0] = yt[LANES * hh:LANES * (hh + 1), :].astype(BF16)
    z = _dot(xn, w_ref[:, 2 * hd:2 * hd + LANES]) + bf_ref[...]
    lf = jnp.minimum(z, 0.0) - jnp.log1p(jnp.exp(-jnp.abs(z)))
    row = _iota(lf.shape, 0)
    sh = 1
    while sh < tm:
        lf = lf + jnp.where(row >= sh, pltpu.roll(lf, sh, 0), 0.0)
        sh *= 2

    @pl.when(ti == 0)
    def _():
        carry_sc[...] = jnp.zeros_like(carry_sc)

    c = lf + carry_sc[0:1, :]
    carry_sc[...] = jnp.broadcast_to(c[tm - 1:tm, :], carry_sc.shape)
    rest = c * (-LOG2E)
    pieces = []
    for _ in range(N_BIAS_PARTS):
        piece = rest.astype(BF16)
        pieces.append(piece)
        rest = rest - piece.astype(F32)
    cf_ref[0] = _dot(jnp.concatenate(pieces, axis=1), sel_ref[...]).astype(BF16)


def _fox_inproj(x, g, w, wvt, bf, qg, kg, sel, tm=ROW_TILE):
    B, T, D = x.shape
    hd = N_HEADS * HEAD_DIM
    act = pl.BlockSpec((1, tm, hd), lambda b, t: (b, t, 0))
    vec = lambda n: pl.BlockSpec((1, n), lambda b, t: (0, 0))
    full = lambda a: pl.BlockSpec(a.shape, lambda b, t: (0,) * a.ndim)
    return pl.pallas_call(
        functools.partial(_fox_inproj_kernel, tm=tm, hd=hd),
        grid=(B, T // tm),
        in_specs=[pl.BlockSpec((1, tm, D), lambda b, t: (b, t, 0)), vec(D), full(w), full(wvt),
                  vec(LANES), vec(LANES), vec(LANES), full(sel)],
        out_specs=[act, act, pl.BlockSpec((1, N_PAIRS, 1, LANES, tm), lambda b, t: (b, 0, t, 0, 0)), act],
        out_shape=[jax.ShapeDtypeStruct((B, T, hd), BF16), jax.ShapeDtypeStruct((B, T, hd), BF16),
                   jax.ShapeDtypeStruct((B, N_PAIRS, T // tm, LANES, tm), BF16),
                   jax.ShapeDtypeStruct((B, T, hd), BF16)],
        scratch_shapes=[pltpu.VMEM((SUBLANES, LANES), F32)],
        compiler_params=_cparams(("arbitrary", "arbitrary")),
        name="fox_inproj",
    )(x, g, w, wvt, bf, qg, kg, sel)


def _fox_attn_kernel(q_ref, k_ref, cf_ref, vt_ref, o_ref, qa_sc, s_sc, m_sc, l_sc, acc_sc, *, tq, tk):
    qi = pl.program_id(2)
    q = q_ref[0]
    lane = _iota(q.shape, 1)
    lo = lane < HEAD_DIM
    zero = jnp.zeros_like(q)
    for hh in range(2):
        qa_sc[hh, :, 0:LANES] = jnp.where(lo, q, zero) if hh == 0 else jnp.where(lo, zero, q)
        feat = jnp.where(lane < N_BIAS_PARTS * (hh + 1), 1.0, 0.0)
        qa_sc[hh, :, LANES:2 * LANES] = jnp.where(lane >= N_BIAS_PARTS * hh, feat, 0.0).astype(BF16)
        _softmax_init(m_sc.at[hh], l_sc.at[hh], acc_sc.at[hh])

    def produce(ki, slot):
        start = pl.multiple_of(ki * tk, tk)
        ka = jnp.concatenate([k_ref[0, pl.ds(start, tk), :], cf_ref[0, pl.ds(start, tk), :]], axis=1)
        for hh in range(2):
            s_sc[slot, hh] = _dot_nt(ka, qa_sc[hh])

    def consume(ki, slot, diag):
        vt = vt_ref[0, 0, ki]
        for hh in range(2):
            s = s_sc[slot, hh]
            if diag:
                s = jnp.where(_iota(s.shape, 0) <= _iota(s.shape, 1), s, NEG)
            _softmax_update_t(s, vt[HEAD_DIM * hh:HEAD_DIM * (hh + 1), :], m_sc.at[hh], l_sc.at[hh], acc_sc.at[hh])

    produce(0, 0)

    def body(j, carry):
        produce(2 * j + 1, 1)
        consume(2 * j, 0, False)
        produce(2 * j + 2, 0)
        consume(2 * j + 1, 1, False)
        return carry

    lax.fori_loop(0, qi // 2, body, 0)

    @pl.when(qi % 2 == 1)
    def _():
        produce(qi, 1)
        consume(qi - 1, 0, False)
        consume(qi, 1, True)

    @pl.when(qi % 2 == 0)
    def _():
        consume(qi, 0, True)

    o_t = jnp.concatenate([_softmax_result(l_sc.at[hh], acc_sc.at[hh]) for hh in range(2)], axis=0)
    o_ref[0] = o_t.T.astype(BF16)


def _fox_attn(q, k, cf, vt, tq=FOX_TQ, tk=FOX_TK):
    B, T, hd = q.shape
    assert tq == tk
    nk = T // tk
    seq = pl.BlockSpec((1, T, LANES), lambda b, p, qi: (b, 0, p))
    return pl.pallas_call(
        functools.partial(_fox_attn_kernel, tq=tq, tk=tk),
        grid=(B, N_PAIRS, T // tq),
        in_specs=[pl.BlockSpec((1, tq, LANES), lambda b, p, qi: (b, qi, p)), seq, seq,
                  pl.BlockSpec((1, 1, nk, LANES, tk), lambda b, p, qi: (b, p, 0, 0, 0))],
        out_specs=pl.BlockSpec((1, tq, LANES), lambda b, p, qi: (b, qi, p)),
        out_shape=jax.ShapeDtypeStruct((B, T, hd), BF16),
        scratch_shapes=[pltpu.VMEM((2, tq, 2 * LANES), BF16), pltpu.VMEM((2, 2, tk, tq), F32), pltpu.VMEM((2, 1, tq), F32),
                        pltpu.VMEM((2, SUBLANES, tq), F32), pltpu.VMEM((2, HEAD_DIM, tq), F32)],
        compiler_params=_cparams(("parallel", "parallel", "arbitrary")),
        name="fox_attn",
    )(q, k, cf, vt)


def _outproj_kernel(h_ref, o_ref, w_ref, out_ref):
    out_ref[...] = h_ref[...] + _dot(o_ref[...], w_ref[...])


def _outproj(h, o, w, tm=ROW_TILE):
    B, T, D = h.shape
    n = B * T
    row = lambda c: pl.BlockSpec((tm, c), lambda i: (i, 0))
    out = pl.pallas_call(
        _outproj_kernel,
        grid=(n // tm,),
        in_specs=[row(D), row(o.shape[-1]), pl.BlockSpec(w.shape, lambda i: (0, 0))],
        out_specs=row(D),
        out_shape=jax.ShapeDtypeStruct((n, D), F32),
        compiler_params=_cparams(("parallel",)),
        name="outproj",
    )(h.reshape(n, D), o.reshape(n, -1), w)
    return out.reshape(B, T, D)


def _nsa_outproj_kernel(h_ref, oc_ref, os_ref, ow_ref, g_ref, w_ref, out_ref, o_sc):
    g = g_ref[...]
    lo = _iota((g.shape[0], LANES), 1) < HEAD_DIM
    for j in range(N_PAIRS):
        acc = None
        for br, ref in enumerate((oc_ref, os_ref, ow_ref)):
            c0 = br * N_HEADS + 2 * j
            gg = jnp.where(lo, g[:, c0:c0 + 1], g[:, c0 + 1:c0 + 2])
            term = gg * ref[:, LANES * j:LANES * (j + 1)].astype(F32)
            acc = term if acc is None else acc + term
        o_sc[:, LANES * j:LANES * (j + 1)] = acc.astype(BF16)
    out_ref[...] = h_ref[...] + _dot(o_sc[...], w_ref[...])


def _nsa_outproj(h, oc, osel, ow, gates, w, tm=ROW_TILE):
    B, T, D = h.shape
    n = B * T
    hd = oc.shape[-1]
    row = lambda c: pl.BlockSpec((tm, c), lambda i: (i, 0))
    out = pl.pallas_call(
        _nsa_outproj_kernel,
        grid=(n // tm,),
        in_specs=[row(D), row(hd), row(hd), row(hd), row(LANES), pl.BlockSpec(w.shape, lambda i: (0, 0))],
        out_specs=row(D),
        out_shape=jax.ShapeDtypeStruct((n, D), F32),
        scratch_shapes=[pltpu.VMEM((tm, hd), BF16)],
        compiler_params=_cparams(("parallel",)),
        name="nsa_outproj",
    )(h.reshape(n, D), oc.reshape(n, hd), osel.reshape(n, hd), ow.reshape(n, hd), gates.reshape(n, LANES), w)
    return out.reshape(B, T, D)


def _ffn_kernel(x_ref, g_ref, wg_ref, wv_ref, cwg_ref, cwv_ref, cbg_ref, cbv_ref, wd_ref,
                out_ref, xn_sc, acc_sc, carry_sc, *, tt, tf, nf):
    ti = pl.program_id(1)
    f = pl.program_id(2)

    @pl.when(f == 0)
    def _():
        xn_sc[...] = _row_rms(x_ref[0], g_ref[...]).astype(BF16)
        acc_sc[...] = jnp.zeros_like(acc_sc)

    @pl.when(ti == 0)
    def _():
        carry_sc[f] = jnp.zeros(carry_sc.shape[1:], F32)

    xn = xn_sc[...]
    prev = carry_sc[f]
    row = _iota((tt, tf), 0)

    def conv(u, cw_ref, cb_ref, prev8):
        um1 = jnp.where(row == 0, prev8[7:8, :], pltpu.roll(u, 1, 0))
        um2 = jnp.where(row == 0, prev8[6:7, :], jnp.where(row == 1, prev8[7:8, :], pltpu.roll(u, 2, 0)))
        cw = cw_ref[...]
        return cb_ref[...] + cw[0:1, :] * um2 + cw[1:2, :] * um1 + cw[2:3, :] * u

    ug = _dot(xn, wg_ref[...])
    uv = _dot(xn, wv_ref[...])
    cg = conv(ug, cwg_ref, cbg_ref, prev[:, :tf])
    cv = conv(uv, cwv_ref, cbv_ref, prev[:, tf:])
    carry_sc[f] = jnp.concatenate([ug[tt - 8:, :], uv[tt - 8:, :]], axis=1)
    a = cg * jax.nn.sigmoid(cg) * cv
    acc_sc[...] += _dot(a.astype(BF16), wd_ref[...])

    @pl.when(f == nf - 1)
    def _():
        out_ref[0] = x_ref[0] + acc_sc[...]


def _conv_ffn(h, g, w_up, conv_w, conv_b, w_down, tt=ROW_TILE, tf=256):
    B, T, D = h.shape
    d_ff = w_down.shape[0]
    nf = d_ff // tf
    act = pl.BlockSpec((1, tt, D), lambda b, t, f: (b, t, 0))
    return pl.pallas_call(
        functools.partial(_ffn_kernel, tt=tt, tf=tf, nf=nf),
        grid=(B, T // tt, nf),
        in_specs=[act, pl.BlockSpec((1, D), lambda b, t, f: (0, 0)),
                  pl.BlockSpec((D, tf), lambda b, t, f: (0, f)),
                  pl.BlockSpec((D, tf), lambda b, t, f: (0, f + nf)),
                  pl.BlockSpec((CONV_W, tf), lambda b, t, f: (0, f)),
                  pl.BlockSpec((CONV_W, tf), lambda b, t, f: (0, f + nf)),
                  pl.BlockSpec((1, tf), lambda b, t, f: (0, f)),
                  pl.BlockSpec((1, tf), lambda b, t, f: (0, f + nf)),
                  pl.BlockSpec((tf, D), lambda b, t, f: (f, 0))],
        out_specs=act,
        out_shape=jax.ShapeDtypeStruct((B, T, D), F32),
        scratch_shapes=[pltpu.VMEM((tt, D), BF16), pltpu.VMEM((tt, D), F32), pltpu.VMEM((nf, 8, 2 * tf), F32)],
        compiler_params=_cparams(("arbitrary", "arbitrary", "arbitrary")),
        name="conv_ffn",
    )(h, g, w_up, w_up, conv_w, conv_w, conv_b, conv_b, w_down)


def _kvproj_kernel(h_ref, g_ref, w_ref, wvt_ref, ksg_ref, kwg_ref, c_ref, s_ref,
                   craw_ref, ks_ref, kw_ref, vst_ref, vwt_ref, *, tm):
    xn = _row_rms(h_ref[0], g_ref[...]).astype(BF16)
    cos = c_ref[0]
    sin = s_ref[0]
    craw_ref[0] = _dot(xn, w_ref[:, 0:256])
    for idx, (ref, gain) in enumerate(((ks_ref, ksg_ref), (kw_ref, kwg_ref))):
        y = _dot(xn, w_ref[:, 256 * (idx + 1):256 * (idx + 2)])
        for grp in range(N_KV_GROUPS):
            blk = _pair_rope(_pair_rms(y[:, LANES * grp:LANES * (grp + 1)], gain[...]), cos, sin)
            ref[0, grp] = blk.astype(BF16)
    yt = _dot_nt(wvt_ref[...], xn).astype(BF16)
    for grp in range(N_KV_GROUPS):
        vst_ref[0, grp, 0] = yt[HEAD_DIM * grp:HEAD_DIM * (grp + 1), :]
        r0 = HEAD_DIM * (N_KV_GROUPS + grp)
        for c in range(tm // WIN_TQ):
            vwt_ref[0, grp, c] = yt[r0:r0 + HEAD_DIM, WIN_TQ * c:WIN_TQ * (c + 1)]


def _kvproj(h, g, w, wvt, ksg, kwg, cos, sin, tm=ROW_TILE):
    B, T, D = h.shape
    G = N_KV_GROUPS
    vec = lambda n: pl.BlockSpec((1, n), lambda b, t: (0, 0))
    full = lambda a: pl.BlockSpec(a.shape, lambda b, t: (0,) * a.ndim)
    tab = pl.BlockSpec((1, tm, LANES), lambda b, t: (b, t, 0))
    dup = pl.BlockSpec((1, G, tm, LANES), lambda b, t: (b, 0, t, 0))
    dup_shape = jax.ShapeDtypeStruct((B, G, T, LANES), BF16)
    nw = tm // WIN_TQ
    return pl.pallas_call(
        functools.partial(_kvproj_kernel, tm=tm),
        grid=(B, T // tm),
        in_specs=[pl.BlockSpec((1, tm, D), lambda b, t: (b, t, 0)), vec(D), full(w), full(wvt),
                  vec(LANES), vec(LANES), tab, tab],
        out_specs=[pl.BlockSpec((1, tm, 256), lambda b, t: (b, t, 0)), dup, dup,
                   pl.BlockSpec((1, G, 1, HEAD_DIM, tm), lambda b, t: (b, 0, t, 0, 0)),
                   pl.BlockSpec((1, G, nw, HEAD_DIM, WIN_TQ), lambda b, t: (b, 0, t, 0, 0))],
        out_shape=[jax.ShapeDtypeStruct((B, T, 256), F32), dup_shape, dup_shape,
                   jax.ShapeDtypeStruct((B, G, T // tm, HEAD_DIM, tm), BF16),
                   jax.ShapeDtypeStruct((B, G, T // WIN_TQ, HEAD_DIM, WIN_TQ), BF16)],
        compiler_params=_cparams(("parallel", "parallel")),
        name="kvproj",
    )(h, g, w, wvt, ksg, kwg, cos, sin)


def _compress_kernel(r_ref, w1_ref, pe_ref, w2_ref, w2t_ref, gain_ref, c_ref, s_ref, kc_ref, vct_ref, *, n_cmp):
    r = r_ref[0, 0, 0].astype(BF16)
    half = r.shape[1]
    a = _dot(r, w1_ref[0, :half, :])
    b = _dot(r, w1_ref[0, half:, :])
    peb = _dot(pe_ref[0], w1_ref[0])[0:1, :]
    rows = r.shape[0]
    hid = a + pltpu.roll(b, rows - 1, 0) + peb
    act = jax.nn.gelu(hid).astype(BF16)
    y = _dot(act, w2_ref[0])
    yk = _pair_rope(_pair_rms(y, gain_ref[...]), c_ref[0], s_ref[0])
    kc_ref[0, 0, 0] = jnp.where(_iota(y.shape, 0) < n_cmp, yk, 0.0).astype(BF16)
    yt = _dot_nt(w2t_ref[0], act)
    vct_ref[0, 0, 0] = jnp.where(_iota(yt.shape, 1) < n_cmp, yt, 0.0).astype(BF16)


def _compress(r, w1, pe, w2, w2t, gain, cos, sin, n_cmp):
    B, _, G, rows, width = r.shape
    per_kv = lambda a: pl.BlockSpec((1,) + a.shape[1:], lambda b, kv, g: (kv,) + (0,) * (a.ndim - 1))
    return pl.pallas_call(
        functools.partial(_compress_kernel, n_cmp=n_cmp),
        grid=(B, 2, G),
        in_specs=[pl.BlockSpec((1, 1, 1, rows, width), lambda b, kv, g: (b, kv, g, 0, 0)),
                  per_kv(w1), per_kv(pe), per_kv(w2), per_kv(w2t),
                  pl.BlockSpec((1, LANES), lambda b, kv, g: (0, 0)),
                  pl.BlockSpec((1, rows, LANES), lambda b, kv, g: (b, 0, 0)),
                  pl.BlockSpec((1, rows, LANES), lambda b, kv, g: (b, 0, 0))],
        out_specs=[pl.BlockSpec((1, 1, 1, rows, LANES), lambda b, kv, g: (b, kv, g, 0, 0)),
                   pl.BlockSpec((1, 1, 1, HEAD_DIM, rows), lambda b, kv, g: (b, kv, g, 0, 0))],
        out_shape=[jax.ShapeDtypeStruct((B, 2, G, rows, LANES), BF16),
                   jax.ShapeDtypeStruct((B, 2, G, HEAD_DIM, rows), BF16)],
        compiler_params=_cparams(("parallel", "parallel", "parallel")),
        name="compress",
    )(r, w1, pe, w2, w2t, gain, cos, sin)


def _nsa_inproj_kernel(h_ref, g_ref, w_ref, bg_ref, qg_ref, c_ref, s_ref, q_ref, gate_ref, *, hd):
    xn = _row_rms(h_ref[0], g_ref[...]).astype(BF16)
    cos = c_ref[0]
    sin = s_ref[0]
    for j in range(hd // 256):
        y = _dot(xn, w_ref[:, 256 * j:256 * (j + 1)])
        for hh in range(2):
            blk = _pair_rope(_pair_rms(y[:, LANES * hh:LANES * (hh + 1)], qg_ref[...]), cos, sin) * Q_SCALE
            q_ref[0, :, 256 * j + LANES * hh:256 * j + LANES * (hh + 1)] = blk.astype(BF16)
    z = _dot(xn, w_ref[:, hd:hd + LANES]) + bg_ref[...]
    gate_ref[0] = jax.nn.sigmoid(z)


def _nsa_inproj(h, g, w, bg, qg, cos, sin, tm=ROW_TILE):
    B, T, D = h.shape
    hd = N_HEADS * HEAD_DIM
    vec = lambda n: pl.BlockSpec((1, n), lambda b, t: (0, 0))
    tab = pl.BlockSpec((1, tm, LANES), lambda b, t: (b, t, 0))
    return pl.pallas_call(
        functools.partial(_nsa_inproj_kernel, hd=hd),
        grid=(B, T // tm),
        in_specs=[pl.BlockSpec((1, tm, D), lambda b, t: (b, t, 0)), vec(D),
                  pl.BlockSpec(w.shape, lambda b, t: (0, 0)), vec(LANES), vec(LANES), tab, tab],
        out_specs=[pl.BlockSpec((1, tm, hd), lambda b, t: (b, t, 0)), tab],
        out_shape=[jax.ShapeDtypeStruct((B, T, hd), BF16), jax.ShapeDtypeStruct((B, T, LANES), F32)],
        compiler_params=_cparams(("parallel", "parallel")),
        name="nsa_inproj",
    )(h, g, w, bg, qg, cos, sin)


def _stack_heads(q, dst_ref, tq):
    lo = _iota((tq, LANES), 1) < HEAD_DIM
    zero = jnp.zeros((tq, LANES), q.dtype)
    for j in range(HEADS_PER_GROUP // 2):
        x = q[:, LANES * j:LANES * (j + 1)]
        dst_ref[(2 * j) * tq:(2 * j + 1) * tq, 0:LANES] = jnp.where(lo, x, zero)
        dst_ref[(2 * j + 1) * tq:(2 * j + 2) * tq, 0:LANES] = jnp.where(lo, zero, x)


def _unstack_heads_t(o_t, o_ref, tq):
    for j in range(HEADS_PER_GROUP // 2):
        pair = jnp.concatenate([o_t[:, (2 * j) * tq:(2 * j + 1) * tq], o_t[:, (2 * j + 1) * tq:(2 * j + 2) * tq]], axis=0)
        o_ref[0, :, LANES * j:LANES * (j + 1)] = pair.T.astype(o_ref.dtype)


def _nsa_cmp_kernel(q_ref, kc_ref, vct_ref, ovt_ref, oc_ref, sel_ref, qs_sc, *, tq, n_slc, top_k):
    qi = pl.program_id(2)
    _stack_heads(q_ref[0], qs_sc, tq)
    kc = kc_ref[0, 0, 0]
    n_rows = kc.shape[0]
    n_chunks = 2
    cw = HEADS_PER_GROUP * tq // n_chunks
    scores = [_dot_nt(kc, qs_sc[c * cw:(c + 1) * cw, :]) for c in range(n_chunks)]
    qpos = qi * tq + (_iota((n_rows, cw), 1) & (tq - 1))
    vis = _iota((n_rows, cw), 0) * CMP_STRIDE + (CMP_LEN - 1) <= qpos
    outs = []
    p_sum = None
    for s in scores:
        s = jnp.where(vis, s, NEG)
        e = jnp.where(vis, jnp.exp2(s - jnp.max(s, axis=0, keepdims=True)), 0.0)
        p = e * (1.0 / jnp.maximum(jnp.sum(e, axis=0, keepdims=True), 1.0))
        outs.append(_dot(vct_ref[0, 0, 0], p.astype(BF16)))
        for h in range(cw // tq):
            term = p[:, h * tq:(h + 1) * tq]
            p_sum = term if p_sum is None else p_sum + term
    _unstack_heads_t(jnp.concatenate(outs, axis=1), oc_ref, tq)
    p_hi = p_sum.astype(BF16)
    p_lo = (p_sum - p_hi.astype(F32)).astype(BF16)
    imp = _dot(ovt_ref[...], p_hi) + _dot(ovt_ref[...], p_lo)
    jb = _iota(imp.shape, 0)
    qp = qi * tq + _iota(imp.shape, 1)
    cur = qp // SLC_LEN
    forced = (jb == 0) | (jb == cur) | (jb == cur - 1)
    score = jnp.where(jb * SLC_LEN <= qp, jnp.where(forced, FORCED_SCORE, imp), NEG)
    score = jnp.where(jb < n_slc, score, -jnp.inf)
    bias = jnp.full(imp.shape, NEG, F32)
    for _ in range(top_k):
        mx = jnp.max(score, axis=0, keepdims=True)
        idx = jnp.min(jnp.where(score == mx, jb, LANES), axis=0, keepdims=True)
        pick = jb == idx
        bias = jnp.where(pick, 0.0, bias)
        score = jnp.where(pick, -jnp.inf, score)
    sel_ref[0, 0] = bias.T.astype(BF16)


def _nsa_cmp(q, kc_all, vct_all, overlap_t, n_slc, tq=CMP_TQ):
    B, T, hd = q.shape
    G = N_KV_GROUPS
    rows = kc_all.shape[3]
    gw = hd // G
    top_k = min(SLC_TOPK, n_slc)
    return pl.pallas_call(
        functools.partial(_nsa_cmp_kernel, tq=tq, n_slc=n_slc, top_k=top_k),
        grid=(B, G, T // tq),
        in_specs=[pl.BlockSpec((1, tq, gw), lambda b, g, qi: (b, qi, g)),
                  pl.BlockSpec((1, 1, 1, rows, LANES), lambda b, g, qi: (b, 0, g, 0, 0)),
                  pl.BlockSpec((1, 1, 1, HEAD_DIM, rows), lambda b, g, qi: (b, 1, g, 0, 0)),
                  pl.BlockSpec(overlap_t.shape, lambda b, g, qi: (0, 0))],
        out_specs=[pl.BlockSpec((1, tq, gw), lambda b, g, qi: (b, qi, g)),
                   pl.BlockSpec((1, 1, tq, LANES), lambda b, g, qi: (b, g, qi, 0))],
        out_shape=[jax.ShapeDtypeStruct((B, T, hd), BF16), jax.ShapeDtypeStruct((B, G, T, LANES), BF16)],
        scratch_shapes=[pltpu.VMEM((HEADS_PER_GROUP * tq, LANES), BF16)],
        compiler_params=_cparams(("parallel", "parallel", "parallel")),
        name="nsa_cmp",
    )(q, kc_all, vct_all, overlap_t)


def _nsa_slc_kernel(q_ref, sel_ref, k_ref, oh_ref, vt_ref, o_ref, qa_sc, s_sc, m_sc, l_sc, acc_sc, *, tq, tk, n_chunks):
    qi = pl.program_id(2)
    last = (qi * tq) // tk
    _stack_heads(q_ref[0], qa_sc, tq)
    sel = sel_ref[0, 0]
    for h in range(HEADS_PER_GROUP):
        qa_sc[h * tq:(h + 1) * tq, LANES:2 * LANES] = sel
    _softmax_init(m_sc, l_sc, acc_sc)
    cw = HEADS_PER_GROUP * tq // n_chunks

    def produce(ki, slot):
        start = pl.multiple_of(ki * tk, tk)
        ka = jnp.concatenate([k_ref[0, 0, pl.ds(start, tk), :], oh_ref[pl.ds(start, tk), :]], axis=1)
        for c in range(n_chunks):
            s_sc[slot, :, c * cw:(c + 1) * cw] = _dot_nt(ka, qa_sc[c * cw:(c + 1) * cw, :])

    def consume(ki, slot, diag):
        vt = vt_ref[0, 0, ki]
        for c in range(n_chunks):
            cols = slice(c * cw, (c + 1) * cw)
            s = s_sc[slot, :, cols]
            if diag:
                kpos = ki * tk + _iota(s.shape, 0)
                qpos = qi * tq + (_iota(s.shape, 1) & (tq - 1))
                s = jnp.where(kpos <= qpos, s, NEG)
            _softmax_update_t(s, vt, m_sc.at[:, cols], l_sc.at[:, cols], acc_sc.at[:, cols])

    produce(0, 0)

    def body(j, carry):
        produce(2 * j + 1, 1)
        consume(2 * j, 0, False)
        produce(2 * j + 2, 0)
        consume(2 * j + 1, 1, False)
        return carry

    lax.fori_loop(0, last // 2, body, 0)

    @pl.when(last % 2 == 1)
    def _():
        produce(last, 1)
        consume(last - 1, 0, False)
        consume(last, 1, True)

    @pl.when(last % 2 == 0)
    def _():
        consume(last, 0, True)

    _unstack_heads_t(_softmax_result(l_sc, acc_sc), o_ref, tq)


def _nsa_slc(q, sel, ks, vst, tq=SLC_TQ, tk=SLC_TK):
    B, T, hd = q.shape
    G = N_KV_GROUPS
    gw = hd // G
    cols = HEADS_PER_GROUP * tq
    assert tk % tq == 0
    onehot = (jnp.arange(T)[:, None] // SLC_LEN == jnp.arange(LANES)[None, :]).astype(BF16)
    return pl.pallas_call(
        functools.partial(_nsa_slc_kernel, tq=tq, tk=tk, n_chunks=2),
        grid=(B, G, T // tq),
        in_specs=[pl.BlockSpec((1, tq, gw), lambda b, g, qi: (b, qi, g)),
                  pl.BlockSpec((1, 1, tq, LANES), lambda b, g, qi: (b, g, qi, 0)),
                  pl.BlockSpec((1, 1, T, LANES), lambda b, g, qi: (b, g, 0, 0)),
                  pl.BlockSpec((T, LANES), lambda b, g, qi: (0, 0)),
                  pl.BlockSpec((1, 1, T // tk, HEAD_DIM, tk), lambda b, g, qi: (b, g, 0, 0, 0))],
        out_specs=pl.BlockSpec((1, tq, gw), lambda b, g, qi: (b, qi, g)),
        out_shape=jax.ShapeDtypeStruct((B, T, hd), BF16),
        scratch_shapes=[pltpu.VMEM((cols, 2 * LANES), BF16), pltpu.VMEM((2, tk, cols), F32), pltpu.VMEM((1, cols), F32),
                        pltpu.VMEM((SUBLANES, cols), F32), pltpu.VMEM((HEAD_DIM, cols), F32)],
        compiler_params=_cparams(("parallel", "parallel", "arbitrary")),
        name="nsa_slc",
    )(q, sel, ks, onehot, vst)


def _nsa_win_kernel(q_ref, k_ref, vt_ref, o_ref, qs_sc, s_sc, m_sc, l_sc, acc_sc, *, tq, n_back, n_chunks):
    qi = pl.program_id(2)
    _stack_heads(q_ref[0], qs_sc, tq)
    _softmax_init(m_sc, l_sc, acc_sc)
    cw = HEADS_PER_GROUP * tq // n_chunks

    def produce(ti, slot):
        start = pl.multiple_of(ti * tq, tq)
        k = k_ref[0, 0, pl.ds(start, tq), :]
        for c in range(n_chunks):
            s_sc[slot, :, c * cw:(c + 1) * cw] = _dot_nt(k, qs_sc[c * cw:(c + 1) * cw, :])

    def consume(ti, slot, kind):
        vt = vt_ref[0, 0, ti]
        for c in range(n_chunks):
            cols = slice(c * cw, (c + 1) * cw)
            s = s_sc[slot, :, cols]
            r = _iota(s.shape, 0)
            q_in_tile = _iota(s.shape, 1) & (tq - 1)
            if kind == "oldest":
                s = jnp.where(r > q_in_tile, s, NEG)
            elif kind == "diag":
                s = jnp.where(r <= q_in_tile, s, NEG)
            _softmax_update_t(s, vt, m_sc.at[:, cols], l_sc.at[:, cols], acc_sc.at[:, cols])

    def kind_of(back):
        return "oldest" if back == n_back else ("diag" if back == 0 else "full")

    for first in range(n_back + 1):
        cond = (qi >= n_back) if first == n_back else (qi == first)

        @pl.when(cond)
        def _(first=first):
            backs = list(range(first, -1, -1))
            produce(qi - backs[0], 0)
            for n, back in enumerate(backs):
                if n + 1 < len(backs):
                    produce(qi - backs[n + 1], (n + 1) % 2)
                consume(qi - back, n % 2, kind_of(back))

    _unstack_heads_t(_softmax_result(l_sc, acc_sc), o_ref, tq)


def _nsa_win(q, kw, vwt, tq=WIN_TQ):
    B, T, hd = q.shape
    G = N_KV_GROUPS
    gw = hd // G
    cols = HEADS_PER_GROUP * tq
    return pl.pallas_call(
        functools.partial(_nsa_win_kernel, tq=tq, n_back=WIN // tq, n_chunks=4),
        grid=(B, G, T // tq),
        in_specs=[pl.BlockSpec((1, tq, gw), lambda b, g, qi: (b, qi, g)),
                  pl.BlockSpec((1, 1, T, LANES), lambda b, g, qi: (b, g, 0, 0)),
                  pl.BlockSpec((1, 1, T // tq, HEAD_DIM, tq), lambda b, g, qi: (b, g, 0, 0, 0))],
        out_specs=pl.BlockSpec((1, tq, gw), lambda b, g, qi: (b, qi, g)),
        out_shape=jax.ShapeDtypeStruct((B, T, hd), BF16),
        scratch_shapes=[pltpu.VMEM((cols, LANES), BF16), pltpu.VMEM((2, tq, cols), F32), pltpu.VMEM((1, cols), F32),
                        pltpu.VMEM((SUBLANES, cols), F32), pltpu.VMEM((HEAD_DIM, cols), F32)],
        compiler_params=_cparams(("parallel", "parallel", "arbitrary")),
        name="nsa_win",
    )(q, kw, vwt)


def _pad_cols(a, n):
    return jnp.pad(a, ((0, 0), (0, n - a.shape[1])))


def _lane_vec(v):
    return jnp.tile(v.astype(F32), 2).reshape(1, LANES)


def _bias_feature_selector():
    sel = np.zeros((N_BIAS_PARTS * LANES, N_HEADS * HEAD_DIM), np.float32)
    for part in range(N_BIAS_PARTS):
        for h in range(N_HEADS):
            sel[part * LANES + h, (h // 2) * LANES + (h % 2) * N_BIAS_PARTS + part] = 1.0
    return jnp.asarray(sel, BF16)


def kernel(x, positions, a_norm, a_w_in, a_b_f, a_q_gain, a_k_gain, a_w_out, kv_norm, kv_w, kc_pe, vc_pe, kc_w1, kc_w2, vc_w1, vc_w2, kc_gain, ks_gain, kw_gain, b_norm, b_w_in, b_b_gate, b_q_gain, b_w_out, f_norm, f_w_up, f_conv_w, f_conv_b, f_w_down):
    B, T, D = x.shape
    hd = N_HEADS * HEAD_DIM
    G = N_KV_GROUPS
    n_a = a_norm.shape[0]
    n_b = b_norm.shape[0]
    depth = n_a + n_b
    n_slc = T // SLC_LEN
    n_cmp = (T - CMP_LEN) // CMP_STRIDE + 1
    assert T % 1024 == 0 and n_slc <= LANES and hd == 1024 and D == 1024

    half = ROT_DIM // 2
    inv = ROPE_THETA ** (-jnp.arange(half, dtype=F32) * (2.0 / ROT_DIM))
    head_inv = jnp.concatenate([inv, inv, jnp.zeros((HEAD_DIM - ROT_DIM,), F32)])
    head_sign = jnp.concatenate([-jnp.ones((half,), F32), jnp.ones((half,), F32), jnp.zeros((HEAD_DIM - ROT_DIM,), F32)])
    inv_lane = jnp.tile(head_inv, 2).reshape(1, LANES)
    sign_lane = jnp.tile(head_sign, 2).reshape(1, LANES)
    cos_t, sin_t = _rope_tables(positions, inv_lane, sign_lane)
    end_pos = positions[:, CMP_LEN - 1::CMP_STRIDE]
    end_pos = jnp.pad(end_pos, ((0, 0), (0, T // CMP_STRIDE - n_cmp)))
    cos_c, sin_c = _rope_tables(end_pos, inv_lane, sign_lane)

    cs = jnp.arange(T // CMP_STRIDE) * CMP_STRIDE
    ss = jnp.arange(LANES) * SLC_LEN
    overlap_t = (jnp.maximum(jnp.minimum(cs[None, :] + CMP_LEN, ss[:, None] + SLC_LEN)
                             - jnp.maximum(cs[None, :], ss[:, None]), 0).astype(F32) / CMP_LEN).astype(BF16)

    h = x
    kv = None
    for layer in range(depth):
        if layer < n_a:
            i = layer
            w = jnp.concatenate([a_w_in[i][:, :2 * hd], _pad_cols(a_w_in[i][:, 3 * hd:], LANES)], axis=1).astype(BF16)
            wvt = a_w_in[i][:, 2 * hd:3 * hd].T.astype(BF16)
            bf = _pad_cols(a_b_f[i].reshape(1, -1), LANES)
            q, k, vt, cf = _fox_inproj(h, a_norm[i].reshape(1, D), w, wvt, bf, _lane_vec(a_q_gain[i]),
                                       _lane_vec(a_k_gain[i]), _bias_feature_selector())
            o = _fox_attn(q, k, cf, vt)
            h = _outproj(h, o, a_w_out[i].astype(BF16))
        else:
            i = layer - n_a
            kc_all, vct_all, ks, kw, vst, vwt = kv
            w = jnp.concatenate([b_w_in[i][:, :hd], _pad_cols(b_w_in[i][:, hd:], LANES)], axis=1).astype(BF16)
            bg = _pad_cols(b_b_gate[i].reshape(1, -1), LANES)
            q, gates = _nsa_inproj(h, b_norm[i].reshape(1, D), w, bg, _lane_vec(b_q_gain[i]), cos_t, sin_t)
            o_c, sel = _nsa_cmp(q, kc_all, vct_all, overlap_t, n_slc)
            o_s = _nsa_slc(q, sel, ks, vst)
            o_w = _nsa_win(q, kw, vwt)
            h = _nsa_outproj(h, o_c, o_s, o_w, gates, b_w_out[i].astype(BF16))
        h = _conv_ffn(h, f_norm[layer].reshape(1, D), f_w_up[layer].astype(BF16), f_conv_w[layer],
                      f_conv_b[layer].reshape(1, -1), f_w_down[layer].astype(BF16))
        if layer == n_a - 1:
            w6 = kv_w.reshape(D, 6, G, HEAD_DIM)
            raw_cols = w6[:, 0:2].reshape(D, 2 * G * HEAD_DIM)
            wk = w6[:, (2, 4)]
            dup_cols = jnp.concatenate([wk, wk], axis=-1).reshape(D, 2 * G * LANES)
            wkv = jnp.concatenate([raw_cols, dup_cols], axis=1).astype(BF16)
            wvt = w6[:, (3, 5)].reshape(D, 2 * G * HEAD_DIM).T.astype(BF16)
            craw, ks, kw, vst, vwt = _kvproj(h, kv_norm.reshape(1, D), wkv, wvt, _lane_vec(ks_gain),
                                             _lane_vec(kw_gain), cos_t, sin_t)
            r = craw.reshape(B, T, 2, G, HEAD_DIM).transpose(0, 2, 3, 1, 4).reshape(B, 2, G, T // CMP_STRIDE, CMP_STRIDE * HEAD_DIM)
            w1 = jnp.stack([kc_w1, vc_w1]).astype(BF16)
            pe = jnp.stack([kc_pe.reshape(1, -1), vc_pe.reshape(1, -1)])
            pe = jnp.pad(pe, ((0, 0), (0, 7), (0, 0))).astype(BF16)
            w2 = jnp.stack([jnp.concatenate([kc_w2, kc_w2], axis=1), jnp.concatenate([vc_w2, vc_w2], axis=1)]).astype(BF16)
            w2t = jnp.stack([kc_w2.T, vc_w2.T]).astype(BF16)
            kc_all, vct_all = _compress(r, w1, pe, w2, w2t, _lane_vec(kc_gain), cos_c, sin_c, n_cmp)
            kv = (kc_all, vct_all, ks, kw, vst, vwt)
    return h
```

```python
import functools

import numpy as np
import jax
import jax.numpy as jnp
from jax import lax
from jax.experimental import pallas as pl
from jax.experimental.pallas import tpu as pltpu

F32 = jnp.float32
BF16 = jnp.bfloat16

LANES = 128
SUBLANES = 8
HEAD_DIM = 64
N_HEADS = 16
N_PAIRS = N_HEADS // 2
N_KV_GROUPS = 2
HEADS_PER_GROUP = N_HEADS // N_KV_GROUPS
ROT_DIM = HEAD_DIM // 4
ROPE_THETA = 500000.0
CMP_LEN = 32
CMP_STRIDE = 16
SLC_LEN = 64
SLC_TOPK = 16
WIN = 512
CONV_W = 3
RMS_EPS = 1e-6
NEG = -1e30
FORCED_SCORE = 1e6
LOG2E = 1.4426950408889634
Q_SCALE = HEAD_DIM ** -0.5 * LOG2E
N_BIAS_PARTS = 3
ONES_ROWS = 16
ACC_ROWS = HEAD_DIM + ONES_ROWS

ROW_TILE = 512
FOX_TQ = 512
FOX_TK = ROW_TILE
SLC_TQ = 128
SLC_TK = ROW_TILE
WIN_TQ = 256
CMP_TQ = 128

VMEM_LIMIT = 48 * 1024 * 1024


def _cparams(sem):
    return pltpu.CompilerParams(dimension_semantics=sem, vmem_limit_bytes=VMEM_LIMIT)


def _iota(shape, axis):
    return lax.broadcasted_iota(jnp.int32, shape, axis)


def _row_rms(x, g):
    ms = jnp.mean(x * x, axis=-1, keepdims=True)
    return x * lax.rsqrt(ms + RMS_EPS) * g


def _pair_rms(y, gain):
    lo = _iota(y.shape, 1) < HEAD_DIM
    y2 = y * y
    s_lo = jnp.sum(jnp.where(lo, y2, 0.0), axis=-1, keepdims=True)
    s_hi = jnp.sum(jnp.where(lo, 0.0, y2), axis=-1, keepdims=True)
    ms = jnp.where(lo, s_lo, s_hi) * (1.0 / HEAD_DIM)
    return y * lax.rsqrt(ms + RMS_EPS) * gain


def _pair_rope(y, cos, sin):
    lane = _iota(y.shape, 1) & (HEAD_DIM - 1)
    partner = jnp.where(lane < ROT_DIM // 2,
                        pltpu.roll(y, LANES - ROT_DIM // 2, 1),
                        pltpu.roll(y, ROT_DIM // 2, 1))
    return y * cos + partner * sin


def _dot(a, b):
    return jnp.dot(a, b, preferred_element_type=F32)


def _dot_nt(a, b):
    return lax.dot_general(a, b, (((1,), (1,)), ((), ())), preferred_element_type=F32)


def _with_ones_rows(vt):
    return jnp.concatenate([vt, jnp.ones((ONES_ROWS, vt.shape[1]), vt.dtype)], axis=0)


def _softmax_probs_t(s, m_ref, alpha_ref, p_ref):
    tk, n = s.shape
    m_old = m_ref[...]
    m_tile = jnp.max(jnp.max(s.reshape(tk // SUBLANES, SUBLANES, n), axis=0), axis=0, keepdims=True)
    m_new = jnp.maximum(m_old, m_tile)
    alpha_ref[...] = jnp.exp2(m_old - m_new)
    p_ref[...] = jnp.exp2((s - m_new).astype(BF16))
    m_ref[...] = m_new


def _softmax_accumulate_t(vta, alpha_ref, p_ref, acc_ref):
    acc_ref[...] = alpha_ref[...] * acc_ref[...] + _dot(vta, p_ref[...])


def _pipelined_tiles(n, stage_a, stage_b, stage_c):
    @pl.when(n == 0)
    def _():
        stage_a(0, 0)
        stage_b(0, 0, True)
        stage_c(0, 0)

    @pl.when(n >= 1)
    def _():
        stage_a(0, 0)
        stage_a(1, 1)
        stage_b(0, 0, False)

    def body(j, carry):
        k = 2 + 2 * j
        stage_a(k, 0)
        stage_b(k - 1, 1, False, parts=(0,))
        stage_c(k - 2, 0)
        stage_b(k - 1, 1, False, parts=(1,))
        stage_a(k + 1, 1)
        stage_b(k, 0, False, parts=(0,))
        stage_c(k - 1, 1)
        stage_b(k, 0, False, parts=(1,))
        return carry

    lax.fori_loop(0, jnp.maximum(n - 1, 0) // 2, body, 0)

    @pl.when((n >= 2) & (n % 2 == 0))
    def _():
        stage_a(n, 0)
        stage_b(n - 1, 1, False)
        stage_c(n - 2, 0)
        stage_b(n, 0, True)
        stage_c(n - 1, 1)
        stage_c(n, 0)

    @pl.when(n % 2 == 1)
    def _():
        stage_b(n, 1, True)
        stage_c(n - 1, 0)
        stage_c(n, 1)


def _softmax_init(m_ref, acc_ref):
    m_ref[...] = jnp.full_like(m_ref, NEG)
    acc_ref[...] = jnp.zeros_like(acc_ref)


def _softmax_result(acc_ref):
    acc = acc_ref[...]
    return acc[:HEAD_DIM, :] * (1.0 / acc[HEAD_DIM:HEAD_DIM + 1, :])


def _rope_table_kernel(pos_ref, inv_ref, sign_ref, c_ref, s_ref):
    ang = pos_ref[0].astype(F32) * inv_ref[...]
    c_ref[0] = jnp.cos(ang)
    s_ref[0] = jnp.sin(ang) * sign_ref[...]


def _rope_tables(pos, inv_lane, sign_lane):
    B, T = pos.shape
    tm = min(T, ROW_TILE)
    spec = pl.BlockSpec((1, tm, LANES), lambda b, t: (b, t, 0))
    vec = pl.BlockSpec((1, LANES), lambda b, t: (0, 0))
    return pl.pallas_call(
        _rope_table_kernel,
        grid=(B, T // tm),
        in_specs=[pl.BlockSpec((1, tm, 1), lambda b, t: (b, t, 0)), vec, vec],
        out_specs=[spec, spec],
        out_shape=[jax.ShapeDtypeStruct((B, T, LANES), F32)] * 2,
        compiler_params=_cparams(("parallel", "parallel")),
        name="rope_tables",
    )(pos.reshape(B, T, 1), inv_lane, sign_lane)


def _fox_inproj_kernel(x_ref, g_ref, w_ref, wvt_ref, bf_ref, qg_ref, kg_ref, sel_ref,
                       q_ref, k_ref, vt_ref, cf_ref, carry_sc, *, tm, hd):
    ti = pl.program_id(1)
    xn = _row_rms(x_ref[0], g_ref[...]).astype(BF16)
    for j in range(hd // 256):
        for part, (ref, gain, mul) in enumerate(((q_ref, qg_ref, Q_SCALE), (k_ref, kg_ref, 1.0))):
            c0 = part * hd + 256 * j
            y = _dot(xn, w_ref[:, c0:c0 + 256])
            for hh in range(2):
                blk = _pair_rms(y[:, LANES * hh:LANES * (hh + 1)], gain[...]) * mul
                ref[0, :, 256 * j + LANES * hh:256 * j + LANES * (hh + 1)] = blk.astype(BF16)
        yt = _dot_nt(wvt_ref[256 * j:256 * (j + 1), :], xn)
        for hh in range(2):
            vt_ref[0, 2 * j + hh, 0] = yt[LANES * hh:LANES * (hh + 1), :].astype(BF16)
    z = _dot(xn, w_ref[:, 2 * hd:2 * hd + LANES]) + bf_ref[...]
    lf = jnp.minimum(z, 0.0) - jnp.log1p(jnp.exp(-jnp.abs(z)))
    row = _iota(lf.shape, 0)
    sh = 1
    while sh < tm:
        lf = lf + jnp.where(row >= sh, pltpu.roll(lf, sh, 0), 0.0)
        sh *= 2

    @pl.when(ti == 0)
    def _():
        carry_sc[...] = jnp.zeros_like(carry_sc)

    c = lf + carry_sc[0:1, :]
    carry_sc[...] = jnp.broadcast_to(c[tm - 1:tm, :], carry_sc.shape)
    rest = c * (-LOG2E)
    pieces = []
    for _ in range(N_BIAS_PARTS):
        piece = rest.astype(BF16)
        pieces.append(piece)
        rest = rest - piece.astype(F32)
    cf_ref[0] = _dot(jnp.concatenate(pieces, axis=1), sel_ref[...]).astype(BF16)


def _fox_inproj(x, g, w, wvt, bf, qg, kg, sel, tm=ROW_TILE):
    B, T, D = x.shape
    hd = N_HEADS * HEAD_DIM
    act = pl.BlockSpec((1, tm, hd), lambda b, t: (b, t, 0))
    vec = lambda n: pl.BlockSpec((1, n), lambda b, t: (0, 0))
    full = lambda a: pl.BlockSpec(a.shape, lambda b, t: (0,) * a.ndim)
    return pl.pallas_call(
        functools.partial(_fox_inproj_kernel, tm=tm, hd=hd),
        grid=(B, T // tm),
        in_specs=[pl.BlockSpec((1, tm, D), lambda b, t: (b, t, 0)), vec(D), full(w), full(wvt),
                  vec(LANES), vec(LANES), vec(LANES), full(sel)],
        out_specs=[act, act, pl.BlockSpec((1, N_PAIRS, 1, LANES, tm), lambda b, t: (b, 0, t, 0, 0)), act],
        out_shape=[jax.ShapeDtypeStruct((B, T, hd), BF16), jax.ShapeDtypeStruct((B, T, hd), BF16),
                   jax.ShapeDtypeStruct((B, N_PAIRS, T // tm, LANES, tm), BF16),
                   jax.ShapeDtypeStruct((B, T, hd), BF16)],
        scratch_shapes=[pltpu.VMEM((SUBLANES, LANES), F32)],
        compiler_params=_cparams(("arbitrary", "arbitrary")),
        name="fox_inproj",
    )(x, g, w, wvt, bf, qg, kg, sel)


def _fox_attn_kernel(q_ref, k_ref, cf_ref, vt_ref, o_ref, qa_sc, s_sc, p_sc, alpha_sc, m_sc, acc_sc, *, tq, tk):
    qi = pl.program_id(2)
    q = q_ref[0]
    lane = _iota(q.shape, 1)
    lo = lane < HEAD_DIM
    zero = jnp.zeros_like(q)
    for hh in range(2):
        qa_sc[hh, :, 0:LANES] = jnp.where(lo, q, zero) if hh == 0 else jnp.where(lo, zero, q)
        feat = jnp.where(lane < N_BIAS_PARTS * (hh + 1), 1.0, 0.0)
        qa_sc[hh, :, LANES:2 * LANES] = jnp.where(lane >= N_BIAS_PARTS * hh, feat, 0.0).astype(BF16)
        _softmax_init(m_sc.at[hh], acc_sc.at[hh])

    def scores(ki, slot):
        start = pl.multiple_of(ki * tk, tk)
        ka = jnp.concatenate([k_ref[0, pl.ds(start, tk), :], cf_ref[0, pl.ds(start, tk), :]], axis=1)
        for hh in range(2):
            s_sc[slot, hh] = _dot_nt(ka, qa_sc[hh])

    def probs(ki, slot, diag, parts=(0, 1)):
        for hh in parts:
            s = s_sc[slot, hh]
            if diag:
                s = jnp.where(_iota(s.shape, 0) <= _iota(s.shape, 1), s, NEG)
            _softmax_probs_t(s, m_sc.at[hh], alpha_sc.at[slot, hh], p_sc.at[slot, hh])

    def values(ki, slot):
        vt = vt_ref[0, 0, ki]
        for hh in range(2):
            vta = _with_ones_rows(vt[HEAD_DIM * hh:HEAD_DIM * (hh + 1), :])
            _softmax_accumulate_t(vta, alpha_sc.at[slot, hh], p_sc.at[slot, hh], acc_sc.at[hh])

    _pipelined_tiles(qi, scores, probs, values)
    o_t = jnp.concatenate([_softmax_result(acc_sc.at[hh]) for hh in range(2)], axis=0)
    o_ref[0] = o_t.T.astype(BF16)


def _fox_attn(q, k, cf, vt, tq=FOX_TQ, tk=FOX_TK):
    B, T, hd = q.shape
    assert tq == tk
    nk = T // tk
    seq = pl.BlockSpec((1, T, LANES), lambda b, p, qi: (b, 0, p))
    return pl.pallas_call(
        functools.partial(_fox_attn_kernel, tq=tq, tk=tk),
        grid=(B, N_PAIRS, T // tq),
        in_specs=[pl.BlockSpec((1, tq, LANES), lambda b, p, qi: (b, qi, p)), seq, seq,
                  pl.BlockSpec((1, 1, nk, LANES, tk), lambda b, p, qi: (b, p, 0, 0, 0))],
        out_specs=pl.BlockSpec((1, tq, LANES), lambda b, p, qi: (b, qi, p)),
        out_shape=jax.ShapeDtypeStruct((B, T, hd), BF16),
        scratch_shapes=[pltpu.VMEM((2, tq, 2 * LANES), BF16), pltpu.VMEM((2, 2, tk, tq), F32),
                        pltpu.VMEM((2, 2, tk, tq), BF16), pltpu.VMEM((2, 2, 1, tq), F32), pltpu.VMEM((2, 1, tq), F32),
                        pltpu.VMEM((2, ACC_ROWS, tq), F32)],
        compiler_params=_cparams(("parallel", "parallel", "arbitrary")),
        name="fox_attn",
    )(q, k, cf, vt)


def _outproj_kernel(h_ref, o_ref, w_ref, out_ref):
    out_ref[...] = h_ref[...] + _dot(o_ref[...], w_ref[...])


def _outproj(h, o, w, tm=ROW_TILE):
    B, T, D = h.shape
    n = B * T
    row = lambda c: pl.BlockSpec((tm, c), lambda i: (i, 0))
    out = pl.pallas_call(
        _outproj_kernel,
        grid=(n // tm,),
        in_specs=[row(D), row(o.shape[-1]), pl.BlockSpec(w.shape, lambda i: (0, 0))],
        out_specs=row(D),
        out_shape=jax.ShapeDtypeStruct((n, D), F32),
        compiler_params=_cparams(("parallel",)),
        name="outproj",
    )(h.reshape(n, D), o.reshape(n, -1), w)
    return out.reshape(B, T, D)


def _nsa_outproj_kernel(h_ref, oc_ref, os_ref, ow_ref, g_ref, w_ref, out_ref, o_sc):
    g = g_ref[...]
    lo = _iota((g.shape[0], LANES), 1) < HEAD_DIM
    for j in range(N_PAIRS):
        acc = None
        for br, ref in enumerate((oc_ref, os_ref, ow_ref)):
            c0 = br * N_HEADS + 2 * j
            gg = jnp.where(lo, g[:, c0:c0 + 1], g[:, c0 + 1:c0 + 2])
            term = gg * ref[:, LANES * j:LANES * (j + 1)].astype(F32)
            acc = term if acc is None else acc + term
        o_sc[:, LANES * j:LANES * (j + 1)] = acc.astype(BF16)
    out_ref[...] = h_ref[...] + _dot(o_sc[...], w_ref[...])


def _nsa_outproj(h, oc, osel, ow, gates, w, tm=ROW_TILE):
    B, T, D = h.shape
    n = B * T
    hd = oc.shape[-1]
    row = lambda c: pl.BlockSpec((tm, c), lambda i: (i, 0))
    out = pl.pallas_call(
        _nsa_outproj_kernel,
        grid=(n // tm,),
        in_specs=[row(D), row(hd), row(hd), row(hd), row(LANES), pl.BlockSpec(w.shape, lambda i: (0, 0))],
        out_specs=row(D),
        out_shape=jax.ShapeDtypeStruct((n, D), F32),
        scratch_shapes=[pltpu.VMEM((tm, hd), BF16)],
        compiler_params=_cparams(("parallel",)),
        name="nsa_outproj",
    )(h.reshape(n, D), oc.reshape(n, hd), osel.reshape(n, hd), ow.reshape(n, hd), gates.reshape(n, LANES), w)
    return out.reshape(B, T, D)


def _ffn_kernel(x_ref, g_ref, wg_ref, wv_ref, cwg_ref, cwv_ref, cbg_ref, cbv_ref, wd_ref,
                out_ref, xn_sc, acc_sc, carry_sc, *, tt, tf, nf):
    ti = pl.program_id(1)
    f = pl.program_id(2)

    @pl.when(f == 0)
    def _():
        xn_sc[...] = _row_rms(x_ref[0], g_ref[...]).astype(BF16)
        acc_sc[...] = jnp.zeros_like(acc_sc)

    @pl.when(ti == 0)
    def _():
        carry_sc[f] = jnp.zeros(carry_sc.shape[1:], F32)

    xn = xn_sc[...]
    prev = carry_sc[f]
    row = _iota((tt, tf), 0)

    def conv(u, cw_ref, cb_ref, prev8):
        um1 = jnp.where(row == 0, prev8[7:8, :], pltpu.roll(u, 1, 0))
        um2 = jnp.where(row == 0, prev8[6:7, :], jnp.where(row == 1, prev8[7:8, :], pltpu.roll(u, 2, 0)))
        cw = cw_ref[...]
        return cb_ref[...] + cw[0:1, :] * um2 + cw[1:2, :] * um1 + cw[2:3, :] * u

    ug = _dot(xn, wg_ref[...])
    uv = _dot(xn, wv_ref[...])
    cg = conv(ug, cwg_ref, cbg_ref, prev[:, :tf])
    cv = conv(uv, cwv_ref, cbv_ref, prev[:, tf:])
    carry_sc[f] = jnp.concatenate([ug[tt - 8:, :], uv[tt - 8:, :]], axis=1)
    a = cg * jax.nn.sigmoid(cg) * cv
    acc_sc[...] += _dot(a.astype(BF16), wd_ref[...])

    @pl.when(f == nf - 1)
    def _():
        out_ref[0] = x_ref[0] + acc_sc[...]


def _conv_ffn(h, g, w_up, conv_w, conv_b, w_down, tt=ROW_TILE, tf=256):
    B, T, D = h.shape
    d_ff = w_down.shape[0]
    nf = d_ff // tf
    act = pl.BlockSpec((1, tt, D), lambda b, t, f: (b, t, 0))
    return pl.pallas_call(
        functools.partial(_ffn_kernel, tt=tt, tf=tf, nf=nf),
        grid=(B, T // tt, nf),
        in_specs=[act, pl.BlockSpec((1, D), lambda b, t, f: (0, 0)),
                  pl.BlockSpec((D, tf), lambda b, t, f: (0, f)),
                  pl.BlockSpec((D, tf), lambda b, t, f: (0, f + nf)),
                  pl.BlockSpec((CONV_W, tf), lambda b, t, f: (0, f)),
                  pl.BlockSpec((CONV_W, tf), lambda b, t, f: (0, f + nf)),
                  pl.BlockSpec((1, tf), lambda b, t, f: (0, f)),
                  pl.BlockSpec((1, tf), lambda b, t, f: (0, f + nf)),
                  pl.BlockSpec((tf, D), lambda b, t, f: (f, 0))],
        out_specs=act,
        out_shape=jax.ShapeDtypeStruct((B, T, D), F32),
        scratch_shapes=[pltpu.VMEM((tt, D), BF16), pltpu.VMEM((tt, D), F32), pltpu.VMEM((nf, 8, 2 * tf), F32)],
        compiler_params=_cparams(("arbitrary", "arbitrary", "arbitrary")),
        name="conv_ffn",
    )(h, g, w_up, w_up, conv_w, conv_w, conv_b, conv_b, w_down)


def _kvproj_kernel(h_ref, g_ref, w_ref, wvt_ref, ksg_ref, kwg_ref, c_ref, s_ref,
                   craw_ref, ks_ref, kw_ref, vst_ref, vwt_ref, *, tm):
    xn = _row_rms(h_ref[0], g_ref[...]).astype(BF16)
    cos = c_ref[0]
    sin = s_ref[0]
    craw_ref[0] = _dot(xn, w_ref[:, 0:256])
    for idx, (ref, gain) in enumerate(((ks_ref, ksg_ref), (kw_ref, kwg_ref))):
        y = _dot(xn, w_ref[:, 256 * (idx + 1):256 * (idx + 2)])
        for grp in range(N_KV_GROUPS):
            blk = _pair_rope(_pair_rms(y[:, LANES * grp:LANES * (grp + 1)], gain[...]), cos, sin)
            ref[0, grp] = blk.astype(BF16)
    yt = _dot_nt(wvt_ref[...], xn).astype(BF16)
    for grp in range(N_KV_GROUPS):
        vst_ref[0, grp, 0] = yt[HEAD_DIM * grp:HEAD_DIM * (grp + 1), :]
        r0 = HEAD_DIM * (N_KV_GROUPS + grp)
        for c in range(tm // WIN_TQ):
            vwt_ref[0, grp, c] = yt[r0:r0 + HEAD_DIM, WIN_TQ * c:WIN_TQ * (c + 1)]


def _kvproj(h, g, w, wvt, ksg, kwg, cos, sin, tm=ROW_TILE):
    B, T, D = h.shape
    G = N_KV_GROUPS
    vec = lambda n: pl.BlockSpec((1, n), lambda b, t: (0, 0))
    full = lambda a: pl.BlockSpec(a.shape, lambda b, t: (0,) * a.ndim)
    tab = pl.BlockSpec((1, tm, LANES), lambda b, t: (b, t, 0))
    dup = pl.BlockSpec((1, G, tm, LANES), lambda b, t: (b, 0, t, 0))
    dup_shape = jax.ShapeDtypeStruct((B, G, T, LANES), BF16)
    nw = tm // WIN_TQ
    return pl.pallas_call(
        functools.partial(_kvproj_kernel, tm=tm),
        grid=(B, T // tm),
        in_specs=[pl.BlockSpec((1, tm, D), lambda b, t: (b, t, 0)), vec(D), full(w), full(wvt),
                  vec(LANES), vec(LANES), tab, tab],
        out_specs=[pl.BlockSpec((1, tm, 256), lambda b, t: (b, t, 0)), dup, dup,
                   pl.BlockSpec((1, G, 1, HEAD_DIM, tm), lambda b, t: (b, 0, t, 0, 0)),
                   pl.BlockSpec((1, G, nw, HEAD_DIM, WIN_TQ), lambda b, t: (b, 0, t, 0, 0))],
        out_shape=[jax.ShapeDtypeStruct((B, T, 256), F32), dup_shape, dup_shape,
                   jax.ShapeDtypeStruct((B, G, T // tm, HEAD_DIM, tm), BF16),
                   jax.ShapeDtypeStruct((B, G, T // WIN_TQ, HEAD_DIM, WIN_TQ), BF16)],
        compiler_params=_cparams(("parallel", "parallel")),
        name="kvproj",
    )(h, g, w, wvt, ksg, kwg, cos, sin)


def _compress_kernel(r_ref, w1_ref, pe_ref, w2_ref, w2t_ref, gain_ref, c_ref, s_ref, kc_ref, vct_ref, *, n_cmp):
    r = r_ref[0, 0, 0].astype(BF16)
    half = r.shape[1]
    a = _dot(r, w1_ref[0, :half, :])
    b = _dot(r, w1_ref[0, half:, :])
    peb = _dot(pe_ref[0], w1_ref[0])[0:1, :]
    rows = r.shape[0]
    hid = a + pltpu.roll(b, rows - 1, 0) + peb
    act = jax.nn.gelu(hid).astype(BF16)
    y = _dot(act, w2_ref[0])
    yk = _pair_rope(_pair_rms(y, gain_ref[...]), c_ref[0], s_ref[0])
    kc_ref[0, 0, 0] = jnp.where(_iota(y.shape, 0) < n_cmp, yk, 0.0).astype(BF16)
    yt = _dot_nt(w2t_ref[0], act)
    vct_ref[0, 0, 0] = jnp.where(_iota(yt.shape, 1) < n_cmp, yt, 0.0).astype(BF16)


def _compress(r, w1, pe, w2, w2t, gain, cos, sin, n_cmp):
    B, _, G, rows, width = r.shape
    per_kv = lambda a: pl.BlockSpec((1,) + a.shape[1:], lambda b, kv, g: (kv,) + (0,) * (a.ndim - 1))
    return pl.pallas_call(
        functools.partial(_compress_kernel, n_cmp=n_cmp),
        grid=(B, 2, G),
        in_specs=[pl.BlockSpec((1, 1, 1, rows, width), lambda b, kv, g: (b, kv, g, 0, 0)),
                  per_kv(w1), per_kv(pe), per_kv(w2), per_kv(w2t),
                  pl.BlockSpec((1, LANES), lambda b, kv, g: (0, 0)),
                  pl.BlockSpec((1, rows, LANES), lambda b, kv, g: (b, 0, 0)),
                  pl.BlockSpec((1, rows, LANES), lambda b, kv, g: (b, 0, 0))],
        out_specs=[pl.BlockSpec((1, 1, 1, rows, LANES), lambda b, kv, g: (b, kv, g, 0, 0)),
                   pl.BlockSpec((1, 1, 1, HEAD_DIM, rows), lambda b, kv, g: (b, kv, g, 0, 0))],
        out_shape=[jax.ShapeDtypeStruct((B, 2, G, rows, LANES), BF16),
                   jax.ShapeDtypeStruct((B, 2, G, HEAD_DIM, rows), BF16)],
        compiler_params=_cparams(("parallel", "parallel", "parallel")),
        name="compress",
    )(r, w1, pe, w2, w2t, gain, cos, sin)


def _nsa_inproj_kernel(h_ref, g_ref, w_ref, bg_ref, qg_ref, c_ref, s_ref, q_ref, gate_ref, *, hd):
    xn = _row_rms(h_ref[0], g_ref[...]).astype(BF16)
    cos = c_ref[0]
    sin = s_ref[0]
    for j in range(hd // 256):
        y = _dot(xn, w_ref[:, 256 * j:256 * (j + 1)])
        for hh in range(2):
            blk = _pair_rope(_pair_rms(y[:, LANES * hh:LANES * (hh + 1)], qg_ref[...]), cos, sin) * Q_SCALE
            q_ref[0, :, 256 * j + LANES * hh:256 * j + LANES * (hh + 1)] = blk.astype(BF16)
    z = _dot(xn, w_ref[:, hd:hd + LANES]) + bg_ref[...]
    gate_ref[0] = jax.nn.sigmoid(z)


def _nsa_inproj(h, g, w, bg, qg, cos, sin, tm=ROW_TILE):
    B, T, D = h.shape
    hd = N_HEADS * HEAD_DIM
    vec = lambda n: pl.BlockSpec((1, n), lambda b, t: (0, 0))
    tab = pl.BlockSpec((1, tm, LANES), lambda b, t: (b, t, 0))
    return pl.pallas_call(
        functools.partial(_nsa_inproj_kernel, hd=hd),
        grid=(B, T // tm),
        in_specs=[pl.BlockSpec((1, tm, D), lambda b, t: (b, t, 0)), vec(D),
                  pl.BlockSpec(w.shape, lambda b, t: (0, 0)), vec(LANES), vec(LANES), tab, tab],
        out_specs=[pl.BlockSpec((1, tm, hd), lambda b, t: (b, t, 0)), tab],
        out_shape=[jax.ShapeDtypeStruct((B, T, hd), BF16), jax.ShapeDtypeStruct((B, T, LANES), F32)],
        compiler_params=_cparams(("parallel", "parallel")),
        name="nsa_inproj",
    )(h, g, w, bg, qg, cos, sin)


def _stack_heads(q, dst_ref, tq):
    lo = _iota((tq, LANES), 1) < HEAD_DIM
    zero = jnp.zeros((tq, LANES), q.dtype)
    for j in range(HEADS_PER_GROUP // 2):
        x = q[:, LANES * j:LANES * (j + 1)]
        dst_ref[(2 * j) * tq:(2 * j + 1) * tq, 0:LANES] = jnp.where(lo, x, zero)
        dst_ref[(2 * j + 1) * tq:(2 * j + 2) * tq, 0:LANES] = jnp.where(lo, zero, x)


def _unstack_heads_t(o_t, o_ref, tq):
    for j in range(HEADS_PER_GROUP // 2):
        pair = jnp.concatenate([o_t[:, (2 * j) * tq:(2 * j + 1) * tq], o_t[:, (2 * j + 1) * tq:(2 * j + 2) * tq]], axis=0)
        o_ref[0, :, LANES * j:LANES * (j + 1)] = pair.T.astype(o_ref.dtype)


def _nsa_cmp_kernel(q_ref, kc_ref, vct_ref, ovt_ref, oc_ref, sel_ref, qs_sc, *, tq, n_slc, top_k):
    qi = pl.program_id(2)
    _stack_heads(q_ref[0], qs_sc, tq)
    kc = kc_ref[0, 0, 0]
    n_rows = kc.shape[0]
    n_chunks = 2
    cw = HEADS_PER_GROUP * tq // n_chunks
    scores = [_dot_nt(kc, qs_sc[c * cw:(c + 1) * cw, :]) for c in range(n_chunks)]
    qpos = qi * tq + (_iota((n_rows, cw), 1) & (tq - 1))
    vis = _iota((n_rows, cw), 0) * CMP_STRIDE + (CMP_LEN - 1) <= qpos
    outs = []
    p_sum = None
    for s in scores:
        s = jnp.where(vis, s, NEG)
        e = jnp.where(vis, jnp.exp2(s - jnp.max(s, axis=0, keepdims=True)), 0.0)
        p = e * (1.0 / jnp.maximum(jnp.sum(e, axis=0, keepdims=True), 1.0))
        outs.append(_dot(vct_ref[0, 0, 0], p.astype(BF16)))
        for h in range(cw // tq):
            term = p[:, h * tq:(h + 1) * tq]
            p_sum = term if p_sum is None else p_sum + term
    _unstack_heads_t(jnp.concatenate(outs, axis=1), oc_ref, tq)
    p_hi = p_sum.astype(BF16)
    p_lo = (p_sum - p_hi.astype(F32)).astype(BF16)
    imp = _dot(ovt_ref[...], p_hi) + _dot(ovt_ref[...], p_lo)
    jb = _iota(imp.shape, 0)
    qp = qi * tq + _iota(imp.shape, 1)
    cur = qp // SLC_LEN
    forced = (jb == 0) | (jb == cur) | (jb == cur - 1)
    score = jnp.where(jb * SLC_LEN <= qp, jnp.where(forced, FORCED_SCORE, imp), NEG)
    score = jnp.where(jb < n_slc, score, -jnp.inf)
    bias = jnp.full(imp.shape, NEG, F32)
    for _ in range(top_k):
        mx = jnp.max(score, axis=0, keepdims=True)
        idx = jnp.min(jnp.where(score == mx, jb, LANES), axis=0, keepdims=True)
        pick = jb == idx
        bias = jnp.where(pick, 0.0, bias)
        score = jnp.where(pick, -jnp.inf, score)
    sel_ref[0, 0] = bias.T.astype(BF16)


def _nsa_cmp(q, kc_all, vct_all, overlap_t, n_slc, tq=CMP_TQ):
    B, T, hd = q.shape
    G = N_KV_GROUPS
    rows = kc_all.shape[3]
    gw = hd // G
    top_k = min(SLC_TOPK, n_slc)
    return pl.pallas_call(
        functools.partial(_nsa_cmp_kernel, tq=tq, n_slc=n_slc, top_k=top_k),
        grid=(B, G, T // tq),
        in_specs=[pl.BlockSpec((1, tq, gw), lambda b, g, qi: (b, qi, g)),
                  pl.BlockSpec((1, 1, 1, rows, LANES), lambda b, g, qi: (b, 0, g, 0, 0)),
                  pl.BlockSpec((1, 1, 1, HEAD_DIM, rows), lambda b, g, qi: (b, 1, g, 0, 0)),
                  pl.BlockSpec(overlap_t.shape, lambda b, g, qi: (0, 0))],
        out_specs=[pl.BlockSpec((1, tq, gw), lambda b, g, qi: (b, qi, g)),
                   pl.BlockSpec((1, 1, tq, LANES), lambda b, g, qi: (b, g, qi, 0))],
        out_shape=[jax.ShapeDtypeStruct((B, T, hd), BF16), jax.ShapeDtypeStruct((B, G, T, LANES), BF16)],
        scratch_shapes=[pltpu.VMEM((HEADS_PER_GROUP * tq, LANES), BF16)],
        compiler_params=_cparams(("parallel", "parallel", "parallel")),
        name="nsa_cmp",
    )(q, kc_all, vct_all, overlap_t)


def _nsa_slc_kernel(q_ref, sel_ref, k_ref, oh_ref, vt_ref, o_ref, qa_sc, s_sc, p_sc, alpha_sc, m_sc, acc_sc, *, tq, tk, n_chunks):
    qi = pl.program_id(2)
    last = (qi * tq) // tk
    _stack_heads(q_ref[0], qa_sc, tq)
    sel = sel_ref[0, 0]
    for h in range(HEADS_PER_GROUP):
        qa_sc[h * tq:(h + 1) * tq, LANES:2 * LANES] = sel
    _softmax_init(m_sc, acc_sc)
    cw = HEADS_PER_GROUP * tq // n_chunks

    def scores(ki, slot):
        start = pl.multiple_of(ki * tk, tk)
        ka = jnp.concatenate([k_ref[0, 0, pl.ds(start, tk), :], oh_ref[pl.ds(start, tk), :]], axis=1)
        for c in range(n_chunks):
            s_sc[slot, :, c * cw:(c + 1) * cw] = _dot_nt(ka, qa_sc[c * cw:(c + 1) * cw, :])

    def probs(ki, slot, diag, parts=(0, 1)):
        for c in [c for c in range(n_chunks) if c * 2 // n_chunks in parts]:
            cols = slice(c * cw, (c + 1) * cw)
            s = s_sc[slot, :, cols]
            if diag:
                kpos = ki * tk + _iota(s.shape, 0)
                qpos = qi * tq + (_iota(s.shape, 1) & (tq - 1))
                s = jnp.where(kpos <= qpos, s, NEG)
            _softmax_probs_t(s, m_sc.at[:, cols], alpha_sc.at[slot, :, cols], p_sc.at[slot, :, cols])

    def values(ki, slot):
        vta = _with_ones_rows(vt_ref[0, 0, ki])
        for c in range(n_chunks):
            cols = slice(c * cw, (c + 1) * cw)
            _softmax_accumulate_t(vta, alpha_sc.at[slot, :, cols], p_sc.at[slot, :, cols], acc_sc.at[:, cols])

    _pipelined_tiles(last, scores, probs, values)
    _unstack_heads_t(_softmax_result(acc_sc), o_ref, tq)


def _nsa_slc(q, sel, ks, vst, tq=SLC_TQ, tk=SLC_TK):
    B, T, hd = q.shape
    G = N_KV_GROUPS
    gw = hd // G
    cols = HEADS_PER_GROUP * tq
    assert tk % tq == 0
    onehot = (jnp.arange(T)[:, None] // SLC_LEN == jnp.arange(LANES)[None, :]).astype(BF16)
    return pl.pallas_call(
        functools.partial(_nsa_slc_kernel, tq=tq, tk=tk, n_chunks=4),
        grid=(B, G, T // tq),
        in_specs=[pl.BlockSpec((1, tq, gw), lambda b, g, qi: (b, qi, g)),
                  pl.BlockSpec((1, 1, tq, LANES), lambda b, g, qi: (b, g, qi, 0)),
                  pl.BlockSpec((1, 1, T, LANES), lambda b, g, qi: (b, g, 0, 0)),
                  pl.BlockSpec((T, LANES), lambda b, g, qi: (0, 0)),
                  pl.BlockSpec((1, 1, T // tk, HEAD_DIM, tk), lambda b, g, qi: (b, g, 0, 0, 0))],
        out_specs=pl.BlockSpec((1, tq, gw), lambda b, g, qi: (b, qi, g)),
        out_shape=jax.ShapeDtypeStruct((B, T, hd), BF16),
        scratch_shapes=[pltpu.VMEM((cols, 2 * LANES), BF16), pltpu.VMEM((2, tk, cols), F32),
                        pltpu.VMEM((2, tk, cols), BF16), pltpu.VMEM((2, 1, cols), F32), pltpu.VMEM((1, cols), F32),
                        pltpu.VMEM((ACC_ROWS, cols), F32)],
        compiler_params=_cparams(("parallel", "parallel", "arbitrary")),
        name="nsa_slc",
    )(q, sel, ks, onehot, vst)


def _nsa_win_kernel(q_ref, k_ref, vt_ref, o_ref, qs_sc, s_sc, p_sc, alpha_sc, m_sc, acc_sc, *, tq, n_back, n_chunks):
    qi = pl.program_id(2)
    _stack_heads(q_ref[0], qs_sc, tq)
    _softmax_init(m_sc, acc_sc)
    cw = HEADS_PER_GROUP * tq // n_chunks

    def scores(ti, slot):
        start = pl.multiple_of(ti * tq, tq)
        k = k_ref[0, 0, pl.ds(start, tq), :]
        for c in range(n_chunks):
            s_sc[slot, :, c * cw:(c + 1) * cw] = _dot_nt(k, qs_sc[c * cw:(c + 1) * cw, :])

    def probs(slot, kind):
        for c in range(n_chunks):
            cols = slice(c * cw, (c + 1) * cw)
            s = s_sc[slot, :, cols]
            r = _iota(s.shape, 0)
            q_in_tile = _iota(s.shape, 1) & (tq - 1)
            if kind == "oldest":
                s = jnp.where(r > q_in_tile, s, NEG)
            elif kind == "diag":
                s = jnp.where(r <= q_in_tile, s, NEG)
            _softmax_probs_t(s, m_sc.at[:, cols], alpha_sc.at[slot, :, cols], p_sc.at[slot, :, cols])

    def values(ti, slot):
        vta = _with_ones_rows(vt_ref[0, 0, ti])
        for c in range(n_chunks):
            cols = slice(c * cw, (c + 1) * cw)
            _softmax_accumulate_t(vta, alpha_sc.at[slot, :, cols], p_sc.at[slot, :, cols], acc_sc.at[:, cols])

    def kind_of(back):
        return "oldest" if back == n_back else ("diag" if back == 0 else "full")

    for first in range(n_back + 1):
        cond = (qi >= n_back) if first == n_back else (qi == first)

        @pl.when(cond)
        def _(first=first):
            backs = list(range(first, -1, -1))
            for step in range(len(backs) + 2):
                if step < len(backs):
                    scores(qi - backs[step], step % 2)
                if 1 <= step <= len(backs):
                    probs((step - 1) % 2, kind_of(backs[step - 1]))
                if step >= 2:
                    values(qi - backs[step - 2], step % 2)

    _unstack_heads_t(_softmax_result(acc_sc), o_ref, tq)


def _nsa_win(q, kw, vwt, tq=WIN_TQ):
    B, T, hd = q.shape
    G = N_KV_GROUPS
    gw = hd // G
    cols = HEADS_PER_GROUP * tq
    return pl.pallas_call(
        functools.partial(_nsa_win_kernel, tq=tq, n_back=WIN // tq, n_chunks=4),
        grid=(B, G, T // tq),
        in_specs=[pl.BlockSpec((1, tq, gw), lambda b, g, qi: (b, qi, g)),
                  pl.BlockSpec((1, 1, T, LANES), lambda b, g, qi: (b, g, 0, 0)),
                  pl.BlockSpec((1, 1, T // tq, HEAD_DIM, tq), lambda b, g, qi: (b, g, 0, 0, 0))],
        out_specs=pl.BlockSpec((1, tq, gw), lambda b, g, qi: (b, qi, g)),
        out_shape=jax.ShapeDtypeStruct((B, T, hd), BF16),
        scratch_shapes=[pltpu.VMEM((cols, LANES), BF16), pltpu.VMEM((2, tq, cols), F32),
                        pltpu.VMEM((2, tq, cols), BF16), pltpu.VMEM((2, 1, cols), F32), pltpu.VMEM((1, cols), F32),
                        pltpu.VMEM((ACC_ROWS, cols), F32)],
        compiler_params=_cparams(("parallel", "parallel", "arbitrary")),
        name="nsa_win",
    )(q, kw, vwt)


def _pad_cols(a, n):
    return jnp.pad(a, ((0, 0), (0, n - a.shape[1])))


def _lane_vec(v):
    return jnp.tile(v.astype(F32), 2).reshape(1, LANES)


def _bias_feature_selector():
    sel = np.zeros((N_BIAS_PARTS * LANES, N_HEADS * HEAD_DIM), np.float32)
    for part in range(N_BIAS_PARTS):
        for h in range(N_HEADS):
            sel[part * LANES + h, (h // 2) * LANES + (h % 2) * N_BIAS_PARTS + part] = 1.0
    return jnp.asarray(sel, BF16)


def kernel(x, positions, a_norm, a_w_in, a_b_f, a_q_gain, a_k_gain, a_w_out, kv_norm, kv_w, kc_pe, vc_pe, kc_w1, kc_w2, vc_w1, vc_w2, kc_gain, ks_gain, kw_gain, b_norm, b_w_in, b_b_gate, b_q_gain, b_w_out, f_norm, f_w_up, f_conv_w, f_conv_b, f_w_down):
    B, T, D = x.shape
    hd = N_HEADS * HEAD_DIM
    G = N_KV_GROUPS
    n_a = a_norm.shape[0]
    n_b = b_norm.shape[0]
    depth = n_a + n_b
    n_slc = T // SLC_LEN
    n_cmp = (T - CMP_LEN) // CMP_STRIDE + 1
    assert T % 1024 == 0 and n_slc <= LANES and hd == 1024 and D == 1024

    half = ROT_DIM // 2
    inv = ROPE_THETA ** (-jnp.arange(half, dtype=F32) * (2.0 / ROT_DIM))
    head_inv = jnp.concatenate([inv, inv, jnp.zeros((HEAD_DIM - ROT_DIM,), F32)])
    head_sign = jnp.concatenate([-jnp.ones((half,), F32), jnp.ones((half,), F32), jnp.zeros((HEAD_DIM - ROT_DIM,), F32)])
    inv_lane = jnp.tile(head_inv, 2).reshape(1, LANES)
    sign_lane = jnp.tile(head_sign, 2).reshape(1, LANES)
    cos_t, sin_t = _rope_tables(positions, inv_lane, sign_lane)
    end_pos = positions[:, CMP_LEN - 1::CMP_STRIDE]
    end_pos = jnp.pad(end_pos, ((0, 0), (0, T // CMP_STRIDE - n_cmp)))
    cos_c, sin_c = _rope_tables(end_pos, inv_lane, sign_lane)

    cs = jnp.arange(T // CMP_STRIDE) * CMP_STRIDE
    ss = jnp.arange(LANES) * SLC_LEN
    overlap_t = (jnp.maximum(jnp.minimum(cs[None, :] + CMP_LEN, ss[:, None] + SLC_LEN)
                             - jnp.maximum(cs[None, :], ss[:, None]), 0).astype(F32) / CMP_LEN).astype(BF16)

    h = x
    kv = None
    for layer in range(depth):
        if layer < n_a:
            i = layer
            w = jnp.concatenate([a_w_in[i][:, :2 * hd], _pad_cols(a_w_in[i][:, 3 * hd:], LANES)], axis=1).astype(BF16)
            wvt = a_w_in[i][:, 2 * hd:3 * hd].T.astype(BF16)
            bf = _pad_cols(a_b_f[i].reshape(1, -1), LANES)
            q, k, vt, cf = _fox_inproj(h, a_norm[i].reshape(1, D), w, wvt, bf, _lane_vec(a_q_gain[i]),
                                       _lane_vec(a_k_gain[i]), _bias_feature_selector())
            o = _fox_attn(q, k, cf, vt)
            h = _outproj(h, o, a_w_out[i].astype(BF16))
        else:
            i = layer - n_a
            kc_all, vct_all, ks, kw, vst, vwt = kv
            w = jnp.concatenate([b_w_in[i][:, :hd], _pad_cols(b_w_in[i][:, hd:], LANES)], axis=1).astype(BF16)
            bg = _pad_cols(b_b_gate[i].reshape(1, -1), LANES)
            q, gates = _nsa_inproj(h, b_norm[i].reshape(1, D), w, bg, _lane_vec(b_q_gain[i]), cos_t, sin_t)
            o_c, sel = _nsa_cmp(q, kc_all, vct_all, overlap_t, n_slc)
            o_s = _nsa_slc(q, sel, ks, vst)
            o_w = _nsa_win(q, kw, vwt)
            h = _nsa_outproj(h, o_c, o_s, o_w, gates, b_w_out[i].astype(BF16))
        h = _conv_ffn(h, f_norm[layer].reshape(1, D), f_w_up[layer].astype(BF16), f_conv_w[layer],
                      f_conv_b[layer].reshape(1, -1), f_w_down[layer].astype(BF16))
        if layer == n_a - 1:
            w6 = kv_w.reshape(D, 6, G, HEAD_DIM)
            raw_cols = w6[:, 0:2].reshape(D, 2 * G * HEAD_DIM)
            wk = w6[:, (2, 4)]
            dup_cols = jnp.concatenate([wk, wk], axis=-1).reshape(D, 2 * G * LANES)
            wkv = jnp.concatenate([raw_cols, dup_cols], axis=1).astype(BF16)
            wvt = w6[:, (3, 5)].reshape(D, 2 * G * HEAD_DIM).T.astype(BF16)
            craw, ks, kw, vst, vwt = _kvproj(h, kv_norm.reshape(1, D), wkv, wvt, _lane_vec(ks_gain),
                                             _lane_vec(kw_gain), cos_t, sin_t)
            r = craw.reshape(B, T, 2, G, HEAD_DIM).transpose(0, 2, 3, 1, 4).reshape(B, 2, G, T // CMP_STRIDE, CMP_STRIDE * HEAD_DIM)
            w1 = jnp.stack([kc_w1, vc_w1]).astype(BF16)
            pe = jnp.stack([kc_pe.reshape(1, -1), vc_pe.reshape(1, -1)])
            pe = jnp.pad(pe, ((0, 0), (0, 7), (0, 0))).astype(BF16)
            w2 = jnp.stack([jnp.concatenate([kc_w2, kc_w2], axis=1), jnp.concatenate([vc_w2, vc_w2], axis=1)]).astype(BF16)
            w2t = jnp.stack([kc_w2.T, vc_w2.T]).astype(BF16)
            kc_all, vct_all = _compress(r, w1, pe, w2, w2t, _lane_vec(kc_gain), cos_c, sin_c, n_cmp)
            kv = (kc_all, vct_all, ks, kw, vst, vwt)
    return h
```

```python
import functools

import numpy as np
import jax
import jax.numpy as jnp
from jax import lax
from jax.experimental import pallas as pl
from jax.experimental.pallas import tpu as pltpu

F32 = jnp.float32
BF16 = jnp.bfloat16

LANES = 128
SUBLANES = 8
HEAD_DIM = 64
N_HEADS = 16
N_PAIRS = N_HEADS // 2
N_KV_GROUPS = 2
HEADS_PER_GROUP = N_HEADS // N_KV_GROUPS
ROT_DIM = HEAD_DIM // 4
ROPE_THETA = 500000.0
CMP_LEN = 32
CMP_STRIDE = 16
SLC_LEN = 64
SLC_TOPK = 16
WIN = 512
CONV_W = 3
RMS_EPS = 1e-6
NEG = -1e30
FORCED_SCORE = 1e6
LOG2E = 1.4426950408889634
Q_SCALE = HEAD_DIM ** -0.5 * LOG2E
N_BIAS_PARTS = 3
ONES_ROWS = 16
ACC_ROWS = HEAD_DIM + ONES_ROWS

ROW_TILE = 512
FOX_TQ = 512
FOX_TK = ROW_TILE
SLC_TQ = 128
SLC_TK = ROW_TILE
WIN_TQ = 256
CMP_TQ = 256

VMEM_LIMIT = 48 * 1024 * 1024


def _cparams(sem, flags=None):
    return pltpu.CompilerParams(dimension_semantics=sem, vmem_limit_bytes=VMEM_LIMIT, flags=flags)


def _iota(shape, axis):
    return lax.broadcasted_iota(jnp.int32, shape, axis)


def _row_rms(x, g):
    ms = jnp.mean(x * x, axis=-1, keepdims=True)
    return x * lax.rsqrt(ms + RMS_EPS) * g


def _pair_rms(y, gain):
    lo = _iota(y.shape, 1) < HEAD_DIM
    y2 = y * y
    s_lo = jnp.sum(jnp.where(lo, y2, 0.0), axis=-1, keepdims=True)
    s_hi = jnp.sum(jnp.where(lo, 0.0, y2), axis=-1, keepdims=True)
    ms = jnp.where(lo, s_lo, s_hi) * (1.0 / HEAD_DIM)
    return y * lax.rsqrt(ms + RMS_EPS) * gain


def _pair_rope(y, cos, sin):
    lane = _iota(y.shape, 1) & (HEAD_DIM - 1)
    partner = jnp.where(lane < ROT_DIM // 2,
                        pltpu.roll(y, LANES - ROT_DIM // 2, 1),
                        pltpu.roll(y, ROT_DIM // 2, 1))
    return y * cos + partner * sin


def _dot(a, b):
    return jnp.dot(a, b, preferred_element_type=F32)


def _dot_nt(a, b):
    return lax.dot_general(a, b, (((1,), (1,)), ((), ())), preferred_element_type=F32)


def _softmax_update_t(s, vt, m_ref, acc_ref):
    tk, n = s.shape
    m_old = m_ref[...]
    m_tile = jnp.max(jnp.max(s.reshape(tk // SUBLANES, SUBLANES, n), axis=0), axis=0, keepdims=True)
    m_new = jnp.maximum(m_old, m_tile)
    alpha = jnp.exp2(m_old - m_new)
    p = jnp.exp2(s - m_new).astype(BF16)
    vta = jnp.concatenate([vt, jnp.ones((ONES_ROWS, tk), vt.dtype)], axis=0)
    acc_ref[...] = alpha * acc_ref[...] + _dot(vta, p)
    m_ref[...] = m_new


def _pipelined_key_loop(n_full, produce, consume, unroll=4):
    assert unroll % 2 == 0

    def run(base, count):
        for i in range(count):
            produce(base + i + 1, (i + 1) % 2)
            consume(base + i, i % 2, False)

    produce(0, 0)

    def body(j, carry):
        run(unroll * j, unroll)
        return carry

    lax.fori_loop(0, n_full // unroll, body, 0)
    base = (n_full // unroll) * unroll
    rem = n_full - base
    step = unroll // 2
    while step >= 2:
        pl.when((rem & step) != 0)(functools.partial(run, base, step))
        base = base + (rem & step)
        step //= 2

    @pl.when((rem & 1) != 0)
    def _():
        run(base, 1)
        consume(base + 1, 1, True)

    @pl.when((rem & 1) == 0)
    def _():
        consume(base, 0, True)


def _softmax_init(m_ref, acc_ref):
    m_ref[...] = jnp.full_like(m_ref, NEG)
    acc_ref[...] = jnp.zeros_like(acc_ref)


def _softmax_result(acc_ref):
    acc = acc_ref[...]
    return acc[:HEAD_DIM, :] * (1.0 / acc[HEAD_DIM:HEAD_DIM + 1, :])


def _rope_table_kernel(pos_ref, inv_ref, sign_ref, c_ref, s_ref):
    ang = pos_ref[0].astype(F32) * inv_ref[...]
    c_ref[0] = jnp.cos(ang)
    s_ref[0] = jnp.sin(ang) * sign_ref[...]


def _rope_tables(pos, inv_lane, sign_lane):
    B, T = pos.shape
    tm = min(T, ROW_TILE)
    spec = pl.BlockSpec((1, tm, LANES), lambda b, t: (b, t, 0))
    vec = pl.BlockSpec((1, LANES), lambda b, t: (0, 0))
    return pl.pallas_call(
        _rope_table_kernel,
        grid=(B, T // tm),
        in_specs=[pl.BlockSpec((1, tm, 1), lambda b, t: (b, t, 0)), vec, vec],
        out_specs=[spec, spec],
        out_shape=[jax.ShapeDtypeStruct((B, T, LANES), F32)] * 2,
        compiler_params=_cparams(("parallel", "parallel")),
        name="rope_tables",
    )(pos.reshape(B, T, 1), inv_lane, sign_lane)


def _fox_inproj_kernel(x_ref, g_ref, w_ref, wvt_ref, bf_ref, qg_ref, kg_ref, sel_ref,
                       q_ref, k_ref, vt_ref, cf_ref, carry_sc, *, tm, hd):
    ti = pl.program_id(1)
    xn = _row_rms(x_ref[0], g_ref[...]).astype(BF16)
    for j in range(hd // 256):
        for part, (ref, gain, mul) in enumerate(((q_ref, qg_ref, Q_SCALE), (k_ref, kg_ref, 1.0))):
            c0 = part * hd + 256 * j
            y = _dot(xn, w_ref[:, c0:c0 + 256])
            for hh in range(2):
                blk = _pair_rms(y[:, LANES * hh:LANES * (hh + 1)], gain[...]) * mul
                ref[0, :, 256 * j + LANES * hh:256 * j + LANES * (hh + 1)] = blk.astype(BF16)
        yt = _dot_nt(wvt_ref[256 * j:256 * (j + 1), :], xn)
        for hh in range(2):
            vt_ref[0, 2 * j + hh, 0] = yt[LANES * hh:LANES * (hh + 1), :].astype(BF16)
    z = _dot(xn, w_ref[:, 2 * hd:2 * hd + LANES]) + bf_ref[...]
    lf = jnp.minimum(z, 0.0) - jnp.log1p(jnp.exp(-jnp.abs(z)))
    row = _iota(lf.shape, 0)
    sh = 1
    while sh < tm:
        lf = lf + jnp.where(row >= sh, pltpu.roll(lf, sh, 0), 0.0)
        sh *= 2

    @pl.when(ti == 0)
    def _():
        carry_sc[...] = jnp.zeros_like(carry_sc)

    c = lf + carry_sc[0:1, :]
    carry_sc[...] = jnp.broadcast_to(c[tm - 1:tm, :], carry_sc.shape)
    rest = c * (-LOG2E)
    pieces = []
    for _ in range(N_BIAS_PARTS):
        piece = rest.astype(BF16)
        pieces.append(piece)
        rest = rest - piece.astype(F32)
    cf_ref[0] = _dot(jnp.concatenate(pieces, axis=1), sel_ref[...]).astype(BF16)


def _fox_inproj(x, g, w, wvt, bf, qg, kg, sel, tm=ROW_TILE):
    B, T, D = x.shape
    hd = N_HEADS * HEAD_DIM
    act = pl.BlockSpec((1, tm, hd), lambda b, t: (b, t, 0))
    vec = lambda n: pl.BlockSpec((1, n), lambda b, t: (0, 0))
    full = lambda a: pl.BlockSpec(a.shape, lambda b, t: (0,) * a.ndim)
    return pl.pallas_call(
        functools.partial(_fox_inproj_kernel, tm=tm, hd=hd),
        grid=(B, T // tm),
        in_specs=[pl.BlockSpec((1, tm, D), lambda b, t: (b, t, 0)), vec(D), full(w), full(wvt),
                  vec(LANES), vec(LANES), vec(LANES), full(sel)],
        out_specs=[act, act, pl.BlockSpec((1, N_PAIRS, 1, LANES, tm), lambda b, t: (b, 0, t, 0, 0)), act],
        out_shape=[jax.ShapeDtypeStruct((B, T, hd), BF16), jax.ShapeDtypeStruct((B, T, hd), BF16),
                   jax.ShapeDtypeStruct((B, N_PAIRS, T // tm, LANES, tm), BF16),
                   jax.ShapeDtypeStruct((B, T, hd), BF16)],
        scratch_shapes=[pltpu.VMEM((SUBLANES, LANES), F32)],
        compiler_params=_cparams(("arbitrary", "arbitrary")),
        name="fox_inproj",
    )(x, g, w, wvt, bf, qg, kg, sel)


def _fox_attn_kernel(q_ref, k_ref, cf_ref, vt_ref, o_ref, qa_sc, s_sc, m_sc, acc_sc, *, tq, tk):
    qi = pl.program_id(2)
    q = q_ref[0]
    lane = _iota(q.shape, 1)
    lo = lane < HEAD_DIM
    zero = jnp.zeros_like(q)
    for hh in range(2):
        qa_sc[hh, :, 0:LANES] = jnp.where(lo, q, zero) if hh == 0 else jnp.where(lo, zero, q)
        feat = jnp.where(lane < N_BIAS_PARTS * (hh + 1), 1.0, 0.0)
        qa_sc[hh, :, LANES:2 * LANES] = jnp.where(lane >= N_BIAS_PARTS * hh, feat, 0.0).astype(BF16)
        _softmax_init(m_sc.at[hh], acc_sc.at[hh])

    def produce(ki, slot):
        start = pl.multiple_of(ki * tk, tk)
        ka = jnp.concatenate([k_ref[0, pl.ds(start, tk), :], cf_ref[0, pl.ds(start, tk), :]], axis=1)
        for hh in range(2):
            s_sc[slot, hh] = _dot_nt(ka, qa_sc[hh])

    def consume(ki, slot, diag):
        vt = vt_ref[0, 0, ki]
        for hh in range(2):
            s = s_sc[slot, hh]
            if diag:
                s = jnp.where(_iota(s.shape, 0) <= _iota(s.shape, 1), s, NEG)
            _softmax_update_t(s, vt[HEAD_DIM * hh:HEAD_DIM * (hh + 1), :], m_sc.at[hh], acc_sc.at[hh])

    _pipelined_key_loop(qi, produce, consume)
    o_t = jnp.concatenate([_softmax_result(acc_sc.at[hh]) for hh in range(2)], axis=0)
    o_ref[0] = o_t.T.astype(BF16)


def _fox_attn(q, k, cf, vt, tq=FOX_TQ, tk=FOX_TK):
    B, T, hd = q.shape
    assert tq == tk
    nk = T // tk
    seq = pl.BlockSpec((1, T, LANES), lambda b, p, qi: (b, 0, p))
    return pl.pallas_call(
        functools.partial(_fox_attn_kernel, tq=tq, tk=tk),
        grid=(B, N_PAIRS, T // tq),
        in_specs=[pl.BlockSpec((1, tq, LANES), lambda b, p, qi: (b, qi, p)), seq, seq,
                  pl.BlockSpec((1, 1, nk, LANES, tk), lambda b, p, qi: (b, p, 0, 0, 0))],
        out_specs=pl.BlockSpec((1, tq, LANES), lambda b, p, qi: (b, qi, p)),
        out_shape=jax.ShapeDtypeStruct((B, T, hd), BF16),
        scratch_shapes=[pltpu.VMEM((2, tq, 2 * LANES), BF16), pltpu.VMEM((2, 2, tk, tq), F32), pltpu.VMEM((2, 1, tq), F32),
                        pltpu.VMEM((2, ACC_ROWS, tq), F32)],
        compiler_params=_cparams(("parallel", "parallel", "arbitrary")),
        name="fox_attn",
    )(q, k, cf, vt)


def _outproj_kernel(h_ref, o_ref, w_ref, out_ref):
    out_ref[...] = h_ref[...] + _dot(o_ref[...], w_ref[...])


def _outproj(h, o, w, tm=ROW_TILE):
    B, T, D = h.shape
    n = B * T
    row = lambda c: pl.BlockSpec((tm, c), lambda i: (i, 0))
    out = pl.pallas_call(
        _outproj_kernel,
        grid=(n // tm,),
        in_specs=[row(D), row(o.shape[-1]), pl.BlockSpec(w.shape, lambda i: (0, 0))],
        out_specs=row(D),
        out_shape=jax.ShapeDtypeStruct((n, D), F32),
        compiler_params=_cparams(("parallel",)),
        name="outproj",
    )(h.reshape(n, D), o.reshape(n, -1), w)
    return out.reshape(B, T, D)


def _nsa_outproj_kernel(h_ref, oc_ref, os_ref, ow_ref, w_ref, out_ref):
    o = oc_ref[...].astype(F32) + os_ref[...].astype(F32) + ow_ref[...].astype(F32)
    out_ref[...] = h_ref[...] + _dot(o.astype(BF16), w_ref[...])


def _nsa_outproj(h, oc, osel, ow, w, tm=ROW_TILE):
    B, T, D = h.shape
    n = B * T
    hd = oc.shape[-1]
    row = lambda c: pl.BlockSpec((tm, c), lambda i: (i, 0))
    out = pl.pallas_call(
        _nsa_outproj_kernel,
        grid=(n // tm,),
        in_specs=[row(D), row(hd), row(hd), row(hd), pl.BlockSpec(w.shape, lambda i: (0, 0))],
        out_specs=row(D),
        out_shape=jax.ShapeDtypeStruct((n, D), F32),
        compiler_params=_cparams(("parallel",)),
        name="nsa_outproj",
    )(h.reshape(n, D), oc.reshape(n, hd), osel.reshape(n, hd), ow.reshape(n, hd), w)
    return out.reshape(B, T, D)


def _ffn_kernel(x_ref, g_ref, wup_ref, cw_ref, cb_ref, wd_ref, out_ref, a_sc, carry_sc, *, tt, tf, d_ff):
    ti = pl.program_id(1)

    @pl.when(ti == 0)
    def _():
        carry_sc[...] = jnp.zeros_like(carry_sc)

    x = x_ref[0]
    xn = _row_rms(x, g_ref[...]).astype(BF16)
    row8 = _iota((SUBLANES, tf), 0)

    def conv(u, c0):
        prev8 = carry_sc[:, c0:c0 + tf]
        um1 = pltpu.roll(u, 1, 0)
        um2 = pltpu.roll(u, 2, 0)
        top1 = jnp.where(row8 == 0, prev8[7:8, :], um1[0:SUBLANES, :])
        top2 = jnp.where(row8 == 0, prev8[6:7, :], jnp.where(row8 == 1, prev8[7:8, :], um2[0:SUBLANES, :]))
        um1 = jnp.concatenate([top1, um1[SUBLANES:, :]], axis=0)
        um2 = jnp.concatenate([top2, um2[SUBLANES:, :]], axis=0)
        carry_sc[:, c0:c0 + tf] = u[tt - SUBLANES:, :]
        cw = cw_ref[:, c0:c0 + tf]
        return cb_ref[:, c0:c0 + tf] + cw[0:1, :] * um2 + cw[1:2, :] * um1 + cw[2:3, :] * u

    for f in range(d_ff // tf):
        g0 = f * tf
        cg = conv(_dot(xn, wup_ref[:, g0:g0 + tf]), g0)
        cv = conv(_dot(xn, wup_ref[:, d_ff + g0:d_ff + g0 + tf]), d_ff + g0)
        a_sc[:, g0:g0 + tf] = (cg * jax.nn.sigmoid(cg) * cv).astype(BF16)
    out_ref[0] = x + _dot(a_sc[...], wd_ref[...])


def _conv_ffn(h, g, w_up, conv_w, conv_b, w_down, tt=ROW_TILE, tf=256):
    B, T, D = h.shape
    d_ff = w_down.shape[0]
    act = pl.BlockSpec((1, tt, D), lambda b, t: (b, t, 0))
    resident = lambda a: pl.BlockSpec(a.shape, lambda b, t: (0,) * a.ndim, pipeline_mode=pl.Buffered(1))
    return pl.pallas_call(
        functools.partial(_ffn_kernel, tt=tt, tf=tf, d_ff=d_ff),
        grid=(B, T // tt),
        in_specs=[act, resident(g), resident(w_up), resident(conv_w), resident(conv_b), resident(w_down)],
        out_specs=act,
        out_shape=jax.ShapeDtypeStruct((B, T, D), F32),
        scratch_shapes=[pltpu.VMEM((tt, d_ff), BF16), pltpu.VMEM((SUBLANES, 2 * d_ff), F32)],
        compiler_params=_cparams(("arbitrary", "arbitrary")),
        name="conv_ffn",
    )(h, g, w_up, conv_w, conv_b, w_down)


def _kvproj_kernel(h_ref, g_ref, w_ref, wvt_ref, ksg_ref, kwg_ref, c_ref, s_ref,
                   craw_ref, ks_ref, kw_ref, vst_ref, vwt_ref, *, tm):
    xn = _row_rms(h_ref[0], g_ref[...]).astype(BF16)
    cos = c_ref[0]
    sin = s_ref[0]
    craw_ref[0] = _dot(xn, w_ref[:, 0:256])
    for idx, (ref, gain) in enumerate(((ks_ref, ksg_ref), (kw_ref, kwg_ref))):
        y = _dot(xn, w_ref[:, 256 * (idx + 1):256 * (idx + 2)])
        for grp in range(N_KV_GROUPS):
            blk = _pair_rope(_pair_rms(y[:, LANES * grp:LANES * (grp + 1)], gain[...]), cos, sin)
            ref[0, grp] = blk.astype(BF16)
    yt = _dot_nt(wvt_ref[...], xn).astype(BF16)
    for grp in range(N_KV_GROUPS):
        vst_ref[0, grp, 0] = yt[HEAD_DIM * grp:HEAD_DIM * (grp + 1), :]
        r0 = HEAD_DIM * (N_KV_GROUPS + grp)
        for c in range(tm // WIN_TQ):
            vwt_ref[0, grp, c] = yt[r0:r0 + HEAD_DIM, WIN_TQ * c:WIN_TQ * (c + 1)]


def _kvproj(h, g, w, wvt, ksg, kwg, cos, sin, tm=ROW_TILE):
    B, T, D = h.shape
    G = N_KV_GROUPS
    vec = lambda n: pl.BlockSpec((1, n), lambda b, t: (0, 0))
    full = lambda a: pl.BlockSpec(a.shape, lambda b, t: (0,) * a.ndim)
    tab = pl.BlockSpec((1, tm, LANES), lambda b, t: (b, t, 0))
    dup = pl.BlockSpec((1, G, tm, LANES), lambda b, t: (b, 0, t, 0))
    dup_shape = jax.ShapeDtypeStruct((B, G, T, LANES), BF16)
    nw = tm // WIN_TQ
    return pl.pallas_call(
        functools.partial(_kvproj_kernel, tm=tm),
        grid=(B, T // tm),
        in_specs=[pl.BlockSpec((1, tm, D), lambda b, t: (b, t, 0)), vec(D), full(w), full(wvt),
                  vec(LANES), vec(LANES), tab, tab],
        out_specs=[pl.BlockSpec((1, tm, 256), lambda b, t: (b, t, 0)), dup, dup,
                   pl.BlockSpec((1, G, 1, HEAD_DIM, tm), lambda b, t: (b, 0, t, 0, 0)),
                   pl.BlockSpec((1, G, nw, HEAD_DIM, WIN_TQ), lambda b, t: (b, 0, t, 0, 0))],
        out_shape=[jax.ShapeDtypeStruct((B, T, 256), F32), dup_shape, dup_shape,
                   jax.ShapeDtypeStruct((B, G, T // tm, HEAD_DIM, tm), BF16),
                   jax.ShapeDtypeStruct((B, G, T // WIN_TQ, HEAD_DIM, WIN_TQ), BF16)],
        compiler_params=_cparams(("parallel", "parallel")),
        name="kvproj",
    )(h, g, w, wvt, ksg, kwg, cos, sin)


def _compress_kernel(r_ref, w1_ref, pe_ref, w2_ref, w2t_ref, gain_ref, c_ref, s_ref, kc_ref, vct_ref, *, n_cmp):
    r = r_ref[0, 0, 0].astype(BF16)
    half = r.shape[1]
    a = _dot(r, w1_ref[0, :half, :])
    b = _dot(r, w1_ref[0, half:, :])
    peb = _dot(pe_ref[0], w1_ref[0])[0:1, :]
    rows = r.shape[0]
    hid = a + pltpu.roll(b, rows - 1, 0) + peb
    act = jax.nn.gelu(hid).astype(BF16)
    y = _dot(act, w2_ref[0])
    yk = _pair_rope(_pair_rms(y, gain_ref[...]), c_ref[0], s_ref[0])
    kc_ref[0, 0, 0] = jnp.where(_iota(y.shape, 0) < n_cmp, yk, 0.0).astype(BF16)
    yt = _dot_nt(w2t_ref[0], act)
    vct_ref[0, 0, 0] = jnp.where(_iota(yt.shape, 1) < n_cmp, yt, 0.0).astype(BF16)


def _compress(r, w1, pe, w2, w2t, gain, cos, sin, n_cmp):
    B, _, G, rows, width = r.shape
    per_kv = lambda a: pl.BlockSpec((1,) + a.shape[1:], lambda b, kv, g: (kv,) + (0,) * (a.ndim - 1))
    return pl.pallas_call(
        functools.partial(_compress_kernel, n_cmp=n_cmp),
        grid=(B, 2, G),
        in_specs=[pl.BlockSpec((1, 1, 1, rows, width), lambda b, kv, g: (b, kv, g, 0, 0)),
                  per_kv(w1), per_kv(pe), per_kv(w2), per_kv(w2t),
                  pl.BlockSpec((1, LANES), lambda b, kv, g: (0, 0)),
                  pl.BlockSpec((1, rows, LANES), lambda b, kv, g: (b, 0, 0)),
                  pl.BlockSpec((1, rows, LANES), lambda b, kv, g: (b, 0, 0))],
        out_specs=[pl.BlockSpec((1, 1, 1, rows, LANES), lambda b, kv, g: (b, kv, g, 0, 0)),
                   pl.BlockSpec((1, 1, 1, HEAD_DIM, rows), lambda b, kv, g: (b, kv, g, 0, 0))],
        out_shape=[jax.ShapeDtypeStruct((B, 2, G, rows, LANES), BF16),
                   jax.ShapeDtypeStruct((B, 2, G, HEAD_DIM, rows), BF16)],
        compiler_params=_cparams(("parallel", "parallel", "parallel")),
        name="compress",
    )(r, w1, pe, w2, w2t, gain, cos, sin)


def _nsa_inproj_kernel(h_ref, g_ref, w_ref, wgt_ref, bgt_ref, qg_ref, c_ref, s_ref, q_ref, gate_ref, *, hd):
    xn = _row_rms(h_ref[0], g_ref[...]).astype(BF16)
    cos = c_ref[0]
    sin = s_ref[0]
    for j in range(hd // 256):
        y = _dot(xn, w_ref[:, 256 * j:256 * (j + 1)])
        for hh in range(2):
            blk = _pair_rope(_pair_rms(y[:, LANES * hh:LANES * (hh + 1)], qg_ref[...]), cos, sin) * Q_SCALE
            q_ref[0, :, 256 * j + LANES * hh:256 * j + LANES * (hh + 1)] = blk.astype(BF16)
    gate_ref[0] = jax.nn.sigmoid(_dot_nt(wgt_ref[...], xn) + bgt_ref[...])


def _nsa_inproj(h, g, w, wgt, bgt, qg, cos, sin, tm=ROW_TILE):
    B, T, D = h.shape
    hd = N_HEADS * HEAD_DIM
    vec = lambda n: pl.BlockSpec((1, n), lambda b, t: (0, 0))
    full = lambda a: pl.BlockSpec(a.shape, lambda b, t: (0,) * a.ndim)
    tab = pl.BlockSpec((1, tm, LANES), lambda b, t: (b, t, 0))
    return pl.pallas_call(
        functools.partial(_nsa_inproj_kernel, hd=hd),
        grid=(B, T // tm),
        in_specs=[pl.BlockSpec((1, tm, D), lambda b, t: (b, t, 0)), vec(D), full(w), full(wgt), full(bgt),
                  vec(LANES), tab, tab],
        out_specs=[pl.BlockSpec((1, tm, hd), lambda b, t: (b, t, 0)),
                   pl.BlockSpec((1, wgt.shape[0], tm), lambda b, t: (b, 0, t))],
        out_shape=[jax.ShapeDtypeStruct((B, T, hd), BF16), jax.ShapeDtypeStruct((B, wgt.shape[0], T), F32)],
        compiler_params=_cparams(("parallel", "parallel")),
        name="nsa_inproj",
    )(h, g, w, wgt, bgt, qg, cos, sin)


def _stack_heads(q, dst_ref, tq):
    lo = _iota((tq, LANES), 1) < HEAD_DIM
    zero = jnp.zeros((tq, LANES), q.dtype)
    for j in range(HEADS_PER_GROUP // 2):
        x = q[:, LANES * j:LANES * (j + 1)]
        dst_ref[(2 * j) * tq:(2 * j + 1) * tq, 0:LANES] = jnp.where(lo, x, zero)
        dst_ref[(2 * j + 1) * tq:(2 * j + 2) * tq, 0:LANES] = jnp.where(lo, zero, x)


def _unstack_heads_t(o_t, gate_ref, o_ref, tq):
    gate = gate_ref[0]
    for j in range(HEADS_PER_GROUP // 2):
        halves = [o_t[:, h * tq:(h + 1) * tq] * gate[h:h + 1, :] for h in (2 * j, 2 * j + 1)]
        o_ref[0, :, LANES * j:LANES * (j + 1)] = jnp.concatenate(halves, axis=0).T.astype(o_ref.dtype)


def _nsa_cmp_kernel(q_ref, kc_ref, vct_ref, ovt_ref, gate_ref, oc_ref, sel_ref, qs_sc, imp_sc, *, tq, n_slc, top_k, n_levels):
    qi = pl.program_id(2)
    _stack_heads(q_ref[0], qs_sc, tq)
    n_rows = kc_ref.shape[3]
    level_rows = n_rows // n_levels
    n_chunks = 2
    cw = HEADS_PER_GROUP * tq // n_chunks
    n_vis = (qi * tq + tq - 1 - (CMP_LEN - 1)) // CMP_STRIDE + 1

    def attend(rows):
        kc = kc_ref[0, 0, 0, :rows, :]
        vct = vct_ref[0, 0, 0, :, :rows]
        scores = [_dot_nt(kc, qs_sc[c * cw:(c + 1) * cw, :]) for c in range(n_chunks)]
        qpos = qi * tq + (_iota((rows, cw), 1) & (tq - 1))
        vis = _iota((rows, cw), 0) * CMP_STRIDE + (CMP_LEN - 1) <= qpos
        outs = []
        p_sum = None
        for s in scores:
            s = jnp.where(vis, s, NEG)
            e = jnp.where(vis, jnp.exp2(s - jnp.max(s, axis=0, keepdims=True)), 0.0)
            p = e * (1.0 / jnp.maximum(jnp.sum(e, axis=0, keepdims=True), 1.0))
            outs.append(_dot(vct, p.astype(BF16)))
            for h in range(cw // tq):
                term = p[:, h * tq:(h + 1) * tq]
                p_sum = term if p_sum is None else p_sum + term
        _unstack_heads_t(jnp.concatenate(outs, axis=1), gate_ref, oc_ref, tq)
        p_hi = p_sum.astype(BF16)
        p_lo = (p_sum - p_hi.astype(F32)).astype(BF16)
        imp_sc[...] = _dot(ovt_ref[:, :rows], p_hi) + _dot(ovt_ref[:, :rows], p_lo)

    for level in range(n_levels):
        lo_rows, hi_rows = level * level_rows, (level + 1) * level_rows
        cond = (n_vis <= hi_rows) if level == 0 else ((n_vis > lo_rows) & (n_vis <= hi_rows))
        pl.when(cond)(functools.partial(attend, hi_rows))

    imp = imp_sc[...]
    jb = _iota(imp.shape, 0)
    qp = qi * tq + _iota(imp.shape, 1)
    cur = qp // SLC_LEN
    forced = (jb == 0) | (jb == cur) | (jb == cur - 1)
    score = jnp.where(jb * SLC_LEN <= qp, jnp.where(forced, FORCED_SCORE, imp), NEG)
    score = jnp.where(jb < n_slc, score, -jnp.inf)
    bias = jnp.full(imp.shape, NEG, F32)
    for _ in range(top_k):
        mx = jnp.max(score, axis=0, keepdims=True)
        idx = jnp.min(jnp.where(score == mx, jb, LANES), axis=0, keepdims=True)
        pick = jb == idx
        bias = jnp.where(pick, 0.0, bias)
        score = jnp.where(pick, -jnp.inf, score)
    sel_ref[0, 0] = bias.T.astype(BF16)


def _gate_spec(branch, tq):
    return pl.BlockSpec((1, HEADS_PER_GROUP, tq), lambda b, g, qi: (b, branch * N_KV_GROUPS + g, qi))


def _nsa_cmp(q, kc_all, vct_all, overlap_t, gates_t, n_slc, tq=CMP_TQ):
    B, T, hd = q.shape
    G = N_KV_GROUPS
    rows = kc_all.shape[3]
    gw = hd // G
    top_k = min(SLC_TOPK, n_slc)
    n_levels = 4 if rows % (4 * LANES) == 0 else 1
    return pl.pallas_call(
        functools.partial(_nsa_cmp_kernel, tq=tq, n_slc=n_slc, top_k=top_k, n_levels=n_levels),
        grid=(B, G, T // tq),
        in_specs=[pl.BlockSpec((1, tq, gw), lambda b, g, qi: (b, qi, g)),
                  pl.BlockSpec((1, 1, 1, rows, LANES), lambda b, g, qi: (b, 0, g, 0, 0)),
                  pl.BlockSpec((1, 1, 1, HEAD_DIM, rows), lambda b, g, qi: (b, 1, g, 0, 0)),
                  pl.BlockSpec(overlap_t.shape, lambda b, g, qi: (0, 0)), _gate_spec(0, tq)],
        out_specs=[pl.BlockSpec((1, tq, gw), lambda b, g, qi: (b, qi, g)),
                   pl.BlockSpec((1, 1, tq, LANES), lambda b, g, qi: (b, g, qi, 0))],
        out_shape=[jax.ShapeDtypeStruct((B, T, hd), BF16), jax.ShapeDtypeStruct((B, G, T, LANES), BF16)],
        scratch_shapes=[pltpu.VMEM((HEADS_PER_GROUP * tq, LANES), BF16), pltpu.VMEM((LANES, tq), F32)],
        compiler_params=_cparams(("parallel", "parallel", "parallel")),
        name="nsa_cmp",
    )(q, kc_all, vct_all, overlap_t, gates_t)


def _nsa_slc_kernel(q_ref, sel_ref, k_ref, oh_ref, vt_ref, gate_ref, o_ref, qa_sc, s_sc, m_sc, acc_sc, *, tq, tk, n_chunks):
    qi = pl.program_id(2)
    last = (qi * tq) // tk
    _stack_heads(q_ref[0], qa_sc, tq)
    sel = sel_ref[0, 0]
    for h in range(HEADS_PER_GROUP):
        qa_sc[h * tq:(h + 1) * tq, LANES:2 * LANES] = sel
    _softmax_init(m_sc, acc_sc)
    cw = HEADS_PER_GROUP * tq // n_chunks

    def produce(ki, slot):
        start = pl.multiple_of(ki * tk, tk)
        ka = jnp.concatenate([k_ref[0, 0, pl.ds(start, tk), :], oh_ref[pl.ds(start, tk), :]], axis=1)
        for c in range(n_chunks):
            s_sc[slot, :, c * cw:(c + 1) * cw] = _dot_nt(ka, qa_sc[c * cw:(c + 1) * cw, :])

    def consume(ki, slot, diag):
        vt = vt_ref[0, 0, ki]
        for c in range(n_chunks):
            cols = slice(c * cw, (c + 1) * cw)
            s = s_sc[slot, :, cols]
            if diag:
                kpos = ki * tk + _iota(s.shape, 0)
                qpos = qi * tq + (_iota(s.shape, 1) & (tq - 1))
                s = jnp.where(kpos <= qpos, s, NEG)
            _softmax_update_t(s, vt, m_sc.at[:, cols], acc_sc.at[:, cols])

    _pipelined_key_loop(last, produce, consume)
    _unstack_heads_t(_softmax_result(acc_sc), gate_ref, o_ref, tq)


def _nsa_slc(q, sel, ks, vst, gates_t, tq=SLC_TQ, tk=SLC_TK):
    B, T, hd = q.shape
    G = N_KV_GROUPS
    gw = hd // G
    cols = HEADS_PER_GROUP * tq
    assert tk % tq == 0
    onehot = (jnp.arange(T)[:, None] // SLC_LEN == jnp.arange(LANES)[None, :]).astype(BF16)
    return pl.pallas_call(
        functools.partial(_nsa_slc_kernel, tq=tq, tk=tk, n_chunks=4),
        grid=(B, G, T // tq),
        in_specs=[pl.BlockSpec((1, tq, gw), lambda b, g, qi: (b, qi, g)),
                  pl.BlockSpec((1, 1, tq, LANES), lambda b, g, qi: (b, g, qi, 0)),
                  pl.BlockSpec((1, 1, T, LANES), lambda b, g, qi: (b, g, 0, 0)),
                  pl.BlockSpec((T, LANES), lambda b, g, qi: (0, 0)),
                  pl.BlockSpec((1, 1, T // tk, HEAD_DIM, tk), lambda b, g, qi: (b, g, 0, 0, 0)), _gate_spec(1, tq)],
        out_specs=pl.BlockSpec((1, tq, gw), lambda b, g, qi: (b, qi, g)),
        out_shape=jax.ShapeDtypeStruct((B, T, hd), BF16),
        scratch_shapes=[pltpu.VMEM((cols, 2 * LANES), BF16), pltpu.VMEM((2, tk, cols), F32), pltpu.VMEM((1, cols), F32),
                        pltpu.VMEM((ACC_ROWS, cols), F32)],
        compiler_params=_cparams(("parallel", "parallel", "arbitrary")),
        name="nsa_slc",
    )(q, sel, ks, onehot, vst, gates_t)


def _nsa_win_kernel(q_ref, k_ref, vt_ref, gate_ref, o_ref, qs_sc, s_sc, m_sc, acc_sc, *, tq, n_back, n_chunks):
    qi = pl.program_id(2)
    _stack_heads(q_ref[0], qs_sc, tq)
    _softmax_init(m_sc, acc_sc)
    cw = HEADS_PER_GROUP * tq // n_chunks

    def produce(ti, slot):
        start = pl.multiple_of(ti * tq, tq)
        k = k_ref[0, 0, pl.ds(start, tq), :]
        for c in range(n_chunks):
            s_sc[slot, :, c * cw:(c + 1) * cw] = _dot_nt(k, qs_sc[c * cw:(c + 1) * cw, :])

    def consume(ti, slot, kind):
        vt = vt_ref[0, 0, ti]
        for c in range(n_chunks):
            cols = slice(c * cw, (c + 1) * cw)
            s = s_sc[slot, :, cols]
            r = _iota(s.shape, 0)
            q_in_tile = _iota(s.shape, 1) & (tq - 1)
            if kind == "oldest":
                s = jnp.where(r > q_in_tile, s, NEG)
            elif kind == "diag":
                s = jnp.where(r <= q_in_tile, s, NEG)
            _softmax_update_t(s, vt, m_sc.at[:, cols], acc_sc.at[:, cols])

    def kind_of(back):
        return "oldest" if back == n_back else ("diag" if back == 0 else "full")

    for first in range(n_back + 1):
        cond = (qi >= n_back) if first == n_back else (qi == first)

        @pl.when(cond)
        def _(first=first):
            backs = list(range(first, -1, -1))
            produce(qi - backs[0], 0)
            for n, back in enumerate(backs):
                if n + 1 < len(backs):
                    produce(qi - backs[n + 1], (n + 1) % 2)
                consume(qi - back, n % 2, kind_of(back))

    _unstack_heads_t(_softmax_result(acc_sc), gate_ref, o_ref, tq)


def _nsa_win(q, kw, vwt, gates_t, tq=WIN_TQ):
    B, T, hd = q.shape
    G = N_KV_GROUPS
    gw = hd // G
    cols = HEADS_PER_GROUP * tq
    return pl.pallas_call(
        functools.partial(_nsa_win_kernel, tq=tq, n_back=WIN // tq, n_chunks=4),
        grid=(B, G, T // tq),
        in_specs=[pl.BlockSpec((1, tq, gw), lambda b, g, qi: (b, qi, g)),
                  pl.BlockSpec((1, 1, T, LANES), lambda b, g, qi: (b, g, 0, 0)),
                  pl.BlockSpec((1, 1, T // tq, HEAD_DIM, tq), lambda b, g, qi: (b, g, 0, 0, 0)), _gate_spec(2, tq)],
        out_specs=pl.BlockSpec((1, tq, gw), lambda b, g, qi: (b, qi, g)),
        out_shape=jax.ShapeDtypeStruct((B, T, hd), BF16),
        scratch_shapes=[pltpu.VMEM((cols, LANES), BF16), pltpu.VMEM((2, tq, cols), F32), pltpu.VMEM((1, cols), F32),
                        pltpu.VMEM((ACC_ROWS, cols), F32)],
        compiler_params=_cparams(("parallel", "parallel", "arbitrary")),
        name="nsa_win",
    )(q, kw, vwt, gates_t)


def _pad_cols(a, n):
    return jnp.pad(a, ((0, 0), (0, n - a.shape[1])))


def _lane_vec(v):
    return jnp.tile(v.astype(F32), 2).reshape(1, LANES)


def _bias_feature_selector():
    sel = np.zeros((N_BIAS_PARTS * LANES, N_HEADS * HEAD_DIM), np.float32)
    for part in range(N_BIAS_PARTS):
        for h in range(N_HEADS):
            sel[part * LANES + h, (h // 2) * LANES + (h % 2) * N_BIAS_PARTS + part] = 1.0
    return jnp.asarray(sel, BF16)


def kernel(x, positions, a_norm, a_w_in, a_b_f, a_q_gain, a_k_gain, a_w_out, kv_norm, kv_w, kc_pe, vc_pe, kc_w1, kc_w2, vc_w1, vc_w2, kc_gain, ks_gain, kw_gain, b_norm, b_w_in, b_b_gate, b_q_gain, b_w_out, f_norm, f_w_up, f_conv_w, f_conv_b, f_w_down):
    B, T, D = x.shape
    hd = N_HEADS * HEAD_DIM
    G = N_KV_GROUPS
    n_a = a_norm.shape[0]
    n_b = b_norm.shape[0]
    depth = n_a + n_b
    n_slc = T // SLC_LEN
    n_cmp = (T - CMP_LEN) // CMP_STRIDE + 1
    assert T % 1024 == 0 and n_slc <= LANES and hd == 1024 and D == 1024

    half = ROT_DIM // 2
    inv = ROPE_THETA ** (-jnp.arange(half, dtype=F32) * (2.0 / ROT_DIM))
    head_inv = jnp.concatenate([inv, inv, jnp.zeros((HEAD_DIM - ROT_DIM,), F32)])
    head_sign = jnp.concatenate([-jnp.ones((half,), F32), jnp.ones((half,), F32), jnp.zeros((HEAD_DIM - ROT_DIM,), F32)])
    inv_lane = jnp.tile(head_inv, 2).reshape(1, LANES)
    sign_lane = jnp.tile(head_sign, 2).reshape(1, LANES)
    cos_t, sin_t = _rope_tables(positions, inv_lane, sign_lane)
    end_pos = positions[:, CMP_LEN - 1::CMP_STRIDE]
    end_pos = jnp.pad(end_pos, ((0, 0), (0, T // CMP_STRIDE - n_cmp)))
    cos_c, sin_c = _rope_tables(end_pos, inv_lane, sign_lane)

    cs = jnp.arange(T // CMP_STRIDE) * CMP_STRIDE
    ss = jnp.arange(LANES) * SLC_LEN
    overlap_t = (jnp.maximum(jnp.minimum(cs[None, :] + CMP_LEN, ss[:, None] + SLC_LEN)
                             - jnp.maximum(cs[None, :], ss[:, None]), 0).astype(F32) / CMP_LEN).astype(BF16)

    h = x
    kv = None
    for layer in range(depth):
        if layer < n_a:
            i = layer
            w = jnp.concatenate([a_w_in[i][:, :2 * hd], _pad_cols(a_w_in[i][:, 3 * hd:], LANES)], axis=1).astype(BF16)
            wvt = a_w_in[i][:, 2 * hd:3 * hd].T.astype(BF16)
            bf = _pad_cols(a_b_f[i].reshape(1, -1), LANES)
            q, k, vt, cf = _fox_inproj(h, a_norm[i].reshape(1, D), w, wvt, bf, _lane_vec(a_q_gain[i]),
                                       _lane_vec(a_k_gain[i]), _bias_feature_selector())
            o = _fox_attn(q, k, cf, vt)
            h = _outproj(h, o, a_w_out[i].astype(BF16))
        else:
            i = layer - n_a
            kc_all, vct_all, ks, kw, vst, vwt = kv
            q, gates_t = _nsa_inproj(h, b_norm[i].reshape(1, D), b_w_in[i][:, :hd].astype(BF16),
                                     b_w_in[i][:, hd:].T.astype(BF16), b_b_gate[i].reshape(-1, 1),
                                     _lane_vec(b_q_gain[i]), cos_t, sin_t)
            o_c, sel = _nsa_cmp(q, kc_all, vct_all, overlap_t, gates_t, n_slc)
            o_s = _nsa_slc(q, sel, ks, vst, gates_t)
            o_w = _nsa_win(q, kw, vwt, gates_t)
            h = _nsa_outproj(h, o_c, o_s, o_w, b_w_out[i].astype(BF16))
        h = _conv_ffn(h, f_norm[layer].reshape(1, D), f_w_up[layer].astype(BF16), f_conv_w[layer],
                      f_conv_b[layer].reshape(1, -1), f_w_down[layer].astype(BF16))
        if layer == n_a - 1:
            w6 = kv_w.reshape(D, 6, G, HEAD_DIM)
            raw_cols = w6[:, 0:2].reshape(D, 2 * G * HEAD_DIM)
            wk = w6[:, (2, 4)]
            dup_cols = jnp.concatenate([wk, wk], axis=-1).reshape(D, 2 * G * LANES)
            wkv = jnp.concatenate([raw_cols, dup_cols], axis=1).astype(BF16)
            wvt = w6[:, (3, 5)].reshape(D, 2 * G * HEAD_DIM).T.astype(BF16)
            craw, ks, kw, vst, vwt = _kvproj(h, kv_norm.reshape(1, D), wkv, wvt, _lane_vec(ks_gain),
                                             _lane_vec(kw_gain), cos_t, sin_t)
            r = craw.reshape(B, T, 2, G, HEAD_DIM).transpose(0, 2, 3, 1, 4).reshape(B, 2, G, T // CMP_STRIDE, CMP_STRIDE * HEAD_DIM)
            w1 = jnp.stack([kc_w1, vc_w1]).astype(BF16)
            pe = jnp.stack([kc_pe.reshape(1, -1), vc_pe.reshape(1, -1)])
            pe = jnp.pad(pe, ((0, 0), (0, 7), (0, 0))).astype(BF16)
            w2 = jnp.stack([jnp.concatenate([kc_w2, kc_w2], axis=1), jnp.concatenate([vc_w2, vc_w2], axis=1)]).astype(BF16)
            w2t = jnp.stack([kc_w2.T, vc_w2.T]).astype(BF16)
            kc_all, vct_all = _compress(r, w1, pe, w2, w2t, _lane_vec(kc_gain), cos_c, sin_c, n_cmp)
            kv = (kc_all, vct_all, ks, kw, vst, vwt)
    return h
```

```python
import functools

import numpy as np
import jax
import jax.numpy as jnp
from jax import lax
from jax.experimental import pallas as pl
from jax.experimental.pallas import tpu as pltpu

F32 = jnp.float32
BF16 = jnp.bfloat16

LANES = 128
SUBLANES = 8
HEAD_DIM = 64
N_HEADS = 16
N_PAIRS = N_HEADS // 2
N_KV_GROUPS = 2
HEADS_PER_GROUP = N_HEADS // N_KV_GROUPS
ROT_DIM = HEAD_DIM // 4
ROPE_THETA = 500000.0
CMP_LEN = 32
CMP_STRIDE = 16
SLC_LEN = 64
SLC_TOPK = 16
WIN = 512
CONV_W = 3
RMS_EPS = 1e-6
NEG = -1e30
FORCED_SCORE = 1e6
LOG2E = 1.4426950408889634
Q_SCALE = HEAD_DIM ** -0.5 * LOG2E
N_BIAS_PARTS = 3
ONES_ROWS = 16
ACC_ROWS = HEAD_DIM + ONES_ROWS

ROW_TILE = 512
FOX_TQ = 512
FOX_TK = ROW_TILE
SLC_TQ = 128
SLC_TK = ROW_TILE
WIN_TQ = 256
CMP_TQ = 256

VMEM_LIMIT = 48 * 1024 * 1024


def _cparams(sem, flags=None):
    return pltpu.CompilerParams(dimension_semantics=sem, vmem_limit_bytes=VMEM_LIMIT, flags=flags)


def _iota(shape, axis):
    return lax.broadcasted_iota(jnp.int32, shape, axis)


def _row_rms(x, g):
    ms = jnp.mean(x * x, axis=-1, keepdims=True)
    return x * lax.rsqrt(ms + RMS_EPS) * g


def _pair_rms(y, gain):
    lo = _iota(y.shape, 1) < HEAD_DIM
    y2 = y * y
    s_lo = jnp.sum(jnp.where(lo, y2, 0.0), axis=-1, keepdims=True)
    s_hi = jnp.sum(jnp.where(lo, 0.0, y2), axis=-1, keepdims=True)
    ms = jnp.where(lo, s_lo, s_hi) * (1.0 / HEAD_DIM)
    return y * lax.rsqrt(ms + RMS_EPS) * gain


def _pair_rope(y, cos, sin):
    lane = _iota(y.shape, 1) & (HEAD_DIM - 1)
    partner = jnp.where(lane < ROT_DIM // 2,
                        pltpu.roll(y, LANES - ROT_DIM // 2, 1),
                        pltpu.roll(y, ROT_DIM // 2, 1))
    return y * cos + partner * sin


def _dot(a, b):
    return jnp.dot(a, b, preferred_element_type=F32)


def _dot_nt(a, b):
    return lax.dot_general(a, b, (((1,), (1,)), ((), ())), preferred_element_type=F32)


def _column_max(s):
    tk, n = s.shape
    return jnp.max(jnp.max(s.reshape(tk // SUBLANES, SUBLANES, n), axis=0), axis=0, keepdims=True)


def _softmax_update_t(s, vt, m_ref, acc_ref, m_tile=None):
    tk, n = s.shape
    m_old = m_ref[...]
    if m_tile is None:
        m_tile = _column_max(s)
    m_new = jnp.maximum(m_old, m_tile)
    alpha = jnp.exp2(m_old - m_new)
    p = jnp.exp2(s - m_new).astype(BF16)
    vta = jnp.concatenate([vt, jnp.ones((ONES_ROWS, tk), vt.dtype)], axis=0)
    acc_ref[...] = alpha * acc_ref[...] + _dot(vta, p)
    m_ref[...] = m_new


def _pipelined_key_loop(n_full, produce, consume, unroll=4):
    assert unroll % 2 == 0

    def run(base, count):
        for i in range(count):
            produce(base + i + 1, (i + 1) % 2)
            consume(base + i, i % 2, False)

    produce(0, 0)

    def body(j, carry):
        run(unroll * j, unroll)
        return carry

    lax.fori_loop(0, n_full // unroll, body, 0)
    base = (n_full // unroll) * unroll
    rem = n_full - base
    step = unroll // 2
    while step >= 2:
        pl.when((rem & step) != 0)(functools.partial(run, base, step))
        base = base + (rem & step)
        step //= 2

    @pl.when((rem & 1) != 0)
    def _():
        run(base, 1)
        consume(base + 1, 1, True)

    @pl.when((rem & 1) == 0)
    def _():
        consume(base, 0, True)


def _softmax_init(m_ref, acc_ref):
    m_ref[...] = jnp.full_like(m_ref, NEG)
    acc_ref[...] = jnp.zeros_like(acc_ref)


def _softmax_result(acc_ref):
    acc = acc_ref[...]
    return acc[:HEAD_DIM, :] * (1.0 / acc[HEAD_DIM:HEAD_DIM + 1, :])


def _rope_table_kernel(pos_ref, inv_ref, sign_ref, c_ref, s_ref):
    ang = pos_ref[0].astype(F32) * inv_ref[...]
    c_ref[0] = jnp.cos(ang)
    s_ref[0] = jnp.sin(ang) * sign_ref[...]


def _rope_tables(pos, inv_lane, sign_lane):
    B, T = pos.shape
    tm = min(T, ROW_TILE)
    spec = pl.BlockSpec((1, tm, LANES), lambda b, t: (b, t, 0))
    vec = pl.BlockSpec((1, LANES), lambda b, t: (0, 0))
    return pl.pallas_call(
        _rope_table_kernel,
        grid=(B, T // tm),
        in_specs=[pl.BlockSpec((1, tm, 1), lambda b, t: (b, t, 0)), vec, vec],
        out_specs=[spec, spec],
        out_shape=[jax.ShapeDtypeStruct((B, T, LANES), F32)] * 2,
        compiler_params=_cparams(("parallel", "parallel")),
        name="rope_tables",
    )(pos.reshape(B, T, 1), inv_lane, sign_lane)


def _fox_inproj_kernel(x_ref, g_ref, w_ref, wvt_ref, bf_ref, qg_ref, kg_ref, sel_ref,
                       q_ref, k_ref, vt_ref, cf_ref, carry_sc, *, tm, hd):
    ti = pl.program_id(1)
    xn = _row_rms(x_ref[0], g_ref[...]).astype(BF16)
    for j in range(hd // 256):
        for part, (ref, gain, mul) in enumerate(((q_ref, qg_ref, Q_SCALE), (k_ref, kg_ref, 1.0))):
            c0 = part * hd + 256 * j
            y = _dot(xn, w_ref[:, c0:c0 + 256])
            for hh in range(2):
                blk = _pair_rms(y[:, LANES * hh:LANES * (hh + 1)], gain[...]) * mul
                ref[0, :, 256 * j + LANES * hh:256 * j + LANES * (hh + 1)] = blk.astype(BF16)
        yt = _dot_nt(wvt_ref[256 * j:256 * (j + 1), :], xn)
        for hh in range(2):
            vt_ref[0, 2 * j + hh, 0] = yt[LANES * hh:LANES * (hh + 1), :].astype(BF16)
    z = _dot(xn, w_ref[:, 2 * hd:2 * hd + LANES]) + bf_ref[...]
    lf = jnp.minimum(z, 0.0) - jnp.log1p(jnp.exp(-jnp.abs(z)))
    row = _iota(lf.shape, 0)
    sh = 1
    while sh < tm:
        lf = lf + jnp.where(row >= sh, pltpu.roll(lf, sh, 0), 0.0)
        sh *= 2

    @pl.when(ti == 0)
    def _():
        carry_sc[...] = jnp.zeros_like(carry_sc)

    c = lf + carry_sc[0:1, :]
    carry_sc[...] = jnp.broadcast_to(c[tm - 1:tm, :], carry_sc.shape)
    rest = c * (-LOG2E)
    pieces = []
    for _ in range(N_BIAS_PARTS):
        piece = rest.astype(BF16)
        pieces.append(piece)
        rest = rest - piece.astype(F32)
    cf_ref[0] = _dot(jnp.concatenate(pieces, axis=1), sel_ref[...]).astype(BF16)


def _fox_inproj(x, g, w, wvt, bf, qg, kg, sel, tm=ROW_TILE):
    B, T, D = x.shape
    hd = N_HEADS * HEAD_DIM
    act = pl.BlockSpec((1, tm, hd), lambda b, t: (b, t, 0))
    vec = lambda n: pl.BlockSpec((1, n), lambda b, t: (0, 0))
    full = lambda a: pl.BlockSpec(a.shape, lambda b, t: (0,) * a.ndim)
    return pl.pallas_call(
        functools.partial(_fox_inproj_kernel, tm=tm, hd=hd),
        grid=(B, T // tm),
        in_specs=[pl.BlockSpec((1, tm, D), lambda b, t: (b, t, 0)), vec(D), full(w), full(wvt),
                  vec(LANES), vec(LANES), vec(LANES), full(sel)],
        out_specs=[act, act, pl.BlockSpec((1, N_PAIRS, 1, LANES, tm), lambda b, t: (b, 0, t, 0, 0)), act],
        out_shape=[jax.ShapeDtypeStruct((B, T, hd), BF16), jax.ShapeDtypeStruct((B, T, hd), BF16),
                   jax.ShapeDtypeStruct((B, N_PAIRS, T // tm, LANES, tm), BF16),
                   jax.ShapeDtypeStruct((B, T, hd), BF16)],
        scratch_shapes=[pltpu.VMEM((SUBLANES, LANES), F32)],
        compiler_params=_cparams(("arbitrary", "arbitrary")),
        name="fox_inproj",
    )(x, g, w, wvt, bf, qg, kg, sel)


def _fox_attn_kernel(q_ref, k_ref, cf_ref, vt_ref, o_ref, qa_sc, s_sc, mt_sc, m_sc, acc_sc, *, tq, tk):
    qi = pl.program_id(2)
    q = q_ref[0]
    lane = _iota(q.shape, 1)
    lo = lane < HEAD_DIM
    zero = jnp.zeros_like(q)
    for hh in range(2):
        qa_sc[hh, :, 0:LANES] = jnp.where(lo, q, zero) if hh == 0 else jnp.where(lo, zero, q)
        feat = jnp.where(lane < N_BIAS_PARTS * (hh + 1), 1.0, 0.0)
        qa_sc[hh, :, LANES:2 * LANES] = jnp.where(lane >= N_BIAS_PARTS * hh, feat, 0.0).astype(BF16)
        _softmax_init(m_sc.at[hh], acc_sc.at[hh])

    def produce(ki, slot):
        start = pl.multiple_of(ki * tk, tk)
        ka = jnp.concatenate([k_ref[0, pl.ds(start, tk), :], cf_ref[0, pl.ds(start, tk), :]], axis=1)
        for hh in range(2):
            s = _dot_nt(ka, qa_sc[hh])
            s_sc[slot, hh] = s
            mt_sc[slot, hh] = _column_max(s)

    def consume(ki, slot, diag):
        vt = vt_ref[0, 0, ki]
        for hh in range(2):
            s = s_sc[slot, hh]
            m_tile = mt_sc[slot, hh]
            if diag:
                s = jnp.where(_iota(s.shape, 0) <= _iota(s.shape, 1), s, NEG)
                m_tile = None
            _softmax_update_t(s, vt[HEAD_DIM * hh:HEAD_DIM * (hh + 1), :], m_sc.at[hh], acc_sc.at[hh], m_tile)

    _pipelined_key_loop(qi, produce, consume)
    o_t = jnp.concatenate([_softmax_result(acc_sc.at[hh]) for hh in range(2)], axis=0)
    o_ref[0] = o_t.T.astype(BF16)


def _fox_attn(q, k, cf, vt, tq=FOX_TQ, tk=FOX_TK):
    B, T, hd = q.shape
    assert tq == tk
    nk = T // tk
    seq = pl.BlockSpec((1, T, LANES), lambda b, p, qi: (b, 0, p))
    return pl.pallas_call(
        functools.partial(_fox_attn_kernel, tq=tq, tk=tk),
        grid=(B, N_PAIRS, T // tq),
        in_specs=[pl.BlockSpec((1, tq, LANES), lambda b, p, qi: (b, qi, p)), seq, seq,
                  pl.BlockSpec((1, 1, nk, LANES, tk), lambda b, p, qi: (b, p, 0, 0, 0))],
        out_specs=pl.BlockSpec((1, tq, LANES), lambda b, p, qi: (b, qi, p)),
        out_shape=jax.ShapeDtypeStruct((B, T, hd), BF16),
        scratch_shapes=[pltpu.VMEM((2, tq, 2 * LANES), BF16), pltpu.VMEM((2, 2, tk, tq), F32),
                        pltpu.VMEM((2, 2, 1, tq), F32), pltpu.VMEM((2, 1, tq), F32),
                        pltpu.VMEM((2, ACC_ROWS, tq), F32)],
        compiler_params=_cparams(("parallel", "parallel", "arbitrary")),
        name="fox_attn",
    )(q, k, cf, vt)


def _outproj_kernel(h_ref, o_ref, w_ref, out_ref):
    out_ref[...] = h_ref[...] + _dot(o_ref[...], w_ref[...])


def _outproj(h, o, w, tm=ROW_TILE):
    B, T, D = h.shape
    n = B * T
    row = lambda c: pl.BlockSpec((tm, c), lambda i: (i, 0))
    out = pl.pallas_call(
        _outproj_kernel,
        grid=(n // tm,),
        in_specs=[row(D), row(o.shape[-1]), pl.BlockSpec(w.shape, lambda i: (0, 0))],
        out_specs=row(D),
        out_shape=jax.ShapeDtypeStruct((n, D), F32),
        compiler_params=_cparams(("parallel",)),
        name="outproj",
    )(h.reshape(n, D), o.reshape(n, -1), w)
    return out.reshape(B, T, D)


def _nsa_outproj_kernel(h_ref, oc_ref, os_ref, ow_ref, w_ref, out_ref):
    o = oc_ref[...].astype(F32) + os_ref[...].astype(F32) + ow_ref[...].astype(F32)
    out_ref[...] = h_ref[...] + _dot(o.astype(BF16), w_ref[...])


def _nsa_outproj(h, oc, osel, ow, w, tm=ROW_TILE):
    B, T, D = h.shape
    n = B * T
    hd = oc.shape[-1]
    row = lambda c: pl.BlockSpec((tm, c), lambda i: (i, 0))
    out = pl.pallas_call(
        _nsa_outproj_kernel,
        grid=(n // tm,),
        in_specs=[row(D), row(hd), row(hd), row(hd), pl.BlockSpec(w.shape, lambda i: (0, 0))],
        out_specs=row(D),
        out_shape=jax.ShapeDtypeStruct((n, D), F32),
        compiler_params=_cparams(("parallel",)),
        name="nsa_outproj",
    )(h.reshape(n, D), oc.reshape(n, hd), osel.reshape(n, hd), ow.reshape(n, hd), w)
    return out.reshape(B, T, D)


def _ffn_kernel(x_ref, g_ref, wup_ref, cw_ref, cb_ref, wd_ref, out_ref, a_sc, carry_sc, *, tt, tf, d_ff):
    ti = pl.program_id(1)

    @pl.when(ti == 0)
    def _():
        carry_sc[...] = jnp.zeros_like(carry_sc)

    x = x_ref[0]
    xn = _row_rms(x, g_ref[...]).astype(BF16)
    row8 = _iota((SUBLANES, tf), 0)

    def conv(u, c0):
        prev8 = carry_sc[:, c0:c0 + tf]
        um1 = pltpu.roll(u, 1, 0)
        um2 = pltpu.roll(u, 2, 0)
        top1 = jnp.where(row8 == 0, prev8[7:8, :], um1[0:SUBLANES, :])
        top2 = jnp.where(row8 == 0, prev8[6:7, :], jnp.where(row8 == 1, prev8[7:8, :], um2[0:SUBLANES, :]))
        um1 = jnp.concatenate([top1, um1[SUBLANES:, :]], axis=0)
        um2 = jnp.concatenate([top2, um2[SUBLANES:, :]], axis=0)
        carry_sc[:, c0:c0 + tf] = u[tt - SUBLANES:, :]
        cw = cw_ref[:, c0:c0 + tf]
        return cb_ref[:, c0:c0 + tf] + cw[0:1, :] * um2 + cw[1:2, :] * um1 + cw[2:3, :] * u

    for f in range(d_ff // tf):
        g0 = f * tf
        cg = conv(_dot(xn, wup_ref[:, g0:g0 + tf]), g0)
        cv = conv(_dot(xn, wup_ref[:, d_ff + g0:d_ff + g0 + tf]), d_ff + g0)
        a_sc[:, g0:g0 + tf] = (cg * jax.nn.sigmoid(cg) * cv).astype(BF16)
    out_ref[0] = x + _dot(a_sc[...], wd_ref[...])


def _conv_ffn(h, g, w_up, conv_w, conv_b, w_down, tt=ROW_TILE, tf=256):
    B, T, D = h.shape
    d_ff = w_down.shape[0]
    act = pl.BlockSpec((1, tt, D), lambda b, t: (b, t, 0))
    resident = lambda a: pl.BlockSpec(a.shape, lambda b, t: (0,) * a.ndim, pipeline_mode=pl.Buffered(1))
    return pl.pallas_call(
        functools.partial(_ffn_kernel, tt=tt, tf=tf, d_ff=d_ff),
        grid=(B, T // tt),
        in_specs=[act, resident(g), resident(w_up), resident(conv_w), resident(conv_b), resident(w_down)],
        out_specs=act,
        out_shape=jax.ShapeDtypeStruct((B, T, D), F32),
        scratch_shapes=[pltpu.VMEM((tt, d_ff), BF16), pltpu.VMEM((SUBLANES, 2 * d_ff), F32)],
        compiler_params=_cparams(("arbitrary", "arbitrary")),
        name="conv_ffn",
    )(h, g, w_up, conv_w, conv_b, w_down)


def _kvproj_kernel(h_ref, g_ref, w_ref, wvt_ref, ksg_ref, kwg_ref, c_ref, s_ref,
                   craw_ref, ks_ref, kw_ref, vst_ref, vwt_ref, *, tm):
    xn = _row_rms(h_ref[0], g_ref[...]).astype(BF16)
    cos = c_ref[0]
    sin = s_ref[0]
    craw_ref[0] = _dot(xn, w_ref[:, 0:256])
    for idx, (ref, gain) in enumerate(((ks_ref, ksg_ref), (kw_ref, kwg_ref))):
        y = _dot(xn, w_ref[:, 256 * (idx + 1):256 * (idx + 2)])
        for grp in range(N_KV_GROUPS):
            blk = _pair_rope(_pair_rms(y[:, LANES * grp:LANES * (grp + 1)], gain[...]), cos, sin)
            ref[0, grp] = blk.astype(BF16)
    yt = _dot_nt(wvt_ref[...], xn).astype(BF16)
    for grp in range(N_KV_GROUPS):
        vst_ref[0, grp, 0] = yt[HEAD_DIM * grp:HEAD_DIM * (grp + 1), :]
        r0 = HEAD_DIM * (N_KV_GROUPS + grp)
        for c in range(tm // WIN_TQ):
            vwt_ref[0, grp, c] = yt[r0:r0 + HEAD_DIM, WIN_TQ * c:WIN_TQ * (c + 1)]


def _kvproj(h, g, w, wvt, ksg, kwg, cos, sin, tm=ROW_TILE):
    B, T, D = h.shape
    G = N_KV_GROUPS
    vec = lambda n: pl.BlockSpec((1, n), lambda b, t: (0, 0))
    full = lambda a: pl.BlockSpec(a.shape, lambda b, t: (0,) * a.ndim)
    tab = pl.BlockSpec((1, tm, LANES), lambda b, t: (b, t, 0))
    dup = pl.BlockSpec((1, G, tm, LANES), lambda b, t: (b, 0, t, 0))
    dup_shape = jax.ShapeDtypeStruct((B, G, T, LANES), BF16)
    nw = tm // WIN_TQ
    return pl.pallas_call(
        functools.partial(_kvproj_kernel, tm=tm),
        grid=(B, T // tm),
        in_specs=[pl.BlockSpec((1, tm, D), lambda b, t: (b, t, 0)), vec(D), full(w), full(wvt),
                  vec(LANES), vec(LANES), tab, tab],
        out_specs=[pl.BlockSpec((1, tm, 256), lambda b, t: (b, t, 0)), dup, dup,
                   pl.BlockSpec((1, G, 1, HEAD_DIM, tm), lambda b, t: (b, 0, t, 0, 0)),
                   pl.BlockSpec((1, G, nw, HEAD_DIM, WIN_TQ), lambda b, t: (b, 0, t, 0, 0))],
        out_shape=[jax.ShapeDtypeStruct((B, T, 256), F32), dup_shape, dup_shape,
                   jax.ShapeDtypeStruct((B, G, T // tm, HEAD_DIM, tm), BF16),
                   jax.ShapeDtypeStruct((B, G, T // WIN_TQ, HEAD_DIM, WIN_TQ), BF16)],
        compiler_params=_cparams(("parallel", "parallel")),
        name="kvproj",
    )(h, g, w, wvt, ksg, kwg, cos, sin)


def _compress_kernel(r_ref, w1_ref, pe_ref, w2_ref, w2t_ref, gain_ref, c_ref, s_ref, kc_ref, vct_ref, *, n_cmp):
    r = r_ref[0, 0, 0].astype(BF16)
    half = r.shape[1]
    a = _dot(r, w1_ref[0, :half, :])
    b = _dot(r, w1_ref[0, half:, :])
    peb = _dot(pe_ref[0], w1_ref[0])[0:1, :]
    rows = r.shape[0]
    hid = a + pltpu.roll(b, rows - 1, 0) + peb
    act = jax.nn.gelu(hid).astype(BF16)
    y = _dot(act, w2_ref[0])
    yk = _pair_rope(_pair_rms(y, gain_ref[...]), c_ref[0], s_ref[0])
    kc_ref[0, 0, 0] = jnp.where(_iota(y.shape, 0) < n_cmp, yk, 0.0).astype(BF16)
    yt = _dot_nt(w2t_ref[0], act)
    vct_ref[0, 0, 0] = jnp.where(_iota(yt.shape, 1) < n_cmp, yt, 0.0).astype(BF16)


def _compress(r, w1, pe, w2, w2t, gain, cos, sin, n_cmp):
    B, _, G, rows, width = r.shape
    per_kv = lambda a: pl.BlockSpec((1,) + a.shape[1:], lambda b, kv, g: (kv,) + (0,) * (a.ndim - 1))
    return pl.pallas_call(
        functools.partial(_compress_kernel, n_cmp=n_cmp),
        grid=(B, 2, G),
        in_specs=[pl.BlockSpec((1, 1, 1, rows, width), lambda b, kv, g: (b, kv, g, 0, 0)),
                  per_kv(w1), per_kv(pe), per_kv(w2), per_kv(w2t),
                  pl.BlockSpec((1, LANES), lambda b, kv, g: (0, 0)),
                  pl.BlockSpec((1, rows, LANES), lambda b, kv, g: (b, 0, 0)),
                  pl.BlockSpec((1, rows, LANES), lambda b, kv, g: (b, 0, 0))],
        out_specs=[pl.BlockSpec((1, 1, 1, rows, LANES), lambda b, kv, g: (b, kv, g, 0, 0)),
                   pl.BlockSpec((1, 1, 1, HEAD_DIM, rows), lambda b, kv, g: (b, kv, g, 0, 0))],
        out_shape=[jax.ShapeDtypeStruct((B, 2, G, rows, LANES), BF16),
                   jax.ShapeDtypeStruct((B, 2, G, HEAD_DIM, rows), BF16)],
        compiler_params=_cparams(("parallel", "parallel", "parallel")),
        name="compress",
    )(r, w1, pe, w2, w2t, gain, cos, sin)


def _nsa_inproj_kernel(h_ref, g_ref, w_ref, wgt_ref, bgt_ref, qg_ref, c_ref, s_ref, q_ref, gate_ref, *, hd):
    xn = _row_rms(h_ref[0], g_ref[...]).astype(BF16)
    cos = c_ref[0]
    sin = s_ref[0]
    for j in range(hd // 256):
        y = _dot(xn, w_ref[:, 256 * j:256 * (j + 1)])
        for hh in range(2):
            blk = _pair_rope(_pair_rms(y[:, LANES * hh:LANES * (hh + 1)], qg_ref[...]), cos, sin) * Q_SCALE
            q_ref[0, :, 256 * j + LANES * hh:256 * j + LANES * (hh + 1)] = blk.astype(BF16)
    gate_ref[0] = jax.nn.sigmoid(_dot_nt(wgt_ref[...], xn) + bgt_ref[...])


def _nsa_inproj(h, g, w, wgt, bgt, qg, cos, sin, tm=ROW_TILE):
    B, T, D = h.shape
    hd = N_HEADS * HEAD_DIM
    vec = lambda n: pl.BlockSpec((1, n), lambda b, t: (0, 0))
    full = lambda a: pl.BlockSpec(a.shape, lambda b, t: (0,) * a.ndim)
    tab = pl.BlockSpec((1, tm, LANES), lambda b, t: (b, t, 0))
    return pl.pallas_call(
        functools.partial(_nsa_inproj_kernel, hd=hd),
        grid=(B, T // tm),
        in_specs=[pl.BlockSpec((1, tm, D), lambda b, t: (b, t, 0)), vec(D), full(w), full(wgt), full(bgt),
                  vec(LANES), tab, tab],
        out_specs=[pl.BlockSpec((1, tm, hd), lambda b, t: (b, t, 0)),
                   pl.BlockSpec((1, wgt.shape[0], tm), lambda b, t: (b, 0, t))],
        out_shape=[jax.ShapeDtypeStruct((B, T, hd), BF16), jax.ShapeDtypeStruct((B, wgt.shape[0], T), F32)],
        compiler_params=_cparams(("parallel", "parallel")),
        name="nsa_inproj",
    )(h, g, w, wgt, bgt, qg, cos, sin)


def _stack_heads(q, dst_ref, tq):
    lo = _iota((tq, LANES), 1) < HEAD_DIM
    zero = jnp.zeros((tq, LANES), q.dtype)
    for j in range(HEADS_PER_GROUP // 2):
        x = q[:, LANES * j:LANES * (j + 1)]
        dst_ref[(2 * j) * tq:(2 * j + 1) * tq, 0:LANES] = jnp.where(lo, x, zero)
        dst_ref[(2 * j + 1) * tq:(2 * j + 2) * tq, 0:LANES] = jnp.where(lo, zero, x)


def _unstack_heads_t(o_t, gate_ref, o_ref, tq):
    gate = gate_ref[0]
    for j in range(HEADS_PER_GROUP // 2):
        halves = [o_t[:, h * tq:(h + 1) * tq] * gate[h:h + 1, :] for h in (2 * j, 2 * j + 1)]
        o_ref[0, :, LANES * j:LANES * (j + 1)] = jnp.concatenate(halves, axis=0).T.astype(o_ref.dtype)


def _nsa_cmp_kernel(q_ref, kc_ref, vct_ref, ovt_ref, gate_ref, oc_ref, sel_ref, qs_sc, imp_sc, *, tq, n_slc, top_k, n_levels):
    qi = pl.program_id(2)
    _stack_heads(q_ref[0], qs_sc, tq)
    n_rows = kc_ref.shape[3]
    level_rows = n_rows // n_levels
    n_chunks = 2
    cw = HEADS_PER_GROUP * tq // n_chunks
    n_vis = (qi * tq + tq - 1 - (CMP_LEN - 1)) // CMP_STRIDE + 1

    def attend(rows):
        kc = kc_ref[0, 0, 0, :rows, :]
        vct = vct_ref[0, 0, 0, :, :rows]
        scores = [_dot_nt(kc, qs_sc[c * cw:(c + 1) * cw, :]) for c in range(n_chunks)]
        qpos = qi * tq + (_iota((rows, cw), 1) & (tq - 1))
        vis = _iota((rows, cw), 0) * CMP_STRIDE + (CMP_LEN - 1) <= qpos
        outs = []
        p_sum = None
        for s in scores:
            s = jnp.where(vis, s, NEG)
            e = jnp.where(vis, jnp.exp2(s - jnp.max(s, axis=0, keepdims=True)), 0.0)
            p = e * (1.0 / jnp.maximum(jnp.sum(e, axis=0, keepdims=True), 1.0))
            outs.append(_dot(vct, p.astype(BF16)))
            for h in range(cw // tq):
                term = p[:, h * tq:(h + 1) * tq]
                p_sum = term if p_sum is None else p_sum + term
        _unstack_heads_t(jnp.concatenate(outs, axis=1), gate_ref, oc_ref, tq)
        p_hi = p_sum.astype(BF16)
        p_lo = (p_sum - p_hi.astype(F32)).astype(BF16)
        imp_sc[...] = _dot(ovt_ref[:, :rows], p_hi) + _dot(ovt_ref[:, :rows], p_lo)

    for level in range(n_levels):
        lo_rows, hi_rows = level * level_rows, (level + 1) * level_rows
        cond = (n_vis <= hi_rows) if level == 0 else ((n_vis > lo_rows) & (n_vis <= hi_rows))
        pl.when(cond)(functools.partial(attend, hi_rows))

    imp = imp_sc[...]
    jb = _iota(imp.shape, 0)
    qp = qi * tq + _iota(imp.shape, 1)
    cur = qp // SLC_LEN
    forced = (jb == 0) | (jb == cur) | (jb == cur - 1)
    score = jnp.where(jb * SLC_LEN <= qp, jnp.where(forced, FORCED_SCORE, imp), NEG)
    score = jnp.where(jb < n_slc, score, -jnp.inf)
    bias = jnp.full(imp.shape, NEG, F32)
    for _ in range(top_k):
        mx = jnp.max(score, axis=0, keepdims=True)
        idx = jnp.min(jnp.where(score == mx, jb, LANES), axis=0, keepdims=True)
        pick = jb == idx
        bias = jnp.where(pick, 0.0, bias)
        score = jnp.where(pick, -jnp.inf, score)
    sel_ref[0, 0] = bias.T.astype(BF16)


def _gate_spec(branch, tq):
    return pl.BlockSpec((1, HEADS_PER_GROUP, tq), lambda b, g, qi: (b, branch * N_KV_GROUPS + g, qi))


def _nsa_cmp(q, kc_all, vct_all, overlap_t, gates_t, n_slc, tq=CMP_TQ):
    B, T, hd = q.shape
    G = N_KV_GROUPS
    rows = kc_all.shape[3]
    gw = hd // G
    top_k = min(SLC_TOPK, n_slc)
    n_levels = 4 if rows % (4 * LANES) == 0 else 1
    return pl.pallas_call(
        functools.partial(_nsa_cmp_kernel, tq=tq, n_slc=n_slc, top_k=top_k, n_levels=n_levels),
        grid=(B, G, T // tq),
        in_specs=[pl.BlockSpec((1, tq, gw), lambda b, g, qi: (b, qi, g)),
                  pl.BlockSpec((1, 1, 1, rows, LANES), lambda b, g, qi: (b, 0, g, 0, 0)),
                  pl.BlockSpec((1, 1, 1, HEAD_DIM, rows), lambda b, g, qi: (b, 1, g, 0, 0)),
                  pl.BlockSpec(overlap_t.shape, lambda b, g, qi: (0, 0)), _gate_spec(0, tq)],
        out_specs=[pl.BlockSpec((1, tq, gw), lambda b, g, qi: (b, qi, g)),
                   pl.BlockSpec((1, 1, tq, LANES), lambda b, g, qi: (b, g, qi, 0))],
        out_shape=[jax.ShapeDtypeStruct((B, T, hd), BF16), jax.ShapeDtypeStruct((B, G, T, LANES), BF16)],
        scratch_shapes=[pltpu.VMEM((HEADS_PER_GROUP * tq, LANES), BF16), pltpu.VMEM((LANES, tq), F32)],
        compiler_params=_cparams(("parallel", "parallel", "parallel")),
        name="nsa_cmp",
    )(q, kc_all, vct_all, overlap_t, gates_t)


def _nsa_slc_kernel(q_ref, sel_ref, k_ref, oh_ref, vt_ref, gate_ref, o_ref, qa_sc, s_sc, mt_sc, m_sc, acc_sc, *, tq, tk, n_chunks):
    qi = pl.program_id(2)
    last = (qi * tq) // tk
    _stack_heads(q_ref[0], qa_sc, tq)
    sel = sel_ref[0, 0]
    for h in range(HEADS_PER_GROUP):
        qa_sc[h * tq:(h + 1) * tq, LANES:2 * LANES] = sel
    _softmax_init(m_sc, acc_sc)
    cw = HEADS_PER_GROUP * tq // n_chunks

    def produce(ki, slot):
        start = pl.multiple_of(ki * tk, tk)
        ka = jnp.concatenate([k_ref[0, 0, pl.ds(start, tk), :], oh_ref[pl.ds(start, tk), :]], axis=1)
        for c in range(n_chunks):
            s = _dot_nt(ka, qa_sc[c * cw:(c + 1) * cw, :])
            s_sc[slot, :, c * cw:(c + 1) * cw] = s
            mt_sc[slot, :, c * cw:(c + 1) * cw] = _column_max(s)

    def consume(ki, slot, diag):
        vt = vt_ref[0, 0, ki]
        for c in range(n_chunks):
            cols = slice(c * cw, (c + 1) * cw)
            s = s_sc[slot, :, cols]
            m_tile = mt_sc[slot, :, cols]
            if diag:
                kpos = ki * tk + _iota(s.shape, 0)
                qpos = qi * tq + (_iota(s.shape, 1) & (tq - 1))
                s = jnp.where(kpos <= qpos, s, NEG)
                m_tile = None
            _softmax_update_t(s, vt, m_sc.at[:, cols], acc_sc.at[:, cols], m_tile)

    _pipelined_key_loop(last, produce, consume)
    _unstack_heads_t(_softmax_result(acc_sc), gate_ref, o_ref, tq)


def _nsa_slc(q, sel, ks, vst, gates_t, tq=SLC_TQ, tk=SLC_TK):
    B, T, hd = q.shape
    G = N_KV_GROUPS
    gw = hd // G
    cols = HEADS_PER_GROUP * tq
    assert tk % tq == 0
    onehot = (jnp.arange(T)[:, None] // SLC_LEN == jnp.arange(LANES)[None, :]).astype(BF16)
    return pl.pallas_call(
        functools.partial(_nsa_slc_kernel, tq=tq, tk=tk, n_chunks=4),
        grid=(B, G, T // tq),
        in_specs=[pl.BlockSpec((1, tq, gw), lambda b, g, qi: (b, qi, g)),
                  pl.BlockSpec((1, 1, tq, LANES), lambda b, g, qi: (b, g, qi, 0)),
                  pl.BlockSpec((1, 1, T, LANES), lambda b, g, qi: (b, g, 0, 0)),
                  pl.BlockSpec((T, LANES), lambda b, g, qi: (0, 0)),
                  pl.BlockSpec((1, 1, T // tk, HEAD_DIM, tk), lambda b, g, qi: (b, g, 0, 0, 0)), _gate_spec(1, tq)],
        out_specs=pl.BlockSpec((1, tq, gw), lambda b, g, qi: (b, qi, g)),
        out_shape=jax.ShapeDtypeStruct((B, T, hd), BF16),
        scratch_shapes=[pltpu.VMEM((cols, 2 * LANES), BF16), pltpu.VMEM((2, tk, cols), F32),
                        pltpu.VMEM((2, 1, cols), F32), pltpu.VMEM((1, cols), F32),
                        pltpu.VMEM((ACC_ROWS, cols), F32)],
        compiler_params=_cparams(("parallel", "parallel", "arbitrary")),
        name="nsa_slc",
    )(q, sel, ks, onehot, vst, gates_t)


def _nsa_win_kernel(q_ref, k_ref, vt_ref, gate_ref, o_ref, qs_sc, s_sc, m_sc, acc_sc, *, tq, n_back, n_chunks):
    qi = pl.program_id(2)
    _stack_heads(q_ref[0], qs_sc, tq)
    _softmax_init(m_sc, acc_sc)
    cw = HEADS_PER_GROUP * tq // n_chunks

    def produce(ti, slot):
        start = pl.multiple_of(ti * tq, tq)
        k = k_ref[0, 0, pl.ds(start, tq), :]
        for c in range(n_chunks):
            s_sc[slot, :, c * cw:(c + 1) * cw] = _dot_nt(k, qs_sc[c * cw:(c + 1) * cw, :])

    def consume(ti, slot, kind):
        vt = vt_ref[0, 0, ti]
        for c in range(n_chunks):
            cols = slice(c * cw, (c + 1) * cw)
            s = s_sc[slot, :, cols]
            r = _iota(s.shape, 0)
            q_in_tile = _iota(s.shape, 1) & (tq - 1)
            if kind == "oldest":
                s = jnp.where(r > q_in_tile, s, NEG)
            elif kind == "diag":
                s = jnp.where(r <= q_in_tile, s, NEG)
            _softmax_update_t(s, vt, m_sc.at[:, cols], acc_sc.at[:, cols])

    def kind_of(back):
        return "oldest" if back == n_back else ("diag" if back == 0 else "full")

    for first in range(n_back + 1):
        cond = (qi >= n_back) if first == n_back else (qi == first)

        @pl.when(cond)
        def _(first=first):
            backs = list(range(first, -1, -1))
            produce(qi - backs[0], 0)
            for n, back in enumerate(backs):
                if n + 1 < len(backs):
                    produce(qi - backs[n + 1], (n + 1) % 2)
                consume(qi - back, n % 2, kind_of(back))

    _unstack_heads_t(_softmax_result(acc_sc), gate_ref, o_ref, tq)


def _nsa_win(q, kw, vwt, gates_t, tq=WIN_TQ):
    B, T, hd = q.shape
    G = N_KV_GROUPS
    gw = hd // G
    cols = HEADS_PER_GROUP * tq
    return pl.pallas_call(
        functools.partial(_nsa_win_kernel, tq=tq, n_back=WIN // tq, n_chunks=4),
        grid=(B, G, T // tq),
        in_specs=[pl.BlockSpec((1, tq, gw), lambda b, g, qi: (b, qi, g)),
                  pl.BlockSpec((1, 1, T, LANES), lambda b, g, qi: (b, g, 0, 0)),
                  pl.BlockSpec((1, 1, T // tq, HEAD_DIM, tq), lambda b, g, qi: (b, g, 0, 0, 0)), _gate_spec(2, tq)],
        out_specs=pl.BlockSpec((1, tq, gw), lambda b, g, qi: (b, qi, g)),
        out_shape=jax.ShapeDtypeStruct((B, T, hd), BF16),
        scratch_shapes=[pltpu.VMEM((cols, LANES), BF16), pltpu.VMEM((2, tq, cols), F32), pltpu.VMEM((1, cols), F32),
                        pltpu.VMEM((ACC_ROWS, cols), F32)],
        compiler_params=_cparams(("parallel", "parallel", "arbitrary")),
        name="nsa_win",
    )(q, kw, vwt, gates_t)


def _pad_cols(a, n):
    return jnp.pad(a, ((0, 0), (0, n - a.shape[1])))


def _lane_vec(v):
    return jnp.tile(v.astype(F32), 2).reshape(1, LANES)


def _bias_feature_selector():
    sel = np.zeros((N_BIAS_PARTS * LANES, N_HEADS * HEAD_DIM), np.float32)
    for part in range(N_BIAS_PARTS):
        for h in range(N_HEADS):
            sel[part * LANES + h, (h // 2) * LANES + (h % 2) * N_BIAS_PARTS + part] = 1.0
    return jnp.asarray(sel, BF16)


def kernel(x, positions, a_norm, a_w_in, a_b_f, a_q_gain, a_k_gain, a_w_out, kv_norm, kv_w, kc_pe, vc_pe, kc_w1, kc_w2, vc_w1, vc_w2, kc_gain, ks_gain, kw_gain, b_norm, b_w_in, b_b_gate, b_q_gain, b_w_out, f_norm, f_w_up, f_conv_w, f_conv_b, f_w_down):
    B, T, D = x.shape
    hd = N_HEADS * HEAD_DIM
    G = N_KV_GROUPS
    n_a = a_norm.shape[0]
    n_b = b_norm.shape[0]
    depth = n_a + n_b
    n_slc = T // SLC_LEN
    n_cmp = (T - CMP_LEN) // CMP_STRIDE + 1
    assert T % 1024 == 0 and n_slc <= LANES and hd == 1024 and D == 1024

    half = ROT_DIM // 2
    inv = ROPE_THETA ** (-jnp.arange(half, dtype=F32) * (2.0 / ROT_DIM))
    head_inv = jnp.concatenate([inv, inv, jnp.zeros((HEAD_DIM - ROT_DIM,), F32)])
    head_sign = jnp.concatenate([-jnp.ones((half,), F32), jnp.ones((half,), F32), jnp.zeros((HEAD_DIM - ROT_DIM,), F32)])
    inv_lane = jnp.tile(head_inv, 2).reshape(1, LANES)
    sign_lane = jnp.tile(head_sign, 2).reshape(1, LANES)
    cos_t, sin_t = _rope_tables(positions, inv_lane, sign_lane)
    end_pos = positions[:, CMP_LEN - 1::CMP_STRIDE]
    end_pos = jnp.pad(end_pos, ((0, 0), (0, T // CMP_STRIDE - n_cmp)))
    cos_c, sin_c = _rope_tables(end_pos, inv_lane, sign_lane)

    cs = jnp.arange(T // CMP_STRIDE) * CMP_STRIDE
    ss = jnp.arange(LANES) * SLC_LEN
    overlap_t = (jnp.maximum(jnp.minimum(cs[None, :] + CMP_LEN, ss[:, None] + SLC_LEN)
                             - jnp.maximum(cs[None, :], ss[:, None]), 0).astype(F32) / CMP_LEN).astype(BF16)

    h = x
    kv = None
    for layer in range(depth):
        if layer < n_a:
            i = layer
            w = jnp.concatenate([a_w_in[i][:, :2 * hd], _pad_cols(a_w_in[i][:, 3 * hd:], LANES)], axis=1).astype(BF16)
            wvt = a_w_in[i][:, 2 * hd:3 * hd].T.astype(BF16)
            bf = _pad_cols(a_b_f[i].reshape(1, -1), LANES)
            q, k, vt, cf = _fox_inproj(h, a_norm[i].reshape(1, D), w, wvt, bf, _lane_vec(a_q_gain[i]),
                                       _lane_vec(a_k_gain[i]), _bias_feature_selector())
            o = _fox_attn(q, k, cf, vt)
            h = _outproj(h, o, a_w_out[i].astype(BF16))
        else:
            i = layer - n_a
            kc_all, vct_all, ks, kw, vst, vwt = kv
            q, gates_t = _nsa_inproj(h, b_norm[i].reshape(1, D), b_w_in[i][:, :hd].astype(BF16),
                                     b_w_in[i][:, hd:].T.astype(BF16), b_b_gate[i].reshape(-1, 1),
                                     _lane_vec(b_q_gain[i]), cos_t, sin_t)
            o_c, sel = _nsa_cmp(q, kc_all, vct_all, overlap_t, gates_t, n_slc)
            o_s = _nsa_slc(q, sel, ks, vst, gates_t)
            o_w = _nsa_win(q, kw, vwt, gates_t)
            h = _nsa_outproj(h, o_c, o_s, o_w, b_w_out[i].astype(BF16))
        h = _conv_ffn(h, f_norm[layer].reshape(1, D), f_w_up[layer].astype(BF16), f_conv_w[layer],
                      f_conv_b[layer].reshape(1, -1), f_w_down[layer].astype(BF16))
        if layer == n_a - 1:
            w6 = kv_w.reshape(D, 6, G, HEAD_DIM)
            raw_cols = w6[:, 0:2].reshape(D, 2 * G * HEAD_DIM)
            wk = w6[:, (2, 4)]
            dup_cols = jnp.concatenate([wk, wk], axis=-1).reshape(D, 2 * G * LANES)
            wkv = jnp.concatenate([raw_cols, dup_cols], axis=1).astype(BF16)
            wvt = w6[:, (3, 5)].reshape(D, 2 * G * HEAD_DIM).T.astype(BF16)
            craw, ks, kw, vst, vwt = _kvproj(h, kv_norm.reshape(1, D), wkv, wvt, _lane_vec(ks_gain),
                                             _lane_vec(kw_gain), cos_t, sin_t)
            r = craw.reshape(B, T, 2, G, HEAD_DIM).transpose(0, 2, 3, 1, 4).reshape(B, 2, G, T // CMP_STRIDE, CMP_STRIDE * HEAD_DIM)
            w1 = jnp.stack([kc_w1, vc_w1]).astype(BF16)
            pe = jnp.stack([kc_pe.reshape(1, -1), vc_pe.reshape(1, -1)])
            pe = jnp.pad(pe, ((0, 0), (0, 7), (0, 0))).astype(BF16)
            w2 = jnp.stack([jnp.concatenate([kc_w2, kc_w2], axis=1), jnp.concatenate([vc_w2, vc_w2], axis=1)]).astype(BF16)
            w2t = jnp.stack([kc_w2.T, vc_w2.T]).astype(BF16)
            kc_all, vct_all = _compress(r, w1, pe, w2, w2t, _lane_vec(kc_gain), cos_c, sin_c, n_cmp)
            kv = (kc_all, vct_all, ks, kw, vst, vwt)
    return h
```

```python
import functools

import numpy as np
import jax
import jax.numpy as jnp
from jax import lax
from jax.experimental import pallas as pl
from jax.experimental.pallas import tpu as pltpu

F32 = jnp.float32
BF16 = jnp.bfloat16

LANES = 128
SUBLANES = 8
HEAD_DIM = 64
N_HEADS = 16
N_PAIRS = N_HEADS // 2
N_KV_GROUPS = 2
HEADS_PER_GROUP = N_HEADS // N_KV_GROUPS
ROT_DIM = HEAD_DIM // 4
ROPE_THETA = 500000.0
CMP_LEN = 32
CMP_STRIDE = 16
SLC_LEN = 64
SLC_TOPK = 16
WIN = 512
CONV_W = 3
RMS_EPS = 1e-6
NEG = -1e30
FORCED_SCORE = 1e6
LOG2E = 1.4426950408889634
Q_SCALE = HEAD_DIM ** -0.5 * LOG2E
N_BIAS_PARTS = 3
ONES_ROWS = 16
ACC_ROWS = HEAD_DIM + ONES_ROWS

ROW_TILE = 512
FOX_TQ = 512
FOX_TK = ROW_TILE
SLC_TQ = 256
SLC_TK = ROW_TILE
WIN_TQ = 256
CMP_TQ = 256

VMEM_LIMIT = 48 * 1024 * 1024


def _cparams(sem, flags=None):
    return pltpu.CompilerParams(dimension_semantics=sem, vmem_limit_bytes=VMEM_LIMIT, flags=flags)


def _iota(shape, axis):
    return lax.broadcasted_iota(jnp.int32, shape, axis)


def _row_rms(x, g):
    ms = jnp.mean(x * x, axis=-1, keepdims=True)
    return x * lax.rsqrt(ms + RMS_EPS) * g


def _pair_rms(y, gain):
    lo = _iota(y.shape, 1) < HEAD_DIM
    y2 = y * y
    s_lo = jnp.sum(jnp.where(lo, y2, 0.0), axis=-1, keepdims=True)
    s_hi = jnp.sum(jnp.where(lo, 0.0, y2), axis=-1, keepdims=True)
    ms = jnp.where(lo, s_lo, s_hi) * (1.0 / HEAD_DIM)
    return y * lax.rsqrt(ms + RMS_EPS) * gain


def _pair_rope(y, cos, sin):
    lane = _iota(y.shape, 1) & (HEAD_DIM - 1)
    partner = jnp.where(lane < ROT_DIM // 2,
                        pltpu.roll(y, LANES - ROT_DIM // 2, 1),
                        pltpu.roll(y, ROT_DIM // 2, 1))
    return y * cos + partner * sin


def _dot(a, b):
    return jnp.dot(a, b, preferred_element_type=F32)


def _dot_nt(a, b):
    return lax.dot_general(a, b, (((1,), (1,)), ((), ())), preferred_element_type=F32)


def _column_max(s):
    tk, n = s.shape
    return jnp.max(jnp.max(s.reshape(tk // SUBLANES, SUBLANES, n), axis=0), axis=0, keepdims=True)


def _softmax_update_t(s, vt, m_ref, acc_ref, m_tile=None):
    tk, n = s.shape
    m_old = m_ref[...]
    if m_tile is None:
        m_tile = _column_max(s)
    m_new = jnp.maximum(m_old, m_tile)
    alpha = jnp.exp2(m_old - m_new)
    p = jnp.exp2(s - m_new).astype(BF16)
    vta = jnp.concatenate([vt, jnp.ones((ONES_ROWS, tk), vt.dtype)], axis=0)
    acc_ref[...] = alpha * acc_ref[...] + _dot(vta, p)
    m_ref[...] = m_new


def _pipelined_key_loop(n_full, produce, consume, unroll=4):
    assert unroll % 2 == 0

    def run(base, count):
        for i in range(count):
            produce(base + i + 1, (i + 1) % 2)
            consume(base + i, i % 2, False)

    produce(0, 0)

    def body(j, carry):
        run(unroll * j, unroll)
        return carry

    lax.fori_loop(0, n_full // unroll, body, 0)
    base = (n_full // unroll) * unroll
    rem = n_full - base
    step = unroll // 2
    while step >= 2:
        pl.when((rem & step) != 0)(functools.partial(run, base, step))
        base = base + (rem & step)
        step //= 2

    @pl.when((rem & 1) != 0)
    def _():
        run(base, 1)
        consume(base + 1, 1, True)

    @pl.when((rem & 1) == 0)
    def _():
        consume(base, 0, True)


def _softmax_init(m_ref, acc_ref):
    m_ref[...] = jnp.full_like(m_ref, NEG)
    acc_ref[...] = jnp.zeros_like(acc_ref)


def _softmax_result(acc_ref):
    acc = acc_ref[...]
    return acc[:HEAD_DIM, :] * (1.0 / acc[HEAD_DIM:HEAD_DIM + 1, :])


def _rope_table_kernel(pos_ref, inv_ref, sign_ref, c_ref, s_ref):
    ang = pos_ref[0].astype(F32) * inv_ref[...]
    c_ref[0] = jnp.cos(ang)
    s_ref[0] = jnp.sin(ang) * sign_ref[...]


def _rope_tables(pos, inv_lane, sign_lane):
    B, T = pos.shape
    tm = min(T, ROW_TILE)
    spec = pl.BlockSpec((1, tm, LANES), lambda b, t: (b, t, 0))
    vec = pl.BlockSpec((1, LANES), lambda b, t: (0, 0))
    return pl.pallas_call(
        _rope_table_kernel,
        grid=(B, T // tm),
        in_specs=[pl.BlockSpec((1, tm, 1), lambda b, t: (b, t, 0)), vec, vec],
        out_specs=[spec, spec],
        out_shape=[jax.ShapeDtypeStruct((B, T, LANES), F32)] * 2,
        compiler_params=_cparams(("parallel", "parallel")),
        name="rope_tables",
    )(pos.reshape(B, T, 1), inv_lane, sign_lane)


def _fox_inproj_kernel(x_ref, g_ref, w_ref, wvt_ref, bf_ref, qg_ref, kg_ref, sel_ref,
                       q_ref, k_ref, vt_ref, cf_ref, carry_sc, *, tm, hd):
    ti = pl.program_id(1)
    xn = _row_rms(x_ref[0], g_ref[...]).astype(BF16)
    for j in range(hd // 256):
        for part, (ref, gain, mul) in enumerate(((q_ref, qg_ref, Q_SCALE), (k_ref, kg_ref, 1.0))):
            c0 = part * hd + 256 * j
            y = _dot(xn, w_ref[:, c0:c0 + 256])
            for hh in range(2):
                blk = _pair_rms(y[:, LANES * hh:LANES * (hh + 1)], gain[...]) * mul
                ref[0, :, 256 * j + LANES * hh:256 * j + LANES * (hh + 1)] = blk.astype(BF16)
        yt = _dot_nt(wvt_ref[256 * j:256 * (j + 1), :], xn)
        for hh in range(2):
            vt_ref[0, 2 * j + hh, 0] = yt[LANES * hh:LANES * (hh + 1), :].astype(BF16)
    z = _dot(xn, w_ref[:, 2 * hd:2 * hd + LANES]) + bf_ref[...]
    lf = jnp.minimum(z, 0.0) - jnp.log1p(jnp.exp(-jnp.abs(z)))
    row = _iota(lf.shape, 0)
    sh = 1
    while sh < tm:
        lf = lf + jnp.where(row >= sh, pltpu.roll(lf, sh, 0), 0.0)
        sh *= 2

    @pl.when(ti == 0)
    def _():
        carry_sc[...] = jnp.zeros_like(carry_sc)

    c = lf + carry_sc[0:1, :]
    carry_sc[...] = jnp.broadcast_to(c[tm - 1:tm, :], carry_sc.shape)
    rest = c * (-LOG2E)
    pieces = []
    for _ in range(N_BIAS_PARTS):
        piece = rest.astype(BF16)
        pieces.append(piece)
        rest = rest - piece.astype(F32)
    cf_ref[0] = _dot(jnp.concatenate(pieces, axis=1), sel_ref[...]).astype(BF16)


def _fox_inproj(x, g, w, wvt, bf, qg, kg, sel, tm=ROW_TILE):
    B, T, D = x.shape
    hd = N_HEADS * HEAD_DIM
    act = pl.BlockSpec((1, tm, hd), lambda b, t: (b, t, 0))
    vec = lambda n: pl.BlockSpec((1, n), lambda b, t: (0, 0))
    full = lambda a: pl.BlockSpec(a.shape, lambda b, t: (0,) * a.ndim)
    return pl.pallas_call(
        functools.partial(_fox_inproj_kernel, tm=tm, hd=hd),
        grid=(B, T // tm),
        in_specs=[pl.BlockSpec((1, tm, D), lambda b, t: (b, t, 0)), vec(D), full(w), full(wvt),
                  vec(LANES), vec(LANES), vec(LANES), full(sel)],
        out_specs=[act, act, pl.BlockSpec((1, N_PAIRS, 1, LANES, tm), lambda b, t: (b, 0, t, 0, 0)), act],
        out_shape=[jax.ShapeDtypeStruct((B, T, hd), BF16), jax.ShapeDtypeStruct((B, T, hd), BF16),
                   jax.ShapeDtypeStruct((B, N_PAIRS, T // tm, LANES, tm), BF16),
                   jax.ShapeDtypeStruct((B, T, hd), BF16)],
        scratch_shapes=[pltpu.VMEM((SUBLANES, LANES), F32)],
        compiler_params=_cparams(("arbitrary", "arbitrary")),
        name="fox_inproj",
    )(x, g, w, wvt, bf, qg, kg, sel)


def _fox_attn_kernel(q_ref, k_ref, cf_ref, vt_ref, o_ref, qa_sc, s_sc, mt_sc, m_sc, acc_sc, *, tq, tk):
    qi = pl.program_id(2)
    q = q_ref[0]
    lane = _iota(q.shape, 1)
    lo = lane < HEAD_DIM
    zero = jnp.zeros_like(q)
    for hh in range(2):
        qa_sc[hh, :, 0:LANES] = jnp.where(lo, q, zero) if hh == 0 else jnp.where(lo, zero, q)
        feat = jnp.where(lane < N_BIAS_PARTS * (hh + 1), 1.0, 0.0)
        qa_sc[hh, :, LANES:2 * LANES] = jnp.where(lane >= N_BIAS_PARTS * hh, feat, 0.0).astype(BF16)
        _softmax_init(m_sc.at[hh], acc_sc.at[hh])

    def produce(ki, slot):
        start = pl.multiple_of(ki * tk, tk)
        ka = jnp.concatenate([k_ref[0, pl.ds(start, tk), :], cf_ref[0, pl.ds(start, tk), :]], axis=1)
        for hh in range(2):
            s = _dot_nt(ka, qa_sc[hh])
            s_sc[slot, hh] = s
            mt_sc[slot, hh] = _column_max(s)

    def consume(ki, slot, diag):
        vt = vt_ref[0, 0, ki]
        for hh in range(2):
            s = s_sc[slot, hh]
            m_tile = mt_sc[slot, hh]
            if diag:
                s = jnp.where(_iota(s.shape, 0) <= _iota(s.shape, 1), s, NEG)
                m_tile = None
            _softmax_update_t(s, vt[HEAD_DIM * hh:HEAD_DIM * (hh + 1), :], m_sc.at[hh], acc_sc.at[hh], m_tile)

    _pipelined_key_loop(qi, produce, consume)
    o_t = jnp.concatenate([_softmax_result(acc_sc.at[hh]) for hh in range(2)], axis=0)
    o_ref[0] = o_t.T.astype(BF16)


def _fox_attn(q, k, cf, vt, tq=FOX_TQ, tk=FOX_TK):
    B, T, hd = q.shape
    assert tq == tk
    nk = T // tk
    seq = pl.BlockSpec((1, T, LANES), lambda b, p, qi: (b, 0, p))
    return pl.pallas_call(
        functools.partial(_fox_attn_kernel, tq=tq, tk=tk),
        grid=(B, N_PAIRS, T // tq),
        in_specs=[pl.BlockSpec((1, tq, LANES), lambda b, p, qi: (b, qi, p)), seq, seq,
                  pl.BlockSpec((1, 1, nk, LANES, tk), lambda b, p, qi: (b, p, 0, 0, 0))],
        out_specs=pl.BlockSpec((1, tq, LANES), lambda b, p, qi: (b, qi, p)),
        out_shape=jax.ShapeDtypeStruct((B, T, hd), BF16),
        scratch_shapes=[pltpu.VMEM((2, tq, 2 * LANES), BF16), pltpu.VMEM((2, 2, tk, tq), F32),
                        pltpu.VMEM((2, 2, 1, tq), F32), pltpu.VMEM((2, 1, tq), F32),
                        pltpu.VMEM((2, ACC_ROWS, tq), F32)],
        compiler_params=_cparams(("parallel", "parallel", "arbitrary")),
        name="fox_attn",
    )(q, k, cf, vt)


def _ffn_kernel(*refs, n_mix, tt, tf, d_ff):
    h_ref = refs[0]
    mix_refs = refs[1:1 + n_mix]
    wo_ref, g_ref, wup_ref, cw_ref, cb_ref, wd_ref, out_ref, a_sc, carry_sc = refs[1 + n_mix:]
    ti = pl.program_id(1)

    @pl.when(ti == 0)
    def _():
        carry_sc[...] = jnp.zeros_like(carry_sc)

    o = mix_refs[0][0]
    if n_mix > 1:
        o = o.astype(F32)
        for ref in mix_refs[1:]:
            o = o + ref[0].astype(F32)
        o = o.astype(BF16)
    x = h_ref[0] + _dot(o, wo_ref[...])
    xn = _row_rms(x, g_ref[...]).astype(BF16)
    row8 = _iota((SUBLANES, tf), 0)

    def conv(u, c0):
        prev8 = carry_sc[:, c0:c0 + tf]
        um1 = pltpu.roll(u, 1, 0)
        um2 = pltpu.roll(u, 2, 0)
        top1 = jnp.where(row8 == 0, prev8[7:8, :], um1[0:SUBLANES, :])
        top2 = jnp.where(row8 == 0, prev8[6:7, :], jnp.where(row8 == 1, prev8[7:8, :], um2[0:SUBLANES, :]))
        um1 = jnp.concatenate([top1, um1[SUBLANES:, :]], axis=0)
        um2 = jnp.concatenate([top2, um2[SUBLANES:, :]], axis=0)
        carry_sc[:, c0:c0 + tf] = u[tt - SUBLANES:, :]
        cw = cw_ref[:, c0:c0 + tf]
        return cb_ref[:, c0:c0 + tf] + cw[0:1, :] * um2 + cw[1:2, :] * um1 + cw[2:3, :] * u

    for f in range(d_ff // tf):
        g0 = f * tf
        cg = conv(_dot(xn, wup_ref[:, g0:g0 + tf]), g0)
        cv = conv(_dot(xn, wup_ref[:, d_ff + g0:d_ff + g0 + tf]), d_ff + g0)
        a_sc[:, g0:g0 + tf] = (cg * jax.nn.sigmoid(cg) * cv).astype(BF16)
    out_ref[0] = x + _dot(a_sc[...], wd_ref[...])


def _mix_out_conv_ffn(h, mix, w_out, g, w_up, conv_w, conv_b, w_down, tt=ROW_TILE, tf=256):
    B, T, D = h.shape
    d_ff = w_down.shape[0]
    act = lambda c: pl.BlockSpec((1, tt, c), lambda b, t: (b, t, 0))
    resident = lambda a: pl.BlockSpec(a.shape, lambda b, t: (0,) * a.ndim, pipeline_mode=pl.Buffered(1))
    return pl.pallas_call(
        functools.partial(_ffn_kernel, n_mix=len(mix), tt=tt, tf=tf, d_ff=d_ff),
        grid=(B, T // tt),
        in_specs=[act(D)] + [act(m.shape[-1]) for m in mix]
                 + [resident(w_out), resident(g), resident(w_up), resident(conv_w), resident(conv_b), resident(w_down)],
        out_specs=act(D),
        out_shape=jax.ShapeDtypeStruct((B, T, D), F32),
        scratch_shapes=[pltpu.VMEM((tt, d_ff), BF16), pltpu.VMEM((SUBLANES, 2 * d_ff), F32)],
        compiler_params=_cparams(("arbitrary", "arbitrary")),
        name="conv_ffn",
    )(h, *mix, w_out, g, w_up, conv_w, conv_b, w_down)


def _kvproj_kernel(h_ref, g_ref, w_ref, wvt_ref, ksg_ref, kwg_ref, c_ref, s_ref,
                   craw_ref, ks_ref, kw_ref, vst_ref, vwt_ref, *, tm):
    xn = _row_rms(h_ref[0], g_ref[...]).astype(BF16)
    cos = c_ref[0]
    sin = s_ref[0]
    craw_ref[0] = _dot(xn, w_ref[:, 0:256])
    for idx, (ref, gain) in enumerate(((ks_ref, ksg_ref), (kw_ref, kwg_ref))):
        y = _dot(xn, w_ref[:, 256 * (idx + 1):256 * (idx + 2)])
        for grp in range(N_KV_GROUPS):
            blk = _pair_rope(_pair_rms(y[:, LANES * grp:LANES * (grp + 1)], gain[...]), cos, sin)
            ref[0, grp] = blk.astype(BF16)
    yt = _dot_nt(wvt_ref[...], xn).astype(BF16)
    for grp in range(N_KV_GROUPS):
        vst_ref[0, grp, 0] = yt[HEAD_DIM * grp:HEAD_DIM * (grp + 1), :]
        r0 = HEAD_DIM * (N_KV_GROUPS + grp)
        for c in range(tm // WIN_TQ):
            vwt_ref[0, grp, c] = yt[r0:r0 + HEAD_DIM, WIN_TQ * c:WIN_TQ * (c + 1)]


def _kvproj(h, g, w, wvt, ksg, kwg, cos, sin, tm=ROW_TILE):
    B, T, D = h.shape
    G = N_KV_GROUPS
    vec = lambda n: pl.BlockSpec((1, n), lambda b, t: (0, 0))
    full = lambda a: pl.BlockSpec(a.shape, lambda b, t: (0,) * a.ndim)
    tab = pl.BlockSpec((1, tm, LANES), lambda b, t: (b, t, 0))
    dup = pl.BlockSpec((1, G, tm, LANES), lambda b, t: (b, 0, t, 0))
    dup_shape = jax.ShapeDtypeStruct((B, G, T, LANES), BF16)
    nw = tm // WIN_TQ
    return pl.pallas_call(
        functools.partial(_kvproj_kernel, tm=tm),
        grid=(B, T // tm),
        in_specs=[pl.BlockSpec((1, tm, D), lambda b, t: (b, t, 0)), vec(D), full(w), full(wvt),
                  vec(LANES), vec(LANES), tab, tab],
        out_specs=[pl.BlockSpec((1, tm, 256), lambda b, t: (b, t, 0)), dup, dup,
                   pl.BlockSpec((1, G, 1, HEAD_DIM, tm), lambda b, t: (b, 0, t, 0, 0)),
                   pl.BlockSpec((1, G, nw, HEAD_DIM, WIN_TQ), lambda b, t: (b, 0, t, 0, 0))],
        out_shape=[jax.ShapeDtypeStruct((B, T, 256), F32), dup_shape, dup_shape,
                   jax.ShapeDtypeStruct((B, G, T // tm, HEAD_DIM, tm), BF16),
                   jax.ShapeDtypeStruct((B, G, T // WIN_TQ, HEAD_DIM, WIN_TQ), BF16)],
        compiler_params=_cparams(("parallel", "parallel")),
        name="kvproj",
    )(h, g, w, wvt, ksg, kwg, cos, sin)


def _compress_kernel(r_ref, w1_ref, pe_ref, w2_ref, w2t_ref, gain_ref, c_ref, s_ref, kc_ref, vct_ref, *, n_cmp):
    r = r_ref[0, 0, 0].astype(BF16)
    half = r.shape[1]
    a = _dot(r, w1_ref[0, :half, :])
    b = _dot(r, w1_ref[0, half:, :])
    peb = _dot(pe_ref[0], w1_ref[0])[0:1, :]
    rows = r.shape[0]
    hid = a + pltpu.roll(b, rows - 1, 0) + peb
    act = jax.nn.gelu(hid).astype(BF16)
    y = _dot(act, w2_ref[0])
    yk = _pair_rope(_pair_rms(y, gain_ref[...]), c_ref[0], s_ref[0])
    kc_ref[0, 0, 0] = jnp.where(_iota(y.shape, 0) < n_cmp, yk, 0.0).astype(BF16)
    yt = _dot_nt(w2t_ref[0], act)
    vct_ref[0, 0, 0] = jnp.where(_iota(yt.shape, 1) < n_cmp, yt, 0.0).astype(BF16)


def _compress(r, w1, pe, w2, w2t, gain, cos, sin, n_cmp):
    B, _, G, rows, width = r.shape
    per_kv = lambda a: pl.BlockSpec((1,) + a.shape[1:], lambda b, kv, g: (kv,) + (0,) * (a.ndim - 1))
    return pl.pallas_call(
        functools.partial(_compress_kernel, n_cmp=n_cmp),
        grid=(B, 2, G),
        in_specs=[pl.BlockSpec((1, 1, 1, rows, width), lambda b, kv, g: (b, kv, g, 0, 0)),
                  per_kv(w1), per_kv(pe), per_kv(w2), per_kv(w2t),
                  pl.BlockSpec((1, LANES), lambda b, kv, g: (0, 0)),
                  pl.BlockSpec((1, rows, LANES), lambda b, kv, g: (b, 0, 0)),
                  pl.BlockSpec((1, rows, LANES), lambda b, kv, g: (b, 0, 0))],
        out_specs=[pl.BlockSpec((1, 1, 1, rows, LANES), lambda b, kv, g: (b, kv, g, 0, 0)),
                   pl.BlockSpec((1, 1, 1, HEAD_DIM, rows), lambda b, kv, g: (b, kv, g, 0, 0))],
        out_shape=[jax.ShapeDtypeStruct((B, 2, G, rows, LANES), BF16),
                   jax.ShapeDtypeStruct((B, 2, G, HEAD_DIM, rows), BF16)],
        compiler_params=_cparams(("parallel", "parallel", "parallel")),
        name="compress",
    )(r, w1, pe, w2, w2t, gain, cos, sin)


def _nsa_inproj_kernel(h_ref, g_ref, w_ref, wgt_ref, bgt_ref, qg_ref, c_ref, s_ref, q_ref, gate_ref, *, hd):
    xn = _row_rms(h_ref[0], g_ref[...]).astype(BF16)
    cos = c_ref[0]
    sin = s_ref[0]
    for j in range(hd // 256):
        y = _dot(xn, w_ref[:, 256 * j:256 * (j + 1)])
        for hh in range(2):
            blk = _pair_rope(_pair_rms(y[:, LANES * hh:LANES * (hh + 1)], qg_ref[...]), cos, sin) * Q_SCALE
            q_ref[0, :, 256 * j + LANES * hh:256 * j + LANES * (hh + 1)] = blk.astype(BF16)
    gate_ref[0] = jax.nn.sigmoid(_dot_nt(wgt_ref[...], xn) + bgt_ref[...])


def _nsa_inproj(h, g, w, wgt, bgt, qg, cos, sin, tm=ROW_TILE):
    B, T, D = h.shape
    hd = N_HEADS * HEAD_DIM
    vec = lambda n: pl.BlockSpec((1, n), lambda b, t: (0, 0))
    full = lambda a: pl.BlockSpec(a.shape, lambda b, t: (0,) * a.ndim)
    tab = pl.BlockSpec((1, tm, LANES), lambda b, t: (b, t, 0))
    return pl.pallas_call(
        functools.partial(_nsa_inproj_kernel, hd=hd),
        grid=(B, T // tm),
        in_specs=[pl.BlockSpec((1, tm, D), lambda b, t: (b, t, 0)), vec(D), full(w), full(wgt), full(bgt),
                  vec(LANES), tab, tab],
        out_specs=[pl.BlockSpec((1, tm, hd), lambda b, t: (b, t, 0)),
                   pl.BlockSpec((1, wgt.shape[0], tm), lambda b, t: (b, 0, t))],
        out_shape=[jax.ShapeDtypeStruct((B, T, hd), BF16), jax.ShapeDtypeStruct((B, wgt.shape[0], T), F32)],
        compiler_params=_cparams(("parallel", "parallel")),
        name="nsa_inproj",
    )(h, g, w, wgt, bgt, qg, cos, sin)


def _stack_heads(q, dst_ref, tq):
    lo = _iota((tq, LANES), 1) < HEAD_DIM
    zero = jnp.zeros((tq, LANES), q.dtype)
    for j in range(HEADS_PER_GROUP // 2):
        x = q[:, LANES * j:LANES * (j + 1)]
        dst_ref[(2 * j) * tq:(2 * j + 1) * tq, 0:LANES] = jnp.where(lo, x, zero)
        dst_ref[(2 * j + 1) * tq:(2 * j + 2) * tq, 0:LANES] = jnp.where(lo, zero, x)


def _unstack_heads_t(o_t, gate_ref, o_ref, tq):
    gate = gate_ref[0]
    for j in range(HEADS_PER_GROUP // 2):
        halves = [o_t[:, h * tq:(h + 1) * tq] * gate[h:h + 1, :] for h in (2 * j, 2 * j + 1)]
        o_ref[0, :, LANES * j:LANES * (j + 1)] = jnp.concatenate(halves, axis=0).T.astype(o_ref.dtype)


def _nsa_cmp_kernel(q_ref, kc_ref, vct_ref, ovt_ref, gate_ref, oc_ref, sel_ref, qs_sc, imp_sc, *, tq, n_slc, top_k, n_levels):
    qi = pl.program_id(2)
    _stack_heads(q_ref[0], qs_sc, tq)
    n_rows = kc_ref.shape[3]
    level_rows = n_rows // n_levels
    n_chunks = 2
    cw = HEADS_PER_GROUP * tq // n_chunks
    n_vis = (qi * tq + tq - 1 - (CMP_LEN - 1)) // CMP_STRIDE + 1

    def attend(rows):
        kc = kc_ref[0, 0, 0, :rows, :]
        vct = vct_ref[0, 0, 0, :, :rows]
        scores = [_dot_nt(kc, qs_sc[c * cw:(c + 1) * cw, :]) for c in range(n_chunks)]
        qpos = qi * tq + (_iota((rows, cw), 1) & (tq - 1))
        vis = _iota((rows, cw), 0) * CMP_STRIDE + (CMP_LEN - 1) <= qpos
        outs = []
        p_sum = None
        for s in scores:
            s = jnp.where(vis, s, NEG)
            e = jnp.where(vis, jnp.exp2(s - jnp.max(s, axis=0, keepdims=True)), 0.0)
            p = e * (1.0 / jnp.maximum(jnp.sum(e, axis=0, keepdims=True), 1.0))
            outs.append(_dot(vct, p.astype(BF16)))
            for h in range(cw // tq):
                term = p[:, h * tq:(h + 1) * tq]
                p_sum = term if p_sum is None else p_sum + term
        _unstack_heads_t(jnp.concatenate(outs, axis=1), gate_ref, oc_ref, tq)
        p_hi = p_sum.astype(BF16)
        p_lo = (p_sum - p_hi.astype(F32)).astype(BF16)
        imp_sc[...] = _dot(ovt_ref[:, :rows], p_hi) + _dot(ovt_ref[:, :rows], p_lo)

    for level in range(n_levels):
        lo_rows, hi_rows = level * level_rows, (level + 1) * level_rows
        cond = (n_vis <= hi_rows) if level == 0 else ((n_vis > lo_rows) & (n_vis <= hi_rows))
        pl.when(cond)(functools.partial(attend, hi_rows))

    imp = imp_sc[...]
    jb = _iota(imp.shape, 0)
    qp = qi * tq + _iota(imp.shape, 1)
    cur = qp // SLC_LEN
    forced = (jb == 0) | (jb == cur) | (jb == cur - 1)
    score = jnp.where(jb * SLC_LEN <= qp, jnp.where(forced, FORCED_SCORE, imp), NEG)
    score = jnp.where(jb < n_slc, score, -jnp.inf)
    bias = jnp.full(imp.shape, NEG, F32)
    for _ in range(top_k):
        mx = jnp.max(score, axis=0, keepdims=True)
        idx = jnp.min(jnp.where(score == mx, jb, LANES), axis=0, keepdims=True)
        pick = jb == idx
        bias = jnp.where(pick, 0.0, bias)
        score = jnp.where(pick, -jnp.inf, score)
    sel_ref[0, 0] = bias.T.astype(BF16)


def _gate_spec(branch, tq):
    return pl.BlockSpec((1, HEADS_PER_GROUP, tq), lambda b, g, qi: (b, branch * N_KV_GROUPS + g, qi))


def _nsa_cmp(q, kc_all, vct_all, overlap_t, gates_t, n_slc, tq=CMP_TQ):
    B, T, hd = q.shape
    G = N_KV_GROUPS
    rows = kc_all.shape[3]
    gw = hd // G
    top_k = min(SLC_TOPK, n_slc)
    n_levels = 4 if rows % (4 * LANES) == 0 else 1
    return pl.pallas_call(
        functools.partial(_nsa_cmp_kernel, tq=tq, n_slc=n_slc, top_k=top_k, n_levels=n_levels),
        grid=(B, G, T // tq),
        in_specs=[pl.BlockSpec((1, tq, gw), lambda b, g, qi: (b, qi, g)),
                  pl.BlockSpec((1, 1, 1, rows, LANES), lambda b, g, qi: (b, 0, g, 0, 0)),
                  pl.BlockSpec((1, 1, 1, HEAD_DIM, rows), lambda b, g, qi: (b, 1, g, 0, 0)),
                  pl.BlockSpec(overlap_t.shape, lambda b, g, qi: (0, 0)), _gate_spec(0, tq)],
        out_specs=[pl.BlockSpec((1, tq, gw), lambda b, g, qi: (b, qi, g)),
                   pl.BlockSpec((1, 1, tq, LANES), lambda b, g, qi: (b, g, qi, 0))],
        out_shape=[jax.ShapeDtypeStruct((B, T, hd), BF16), jax.ShapeDtypeStruct((B, G, T, LANES), BF16)],
        scratch_shapes=[pltpu.VMEM((HEADS_PER_GROUP * tq, LANES), BF16), pltpu.VMEM((LANES, tq), F32)],
        compiler_params=_cparams(("parallel", "parallel", "parallel")),
        name="nsa_cmp",
    )(q, kc_all, vct_all, overlap_t, gates_t)


def _nsa_slc_kernel(q_ref, sel_ref, k_ref, oh_ref, vt_ref, gate_ref, o_ref, qa_sc, s_sc, mt_sc, m_sc, acc_sc, *, tq, tk, n_chunks):
    qi = pl.program_id(2)
    last = (qi * tq) // tk
    _stack_heads(q_ref[0], qa_sc, tq)
    sel = sel_ref[0, 0]
    for h in range(HEADS_PER_GROUP):
        qa_sc[h * tq:(h + 1) * tq, LANES:2 * LANES] = sel
    _softmax_init(m_sc, acc_sc)
    cw = HEADS_PER_GROUP * tq // n_chunks

    def produce(ki, slot):
        start = pl.multiple_of(ki * tk, tk)
        ka = jnp.concatenate([k_ref[0, 0, pl.ds(start, tk), :], oh_ref[pl.ds(start, tk), :]], axis=1)
        for c in range(n_chunks):
            s = _dot_nt(ka, qa_sc[c * cw:(c + 1) * cw, :])
            s_sc[slot, :, c * cw:(c + 1) * cw] = s
            mt_sc[slot, :, c * cw:(c + 1) * cw] = _column_max(s)

    def consume(ki, slot, diag):
        vt = vt_ref[0, 0, ki]
        for c in range(n_chunks):
            cols = slice(c * cw, (c + 1) * cw)
            s = s_sc[slot, :, cols]
            m_tile = mt_sc[slot, :, cols]
            if diag:
                kpos = ki * tk + _iota(s.shape, 0)
                qpos = qi * tq + (_iota(s.shape, 1) & (tq - 1))
                s = jnp.where(kpos <= qpos, s, NEG)
                m_tile = None
            _softmax_update_t(s, vt, m_sc.at[:, cols], acc_sc.at[:, cols], m_tile)

    _pipelined_key_loop(last, produce, consume)
    _unstack_heads_t(_softmax_result(acc_sc), gate_ref, o_ref, tq)


def _nsa_slc(q, sel, ks, vst, gates_t, tq=SLC_TQ, tk=SLC_TK):
    B, T, hd = q.shape
    G = N_KV_GROUPS
    gw = hd // G
    cols = HEADS_PER_GROUP * tq
    assert tk % tq == 0
    onehot = (jnp.arange(T)[:, None] // SLC_LEN == jnp.arange(LANES)[None, :]).astype(BF16)
    return pl.pallas_call(
        functools.partial(_nsa_slc_kernel, tq=tq, tk=tk, n_chunks=4),
        grid=(B, G, T // tq),
        in_specs=[pl.BlockSpec((1, tq, gw), lambda b, g, qi: (b, qi, g)),
                  pl.BlockSpec((1, 1, tq, LANES), lambda b, g, qi: (b, g, qi, 0)),
                  pl.BlockSpec((1, 1, T, LANES), lambda b, g, qi: (b, g, 0, 0)),
                  pl.BlockSpec((T, LANES), lambda b, g, qi: (0, 0)),
                  pl.BlockSpec((1, 1, T // tk, HEAD_DIM, tk), lambda b, g, qi: (b, g, 0, 0, 0)), _gate_spec(1, tq)],
        out_specs=pl.BlockSpec((1, tq, gw), lambda b, g, qi: (b, qi, g)),
        out_shape=jax.ShapeDtypeStruct((B, T, hd), BF16),
        scratch_shapes=[pltpu.VMEM((cols, 2 * LANES), BF16), pltpu.VMEM((2, tk, cols), F32),
                        pltpu.VMEM((2, 1, cols), F32), pltpu.VMEM((1, cols), F32),
                        pltpu.VMEM((ACC_ROWS, cols), F32)],
        compiler_params=_cparams(("parallel", "parallel", "arbitrary")),
        name="nsa_slc",
    )(q, sel, ks, onehot, vst, gates_t)


def _nsa_win_kernel(q_ref, k_ref, vt_ref, gate_ref, o_ref, qs_sc, s_sc, m_sc, acc_sc, *, tq, n_back, n_chunks):
    qi = pl.program_id(2)
    _stack_heads(q_ref[0], qs_sc, tq)
    _softmax_init(m_sc, acc_sc)
    cw = HEADS_PER_GROUP * tq // n_chunks

    def produce(ti, slot):
        start = pl.multiple_of(ti * tq, tq)
        k = k_ref[0, 0, pl.ds(start, tq), :]
        for c in range(n_chunks):
            s_sc[slot, :, c * cw:(c + 1) * cw] = _dot_nt(k, qs_sc[c * cw:(c + 1) * cw, :])

    def consume(ti, slot, kind):
        vt = vt_ref[0, 0, ti]
        for c in range(n_chunks):
            cols = slice(c * cw, (c + 1) * cw)
            s = s_sc[slot, :, cols]
            r = _iota(s.shape, 0)
            q_in_tile = _iota(s.shape, 1) & (tq - 1)
            if kind == "oldest":
                s = jnp.where(r > q_in_tile, s, NEG)
            elif kind == "diag":
                s = jnp.where(r <= q_in_tile, s, NEG)
            _softmax_update_t(s, vt, m_sc.at[:, cols], acc_sc.at[:, cols])

    def kind_of(back):
        return "oldest" if back == n_back else ("diag" if back == 0 else "full")

    for first in range(n_back + 1):
        cond = (qi >= n_back) if first == n_back else (qi == first)

        @pl.when(cond)
        def _(first=first):
            backs = list(range(first, -1, -1))
            produce(qi - backs[0], 0)
            for n, back in enumerate(backs):
                if n + 1 < len(backs):
                    produce(qi - backs[n + 1], (n + 1) % 2)
                consume(qi - back, n % 2, kind_of(back))

    _unstack_heads_t(_softmax_result(acc_sc), gate_ref, o_ref, tq)


def _nsa_win(q, kw, vwt, gates_t, tq=WIN_TQ):
    B, T, hd = q.shape
    G = N_KV_GROUPS
    gw = hd // G
    cols = HEADS_PER_GROUP * tq
    return pl.pallas_call(
        functools.partial(_nsa_win_kernel, tq=tq, n_back=WIN // tq, n_chunks=4),
        grid=(B, G, T // tq),
        in_specs=[pl.BlockSpec((1, tq, gw), lambda b, g, qi: (b, qi, g)),
                  pl.BlockSpec((1, 1, T, LANES), lambda b, g, qi: (b, g, 0, 0)),
                  pl.BlockSpec((1, 1, T // tq, HEAD_DIM, tq), lambda b, g, qi: (b, g, 0, 0, 0)), _gate_spec(2, tq)],
        out_specs=pl.BlockSpec((1, tq, gw), lambda b, g, qi: (b, qi, g)),
        out_shape=jax.ShapeDtypeStruct((B, T, hd), BF16),
        scratch_shapes=[pltpu.VMEM((cols, LANES), BF16), pltpu.VMEM((2, tq, cols), F32), pltpu.VMEM((1, cols), F32),
                        pltpu.VMEM((ACC_ROWS, cols), F32)],
        compiler_params=_cparams(("parallel", "parallel", "arbitrary")),
        name="nsa_win",
    )(q, kw, vwt, gates_t)


def _pad_cols(a, n):
    return jnp.pad(a, ((0, 0), (0, n - a.shape[1])))


def _lane_vec(v):
    return jnp.tile(v.astype(F32), 2).reshape(1, LANES)


def _bias_feature_selector():
    sel = np.zeros((N_BIAS_PARTS * LANES, N_HEADS * HEAD_DIM), np.float32)
    for part in range(N_BIAS_PARTS):
        for h in range(N_HEADS):
            sel[part * LANES + h, (h // 2) * LANES + (h % 2) * N_BIAS_PARTS + part] = 1.0
    return jnp.asarray(sel, BF16)


def kernel(x, positions, a_norm, a_w_in, a_b_f, a_q_gain, a_k_gain, a_w_out, kv_norm, kv_w, kc_pe, vc_pe, kc_w1, kc_w2, vc_w1, vc_w2, kc_gain, ks_gain, kw_gain, b_norm, b_w_in, b_b_gate, b_q_gain, b_w_out, f_norm, f_w_up, f_conv_w, f_conv_b, f_w_down):
    B, T, D = x.shape
    hd = N_HEADS * HEAD_DIM
    G = N_KV_GROUPS
    n_a = a_norm.shape[0]
    n_b = b_norm.shape[0]
    depth = n_a + n_b
    n_slc = T // SLC_LEN
    n_cmp = (T - CMP_LEN) // CMP_STRIDE + 1
    assert T % 1024 == 0 and n_slc <= LANES and hd == 1024 and D == 1024

    half = ROT_DIM // 2
    inv = ROPE_THETA ** (-jnp.arange(half, dtype=F32) * (2.0 / ROT_DIM))
    head_inv = jnp.concatenate([inv, inv, jnp.zeros((HEAD_DIM - ROT_DIM,), F32)])
    head_sign = jnp.concatenate([-jnp.ones((half,), F32), jnp.ones((half,), F32), jnp.zeros((HEAD_DIM - ROT_DIM,), F32)])
    inv_lane = jnp.tile(head_inv, 2).reshape(1, LANES)
    sign_lane = jnp.tile(head_sign, 2).reshape(1, LANES)
    cos_t, sin_t = _rope_tables(positions, inv_lane, sign_lane)
    end_pos = positions[:, CMP_LEN - 1::CMP_STRIDE]
    end_pos = jnp.pad(end_pos, ((0, 0), (0, T // CMP_STRIDE - n_cmp)))
    cos_c, sin_c = _rope_tables(end_pos, inv_lane, sign_lane)

    cs = jnp.arange(T // CMP_STRIDE) * CMP_STRIDE
    ss = jnp.arange(LANES) * SLC_LEN
    overlap_t = (jnp.maximum(jnp.minimum(cs[None, :] + CMP_LEN, ss[:, None] + SLC_LEN)
                             - jnp.maximum(cs[None, :], ss[:, None]), 0).astype(F32) / CMP_LEN).astype(BF16)

    h = x
    kv = None
    for layer in range(depth):
        if layer < n_a:
            i = layer
            w = jnp.concatenate([a_w_in[i][:, :2 * hd], _pad_cols(a_w_in[i][:, 3 * hd:], LANES)], axis=1).astype(BF16)
            wvt = a_w_in[i][:, 2 * hd:3 * hd].T.astype(BF16)
            bf = _pad_cols(a_b_f[i].reshape(1, -1), LANES)
            q, k, vt, cf = _fox_inproj(h, a_norm[i].reshape(1, D), w, wvt, bf, _lane_vec(a_q_gain[i]),
                                       _lane_vec(a_k_gain[i]), _bias_feature_selector())
            mix, w_out = [_fox_attn(q, k, cf, vt)], a_w_out[i]
        else:
            i = layer - n_a
            kc_all, vct_all, ks, kw, vst, vwt = kv
            q, gates_t = _nsa_inproj(h, b_norm[i].reshape(1, D), b_w_in[i][:, :hd].astype(BF16),
                                     b_w_in[i][:, hd:].T.astype(BF16), b_b_gate[i].reshape(-1, 1),
                                     _lane_vec(b_q_gain[i]), cos_t, sin_t)
            o_c, sel = _nsa_cmp(q, kc_all, vct_all, overlap_t, gates_t, n_slc)
            o_s = _nsa_slc(q, sel, ks, vst, gates_t)
            o_w = _nsa_win(q, kw, vwt, gates_t)
            mix, w_out = [o_c, o_s, o_w], b_w_out[i]
        h = _mix_out_conv_ffn(h, mix, w_out.astype(BF16), f_norm[layer].reshape(1, D), f_w_up[layer].astype(BF16),
                              f_conv_w[layer], f_conv_b[layer].reshape(1, -1), f_w_down[layer].astype(BF16))
        if layer == n_a - 1:
            w6 = kv_w.reshape(D, 6, G, HEAD_DIM)
            raw_cols = w6[:, 0:2].reshape(D, 2 * G * HEAD_DIM)
            wk = w6[:, (2, 4)]
            dup_cols = jnp.concatenate([wk, wk], axis=-1).reshape(D, 2 * G * LANES)
            wkv = jnp.concatenate([raw_cols, dup_cols], axis=1).astype(BF16)
            wvt = w6[:, (3, 5)].reshape(D, 2 * G * HEAD_DIM).T.astype(BF16)
            craw, ks, kw, vst, vwt = _kvproj(h, kv_norm.reshape(1, D), wkv, wvt, _lane_vec(ks_gain),
                                             _lane_vec(kw_gain), cos_t, sin_t)
            r = craw.reshape(B, T, 2, G, HEAD_DIM).transpose(0, 2, 3, 1, 4).reshape(B, 2, G, T // CMP_STRIDE, CMP_STRIDE * HEAD_DIM)
            w1 = jnp.stack([kc_w1, vc_w1]).astype(BF16)
            pe = jnp.stack([kc_pe.reshape(1, -1), vc_pe.reshape(1, -1)])
            pe = jnp.pad(pe, ((0, 0), (0, 7), (0, 0))).astype(BF16)
            w2 = jnp.stack([jnp.concatenate([kc_w2, kc_w2], axis=1), jnp.concatenate([vc_w2, vc_w2], axis=1)]).astype(BF16)
            w2t = jnp.stack([kc_w2.T, vc_w2.T]).astype(BF16)
            kc_all, vct_all = _compress(r, w1, pe, w2, w2t, _lane_vec(kc_gain), cos_c, sin_c, n_cmp)
            kv = (kc_all, vct_all, ks, kw, vst, vwt)
    return h
```

```python
import functools

import numpy as np
import jax
import jax.numpy as jnp
from jax import lax
from jax.experimental import pallas as pl
from jax.experimental.pallas import tpu as pltpu

F32 = jnp.float32
BF16 = jnp.bfloat16

LANES = 128
SUBLANES = 8
HEAD_DIM = 64
N_HEADS = 16
N_PAIRS = N_HEADS // 2
N_KV_GROUPS = 2
HEADS_PER_GROUP = N_HEADS // N_KV_GROUPS
ROT_DIM = HEAD_DIM // 4
ROPE_THETA = 500000.0
CMP_LEN = 32
CMP_STRIDE = 16
SLC_LEN = 64
SLC_TOPK = 16
WIN = 512
CONV_W = 3
RMS_EPS = 1e-6
NEG = -1e30
FORCED_SCORE = 1e6
LOG2E = 1.4426950408889634
Q_SCALE = HEAD_DIM ** -0.5 * LOG2E
N_BIAS_PARTS = 3
ONES_ROWS = 16
ACC_ROWS = HEAD_DIM + ONES_ROWS

ROW_TILE = 512
FOX_TQ = 512
FOX_TK = ROW_TILE
SLC_TQ = 256
SLC_TK = ROW_TILE
WIN_TQ = 256
CMP_TQ = 256
CMP_SUB = 128

VMEM_LIMIT = 48 * 1024 * 1024


def _cparams(sem, flags=None):
    return pltpu.CompilerParams(dimension_semantics=sem, vmem_limit_bytes=VMEM_LIMIT, flags=flags)


def _iota(shape, axis):
    return lax.broadcasted_iota(jnp.int32, shape, axis)


def _row_rms(x, g):
    ms = jnp.mean(x * x, axis=-1, keepdims=True)
    return x * lax.rsqrt(ms + RMS_EPS) * g


def _pair_rms(y, gain):
    lo = _iota(y.shape, 1) < HEAD_DIM
    y2 = y * y
    s_lo = jnp.sum(jnp.where(lo, y2, 0.0), axis=-1, keepdims=True)
    s_hi = jnp.sum(jnp.where(lo, 0.0, y2), axis=-1, keepdims=True)
    ms = jnp.where(lo, s_lo, s_hi) * (1.0 / HEAD_DIM)
    return y * lax.rsqrt(ms + RMS_EPS) * gain


def _pair_rope(y, cos, sin):
    lane = _iota(y.shape, 1) & (HEAD_DIM - 1)
    partner = jnp.where(lane < ROT_DIM // 2,
                        pltpu.roll(y, LANES - ROT_DIM // 2, 1),
                        pltpu.roll(y, ROT_DIM // 2, 1))
    return y * cos + partner * sin


def _dot(a, b):
    return jnp.dot(a, b, preferred_element_type=F32)


def _dot_nt(a, b):
    return lax.dot_general(a, b, (((1,), (1,)), ((), ())), preferred_element_type=F32)


def _column_max(s):
    tk, n = s.shape
    return jnp.max(jnp.max(s.reshape(tk // SUBLANES, SUBLANES, n), axis=0), axis=0, keepdims=True)


def _softmax_update_t(s, vt, m_ref, acc_ref, m_tile=None):
    tk, n = s.shape
    m_old = m_ref[...]
    if m_tile is None:
        m_tile = _column_max(s)
    m_new = jnp.maximum(m_old, m_tile)
    alpha = jnp.exp2(m_old - m_new)
    p = jnp.exp2(s - m_new).astype(BF16)
    vta = jnp.concatenate([vt, jnp.ones((ONES_ROWS, tk), vt.dtype)], axis=0)
    acc_ref[...] = alpha * acc_ref[...] + _dot(vta, p)
    m_ref[...] = m_new


def _pipelined_key_loop(n_full, produce, consume, unroll=4, n_diag=1):
    assert unroll % 2 == 0

    def run(base, count):
        for i in range(count):
            produce(base + i + 1, (i + 1) % 2)
            consume(base + i, i % 2, False)

    produce(0, 0)

    def body(j, carry):
        run(unroll * j, unroll)
        return carry

    lax.fori_loop(0, n_full // unroll, body, 0)
    base = (n_full // unroll) * unroll
    rem = n_full - base
    step = unroll // 2
    while step >= 2:
        pl.when((rem & step) != 0)(functools.partial(run, base, step))
        base = base + (rem & step)
        step //= 2

    def tail(first, slot):
        for d in range(n_diag):
            if d + 1 < n_diag:
                produce(first + d + 1, (slot + d + 1) % 2)
            consume(first + d, (slot + d) % 2, True)

    @pl.when((rem & 1) != 0)
    def _():
        run(base, 1)
        tail(base + 1, 1)

    @pl.when((rem & 1) == 0)
    def _():
        tail(base, 0)


def _softmax_init(m_ref, acc_ref):
    m_ref[...] = jnp.full_like(m_ref, NEG)
    acc_ref[...] = jnp.zeros_like(acc_ref)


def _softmax_result(acc_ref):
    acc = acc_ref[...]
    return acc[:HEAD_DIM, :] * (1.0 / acc[HEAD_DIM:HEAD_DIM + 1, :])


def _rope_table_kernel(pos_ref, inv_ref, sign_ref, c_ref, s_ref):
    ang = pos_ref[0].astype(F32) * inv_ref[...]
    c_ref[0] = jnp.cos(ang)
    s_ref[0] = jnp.sin(ang) * sign_ref[...]


def _rope_tables(pos, inv_lane, sign_lane):
    B, T = pos.shape
    tm = min(T, ROW_TILE)
    spec = pl.BlockSpec((1, tm, LANES), lambda b, t: (b, t, 0))
    vec = pl.BlockSpec((1, LANES), lambda b, t: (0, 0))
    return pl.pallas_call(
        _rope_table_kernel,
        grid=(B, T // tm),
        in_specs=[pl.BlockSpec((1, tm, 1), lambda b, t: (b, t, 0)), vec, vec],
        out_specs=[spec, spec],
        out_shape=[jax.ShapeDtypeStruct((B, T, LANES), F32)] * 2,
        compiler_params=_cparams(("parallel", "parallel")),
        name="rope_tables",
    )(pos.reshape(B, T, 1), inv_lane, sign_lane)


def _fox_inproj_kernel(x_ref, g_ref, w_ref, wvt_ref, bf_ref, qg_ref, kg_ref, sel_ref,
                       q_ref, k_ref, vt_ref, cf_ref, carry_sc, *, tm, hd):
    ti = pl.program_id(1)
    xn = _row_rms(x_ref[0], g_ref[...]).astype(BF16)
    for j in range(hd // 256):
        for part, (ref, gain, mul) in enumerate(((q_ref, qg_ref, Q_SCALE), (k_ref, kg_ref, 1.0))):
            c0 = part * hd + 256 * j
            y = _dot(xn, w_ref[:, c0:c0 + 256])
            for hh in range(2):
                blk = _pair_rms(y[:, LANES * hh:LANES * (hh + 1)], gain[...]) * mul
                ref[0, :, 256 * j + LANES * hh:256 * j + LANES * (hh + 1)] = blk.astype(BF16)
        yt = _dot_nt(wvt_ref[256 * j:256 * (j + 1), :], xn)
        for hh in range(2):
            vt_ref[0, 2 * j + hh, 0] = yt[LANES * hh:LANES * (hh + 1), :].astype(BF16)
    z = _dot(xn, w_ref[:, 2 * hd:2 * hd + LANES]) + bf_ref[...]
    lf = jnp.minimum(z, 0.0) - jnp.log1p(jnp.exp(-jnp.abs(z)))
    row = _iota(lf.shape, 0)
    sh = 1
    while sh < tm:
        lf = lf + jnp.where(row >= sh, pltpu.roll(lf, sh, 0), 0.0)
        sh *= 2

    @pl.when(ti == 0)
    def _():
        carry_sc[...] = jnp.zeros_like(carry_sc)

    c = lf + carry_sc[0:1, :]
    carry_sc[...] = jnp.broadcast_to(c[tm - 1:tm, :], carry_sc.shape)
    rest = c * (-LOG2E)
    pieces = []
    for _ in range(N_BIAS_PARTS):
        piece = rest.astype(BF16)
        pieces.append(piece)
        rest = rest - piece.astype(F32)
    cf_ref[0] = _dot(jnp.concatenate(pieces, axis=1), sel_ref[...]).astype(BF16)


def _fox_inproj(x, g, w, wvt, bf, qg, kg, sel, tm=ROW_TILE):
    B, T, D = x.shape
    hd = N_HEADS * HEAD_DIM
    act = pl.BlockSpec((1, tm, hd), lambda b, t: (b, t, 0))
    vec = lambda n: pl.BlockSpec((1, n), lambda b, t: (0, 0))
    full = lambda a: pl.BlockSpec(a.shape, lambda b, t: (0,) * a.ndim)
    return pl.pallas_call(
        functools.partial(_fox_inproj_kernel, tm=tm, hd=hd),
        grid=(B, T // tm),
        in_specs=[pl.BlockSpec((1, tm, D), lambda b, t: (b, t, 0)), vec(D), full(w), full(wvt),
                  vec(LANES), vec(LANES), vec(LANES), full(sel)],
        out_specs=[act, act, pl.BlockSpec((1, N_PAIRS, 1, LANES, tm), lambda b, t: (b, 0, t, 0, 0)), act],
        out_shape=[jax.ShapeDtypeStruct((B, T, hd), BF16), jax.ShapeDtypeStruct((B, T, hd), BF16),
                   jax.ShapeDtypeStruct((B, N_PAIRS, T // tm, LANES, tm), BF16),
                   jax.ShapeDtypeStruct((B, T, hd), BF16)],
        scratch_shapes=[pltpu.VMEM((SUBLANES, LANES), F32)],
        compiler_params=_cparams(("arbitrary", "arbitrary")),
        name="fox_inproj",
    )(x, g, w, wvt, bf, qg, kg, sel)


def _fox_attn_kernel(q_ref, k_ref, cf_ref, vt_ref, o_ref, qa_sc, s_sc, mt_sc, m_sc, acc_sc, *, tq, tk):
    qi = pl.program_id(2)
    q = q_ref[0]
    lane = _iota(q.shape, 1)
    lo = lane < HEAD_DIM
    zero = jnp.zeros_like(q)
    for hh in range(2):
        qa_sc[hh, :, 0:LANES] = jnp.where(lo, q, zero) if hh == 0 else jnp.where(lo, zero, q)
        feat = jnp.where(lane < N_BIAS_PARTS * (hh + 1), 1.0, 0.0)
        qa_sc[hh, :, LANES:2 * LANES] = jnp.where(lane >= N_BIAS_PARTS * hh, feat, 0.0).astype(BF16)
        _softmax_init(m_sc.at[hh], acc_sc.at[hh])

    def produce(ki, slot):
        start = pl.multiple_of(ki * tk, tk)
        ka = jnp.concatenate([k_ref[0, pl.ds(start, tk), :], cf_ref[0, pl.ds(start, tk), :]], axis=1)
        for hh in range(2):
            s = _dot_nt(ka, qa_sc[hh])
            s_sc[slot, hh] = s
            mt_sc[slot, hh] = _column_max(s)

    def consume(ki, slot, diag):
        vt = vt_ref[0, 0, ki]
        for hh in range(2):
            s = s_sc[slot, hh]
            m_tile = mt_sc[slot, hh]
            if diag:
                s = jnp.where(ki * tk + _iota(s.shape, 0) <= qi * tq + _iota(s.shape, 1), s, NEG)
                m_tile = None
            _softmax_update_t(s, vt[HEAD_DIM * hh:HEAD_DIM * (hh + 1), :], m_sc.at[hh], acc_sc.at[hh], m_tile)

    _pipelined_key_loop(qi * (tq // tk), produce, consume, n_diag=tq // tk)
    o_t = jnp.concatenate([_softmax_result(acc_sc.at[hh]) for hh in range(2)], axis=0)
    o_ref[0] = o_t.T.astype(BF16)


def _fox_attn(q, k, cf, vt, tq=FOX_TQ, tk=FOX_TK):
    B, T, hd = q.shape
    assert tq % tk == 0
    nk = T // tk
    seq = pl.BlockSpec((1, T, LANES), lambda b, p, qi: (b, 0, p))
    return pl.pallas_call(
        functools.partial(_fox_attn_kernel, tq=tq, tk=tk),
        grid=(B, N_PAIRS, T // tq),
        in_specs=[pl.BlockSpec((1, tq, LANES), lambda b, p, qi: (b, qi, p)), seq, seq,
                  pl.BlockSpec((1, 1, nk, LANES, tk), lambda b, p, qi: (b, p, 0, 0, 0))],
        out_specs=pl.BlockSpec((1, tq, LANES), lambda b, p, qi: (b, qi, p)),
        out_shape=jax.ShapeDtypeStruct((B, T, hd), BF16),
        scratch_shapes=[pltpu.VMEM((2, tq, 2 * LANES), BF16), pltpu.VMEM((2, 2, tk, tq), F32),
                        pltpu.VMEM((2, 2, 1, tq), F32), pltpu.VMEM((2, 1, tq), F32),
                        pltpu.VMEM((2, ACC_ROWS, tq), F32)],
        compiler_params=_cparams(("parallel", "parallel", "arbitrary")),
        name="fox_attn",
    )(q, k, cf, vt)


def _ffn_kernel(*refs, n_mix, tt, tf, d_ff):
    h_ref = refs[0]
    mix_refs = refs[1:1 + n_mix]
    wo_ref, g_ref, wup_ref, cw_ref, cb_ref, wd_ref, out_ref, a_sc, carry_sc = refs[1 + n_mix:]
    ti = pl.program_id(1)

    @pl.when(ti == 0)
    def _():
        carry_sc[...] = jnp.zeros_like(carry_sc)

    o = mix_refs[0][0]
    if n_mix > 1:
        o = o.astype(F32)
        for ref in mix_refs[1:]:
            o = o + ref[0].astype(F32)
        o = o.astype(BF16)
    x = h_ref[0] + _dot(o, wo_ref[...])
    xn = _row_rms(x, g_ref[...]).astype(BF16)
    row8 = _iota((SUBLANES, tf), 0)

    def conv(u, c0):
        prev8 = carry_sc[:, c0:c0 + tf]
        um1 = pltpu.roll(u, 1, 0)
        um2 = pltpu.roll(u, 2, 0)
        top1 = jnp.where(row8 == 0, prev8[7:8, :], um1[0:SUBLANES, :])
        top2 = jnp.where(row8 == 0, prev8[6:7, :], jnp.where(row8 == 1, prev8[7:8, :], um2[0:SUBLANES, :]))
        um1 = jnp.concatenate([top1, um1[SUBLANES:, :]], axis=0)
        um2 = jnp.concatenate([top2, um2[SUBLANES:, :]], axis=0)
        carry_sc[:, c0:c0 + tf] = u[tt - SUBLANES:, :]
        cw = cw_ref[:, c0:c0 + tf]
        return cb_ref[:, c0:c0 + tf] + cw[0:1, :] * um2 + cw[1:2, :] * um1 + cw[2:3, :] * u

    for f in range(d_ff // tf):
        g0 = f * tf
        cg = conv(_dot(xn, wup_ref[:, g0:g0 + tf]), g0)
        cv = conv(_dot(xn, wup_ref[:, d_ff + g0:d_ff + g0 + tf]), d_ff + g0)
        a_sc[:, g0:g0 + tf] = (cg * jax.nn.sigmoid(cg) * cv).astype(BF16)
    out_ref[0] = x + _dot(a_sc[...], wd_ref[...])


def _mix_out_conv_ffn(h, mix, w_out, g, w_up, conv_w, conv_b, w_down, tt=ROW_TILE, tf=256):
    B, T, D = h.shape
    d_ff = w_down.shape[0]
    act = lambda c: pl.BlockSpec((1, tt, c), lambda b, t: (b, t, 0))
    resident = lambda a: pl.BlockSpec(a.shape, lambda b, t: (0,) * a.ndim, pipeline_mode=pl.Buffered(1))
    return pl.pallas_call(
        functools.partial(_ffn_kernel, n_mix=len(mix), tt=tt, tf=tf, d_ff=d_ff),
        grid=(B, T // tt),
        in_specs=[act(D)] + [act(m.shape[-1]) for m in mix]
                 + [resident(w_out), resident(g), resident(w_up), resident(conv_w), resident(conv_b), resident(w_down)],
        out_specs=act(D),
        out_shape=jax.ShapeDtypeStruct((B, T, D), F32),
        scratch_shapes=[pltpu.VMEM((tt, d_ff), BF16), pltpu.VMEM((SUBLANES, 2 * d_ff), F32)],
        compiler_params=_cparams(("arbitrary", "arbitrary")),
        name="conv_ffn",
    )(h, *mix, w_out, g, w_up, conv_w, conv_b, w_down)


def _kvproj_kernel(h_ref, g_ref, w_ref, wvt_ref, ksg_ref, kwg_ref, c_ref, s_ref,
                   craw_ref, ks_ref, kw_ref, vst_ref, vwt_ref, *, tm):
    xn = _row_rms(h_ref[0], g_ref[...]).astype(BF16)
    cos = c_ref[0]
    sin = s_ref[0]
    craw_ref[0] = _dot(xn, w_ref[:, 0:256])
    for idx, (ref, gain) in enumerate(((ks_ref, ksg_ref), (kw_ref, kwg_ref))):
        y = _dot(xn, w_ref[:, 256 * (idx + 1):256 * (idx + 2)])
        for grp in range(N_KV_GROUPS):
            blk = _pair_rope(_pair_rms(y[:, LANES * grp:LANES * (grp + 1)], gain[...]), cos, sin)
            ref[0, grp] = blk.astype(BF16)
    yt = _dot_nt(wvt_ref[...], xn).astype(BF16)
    for grp in range(N_KV_GROUPS):
        vst_ref[0, grp, 0] = yt[HEAD_DIM * grp:HEAD_DIM * (grp + 1), :]
        r0 = HEAD_DIM * (N_KV_GROUPS + grp)
        for c in range(tm // WIN_TQ):
            vwt_ref[0, grp, c] = yt[r0:r0 + HEAD_DIM, WIN_TQ * c:WIN_TQ * (c + 1)]


def _kvproj(h, g, w, wvt, ksg, kwg, cos, sin, tm=ROW_TILE):
    B, T, D = h.shape
    G = N_KV_GROUPS
    vec = lambda n: pl.BlockSpec((1, n), lambda b, t: (0, 0))
    full = lambda a: pl.BlockSpec(a.shape, lambda b, t: (0,) * a.ndim)
    tab = pl.BlockSpec((1, tm, LANES), lambda b, t: (b, t, 0))
    dup = pl.BlockSpec((1, G, tm, LANES), lambda b, t: (b, 0, t, 0))
    dup_shape = jax.ShapeDtypeStruct((B, G, T, LANES), BF16)
    nw = tm // WIN_TQ
    return pl.pallas_call(
        functools.partial(_kvproj_kernel, tm=tm),
        grid=(B, T // tm),
        in_specs=[pl.BlockSpec((1, tm, D), lambda b, t: (b, t, 0)), vec(D), full(w), full(wvt),
                  vec(LANES), vec(LANES), tab, tab],
        out_specs=[pl.BlockSpec((1, tm, 256), lambda b, t: (b, t, 0)), dup, dup,
                   pl.BlockSpec((1, G, 1, HEAD_DIM, tm), lambda b, t: (b, 0, t, 0, 0)),
                   pl.BlockSpec((1, G, nw, HEAD_DIM, WIN_TQ), lambda b, t: (b, 0, t, 0, 0))],
        out_shape=[jax.ShapeDtypeStruct((B, T, 256), F32), dup_shape, dup_shape,
                   jax.ShapeDtypeStruct((B, G, T // tm, HEAD_DIM, tm), BF16),
                   jax.ShapeDtypeStruct((B, G, T // WIN_TQ, HEAD_DIM, WIN_TQ), BF16)],
        compiler_params=_cparams(("parallel", "parallel")),
        name="kvproj",
    )(h, g, w, wvt, ksg, kwg, cos, sin)


def _compress_kernel(r_ref, w1_ref, pe_ref, w2_ref, w2t_ref, gain_ref, c_ref, s_ref, kc_ref, vct_ref, *, n_cmp):
    r = r_ref[0, 0, 0].astype(BF16)
    half = r.shape[1]
    a = _dot(r, w1_ref[0, :half, :])
    b = _dot(r, w1_ref[0, half:, :])
    peb = _dot(pe_ref[0], w1_ref[0])[0:1, :]
    rows = r.shape[0]
    hid = a + pltpu.roll(b, rows - 1, 0) + peb
    act = jax.nn.gelu(hid).astype(BF16)
    y = _dot(act, w2_ref[0])
    yk = _pair_rope(_pair_rms(y, gain_ref[...]), c_ref[0], s_ref[0])
    kc_ref[0, 0, 0] = jnp.where(_iota(y.shape, 0) < n_cmp, yk, 0.0).astype(BF16)
    yt = _dot_nt(w2t_ref[0], act)
    vct_ref[0, 0, 0] = jnp.where(_iota(yt.shape, 1) < n_cmp, yt, 0.0).astype(BF16)


def _compress(r, w1, pe, w2, w2t, gain, cos, sin, n_cmp):
    B, _, G, rows, width = r.shape
    per_kv = lambda a: pl.BlockSpec((1,) + a.shape[1:], lambda b, kv, g: (kv,) + (0,) * (a.ndim - 1))
    return pl.pallas_call(
        functools.partial(_compress_kernel, n_cmp=n_cmp),
        grid=(B, 2, G),
        in_specs=[pl.BlockSpec((1, 1, 1, rows, width), lambda b, kv, g: (b, kv, g, 0, 0)),
                  per_kv(w1), per_kv(pe), per_kv(w2), per_kv(w2t),
                  pl.BlockSpec((1, LANES), lambda b, kv, g: (0, 0)),
                  pl.BlockSpec((1, rows, LANES), lambda b, kv, g: (b, 0, 0)),
                  pl.BlockSpec((1, rows, LANES), lambda b, kv, g: (b, 0, 0))],
        out_specs=[pl.BlockSpec((1, 1, 1, rows, LANES), lambda b, kv, g: (b, kv, g, 0, 0)),
                   pl.BlockSpec((1, 1, 1, HEAD_DIM, rows), lambda b, kv, g: (b, kv, g, 0, 0))],
        out_shape=[jax.ShapeDtypeStruct((B, 2, G, rows, LANES), BF16),
                   jax.ShapeDtypeStruct((B, 2, G, HEAD_DIM, rows), BF16)],
        compiler_params=_cparams(("parallel", "parallel", "parallel")),
        name="compress",
    )(r, w1, pe, w2, w2t, gain, cos, sin)


def _nsa_inproj_kernel(h_ref, g_ref, w_ref, wgt_ref, bgt_ref, qg_ref, c_ref, s_ref, q_ref, gate_ref, *, hd):
    xn = _row_rms(h_ref[0], g_ref[...]).astype(BF16)
    cos = c_ref[0]
    sin = s_ref[0]
    for j in range(hd // 256):
        y = _dot(xn, w_ref[:, 256 * j:256 * (j + 1)])
        for hh in range(2):
            blk = _pair_rope(_pair_rms(y[:, LANES * hh:LANES * (hh + 1)], qg_ref[...]), cos, sin) * Q_SCALE
            q_ref[0, :, 256 * j + LANES * hh:256 * j + LANES * (hh + 1)] = blk.astype(BF16)
    gate_ref[0] = jax.nn.sigmoid(_dot_nt(wgt_ref[...], xn) + bgt_ref[...])


def _nsa_inproj(h, g, w, wgt, bgt, qg, cos, sin, tm=ROW_TILE):
    B, T, D = h.shape
    hd = N_HEADS * HEAD_DIM
    vec = lambda n: pl.BlockSpec((1, n), lambda b, t: (0, 0))
    full = lambda a: pl.BlockSpec(a.shape, lambda b, t: (0,) * a.ndim)
    tab = pl.BlockSpec((1, tm, LANES), lambda b, t: (b, t, 0))
    return pl.pallas_call(
        functools.partial(_nsa_inproj_kernel, hd=hd),
        grid=(B, T // tm),
        in_specs=[pl.BlockSpec((1, tm, D), lambda b, t: (b, t, 0)), vec(D), full(w), full(wgt), full(bgt),
                  vec(LANES), tab, tab],
        out_specs=[pl.BlockSpec((1, tm, hd), lambda b, t: (b, t, 0)),
                   pl.BlockSpec((1, wgt.shape[0], tm), lambda b, t: (b, 0, t))],
        out_shape=[jax.ShapeDtypeStruct((B, T, hd), BF16), jax.ShapeDtypeStruct((B, wgt.shape[0], T), F32)],
        compiler_params=_cparams(("parallel", "parallel")),
        name="nsa_inproj",
    )(h, g, w, wgt, bgt, qg, cos, sin)


def _stack_heads(q, dst_ref, tq):
    lo = _iota((tq, LANES), 1) < HEAD_DIM
    zero = jnp.zeros((tq, LANES), q.dtype)
    for j in range(HEADS_PER_GROUP // 2):
        x = q[:, LANES * j:LANES * (j + 1)]
        dst_ref[(2 * j) * tq:(2 * j + 1) * tq, 0:LANES] = jnp.where(lo, x, zero)
        dst_ref[(2 * j + 1) * tq:(2 * j + 2) * tq, 0:LANES] = jnp.where(lo, zero, x)


def _unstack_heads_t(o_t, gate, o_ref, tq, row0=0):
    for j in range(HEADS_PER_GROUP // 2):
        halves = [o_t[:, h * tq:(h + 1) * tq] * gate[h:h + 1, :] for h in (2 * j, 2 * j + 1)]
        o_ref[0, row0:row0 + tq, LANES * j:LANES * (j + 1)] = jnp.concatenate(halves, axis=0).T.astype(o_ref.dtype)


def _cmp_query_mask_features(sub):
    j = np.arange(sub)
    v = (j // CMP_STRIDE) + ((j % CMP_STRIDE) == CMP_STRIDE - 1)
    feat = np.zeros((sub, LANES), np.float32)
    feat[j, v] = NEG
    return jnp.asarray(feat, BF16)


def _nsa_cmp_kernel(q_ref, kc_ref, vct_ref, ovt_ref, gate_ref, qf_ref, oc_ref, sel_ref, qs_sc, *, tq, sub, n_slc, top_k, n_levels):
    qi = pl.program_id(2)
    n_sub = tq // sub
    n_feats = sub // CMP_STRIDE + 1
    qf = qf_ref[...]
    for u in range(n_sub):
        _stack_heads(q_ref[0, u * sub:(u + 1) * sub, :], qs_sc.at[u], sub)
        for h in range(HEADS_PER_GROUP):
            qs_sc[u, h * sub:(h + 1) * sub, LANES:2 * LANES] = qf
    n_rows = kc_ref.shape[3]
    level_rows = n_rows // n_levels
    n_chunks = 2
    cw = HEADS_PER_GROUP * sub // n_chunks
    n_vis = (qi * tq + tq - 1 - (CMP_LEN - 1)) // CMP_STRIDE + 1

    def attend(rows, u):
        base = (qi * tq + u * sub) // CMP_STRIDE - 2
        n_minus_v = _iota((rows, LANES), 0) - _iota((rows, LANES), 1)
        kf = jnp.where(_iota((rows, LANES), 1) < n_feats, jnp.where(n_minus_v > base, 1.0, 0.0), 0.0)
        ka = jnp.concatenate([kc_ref[0, 0, 0, :rows, :], kf.astype(BF16)], axis=1)
        vct = vct_ref[0, 0, 0, :, :rows]
        scores = [_dot_nt(ka, qs_sc[u, c * cw:(c + 1) * cw, :]) for c in range(n_chunks)]
        outs = []
        p_sum = None
        for s in scores:
            mx = jnp.max(s, axis=0, keepdims=True)
            e = jnp.exp2(s - mx)
            inv = jnp.where(mx > 0.5 * NEG, 1.0 / jnp.maximum(jnp.sum(e, axis=0, keepdims=True), 1.0), 0.0)
            p = e * inv
            outs.append(_dot(vct, p.astype(BF16)))
            for h in range(cw // sub):
                term = p[:, h * sub:(h + 1) * sub]
                p_sum = term if p_sum is None else p_sum + term
        _unstack_heads_t(jnp.concatenate(outs, axis=1), gate_ref[0, :, u * sub:(u + 1) * sub], oc_ref, sub, u * sub)
        p_hi = p_sum.astype(BF16)
        p_lo = (p_sum - p_hi.astype(F32)).astype(BF16)
        return _dot(ovt_ref[:, :rows], p_hi) + _dot(ovt_ref[:, :rows], p_lo)

    def select(imp, u):
        jb = _iota(imp.shape, 0)
        qp = qi * tq + u * sub + _iota(imp.shape, 1)
        cur = qp // SLC_LEN
        forced = (jb == 0) | (jb == cur) | (jb == cur - 1)
        score = jnp.where(jb * SLC_LEN <= qp, jnp.where(forced, FORCED_SCORE, imp), NEG)
        score = jnp.where(jb < n_slc, score, -jnp.inf)
        bias = jnp.full(imp.shape, NEG, F32)
        for _ in range(top_k):
            mx = jnp.max(score, axis=0, keepdims=True)
            idx = jnp.min(jnp.where(score == mx, jb, LANES), axis=0, keepdims=True)
            pick = jb == idx
            bias = jnp.where(pick, 0.0, bias)
            score = jnp.where(pick, -jnp.inf, score)
        sel_ref[0, 0, u * sub:(u + 1) * sub, :] = bias.T.astype(BF16)

    def tile(rows):
        imps = [attend(rows, u) for u in range(n_sub)]
        for u in range(n_sub):
            select(imps[u], u)

    for level in range(n_levels):
        lo_rows, hi_rows = level * level_rows, (level + 1) * level_rows
        cond = (n_vis <= hi_rows) if level == 0 else ((n_vis > lo_rows) & (n_vis <= hi_rows))
        pl.when(cond)(functools.partial(tile, hi_rows))


def _gate_spec(branch, tq):
    return pl.BlockSpec((1, HEADS_PER_GROUP, tq), lambda b, g, qi: (b, branch * N_KV_GROUPS + g, qi))


def _nsa_cmp(q, kc_all, vct_all, overlap_t, gates_t, n_slc, tq=CMP_TQ):
    B, T, hd = q.shape
    G = N_KV_GROUPS
    rows = kc_all.shape[3]
    gw = hd // G
    top_k = min(SLC_TOPK, n_slc)
    n_levels = 4 if rows % (4 * LANES) == 0 else 1
    return pl.pallas_call(
        functools.partial(_nsa_cmp_kernel, tq=tq, sub=CMP_SUB, n_slc=n_slc, top_k=top_k, n_levels=n_levels),
        grid=(B, G, T // tq),
        in_specs=[pl.BlockSpec((1, tq, gw), lambda b, g, qi: (b, qi, g)),
                  pl.BlockSpec((1, 1, 1, rows, LANES), lambda b, g, qi: (b, 0, g, 0, 0)),
                  pl.BlockSpec((1, 1, 1, HEAD_DIM, rows), lambda b, g, qi: (b, 1, g, 0, 0)),
                  pl.BlockSpec(overlap_t.shape, lambda b, g, qi: (0, 0)), _gate_spec(0, tq),
                  pl.BlockSpec((CMP_SUB, LANES), lambda b, g, qi: (0, 0))],
        out_specs=[pl.BlockSpec((1, tq, gw), lambda b, g, qi: (b, qi, g)),
                   pl.BlockSpec((1, 1, tq, LANES), lambda b, g, qi: (b, g, qi, 0))],
        out_shape=[jax.ShapeDtypeStruct((B, T, hd), BF16), jax.ShapeDtypeStruct((B, G, T, LANES), BF16)],
        scratch_shapes=[pltpu.VMEM((tq // CMP_SUB, HEADS_PER_GROUP * CMP_SUB, 2 * LANES), BF16)],
        compiler_params=_cparams(("parallel", "parallel", "parallel")),
        name="nsa_cmp",
    )(q, kc_all, vct_all, overlap_t, gates_t, _cmp_query_mask_features(CMP_SUB))


def _nsa_slc_kernel(q_ref, sel_ref, k_ref, oh_ref, vt_ref, gate_ref, o_ref, qa_sc, s_sc, mt_sc, m_sc, acc_sc, *, tq, tk, n_chunks):
    qi = pl.program_id(2)
    last = (qi * tq) // tk
    _stack_heads(q_ref[0], qa_sc, tq)
    sel = sel_ref[0, 0]
    for h in range(HEADS_PER_GROUP):
        qa_sc[h * tq:(h + 1) * tq, LANES:2 * LANES] = sel
    _softmax_init(m_sc, acc_sc)
    cw = HEADS_PER_GROUP * tq // n_chunks

    def produce(ki, slot):
        start = pl.multiple_of(ki * tk, tk)
        ka = jnp.concatenate([k_ref[0, 0, pl.ds(start, tk), :], oh_ref[pl.ds(start, tk), :]], axis=1)
        for c in range(n_chunks):
            s = _dot_nt(ka, qa_sc[c * cw:(c + 1) * cw, :])
            s_sc[slot, :, c * cw:(c + 1) * cw] = s
            mt_sc[slot, :, c * cw:(c + 1) * cw] = _column_max(s)

    def consume(ki, slot, diag):
        vt = vt_ref[0, 0, ki]
        for c in range(n_chunks):
            cols = slice(c * cw, (c + 1) * cw)
            s = s_sc[slot, :, cols]
            m_tile = mt_sc[slot, :, cols]
            if diag:
                kpos = ki * tk + _iota(s.shape, 0)
                qpos = qi * tq + (_iota(s.shape, 1) & (tq - 1))
                s = jnp.where(kpos <= qpos, s, NEG)
                m_tile = None
            _softmax_update_t(s, vt, m_sc.at[:, cols], acc_sc.at[:, cols], m_tile)

    _pipelined_key_loop(last, produce, consume)
    _unstack_heads_t(_softmax_result(acc_sc), gate_ref[0], o_ref, tq)


def _nsa_slc(q, sel, ks, vst, gates_t, tq=SLC_TQ, tk=SLC_TK):
    B, T, hd = q.shape
    G = N_KV_GROUPS
    gw = hd // G
    cols = HEADS_PER_GROUP * tq
    assert tk % tq == 0
    onehot = (jnp.arange(T)[:, None] // SLC_LEN == jnp.arange(LANES)[None, :]).astype(BF16)
    return pl.pallas_call(
        functools.partial(_nsa_slc_kernel, tq=tq, tk=tk, n_chunks=4),
        grid=(B, G, T // tq),
        in_specs=[pl.BlockSpec((1, tq, gw), lambda b, g, qi: (b, qi, g)),
                  pl.BlockSpec((1, 1, tq, LANES), lambda b, g, qi: (b, g, qi, 0)),
                  pl.BlockSpec((1, 1, T, LANES), lambda b, g, qi: (b, g, 0, 0)),
                  pl.BlockSpec((T, LANES), lambda b, g, qi: (0, 0)),
                  pl.BlockSpec((1, 1, T // tk, HEAD_DIM, tk), lambda b, g, qi: (b, g, 0, 0, 0)), _gate_spec(1, tq)],
        out_specs=pl.BlockSpec((1, tq, gw), lambda b, g, qi: (b, qi, g)),
        out_shape=jax.ShapeDtypeStruct((B, T, hd), BF16),
        scratch_shapes=[pltpu.VMEM((cols, 2 * LANES), BF16), pltpu.VMEM((2, tk, cols), F32),
                        pltpu.VMEM((2, 1, cols), F32), pltpu.VMEM((1, cols), F32),
                        pltpu.VMEM((ACC_ROWS, cols), F32)],
        compiler_params=_cparams(("parallel", "parallel", "arbitrary")),
        name="nsa_slc",
    )(q, sel, ks, onehot, vst, gates_t)


def _nsa_win_kernel(q_ref, k_ref, vt_ref, gate_ref, o_ref, qs_sc, s_sc, m_sc, acc_sc, *, tq, n_back, n_chunks):
    qi = pl.program_id(2)
    _stack_heads(q_ref[0], qs_sc, tq)
    _softmax_init(m_sc, acc_sc)
    cw = HEADS_PER_GROUP * tq // n_chunks

    def produce(ti, slot):
        start = pl.multiple_of(ti * tq, tq)
        k = k_ref[0, 0, pl.ds(start, tq), :]
        for c in range(n_chunks):
            s_sc[slot, :, c * cw:(c + 1) * cw] = _dot_nt(k, qs_sc[c * cw:(c + 1) * cw, :])

    def consume(ti, slot, kind):
        vt = vt_ref[0, 0, ti]
        for c in range(n_chunks):
            cols = slice(c * cw, (c + 1) * cw)
            s = s_sc[slot, :, cols]
            r = _iota(s.shape, 0)
            q_in_tile = _iota(s.shape, 1) & (tq - 1)
            if kind == "oldest":
                s = jnp.where(r > q_in_tile, s, NEG)
            elif kind == "diag":
                s = jnp.where(r <= q_in_tile, s, NEG)
            _softmax_update_t(s, vt, m_sc.at[:, cols], acc_sc.at[:, cols])

    def kind_of(back):
        return "oldest" if back == n_back else ("diag" if back == 0 else "full")

    for first in range(n_back + 1):
        cond = (qi >= n_back) if first == n_back else (qi == first)

        @pl.when(cond)
        def _(first=first):
            backs = list(range(first, -1, -1))
            produce(qi - backs[0], 0)
            for n, back in enumerate(backs):
                if n + 1 < len(backs):
                    produce(qi - backs[n + 1], (n + 1) % 2)
                consume(qi - back, n % 2, kind_of(back))

    _unstack_heads_t(_softmax_result(acc_sc), gate_ref[0], o_ref, tq)


def _nsa_win(q, kw, vwt, gates_t, tq=WIN_TQ):
    B, T, hd = q.shape
    G = N_KV_GROUPS
    gw = hd // G
    cols = HEADS_PER_GROUP * tq
    return pl.pallas_call(
        functools.partial(_nsa_win_kernel, tq=tq, n_back=WIN // tq, n_chunks=4),
        grid=(B, G, T // tq),
        in_specs=[pl.BlockSpec((1, tq, gw), lambda b, g, qi: (b, qi, g)),
                  pl.BlockSpec((1, 1, T, LANES), lambda b, g, qi: (b, g, 0, 0)),
                  pl.BlockSpec((1, 1, T // tq, HEAD_DIM, tq), lambda b, g, qi: (b, g, 0, 0, 0)), _gate_spec(2, tq)],
        out_specs=pl.BlockSpec((1, tq, gw), lambda b, g, qi: (b, qi, g)),
        out_shape=jax.ShapeDtypeStruct((B, T, hd), BF16),
        scratch_shapes=[pltpu.VMEM((cols, LANES), BF16), pltpu.VMEM((2, tq, cols), F32), pltpu.VMEM((1, cols), F32),
                        pltpu.VMEM((ACC_ROWS, cols), F32)],
        compiler_params=_cparams(("parallel", "parallel", "arbitrary")),
        name="nsa_win",
    )(q, kw, vwt, gates_t)


def _pad_cols(a, n):
    return jnp.pad(a, ((0, 0), (0, n - a.shape[1])))


def _lane_vec(v):
    return jnp.tile(v.astype(F32), 2).reshape(1, LANES)


def _bias_feature_selector():
    sel = np.zeros((N_BIAS_PARTS * LANES, N_HEADS * HEAD_DIM), np.float32)
    for part in range(N_BIAS_PARTS):
        for h in range(N_HEADS):
            sel[part * LANES + h, (h // 2) * LANES + (h % 2) * N_BIAS_PARTS + part] = 1.0
    return jnp.asarray(sel, BF16)


def kernel(x, positions, a_norm, a_w_in, a_b_f, a_q_gain, a_k_gain, a_w_out, kv_norm, kv_w, kc_pe, vc_pe, kc_w1, kc_w2, vc_w1, vc_w2, kc_gain, ks_gain, kw_gain, b_norm, b_w_in, b_b_gate, b_q_gain, b_w_out, f_norm, f_w_up, f_conv_w, f_conv_b, f_w_down):
    B, T, D = x.shape
    hd = N_HEADS * HEAD_DIM
    G = N_KV_GROUPS
    n_a = a_norm.shape[0]
    n_b = b_norm.shape[0]
    depth = n_a + n_b
    n_slc = T // SLC_LEN
    n_cmp = (T - CMP_LEN) // CMP_STRIDE + 1
    assert T % 1024 == 0 and n_slc <= LANES and hd == 1024 and D == 1024

    half = ROT_DIM // 2
    inv = ROPE_THETA ** (-jnp.arange(half, dtype=F32) * (2.0 / ROT_DIM))
    head_inv = jnp.concatenate([inv, inv, jnp.zeros((HEAD_DIM - ROT_DIM,), F32)])
    head_sign = jnp.concatenate([-jnp.ones((half,), F32), jnp.ones((half,), F32), jnp.zeros((HEAD_DIM - ROT_DIM,), F32)])
    inv_lane = jnp.tile(head_inv, 2).reshape(1, LANES)
    sign_lane = jnp.tile(head_sign, 2).reshape(1, LANES)
    cos_t, sin_t = _rope_tables(positions, inv_lane, sign_lane)
    end_pos = positions[:, CMP_LEN - 1::CMP_STRIDE]
    end_pos = jnp.pad(end_pos, ((0, 0), (0, T // CMP_STRIDE - n_cmp)))
    cos_c, sin_c = _rope_tables(end_pos, inv_lane, sign_lane)

    cs = jnp.arange(T // CMP_STRIDE) * CMP_STRIDE
    ss = jnp.arange(LANES) * SLC_LEN
    overlap_t = (jnp.maximum(jnp.minimum(cs[None, :] + CMP_LEN, ss[:, None] + SLC_LEN)
                             - jnp.maximum(cs[None, :], ss[:, None]), 0).astype(F32) / CMP_LEN).astype(BF16)

    h = x
    kv = None
    for layer in range(depth):
        if layer < n_a:
            i = layer
            w = jnp.concatenate([a_w_in[i][:, :2 * hd], _pad_cols(a_w_in[i][:, 3 * hd:], LANES)], axis=1).astype(BF16)
            wvt = a_w_in[i][:, 2 * hd:3 * hd].T.astype(BF16)
            bf = _pad_cols(a_b_f[i].reshape(1, -1), LANES)
            q, k, vt, cf = _fox_inproj(h, a_norm[i].reshape(1, D), w, wvt, bf, _lane_vec(a_q_gain[i]),
                                       _lane_vec(a_k_gain[i]), _bias_feature_selector())
            mix, w_out = [_fox_attn(q, k, cf, vt)], a_w_out[i]
        else:
            i = layer - n_a
            kc_all, vct_all, ks, kw, vst, vwt = kv
            q, gates_t = _nsa_inproj(h, b_norm[i].reshape(1, D), b_w_in[i][:, :hd].astype(BF16),
                                     b_w_in[i][:, hd:].T.astype(BF16), b_b_gate[i].reshape(-1, 1),
                                     _lane_vec(b_q_gain[i]), cos_t, sin_t)
            o_c, sel = _nsa_cmp(q, kc_all, vct_all, overlap_t, gates_t, n_slc)
            o_s = _nsa_slc(q, sel, ks, vst, gates_t)
            o_w = _nsa_win(q, kw, vwt, gates_t)
            mix, w_out = [o_c, o_s, o_w], b_w_out[i]
        h = _mix_out_conv_ffn(h, mix, w_out.astype(BF16), f_norm[layer].reshape(1, D), f_w_up[layer].astype(BF16),
                              f_conv_w[layer], f_conv_b[layer].reshape(1, -1), f_w_down[layer].astype(BF16))
        if layer == n_a - 1:
            w6 = kv_w.reshape(D, 6, G, HEAD_DIM)
            raw_cols = w6[:, 0:2].reshape(D, 2 * G * HEAD_DIM)
            wk = w6[:, (2, 4)]
            dup_cols = jnp.concatenate([wk, wk], axis=-1).reshape(D, 2 * G * LANES)
            wkv = jnp.concatenate([raw_cols, dup_cols], axis=1).astype(BF16)
            wvt = w6[:, (3, 5)].reshape(D, 2 * G * HEAD_DIM).T.astype(BF16)
            craw, ks, kw, vst, vwt = _kvproj(h, kv_norm.reshape(1, D), wkv, wvt, _lane_vec(ks_gain),
                                             _lane_vec(kw_gain), cos_t, sin_t)
            r = craw.reshape(B, T, 2, G, HEAD_DIM).transpose(0, 2, 3, 1, 4).reshape(B, 2, G, T // CMP_STRIDE, CMP_STRIDE * HEAD_DIM)
            w1 = jnp.stack([kc_w1, vc_w1]).astype(BF16)
            pe = jnp.stack([kc_pe.reshape(1, -1), vc_pe.reshape(1, -1)])
            pe = jnp.pad(pe, ((0, 0), (0, 7), (0, 0))).astype(BF16)
            w2 = jnp.stack([jnp.concatenate([kc_w2, kc_w2], axis=1), jnp.concatenate([vc_w2, vc_w2], axis=1)]).astype(BF16)
            w2t = jnp.stack([kc_w2.T, vc_w2.T]).astype(BF16)
            kc_all, vct_all = _compress(r, w1, pe, w2, w2t, _lane_vec(kc_gain), cos_c, sin_c, n_cmp)
            kv = (kc_all, vct_all, ks, kw, vst, vwt)
    return h
```

```python
import functools

import numpy as np
import jax
import jax.numpy as jnp
from jax import lax
from jax.experimental import pallas as pl
from jax.experimental.pallas import tpu as pltpu

F32 = jnp.float32
BF16 = jnp.bfloat16

LANES = 128
SUBLANES = 8
HEAD_DIM = 64
N_HEADS = 16
N_PAIRS = N_HEADS // 2
N_KV_GROUPS = 2
HEADS_PER_GROUP = N_HEADS // N_KV_GROUPS
ROT_DIM = HEAD_DIM // 4
ROPE_THETA = 500000.0
CMP_LEN = 32
CMP_STRIDE = 16
SLC_LEN = 64
SLC_TOPK = 16
WIN = 512
CONV_W = 3
RMS_EPS = 1e-6
NEG = -1e30
FORCED_SCORE = 1e6
LOG2E = 1.4426950408889634
Q_SCALE = HEAD_DIM ** -0.5 * LOG2E
N_BIAS_PARTS = 3
ONES_ROWS = 16
ACC_ROWS = HEAD_DIM + ONES_ROWS

ROW_TILE = 512
FOX_TQ = 512
FOX_TK = ROW_TILE
SLC_TQ = 256
SLC_TK = ROW_TILE
WIN_TQ = 256
CMP_TQ = 256
CMP_SUB = 128

VMEM_LIMIT = 48 * 1024 * 1024


def _cparams(sem, flags=None):
    return pltpu.CompilerParams(dimension_semantics=sem, vmem_limit_bytes=VMEM_LIMIT, flags=flags)


def _iota(shape, axis):
    return lax.broadcasted_iota(jnp.int32, shape, axis)


def _row_rms(x, g):
    ms = jnp.mean(x * x, axis=-1, keepdims=True)
    return x * lax.rsqrt(ms + RMS_EPS) * g


def _pair_rms(y, gain):
    lo = _iota(y.shape, 1) < HEAD_DIM
    y2 = y * y
    s_lo = jnp.sum(jnp.where(lo, y2, 0.0), axis=-1, keepdims=True)
    s_hi = jnp.sum(jnp.where(lo, 0.0, y2), axis=-1, keepdims=True)
    ms = jnp.where(lo, s_lo, s_hi) * (1.0 / HEAD_DIM)
    return y * lax.rsqrt(ms + RMS_EPS) * gain


def _pair_rope(y, cos, sin):
    lane = _iota(y.shape, 1) & (HEAD_DIM - 1)
    partner = jnp.where(lane < ROT_DIM // 2,
                        pltpu.roll(y, LANES - ROT_DIM // 2, 1),
                        pltpu.roll(y, ROT_DIM // 2, 1))
    return y * cos + partner * sin


def _dot(a, b):
    return jnp.dot(a, b, preferred_element_type=F32)


def _dot_nt(a, b):
    return lax.dot_general(a, b, (((1,), (1,)), ((), ())), preferred_element_type=F32)


def _column_max(s):
    tk, n = s.shape
    return jnp.max(jnp.max(s.reshape(tk // SUBLANES, SUBLANES, n), axis=0), axis=0, keepdims=True)


def _softmax_update_t(s, vt, m_ref, acc_ref, m_tile=None):
    tk, n = s.shape
    m_old = m_ref[...]
    if m_tile is None:
        m_tile = _column_max(s)
    m_new = jnp.maximum(m_old, m_tile)
    alpha = jnp.exp2(m_old - m_new)
    p = jnp.exp2(s - m_new).astype(BF16)
    vta = jnp.concatenate([vt, jnp.ones((ONES_ROWS, tk), vt.dtype)], axis=0)
    acc_ref[...] = alpha * acc_ref[...] + _dot(vta, p)
    m_ref[...] = m_new


def _pipelined_key_loop(n_full, produce, consume, unroll=4):
    assert unroll % 2 == 0

    def run(base, count):
        for i in range(count):
            produce(base + i + 1, (i + 1) % 2)
            consume(base + i, i % 2, False)

    produce(0, 0)

    def body(j, carry):
        run(unroll * j, unroll)
        return carry

    lax.fori_loop(0, n_full // unroll, body, 0)
    base = (n_full // unroll) * unroll
    rem = n_full - base
    step = unroll // 2
    while step >= 2:
        pl.when((rem & step) != 0)(functools.partial(run, base, step))
        base = base + (rem & step)
        step //= 2

    @pl.when((rem & 1) != 0)
    def _():
        run(base, 1)
        consume(base + 1, 1, True)

    @pl.when((rem & 1) == 0)
    def _():
        consume(base, 0, True)


def _softmax_init(m_ref, acc_ref):
    m_ref[...] = jnp.full_like(m_ref, NEG)
    acc_ref[...] = jnp.zeros_like(acc_ref)


def _softmax_result(acc_ref):
    acc = acc_ref[...]
    return acc[:HEAD_DIM, :] * (1.0 / acc[HEAD_DIM:HEAD_DIM + 1, :])


def _rope_table_kernel(pos_ref, inv_ref, sign_ref, c_ref, s_ref):
    ang = pos_ref[0].astype(F32) * inv_ref[...]
    c_ref[0] = jnp.cos(ang)
    s_ref[0] = jnp.sin(ang) * sign_ref[...]


def _rope_tables(pos, inv_lane, sign_lane):
    B, T = pos.shape
    tm = min(T, ROW_TILE)
    spec = pl.BlockSpec((1, tm, LANES), lambda b, t: (b, t, 0))
    vec = pl.BlockSpec((1, LANES), lambda b, t: (0, 0))
    return pl.pallas_call(
        _rope_table_kernel,
        grid=(B, T // tm),
        in_specs=[pl.BlockSpec((1, tm, 1), lambda b, t: (b, t, 0)), vec, vec],
        out_specs=[spec, spec],
        out_shape=[jax.ShapeDtypeStruct((B, T, LANES), F32)] * 2,
        compiler_params=_cparams(("parallel", "parallel")),
        name="rope_tables",
    )(pos.reshape(B, T, 1), inv_lane, sign_lane)


def _fox_inproj_kernel(x_ref, g_ref, w_ref, wvt_ref, bf_ref, qg_ref, kg_ref, sel_ref,
                       q_ref, k_ref, vt_ref, cf_ref, carry_sc, *, tm, hd):
    ti = pl.program_id(1)
    xn = _row_rms(x_ref[0], g_ref[...]).astype(BF16)
    for j in range(hd // 256):
        for part, (ref, gain, mul) in enumerate(((q_ref, qg_ref, Q_SCALE), (k_ref, kg_ref, 1.0))):
            c0 = part * hd + 256 * j
            y = _dot(xn, w_ref[:, c0:c0 + 256])
            for hh in range(2):
                blk = _pair_rms(y[:, LANES * hh:LANES * (hh + 1)], gain[...]) * mul
                ref[0, :, 256 * j + LANES * hh:256 * j + LANES * (hh + 1)] = blk.astype(BF16)
        yt = _dot_nt(wvt_ref[256 * j:256 * (j + 1), :], xn)
        for hh in range(2):
            vt_ref[0, 2 * j + hh, 0] = yt[LANES * hh:LANES * (hh + 1), :].astype(BF16)
    z = _dot(xn, w_ref[:, 2 * hd:2 * hd + LANES]) + bf_ref[...]
    lf = jnp.minimum(z, 0.0) - jnp.log1p(jnp.exp(-jnp.abs(z)))
    row = _iota(lf.shape, 0)
    sh = 1
    while sh < tm:
        lf = lf + jnp.where(row >= sh, pltpu.roll(lf, sh, 0), 0.0)
        sh *= 2

    @pl.when(ti == 0)
    def _():
        carry_sc[...] = jnp.zeros_like(carry_sc)

    c = lf + carry_sc[0:1, :]
    carry_sc[...] = jnp.broadcast_to(c[tm - 1:tm, :], carry_sc.shape)
    rest = c * (-LOG2E)
    pieces = []
    for _ in range(N_BIAS_PARTS):
        piece = rest.astype(BF16)
        pieces.append(piece)
        rest = rest - piece.astype(F32)
    cf_ref[0] = _dot(jnp.concatenate(pieces, axis=1), sel_ref[...]).astype(BF16)


def _fox_inproj(x, g, w, wvt, bf, qg, kg, sel, tm=ROW_TILE):
    B, T, D = x.shape
    hd = N_HEADS * HEAD_DIM
    act = pl.BlockSpec((1, tm, hd), lambda b, t: (b, t, 0))
    vec = lambda n: pl.BlockSpec((1, n), lambda b, t: (0, 0))
    full = lambda a: pl.BlockSpec(a.shape, lambda b, t: (0,) * a.ndim)
    return pl.pallas_call(
        functools.partial(_fox_inproj_kernel, tm=tm, hd=hd),
        grid=(B, T // tm),
        in_specs=[pl.BlockSpec((1, tm, D), lambda b, t: (b, t, 0)), vec(D), full(w), full(wvt),
                  vec(LANES), vec(LANES), vec(LANES), full(sel)],
        out_specs=[act, act, pl.BlockSpec((1, N_PAIRS, 1, LANES, tm), lambda b, t: (b, 0, t, 0, 0)), act],
        out_shape=[jax.ShapeDtypeStruct((B, T, hd), BF16), jax.ShapeDtypeStruct((B, T, hd), BF16),
                   jax.ShapeDtypeStruct((B, N_PAIRS, T // tm, LANES, tm), BF16),
                   jax.ShapeDtypeStruct((B, T, hd), BF16)],
        scratch_shapes=[pltpu.VMEM((SUBLANES, LANES), F32)],
        compiler_params=_cparams(("arbitrary", "arbitrary")),
        name="fox_inproj",
    )(x, g, w, wvt, bf, qg, kg, sel)


def _fox_attn_kernel(q_ref, k_ref, cf_ref, vt_ref, o_ref, qa_sc, s_sc, mt_sc, m_sc, acc_sc, *, tq, tk):
    qi = pl.program_id(2)
    q = q_ref[0]
    lane = _iota(q.shape, 1)
    lo = lane < HEAD_DIM
    zero = jnp.zeros_like(q)
    for hh in range(2):
        qa_sc[hh, :, 0:LANES] = jnp.where(lo, q, zero) if hh == 0 else jnp.where(lo, zero, q)
        feat = jnp.where(lane < N_BIAS_PARTS * (hh + 1), 1.0, 0.0)
        qa_sc[hh, :, LANES:2 * LANES] = jnp.where(lane >= N_BIAS_PARTS * hh, feat, 0.0).astype(BF16)
        _softmax_init(m_sc.at[hh], acc_sc.at[hh])

    def produce(ki, slot):
        start = pl.multiple_of(ki * tk, tk)
        ka = jnp.concatenate([k_ref[0, pl.ds(start, tk), :], cf_ref[0, pl.ds(start, tk), :]], axis=1)
        for hh in range(2):
            s = _dot_nt(ka, qa_sc[hh])
            s_sc[slot, hh] = s
            mt_sc[slot, hh] = _column_max(s)

    def consume(ki, slot, diag):
        vt = vt_ref[0, 0, ki]
        for hh in range(2):
            s = s_sc[slot, hh]
            m_tile = mt_sc[slot, hh]
            if diag:
                s = jnp.where(_iota(s.shape, 0) <= _iota(s.shape, 1), s, NEG)
                m_tile = None
            _softmax_update_t(s, vt[HEAD_DIM * hh:HEAD_DIM * (hh + 1), :], m_sc.at[hh], acc_sc.at[hh], m_tile)

    _pipelined_key_loop(qi, produce, consume)
    o_t = jnp.concatenate([_softmax_result(acc_sc.at[hh]) for hh in range(2)], axis=0)
    o_ref[0] = o_t.T.astype(BF16)


def _fox_attn(q, k, cf, vt, tq=FOX_TQ, tk=FOX_TK):
    B, T, hd = q.shape
    assert tq == tk
    nk = T // tk
    seq = pl.BlockSpec((1, T, LANES), lambda b, p, qi: (b, 0, p))
    return pl.pallas_call(
        functools.partial(_fox_attn_kernel, tq=tq, tk=tk),
        grid=(B, N_PAIRS, T // tq),
        in_specs=[pl.BlockSpec((1, tq, LANES), lambda b, p, qi: (b, qi, p)), seq, seq,
                  pl.BlockSpec((1, 1, nk, LANES, tk), lambda b, p, qi: (b, p, 0, 0, 0))],
        out_specs=pl.BlockSpec((1, tq, LANES), lambda b, p, qi: (b, qi, p)),
        out_shape=jax.ShapeDtypeStruct((B, T, hd), BF16),
        scratch_shapes=[pltpu.VMEM((2, tq, 2 * LANES), BF16), pltpu.VMEM((2, 2, tk, tq), F32),
                        pltpu.VMEM((2, 2, 1, tq), F32), pltpu.VMEM((2, 1, tq), F32),
                        pltpu.VMEM((2, ACC_ROWS, tq), F32)],
        compiler_params=_cparams(("parallel", "parallel", "arbitrary")),
        name="fox_attn",
    )(q, k, cf, vt)


def _ffn_kernel(*refs, n_mix, tt, tf, d_ff):
    h_ref = refs[0]
    mix_refs = refs[1:1 + n_mix]
    wo_ref, g_ref, wup_ref, cw_ref, cb_ref, wd_ref, out_ref, a_sc, carry_sc = refs[1 + n_mix:]
    ti = pl.program_id(1)

    @pl.when(ti == 0)
    def _():
        carry_sc[...] = jnp.zeros_like(carry_sc)

    o = mix_refs[0][0]
    if n_mix > 1:
        o = o.astype(F32)
        for ref in mix_refs[1:]:
            o = o + ref[0].astype(F32)
        o = o.astype(BF16)
    x = h_ref[0] + _dot(o, wo_ref[...])
    xn = _row_rms(x, g_ref[...]).astype(BF16)
    row8 = _iota((SUBLANES, tf), 0)

    def conv(u, c0):
        prev8 = carry_sc[:, c0:c0 + tf]
        um1 = pltpu.roll(u, 1, 0)
        um2 = pltpu.roll(u, 2, 0)
        top1 = jnp.where(row8 == 0, prev8[7:8, :], um1[0:SUBLANES, :])
        top2 = jnp.where(row8 == 0, prev8[6:7, :], jnp.where(row8 == 1, prev8[7:8, :], um2[0:SUBLANES, :]))
        um1 = jnp.concatenate([top1, um1[SUBLANES:, :]], axis=0)
        um2 = jnp.concatenate([top2, um2[SUBLANES:, :]], axis=0)
        carry_sc[:, c0:c0 + tf] = u[tt - SUBLANES:, :]
        cw = cw_ref[:, c0:c0 + tf]
        return cb_ref[:, c0:c0 + tf] + cw[0:1, :] * um2 + cw[1:2, :] * um1 + cw[2:3, :] * u

    for f in range(d_ff // tf):
        g0 = f * tf
        cg = conv(_dot(xn, wup_ref[:, g0:g0 + tf]), g0)
        cv = conv(_dot(xn, wup_ref[:, d_ff + g0:d_ff + g0 + tf]), d_ff + g0)
        a_sc[:, g0:g0 + tf] = (cg * jax.nn.sigmoid(cg) * cv).astype(BF16)
    out_ref[0] = x + _dot(a_sc[...], wd_ref[...])


def _mix_out_conv_ffn(h, mix, w_out, g, w_up, conv_w, conv_b, w_down, tt=ROW_TILE, tf=256):
    B, T, D = h.shape
    d_ff = w_down.shape[0]
    act = lambda c: pl.BlockSpec((1, tt, c), lambda b, t: (b, t, 0))
    resident = lambda a: pl.BlockSpec(a.shape, lambda b, t: (0,) * a.ndim, pipeline_mode=pl.Buffered(1))
    return pl.pallas_call(
        functools.partial(_ffn_kernel, n_mix=len(mix), tt=tt, tf=tf, d_ff=d_ff),
        grid=(B, T // tt),
        in_specs=[act(D)] + [act(m.shape[-1]) for m in mix]
                 + [resident(w_out), resident(g), resident(w_up), resident(conv_w), resident(conv_b), resident(w_down)],
        out_specs=act(D),
        out_shape=jax.ShapeDtypeStruct((B, T, D), F32),
        scratch_shapes=[pltpu.VMEM((tt, d_ff), BF16), pltpu.VMEM((SUBLANES, 2 * d_ff), F32)],
        compiler_params=_cparams(("arbitrary", "arbitrary")),
        name="conv_ffn",
    )(h, *mix, w_out, g, w_up, conv_w, conv_b, w_down)


def _kvproj_kernel(h_ref, g_ref, w_ref, wvt_ref, ksg_ref, kwg_ref, c_ref, s_ref,
                   craw_ref, ks_ref, kw_ref, vst_ref, vwt_ref, *, tm):
    xn = _row_rms(h_ref[0], g_ref[...]).astype(BF16)
    cos = c_ref[0]
    sin = s_ref[0]
    craw_ref[0] = _dot(xn, w_ref[:, 0:256])
    for idx, (ref, gain) in enumerate(((ks_ref, ksg_ref), (kw_ref, kwg_ref))):
        y = _dot(xn, w_ref[:, 256 * (idx + 1):256 * (idx + 2)])
        for grp in range(N_KV_GROUPS):
            blk = _pair_rope(_pair_rms(y[:, LANES * grp:LANES * (grp + 1)], gain[...]), cos, sin)
            ref[0, grp] = blk.astype(BF16)
    yt = _dot_nt(wvt_ref[...], xn).astype(BF16)
    for grp in range(N_KV_GROUPS):
        vst_ref[0, grp, 0] = yt[HEAD_DIM * grp:HEAD_DIM * (grp + 1), :]
        r0 = HEAD_DIM * (N_KV_GROUPS + grp)
        for c in range(tm // WIN_TQ):
            vwt_ref[0, grp, c] = yt[r0:r0 + HEAD_DIM, WIN_TQ * c:WIN_TQ * (c + 1)]


def _kvproj(h, g, w, wvt, ksg, kwg, cos, sin, tm=ROW_TILE):
    B, T, D = h.shape
    G = N_KV_GROUPS
    vec = lambda n: pl.BlockSpec((1, n), lambda b, t: (0, 0))
    full = lambda a: pl.BlockSpec(a.shape, lambda b, t: (0,) * a.ndim)
    tab = pl.BlockSpec((1, tm, LANES), lambda b, t: (b, t, 0))
    dup = pl.BlockSpec((1, G, tm, LANES), lambda b, t: (b, 0, t, 0))
    dup_shape = jax.ShapeDtypeStruct((B, G, T, LANES), BF16)
    nw = tm // WIN_TQ
    return pl.pallas_call(
        functools.partial(_kvproj_kernel, tm=tm),
        grid=(B, T // tm),
        in_specs=[pl.BlockSpec((1, tm, D), lambda b, t: (b, t, 0)), vec(D), full(w), full(wvt),
                  vec(LANES), vec(LANES), tab, tab],
        out_specs=[pl.BlockSpec((1, tm, 256), lambda b, t: (b, t, 0)), dup, dup,
                   pl.BlockSpec((1, G, 1, HEAD_DIM, tm), lambda b, t: (b, 0, t, 0, 0)),
                   pl.BlockSpec((1, G, nw, HEAD_DIM, WIN_TQ), lambda b, t: (b, 0, t, 0, 0))],
        out_shape=[jax.ShapeDtypeStruct((B, T, 256), F32), dup_shape, dup_shape,
                   jax.ShapeDtypeStruct((B, G, T // tm, HEAD_DIM, tm), BF16),
                   jax.ShapeDtypeStruct((B, G, T // WIN_TQ, HEAD_DIM, WIN_TQ), BF16)],
        compiler_params=_cparams(("parallel", "parallel")),
        name="kvproj",
    )(h, g, w, wvt, ksg, kwg, cos, sin)


def _compress_kernel(r_ref, w1_ref, pe_ref, w2_ref, w2t_ref, gain_ref, c_ref, s_ref, kc_ref, vct_ref, *, n_cmp):
    r = r_ref[0, 0, 0].astype(BF16)
    half = r.shape[1]
    a = _dot(r, w1_ref[0, :half, :])
    b = _dot(r, w1_ref[0, half:, :])
    peb = _dot(pe_ref[0], w1_ref[0])[0:1, :]
    rows = r.shape[0]
    hid = a + pltpu.roll(b, rows - 1, 0) + peb
    act = jax.nn.gelu(hid).astype(BF16)
    y = _dot(act, w2_ref[0])
    yk = _pair_rope(_pair_rms(y, gain_ref[...]), c_ref[0], s_ref[0])
    kc_ref[0, 0, 0] = jnp.where(_iota(y.shape, 0) < n_cmp, yk, 0.0).astype(BF16)
    yt = _dot_nt(w2t_ref[0], act)
    vct_ref[0, 0, 0] = jnp.where(_iota(yt.shape, 1) < n_cmp, yt, 0.0).astype(BF16)


def _compress(r, w1, pe, w2, w2t, gain, cos, sin, n_cmp):
    B, _, G, rows, width = r.shape
    per_kv = lambda a: pl.BlockSpec((1,) + a.shape[1:], lambda b, kv, g: (kv,) + (0,) * (a.ndim - 1))
    return pl.pallas_call(
        functools.partial(_compress_kernel, n_cmp=n_cmp),
        grid=(B, 2, G),
        in_specs=[pl.BlockSpec((1, 1, 1, rows, width), lambda b, kv, g: (b, kv, g, 0, 0)),
                  per_kv(w1), per_kv(pe), per_kv(w2), per_kv(w2t),
                  pl.BlockSpec((1, LANES), lambda b, kv, g: (0, 0)),
                  pl.BlockSpec((1, rows, LANES), lambda b, kv, g: (b, 0, 0)),
                  pl.BlockSpec((1, rows, LANES), lambda b, kv, g: (b, 0, 0))],
        out_specs=[pl.BlockSpec((1, 1, 1, rows, LANES), lambda b, kv, g: (b, kv, g, 0, 0)),
                   pl.BlockSpec((1, 1, 1, HEAD_DIM, rows), lambda b, kv, g: (b, kv, g, 0, 0))],
        out_shape=[jax.ShapeDtypeStruct((B, 2, G, rows, LANES), BF16),
                   jax.ShapeDtypeStruct((B, 2, G, HEAD_DIM, rows), BF16)],
        compiler_params=_cparams(("parallel", "parallel", "parallel")),
        name="compress",
    )(r, w1, pe, w2, w2t, gain, cos, sin)


def _nsa_inproj_kernel(h_ref, g_ref, w_ref, wgt_ref, bgt_ref, qg_ref, c_ref, s_ref, q_ref, gate_ref, *, hd):
    xn = _row_rms(h_ref[0], g_ref[...]).astype(BF16)
    cos = c_ref[0]
    sin = s_ref[0]
    for j in range(hd // 256):
        y = _dot(xn, w_ref[:, 256 * j:256 * (j + 1)])
        for hh in range(2):
            blk = _pair_rope(_pair_rms(y[:, LANES * hh:LANES * (hh + 1)], qg_ref[...]), cos, sin) * Q_SCALE
            q_ref[0, :, 256 * j + LANES * hh:256 * j + LANES * (hh + 1)] = blk.astype(BF16)
    gate_ref[0] = jax.nn.sigmoid(_dot_nt(wgt_ref[...], xn) + bgt_ref[...])


def _nsa_inproj(h, g, w, wgt, bgt, qg, cos, sin, tm=ROW_TILE):
    B, T, D = h.shape
    hd = N_HEADS * HEAD_DIM
    vec = lambda n: pl.BlockSpec((1, n), lambda b, t: (0, 0))
    full = lambda a: pl.BlockSpec(a.shape, lambda b, t: (0,) * a.ndim)
    tab = pl.BlockSpec((1, tm, LANES), lambda b, t: (b, t, 0))
    return pl.pallas_call(
        functools.partial(_nsa_inproj_kernel, hd=hd),
        grid=(B, T // tm),
        in_specs=[pl.BlockSpec((1, tm, D), lambda b, t: (b, t, 0)), vec(D), full(w), full(wgt), full(bgt),
                  vec(LANES), tab, tab],
        out_specs=[pl.BlockSpec((1, tm, hd), lambda b, t: (b, t, 0)),
                   pl.BlockSpec((1, wgt.shape[0], tm), lambda b, t: (b, 0, t))],
        out_shape=[jax.ShapeDtypeStruct((B, T, hd), BF16), jax.ShapeDtypeStruct((B, wgt.shape[0], T), F32)],
        compiler_params=_cparams(("parallel", "parallel")),
        name="nsa_inproj",
    )(h, g, w, wgt, bgt, qg, cos, sin)


def _stack_heads(q, dst_ref, tq):
    lo = _iota((tq, LANES), 1) < HEAD_DIM
    zero = jnp.zeros((tq, LANES), q.dtype)
    for j in range(HEADS_PER_GROUP // 2):
        x = q[:, LANES * j:LANES * (j + 1)]
        dst_ref[(2 * j) * tq:(2 * j + 1) * tq, 0:LANES] = jnp.where(lo, x, zero)
        dst_ref[(2 * j + 1) * tq:(2 * j + 2) * tq, 0:LANES] = jnp.where(lo, zero, x)


def _unstack_heads_t(o_t, gate, o_ref, tq, row0=0):
    for j in range(HEADS_PER_GROUP // 2):
        halves = [o_t[:, h * tq:(h + 1) * tq] * gate[h:h + 1, :] for h in (2 * j, 2 * j + 1)]
        o_ref[0, row0:row0 + tq, LANES * j:LANES * (j + 1)] = jnp.concatenate(halves, axis=0).T.astype(o_ref.dtype)


def _cmp_query_mask_features(sub):
    j = np.arange(sub)
    v = (j // CMP_STRIDE) + ((j % CMP_STRIDE) == CMP_STRIDE - 1)
    feat = np.zeros((sub, LANES), np.float32)
    feat[j, v] = NEG
    return jnp.asarray(feat, BF16)


def _nsa_cmp_kernel(q_ref, kc_ref, vct_ref, ovt_ref, gate_ref, qf_ref, oc_ref, sel_ref, qs_sc, *, tq, sub, n_slc, top_k, n_levels):
    qi = pl.program_id(2)
    n_sub = tq // sub
    n_feats = sub // CMP_STRIDE + 1
    qf = qf_ref[...]
    for u in range(n_sub):
        _stack_heads(q_ref[0, u * sub:(u + 1) * sub, :], qs_sc.at[u], sub)
        for h in range(HEADS_PER_GROUP):
            qs_sc[u, h * sub:(h + 1) * sub, LANES:2 * LANES] = qf
    n_rows = kc_ref.shape[3]
    level_rows = n_rows // n_levels
    n_chunks = 2
    cw = HEADS_PER_GROUP * sub // n_chunks
    n_vis = (qi * tq + tq - 1 - (CMP_LEN - 1)) // CMP_STRIDE + 1

    def attend(rows, u):
        base = (qi * tq + u * sub) // CMP_STRIDE - 2
        n_minus_v = _iota((rows, LANES), 0) - _iota((rows, LANES), 1)
        kf = jnp.where(_iota((rows, LANES), 1) < n_feats, jnp.where(n_minus_v > base, 1.0, 0.0), 0.0)
        ka = jnp.concatenate([kc_ref[0, 0, 0, :rows, :], kf.astype(BF16)], axis=1)
        vct = vct_ref[0, 0, 0, :, :rows]
        scores = [_dot_nt(ka, qs_sc[u, c * cw:(c + 1) * cw, :]) for c in range(n_chunks)]
        outs = []
        p_sum = None
        for s in scores:
            mx = jnp.max(s, axis=0, keepdims=True)
            e = jnp.exp2(s - mx)
            inv = jnp.where(mx > 0.5 * NEG, 1.0 / jnp.maximum(jnp.sum(e, axis=0, keepdims=True), 1.0), 0.0)
            p = e * inv
            outs.append(_dot(vct, p.astype(BF16)))
            for h in range(cw // sub):
                term = p[:, h * sub:(h + 1) * sub]
                p_sum = term if p_sum is None else p_sum + term
        _unstack_heads_t(jnp.concatenate(outs, axis=1), gate_ref[0, :, u * sub:(u + 1) * sub], oc_ref, sub, u * sub)
        p_hi = p_sum.astype(BF16)
        p_lo = (p_sum - p_hi.astype(F32)).astype(BF16)
        return _dot(ovt_ref[:, :rows], p_hi) + _dot(ovt_ref[:, :rows], p_lo)

    def select(imp, u):
        jb = _iota(imp.shape, 0)
        qp = qi * tq + u * sub + _iota(imp.shape, 1)
        cur = qp // SLC_LEN
        forced = (jb == 0) | (jb == cur) | (jb == cur - 1)
        score = jnp.where(jb * SLC_LEN <= qp, jnp.where(forced, FORCED_SCORE, imp), NEG)
        score = jnp.where(jb < n_slc, score, -jnp.inf)
        bias = jnp.full(imp.shape, NEG, F32)
        for _ in range(top_k):
            mx = jnp.max(score, axis=0, keepdims=True)
            idx = jnp.min(jnp.where(score == mx, jb, LANES), axis=0, keepdims=True)
            pick = jb == idx
            bias = jnp.where(pick, 0.0, bias)
            score = jnp.where(pick, -jnp.inf, score)
        sel_ref[0, 0, u * sub:(u + 1) * sub, :] = bias.T.astype(BF16)

    def tile(rows):
        imps = [attend(rows, u) for u in range(n_sub)]
        for u in range(n_sub):
            select(imps[u], u)

    for level in range(n_levels):
        lo_rows, hi_rows = level * level_rows, (level + 1) * level_rows
        cond = (n_vis <= hi_rows) if level == 0 else ((n_vis > lo_rows) & (n_vis <= hi_rows))
        pl.when(cond)(functools.partial(tile, hi_rows))


def _gate_spec(branch, tq):
    return pl.BlockSpec((1, HEADS_PER_GROUP, tq), lambda b, g, qi: (b, branch * N_KV_GROUPS + g, qi))


def _nsa_cmp(q, kc_all, vct_all, overlap_t, gates_t, n_slc, tq=CMP_TQ):
    B, T, hd = q.shape
    G = N_KV_GROUPS
    rows = kc_all.shape[3]
    gw = hd // G
    top_k = min(SLC_TOPK, n_slc)
    n_levels = 4 if rows % (4 * LANES) == 0 else 1
    return pl.pallas_call(
        functools.partial(_nsa_cmp_kernel, tq=tq, sub=CMP_SUB, n_slc=n_slc, top_k=top_k, n_levels=n_levels),
        grid=(B, G, T // tq),
        in_specs=[pl.BlockSpec((1, tq, gw), lambda b, g, qi: (b, qi, g)),
                  pl.BlockSpec((1, 1, 1, rows, LANES), lambda b, g, qi: (b, 0, g, 0, 0)),
                  pl.BlockSpec((1, 1, 1, HEAD_DIM, rows), lambda b, g, qi: (b, 1, g, 0, 0)),
                  pl.BlockSpec(overlap_t.shape, lambda b, g, qi: (0, 0)), _gate_spec(0, tq),
                  pl.BlockSpec((CMP_SUB, LANES), lambda b, g, qi: (0, 0))],
        out_specs=[pl.BlockSpec((1, tq, gw), lambda b, g, qi: (b, qi, g)),
                   pl.BlockSpec((1, 1, tq, LANES), lambda b, g, qi: (b, g, qi, 0))],
        out_shape=[jax.ShapeDtypeStruct((B, T, hd), BF16), jax.ShapeDtypeStruct((B, G, T, LANES), BF16)],
        scratch_shapes=[pltpu.VMEM((tq // CMP_SUB, HEADS_PER_GROUP * CMP_SUB, 2 * LANES), BF16)],
        compiler_params=_cparams(("parallel", "parallel", "parallel")),
        name="nsa_cmp",
    )(q, kc_all, vct_all, overlap_t, gates_t, _cmp_query_mask_features(CMP_SUB))


def _nsa_slc_kernel(q_ref, sel_ref, k_ref, oh_ref, vt_ref, gate_ref, o_ref, qa_sc, s_sc, mt_sc, m_sc, acc_sc, *, tq, tk, n_chunks):
    qi = pl.program_id(2)
    last = (qi * tq) // tk
    _stack_heads(q_ref[0], qa_sc, tq)
    sel = sel_ref[0, 0]
    for h in range(HEADS_PER_GROUP):
        qa_sc[h * tq:(h + 1) * tq, LANES:2 * LANES] = sel
    _softmax_init(m_sc, acc_sc)
    cw = HEADS_PER_GROUP * tq // n_chunks

    def produce(ki, slot):
        start = pl.multiple_of(ki * tk, tk)
        ka = jnp.concatenate([k_ref[0, 0, pl.ds(start, tk), :], oh_ref[pl.ds(start, tk), :]], axis=1)
        for c in range(n_chunks):
            s = _dot_nt(ka, qa_sc[c * cw:(c + 1) * cw, :])
            s_sc[slot, :, c * cw:(c + 1) * cw] = s
            mt_sc[slot, :, c * cw:(c + 1) * cw] = _column_max(s)

    def consume(ki, slot, diag):
        vt = vt_ref[0, 0, ki]
        for c in range(n_chunks):
            cols = slice(c * cw, (c + 1) * cw)
            s = s_sc[slot, :, cols]
            m_tile = mt_sc[slot, :, cols]
            if diag:
                kpos = ki * tk + _iota(s.shape, 0)
                qpos = qi * tq + (_iota(s.shape, 1) & (tq - 1))
                s = jnp.where(kpos <= qpos, s, NEG)
                m_tile = None
            _softmax_update_t(s, vt, m_sc.at[:, cols], acc_sc.at[:, cols], m_tile)

    _pipelined_key_loop(last, produce, consume)
    _unstack_heads_t(_softmax_result(acc_sc), gate_ref[0], o_ref, tq)


def _nsa_slc(q, sel, ks, vst, gates_t, tq=SLC_TQ, tk=SLC_TK):
    B, T, hd = q.shape
    G = N_KV_GROUPS
    gw = hd // G
    cols = HEADS_PER_GROUP * tq
    assert tk % tq == 0
    onehot = (jnp.arange(T)[:, None] // SLC_LEN == jnp.arange(LANES)[None, :]).astype(BF16)
    return pl.pallas_call(
        functools.partial(_nsa_slc_kernel, tq=tq, tk=tk, n_chunks=4),
        grid=(B, G, T // tq),
        in_specs=[pl.BlockSpec((1, tq, gw), lambda b, g, qi: (b, qi, g)),
                  pl.BlockSpec((1, 1, tq, LANES), lambda b, g, qi: (b, g, qi, 0)),
                  pl.BlockSpec((1, 1, T, LANES), lambda b, g, qi: (b, g, 0, 0)),
                  pl.BlockSpec((T, LANES), lambda b, g, qi: (0, 0)),
                  pl.BlockSpec((1, 1, T // tk, HEAD_DIM, tk), lambda b, g, qi: (b, g, 0, 0, 0)), _gate_spec(1, tq)],
        out_specs=pl.BlockSpec((1, tq, gw), lambda b, g, qi: (b, qi, g)),
        out_shape=jax.ShapeDtypeStruct((B, T, hd), BF16),
        scratch_shapes=[pltpu.VMEM((cols, 2 * LANES), BF16), pltpu.VMEM((2, tk, cols), F32),
                        pltpu.VMEM((2, 1, cols), F32), pltpu.VMEM((1, cols), F32),
                        pltpu.VMEM((ACC_ROWS, cols), F32)],
        compiler_params=_cparams(("parallel", "parallel", "arbitrary")),
        name="nsa_slc",
    )(q, sel, ks, onehot, vst, gates_t)


def _nsa_win_kernel(q_ref, k_ref, vt_ref, gate_ref, o_ref, qs_sc, s_sc, m_sc, acc_sc, *, tq, n_back, n_chunks):
    qi = pl.program_id(2)
    _stack_heads(q_ref[0], qs_sc, tq)
    _softmax_init(m_sc, acc_sc)
    cw = HEADS_PER_GROUP * tq // n_chunks

    def produce(ti, slot):
        start = pl.multiple_of(ti * tq, tq)
        k = k_ref[0, 0, pl.ds(start, tq), :]
        for c in range(n_chunks):
            s_sc[slot, :, c * cw:(c + 1) * cw] = _dot_nt(k, qs_sc[c * cw:(c + 1) * cw, :])

    def consume(ti, slot, kind):
        vt = vt_ref[0, 0, ti]
        for c in range(n_chunks):
            cols = slice(c * cw, (c + 1) * cw)
            s = s_sc[slot, :, cols]
            r = _iota(s.shape, 0)
            q_in_tile = _iota(s.shape, 1) & (tq - 1)
            if kind == "oldest":
                s = jnp.where(r > q_in_tile, s, NEG)
            elif kind == "diag":
                s = jnp.where(r <= q_in_tile, s, NEG)
            _softmax_update_t(s, vt, m_sc.at[:, cols], acc_sc.at[:, cols])

    def kind_of(back):
        return "oldest" if back == n_back else ("diag" if back == 0 else "full")

    for first in range(n_back + 1):
        cond = (qi >= n_back) if first == n_back else (qi == first)

        @pl.when(cond)
        def _(first=first):
            backs = list(range(first, -1, -1))
            produce(qi - backs[0], 0)
            for n, back in enumerate(backs):
                if n + 1 < len(backs):
                    produce(qi - backs[n + 1], (n + 1) % 2)
                consume(qi - back, n % 2, kind_of(back))

    _unstack_heads_t(_softmax_result(acc_sc), gate_ref[0], o_ref, tq)


def _nsa_win(q, kw, vwt, gates_t, tq=WIN_TQ):
    B, T, hd = q.shape
    G = N_KV_GROUPS
    gw = hd // G
    cols = HEADS_PER_GROUP * tq
    return pl.pallas_call(
        functools.partial(_nsa_win_kernel, tq=tq, n_back=WIN // tq, n_chunks=4),
        grid=(B, G, T // tq),
        in_specs=[pl.BlockSpec((1, tq, gw), lambda b, g, qi: (b, qi, g)),
                  pl.BlockSpec((1, 1, T, LANES), lambda b, g, qi: (b, g, 0, 0)),
                  pl.BlockSpec((1, 1, T // tq, HEAD_DIM, tq), lambda b, g, qi: (b, g, 0, 0, 0)), _gate_spec(2, tq)],
        out_specs=pl.BlockSpec((1, tq, gw), lambda b, g, qi: (b, qi, g)),
        out_shape=jax.ShapeDtypeStruct((B, T, hd), BF16),
        scratch_shapes=[pltpu.VMEM((cols, LANES), BF16), pltpu.VMEM((2, tq, cols), F32), pltpu.VMEM((1, cols), F32),
                        pltpu.VMEM((ACC_ROWS, cols), F32)],
        compiler_params=_cparams(("parallel", "parallel", "arbitrary")),
        name="nsa_win",
    )(q, kw, vwt, gates_t)


def _pad_cols(a, n):
    return jnp.pad(a, ((0, 0), (0, n - a.shape[1])))


def _lane_vec(v):
    return jnp.tile(v.astype(F32), 2).reshape(1, LANES)


def _bias_feature_selector():
    sel = np.zeros((N_BIAS_PARTS * LANES, N_HEADS * HEAD_DIM), np.float32)
    for part in range(N_BIAS_PARTS):
        for h in range(N_HEADS):
            sel[part * LANES + h, (h // 2) * LANES + (h % 2) * N_BIAS_PARTS + part] = 1.0
    return jnp.asarray(sel, BF16)


def kernel(x, positions, a_norm, a_w_in, a_b_f, a_q_gain, a_k_gain, a_w_out, kv_norm, kv_w, kc_pe, vc_pe, kc_w1, kc_w2, vc_w1, vc_w2, kc_gain, ks_gain, kw_gain, b_norm, b_w_in, b_b_gate, b_q_gain, b_w_out, f_norm, f_w_up, f_conv_w, f_conv_b, f_w_down):
    B, T, D = x.shape
    hd = N_HEADS * HEAD_DIM
    G = N_KV_GROUPS
    n_a = a_norm.shape[0]
    n_b = b_norm.shape[0]
    depth = n_a + n_b
    n_slc = T // SLC_LEN
    n_cmp = (T - CMP_LEN) // CMP_STRIDE + 1
    assert T % 1024 == 0 and n_slc <= LANES and hd == 1024 and D == 1024

    half = ROT_DIM // 2
    inv = ROPE_THETA ** (-jnp.arange(half, dtype=F32) * (2.0 / ROT_DIM))
    head_inv = jnp.concatenate([inv, inv, jnp.zeros((HEAD_DIM - ROT_DIM,), F32)])
    head_sign = jnp.concatenate([-jnp.ones((half,), F32), jnp.ones((half,), F32), jnp.zeros((HEAD_DIM - ROT_DIM,), F32)])
    inv_lane = jnp.tile(head_inv, 2).reshape(1, LANES)
    sign_lane = jnp.tile(head_sign, 2).reshape(1, LANES)
    cos_t, sin_t = _rope_tables(positions, inv_lane, sign_lane)
    end_pos = positions[:, CMP_LEN - 1::CMP_STRIDE]
    end_pos = jnp.pad(end_pos, ((0, 0), (0, T // CMP_STRIDE - n_cmp)))
    cos_c, sin_c = _rope_tables(end_pos, inv_lane, sign_lane)

    cs = jnp.arange(T // CMP_STRIDE) * CMP_STRIDE
    ss = jnp.arange(LANES) * SLC_LEN
    overlap_t = (jnp.maximum(jnp.minimum(cs[None, :] + CMP_LEN, ss[:, None] + SLC_LEN)
                             - jnp.maximum(cs[None, :], ss[:, None]), 0).astype(F32) / CMP_LEN).astype(BF16)

    h = x
    kv = None
    for layer in range(depth):
        if layer < n_a:
            i = layer
            w = jnp.concatenate([a_w_in[i][:, :2 * hd], _pad_cols(a_w_in[i][:, 3 * hd:], LANES)], axis=1).astype(BF16)
            wvt = a_w_in[i][:, 2 * hd:3 * hd].T.astype(BF16)
            bf = _pad_cols(a_b_f[i].reshape(1, -1), LANES)
            q, k, vt, cf = _fox_inproj(h, a_norm[i].reshape(1, D), w, wvt, bf, _lane_vec(a_q_gain[i]),
                                       _lane_vec(a_k_gain[i]), _bias_feature_selector())
            mix, w_out = [_fox_attn(q, k, cf, vt)], a_w_out[i]
        else:
            i = layer - n_a
            kc_all, vct_all, ks, kw, vst, vwt = kv
            q, gates_t = _nsa_inproj(h, b_norm[i].reshape(1, D), b_w_in[i][:, :hd].astype(BF16),
                                     b_w_in[i][:, hd:].T.astype(BF16), b_b_gate[i].reshape(-1, 1),
                                     _lane_vec(b_q_gain[i]), cos_t, sin_t)
            o_c, sel = _nsa_cmp(q, kc_all, vct_all, overlap_t, gates_t, n_slc)
            o_s = _nsa_slc(q, sel, ks, vst, gates_t)
            o_w = _nsa_win(q, kw, vwt, gates_t)
            mix, w_out = [o_c, o_s, o_w], b_w_out[i]
        h = _mix_out_conv_ffn(h, mix, w_out.astype(BF16), f_norm[layer].reshape(1, D), f_w_up[layer].astype(BF16),
                              f_conv_w[layer], f_conv_b[layer].reshape(1, -1), f_w_down[layer].astype(BF16))
        if layer == n_a - 1:
            w6 = kv_w.reshape(D, 6, G, HEAD_DIM)
            raw_cols = w6[:, 0:2].reshape(D, 2 * G * HEAD_DIM)
            wk = w6[:, (2, 4)]
            dup_cols = jnp.concatenate([wk, wk], axis=-1).reshape(D, 2 * G * LANES)
            wkv = jnp.concatenate([raw_cols, dup_cols], axis=1).astype(BF16)
            wvt = w6[:, (3, 5)].reshape(D, 2 * G * HEAD_DIM).T.astype(BF16)
            craw, ks, kw, vst, vwt = _kvproj(h, kv_norm.reshape(1, D), wkv, wvt, _lane_vec(ks_gain),
                                             _lane_vec(kw_gain), cos_t, sin_t)
            r = craw.reshape(B, T, 2, G, HEAD_DIM).transpose(0, 2, 3, 1, 4).reshape(B, 2, G, T // CMP_STRIDE, CMP_STRIDE * HEAD_DIM)
            w1 = jnp.stack([kc_w1, vc_w1]).astype(BF16)
            pe = jnp.stack([kc_pe.reshape(1, -1), vc_pe.reshape(1, -1)])
            pe = jnp.pad(pe, ((0, 0), (0, 7), (0, 0))).astype(BF16)
            w2 = jnp.stack([jnp.concatenate([kc_w2, kc_w2], axis=1), jnp.concatenate([vc_w2, vc_w2], axis=1)]).astype(BF16)
            w2t = jnp.stack([kc_w2.T, vc_w2.T]).astype(BF16)
            kc_all, vct_all = _compress(r, w1, pe, w2, w2t, _lane_vec(kc_gain), cos_c, sin_c, n_cmp)
            kv = (kc_all, vct_all, ks, kw, vst, vwt)
    return h
```

```python
import functools

import numpy as np
import jax
import jax.numpy as jnp
from jax import lax
from jax.experimental import pallas as pl
from jax.experimental.pallas import tpu as pltpu

F32 = jnp.float32
BF16 = jnp.bfloat16

LANES = 128
SUBLANES = 8
HEAD_DIM = 64
N_HEADS = 16
N_PAIRS = N_HEADS // 2
N_KV_GROUPS = 2
HEADS_PER_GROUP = N_HEADS // N_KV_GROUPS
ROT_DIM = HEAD_DIM // 4
ROPE_THETA = 500000.0
CMP_LEN = 32
CMP_STRIDE = 16
SLC_LEN = 64
SLC_TOPK = 16
WIN = 512
CONV_W = 3
RMS_EPS = 1e-6
NEG = -1e30
FORCED_SCORE = 1e6
LOG2E = 1.4426950408889634
Q_SCALE = HEAD_DIM ** -0.5 * LOG2E
N_BIAS_PARTS = 3
ONES_ROWS = 16
ACC_ROWS = HEAD_DIM + ONES_ROWS

ROW_TILE = 512
FOX_TQ = 512
FOX_TK = ROW_TILE
SLC_TQ = 256
SLC_TK = ROW_TILE
WIN_TQ = 256
CMP_TQ = 256
CMP_SUB = 128

VMEM_LIMIT = 48 * 1024 * 1024


def _cparams(sem, flags=None):
    return pltpu.CompilerParams(dimension_semantics=sem, vmem_limit_bytes=VMEM_LIMIT, flags=flags)


def _iota(shape, axis):
    return lax.broadcasted_iota(jnp.int32, shape, axis)


def _row_rms(x, g):
    ms = jnp.mean(x * x, axis=-1, keepdims=True)
    return x * lax.rsqrt(ms + RMS_EPS) * g


def _pair_rms(y, gain):
    lo = _iota(y.shape, 1) < HEAD_DIM
    y2 = y * y
    s_lo = jnp.sum(jnp.where(lo, y2, 0.0), axis=-1, keepdims=True)
    s_hi = jnp.sum(jnp.where(lo, 0.0, y2), axis=-1, keepdims=True)
    ms = jnp.where(lo, s_lo, s_hi) * (1.0 / HEAD_DIM)
    return y * lax.rsqrt(ms + RMS_EPS) * gain


def _pair_rope(y, cos, sin):
    lane = _iota(y.shape, 1) & (HEAD_DIM - 1)
    partner = jnp.where(lane < ROT_DIM // 2,
                        pltpu.roll(y, LANES - ROT_DIM // 2, 1),
                        pltpu.roll(y, ROT_DIM // 2, 1))
    return y * cos + partner * sin


def _dot(a, b):
    return jnp.dot(a, b, preferred_element_type=F32)


def _dot_nt(a, b):
    return lax.dot_general(a, b, (((1,), (1,)), ((), ())), preferred_element_type=F32)


def _column_max(s):
    tk, n = s.shape
    return jnp.max(jnp.max(s.reshape(tk // SUBLANES, SUBLANES, n), axis=0), axis=0, keepdims=True)


def _softmax_update_t(s, vt, m_ref, acc_ref, m_tile=None):
    tk, n = s.shape
    m_old = m_ref[...]
    if m_tile is None:
        m_tile = _column_max(s)
    m_new = jnp.maximum(m_old, m_tile)
    alpha = jnp.exp2(m_old - m_new)
    p = jnp.exp2(s - m_new).astype(BF16)
    vta = jnp.concatenate([vt, jnp.ones((ONES_ROWS, tk), vt.dtype)], axis=0)
    acc_ref[...] = alpha * acc_ref[...] + _dot(vta, p)
    m_ref[...] = m_new


def _pipelined_key_loop(n_full, produce, consume, unroll=4):
    assert unroll % 2 == 0

    def run(base, count):
        for i in range(count):
            produce(base + i + 1, (i + 1) % 2)
            consume(base + i, i % 2, False)

    produce(0, 0)

    def body(j, carry):
        run(unroll * j, unroll)
        return carry

    lax.fori_loop(0, n_full // unroll, body, 0)
    base = (n_full // unroll) * unroll
    rem = n_full - base
    step = unroll // 2
    while step >= 2:
        pl.when((rem & step) != 0)(functools.partial(run, base, step))
        base = base + (rem & step)
        step //= 2

    @pl.when((rem & 1) != 0)
    def _():
        run(base, 1)
        consume(base + 1, 1, True)

    @pl.when((rem & 1) == 0)
    def _():
        consume(base, 0, True)


def _softmax_init(m_ref, acc_ref):
    m_ref[...] = jnp.full_like(m_ref, NEG)
    acc_ref[...] = jnp.zeros_like(acc_ref)


def _softmax_result(acc_ref):
    acc = acc_ref[...]
    return acc[:HEAD_DIM, :] * (1.0 / acc[HEAD_DIM:HEAD_DIM + 1, :])


def _rope_table_kernel(pos_ref, inv_ref, c_ref, s_ref):
    ang = pos_ref[0].astype(F32) * inv_ref[...]
    c_ref[0] = jnp.cos(ang)
    s_ref[0] = jnp.sin(ang)


def _rope_tables(pos, inv):
    B, T = pos.shape
    half = inv.shape[0]
    per_row = LANES // half
    rows = T // per_row
    pos_rep = jnp.repeat(pos.reshape(B, rows, per_row), half, axis=-1)
    inv_row = jnp.tile(inv, per_row).reshape(1, LANES)
    spec = pl.BlockSpec((1, rows, LANES), lambda b: (b, 0, 0))
    cos, sin = pl.pallas_call(
        _rope_table_kernel,
        grid=(B,),
        in_specs=[spec, pl.BlockSpec((1, LANES), lambda b: (0, 0))],
        out_specs=[spec, spec],
        out_shape=[jax.ShapeDtypeStruct((B, rows, LANES), F32)] * 2,
        compiler_params=_cparams(("parallel",)),
        name="rope_tables",
    )(pos_rep, inv_row)
    cos = cos.reshape(B, T, half)
    sin = sin.reshape(B, T, half)
    rest = HEAD_DIM - 2 * half
    cos_head = jnp.concatenate([cos, cos, jnp.ones((B, T, rest), F32)], axis=-1)
    sin_head = jnp.concatenate([-sin, sin, jnp.zeros((B, T, rest), F32)], axis=-1)
    return jnp.tile(cos_head, (1, 1, LANES // HEAD_DIM)), jnp.tile(sin_head, (1, 1, LANES // HEAD_DIM))


def _fox_inproj_kernel(x_ref, g_ref, w_ref, wvt_ref, bf_ref, qg_ref, kg_ref, sel_ref,
                       q_ref, k_ref, vt_ref, cf_ref, carry_sc, *, tm, hd):
    ti = pl.program_id(1)
    xn = _row_rms(x_ref[0], g_ref[...]).astype(BF16)
    for j in range(hd // 256):
        for part, (ref, gain, mul) in enumerate(((q_ref, qg_ref, Q_SCALE), (k_ref, kg_ref, 1.0))):
            c0 = part * hd + 256 * j
            y = _dot(xn, w_ref[:, c0:c0 + 256])
            for hh in range(2):
                blk = _pair_rms(y[:, LANES * hh:LANES * (hh + 1)], gain[...]) * mul
                ref[0, :, 256 * j + LANES * hh:256 * j + LANES * (hh + 1)] = blk.astype(BF16)
        yt = _dot_nt(wvt_ref[256 * j:256 * (j + 1), :], xn)
        for hh in range(2):
            vt_ref[0, 2 * j + hh, 0] = yt[LANES * hh:LANES * (hh + 1), :].astype(BF16)
    z = _dot(xn, w_ref[:, 2 * hd:2 * hd + LANES]) + bf_ref[...]
    lf = jnp.minimum(z, 0.0) - jnp.log1p(jnp.exp(-jnp.abs(z)))
    row = _iota(lf.shape, 0)
    sh = 1
    while sh < tm:
        lf = lf + jnp.where(row >= sh, pltpu.roll(lf, sh, 0), 0.0)
        sh *= 2

    @pl.when(ti == 0)
    def _():
        carry_sc[...] = jnp.zeros_like(carry_sc)

    c = lf + carry_sc[0:1, :]
    carry_sc[...] = jnp.broadcast_to(c[tm - 1:tm, :], carry_sc.shape)
    rest = c * (-LOG2E)
    pieces = []
    for _ in range(N_BIAS_PARTS):
        piece = rest.astype(BF16)
        pieces.append(piece)
        rest = rest - piece.astype(F32)
    cf_ref[0] = _dot(jnp.concatenate(pieces, axis=1), sel_ref[...]).astype(BF16)


def _fox_inproj(x, g, w, wvt, bf, qg, kg, sel, tm=ROW_TILE):
    B, T, D = x.shape
    hd = N_HEADS * HEAD_DIM
    act = pl.BlockSpec((1, tm, hd), lambda b, t: (b, t, 0))
    vec = lambda n: pl.BlockSpec((1, n), lambda b, t: (0, 0))
    full = lambda a: pl.BlockSpec(a.shape, lambda b, t: (0,) * a.ndim)
    return pl.pallas_call(
        functools.partial(_fox_inproj_kernel, tm=tm, hd=hd),
        grid=(B, T // tm),
        in_specs=[pl.BlockSpec((1, tm, D), lambda b, t: (b, t, 0)), vec(D), full(w), full(wvt),
                  vec(LANES), vec(LANES), vec(LANES), full(sel)],
        out_specs=[act, act, pl.BlockSpec((1, N_PAIRS, 1, LANES, tm), lambda b, t: (b, 0, t, 0, 0)), act],
        out_shape=[jax.ShapeDtypeStruct((B, T, hd), BF16), jax.ShapeDtypeStruct((B, T, hd), BF16),
                   jax.ShapeDtypeStruct((B, N_PAIRS, T // tm, LANES, tm), BF16),
                   jax.ShapeDtypeStruct((B, T, hd), BF16)],
        scratch_shapes=[pltpu.VMEM((SUBLANES, LANES), F32)],
        compiler_params=_cparams(("arbitrary", "arbitrary")),
        name="fox_inproj",
    )(x, g, w, wvt, bf, qg, kg, sel)


def _fox_attn_kernel(q_ref, k_ref, cf_ref, vt_ref, o_ref, qa_sc, s_sc, mt_sc, m_sc, acc_sc, *, tq, tk):
    qi = pl.program_id(2)
    q = q_ref[0]
    lane = _iota(q.shape, 1)
    lo = lane < HEAD_DIM
    zero = jnp.zeros_like(q)
    for hh in range(2):
        qa_sc[hh, :, 0:LANES] = jnp.where(lo, q, zero) if hh == 0 else jnp.where(lo, zero, q)
        feat = jnp.where(lane < N_BIAS_PARTS * (hh + 1), 1.0, 0.0)
        qa_sc[hh, :, LANES:2 * LANES] = jnp.where(lane >= N_BIAS_PARTS * hh, feat, 0.0).astype(BF16)
        _softmax_init(m_sc.at[hh], acc_sc.at[hh])

    def produce(ki, slot):
        start = pl.multiple_of(ki * tk, tk)
        ka = jnp.concatenate([k_ref[0, pl.ds(start, tk), :], cf_ref[0, pl.ds(start, tk), :]], axis=1)
        for hh in range(2):
            s = _dot_nt(ka, qa_sc[hh])
            s_sc[slot, hh] = s
            mt_sc[slot, hh] = _column_max(s)

    def consume(ki, slot, diag):
        vt = vt_ref[0, 0, ki]
        for hh in range(2):
            s = s_sc[slot, hh]
            m_tile = mt_sc[slot, hh]
            if diag:
                s = jnp.where(_iota(s.shape, 0) <= _iota(s.shape, 1), s, NEG)
                m_tile = None
            _softmax_update_t(s, vt[HEAD_DIM * hh:HEAD_DIM * (hh + 1), :], m_sc.at[hh], acc_sc.at[hh], m_tile)

    _pipelined_key_loop(qi, produce, consume)
    o_t = jnp.concatenate([_softmax_result(acc_sc.at[hh]) for hh in range(2)], axis=0)
    o_ref[0] = o_t.T.astype(BF16)


def _fox_attn(q, k, cf, vt, tq=FOX_TQ, tk=FOX_TK):
    B, T, hd = q.shape
    assert tq == tk
    nk = T // tk
    seq = pl.BlockSpec((1, T, LANES), lambda b, p, qi: (b, 0, p))
    return pl.pallas_call(
        functools.partial(_fox_attn_kernel, tq=tq, tk=tk),
        grid=(B, N_PAIRS, T // tq),
        in_specs=[pl.BlockSpec((1, tq, LANES), lambda b, p, qi: (b, qi, p)), seq, seq,
                  pl.BlockSpec((1, 1, nk, LANES, tk), lambda b, p, qi: (b, p, 0, 0, 0))],
        out_specs=pl.BlockSpec((1, tq, LANES), lambda b, p, qi: (b, qi, p)),
        out_shape=jax.ShapeDtypeStruct((B, T, hd), BF16),
        scratch_shapes=[pltpu.VMEM((2, tq, 2 * LANES), BF16), pltpu.VMEM((2, 2, tk, tq), F32),
                        pltpu.VMEM((2, 2, 1, tq), F32), pltpu.VMEM((2, 1, tq), F32),
                        pltpu.VMEM((2, ACC_ROWS, tq), F32)],
        compiler_params=_cparams(("parallel", "parallel", "arbitrary")),
        name="fox_attn",
    )(q, k, cf, vt)


def _ffn_kernel(*refs, n_mix, tt, tf, d_ff):
    h_ref = refs[0]
    mix_refs = refs[1:1 + n_mix]
    wo_ref, g_ref, wup_ref, cw_ref, cb_ref, wd_ref, out_ref, a_sc, carry_sc = refs[1 + n_mix:]
    ti = pl.program_id(1)

    @pl.when(ti == 0)
    def _():
        carry_sc[...] = jnp.zeros_like(carry_sc)

    o = mix_refs[0][0]
    if n_mix > 1:
        o = o.astype(F32)
        for ref in mix_refs[1:]:
            o = o + ref[0].astype(F32)
        o = o.astype(BF16)
    x = h_ref[0] + _dot(o, wo_ref[...])
    xn = _row_rms(x, g_ref[...]).astype(BF16)
    row8 = _iota((SUBLANES, tf), 0)

    def conv(u, c0):
        prev8 = carry_sc[:, c0:c0 + tf]
        um1 = pltpu.roll(u, 1, 0)
        um2 = pltpu.roll(u, 2, 0)
        top1 = jnp.where(row8 == 0, prev8[7:8, :], um1[0:SUBLANES, :])
        top2 = jnp.where(row8 == 0, prev8[6:7, :], jnp.where(row8 == 1, prev8[7:8, :], um2[0:SUBLANES, :]))
        um1 = jnp.concatenate([top1, um1[SUBLANES:, :]], axis=0)
        um2 = jnp.concatenate([top2, um2[SUBLANES:, :]], axis=0)
        carry_sc[:, c0:c0 + tf] = u[tt - SUBLANES:, :]
        cw = cw_ref[:, c0:c0 + tf]
        return cb_ref[:, c0:c0 + tf] + cw[0:1, :] * um2 + cw[1:2, :] * um1 + cw[2:3, :] * u

    for f in range(d_ff // tf):
        g0 = f * tf
        cg = conv(_dot(xn, wup_ref[:, g0:g0 + tf]), g0)
        cv = conv(_dot(xn, wup_ref[:, d_ff + g0:d_ff + g0 + tf]), d_ff + g0)
        a_sc[:, g0:g0 + tf] = (cg * jax.nn.sigmoid(cg) * cv).astype(BF16)
    out_ref[0] = x + _dot(a_sc[...], wd_ref[...])


def _mix_out_conv_ffn(h, mix, w_out, g, w_up, conv_w, conv_b, w_down, tt=ROW_TILE, tf=256):
    B, T, D = h.shape
    d_ff = w_down.shape[0]
    act = lambda c: pl.BlockSpec((1, tt, c), lambda b, t: (b, t, 0))
    resident = lambda a: pl.BlockSpec(a.shape, lambda b, t: (0,) * a.ndim, pipeline_mode=pl.Buffered(1))
    return pl.pallas_call(
        functools.partial(_ffn_kernel, n_mix=len(mix), tt=tt, tf=tf, d_ff=d_ff),
        grid=(B, T // tt),
        in_specs=[act(D)] + [act(m.shape[-1]) for m in mix]
                 + [resident(w_out), resident(g), resident(w_up), resident(conv_w), resident(conv_b), resident(w_down)],
        out_specs=act(D),
        out_shape=jax.ShapeDtypeStruct((B, T, D), F32),
        scratch_shapes=[pltpu.VMEM((tt, d_ff), BF16), pltpu.VMEM((SUBLANES, 2 * d_ff), F32)],
        compiler_params=_cparams(("arbitrary", "arbitrary")),
        name="conv_ffn",
    )(h, *mix, w_out, g, w_up, conv_w, conv_b, w_down)


def _kvproj_kernel(h_ref, g_ref, w_ref, wvt_ref, ksg_ref, kwg_ref, c_ref, s_ref,
                   craw_ref, ks_ref, kw_ref, vst_ref, vwt_ref, *, tm):
    xn = _row_rms(h_ref[0], g_ref[...]).astype(BF16)
    cos = c_ref[0]
    sin = s_ref[0]
    craw_ref[0] = _dot(xn, w_ref[:, 0:256])
    for idx, (ref, gain) in enumerate(((ks_ref, ksg_ref), (kw_ref, kwg_ref))):
        y = _dot(xn, w_ref[:, 256 * (idx + 1):256 * (idx + 2)])
        for grp in range(N_KV_GROUPS):
            blk = _pair_rope(_pair_rms(y[:, LANES * grp:LANES * (grp + 1)], gain[...]), cos, sin)
            ref[0, grp] = blk.astype(BF16)
    yt = _dot_nt(wvt_ref[...], xn).astype(BF16)
    for grp in range(N_KV_GROUPS):
        vst_ref[0, grp, 0] = yt[HEAD_DIM * grp:HEAD_DIM * (grp + 1), :]
        r0 = HEAD_DIM * (N_KV_GROUPS + grp)
        for c in range(tm // WIN_TQ):
            vwt_ref[0, grp, c] = yt[r0:r0 + HEAD_DIM, WIN_TQ * c:WIN_TQ * (c + 1)]


def _kvproj(h, g, w, wvt, ksg, kwg, cos, sin, tm=ROW_TILE):
    B, T, D = h.shape
    G = N_KV_GROUPS
    vec = lambda n: pl.BlockSpec((1, n), lambda b, t: (0, 0))
    full = lambda a: pl.BlockSpec(a.shape, lambda b, t: (0,) * a.ndim)
    tab = pl.BlockSpec((1, tm, LANES), lambda b, t: (b, t, 0))
    dup = pl.BlockSpec((1, G, tm, LANES), lambda b, t: (b, 0, t, 0))
    dup_shape = jax.ShapeDtypeStruct((B, G, T, LANES), BF16)
    nw = tm // WIN_TQ
    return pl.pallas_call(
        functools.partial(_kvproj_kernel, tm=tm),
        grid=(B, T // tm),
        in_specs=[pl.BlockSpec((1, tm, D), lambda b, t: (b, t, 0)), vec(D), full(w), full(wvt),
                  vec(LANES), vec(LANES), tab, tab],
        out_specs=[pl.BlockSpec((1, tm, 256), lambda b, t: (b, t, 0)), dup, dup,
                   pl.BlockSpec((1, G, 1, HEAD_DIM, tm), lambda b, t: (b, 0, t, 0, 0)),
                   pl.BlockSpec((1, G, nw, HEAD_DIM, WIN_TQ), lambda b, t: (b, 0, t, 0, 0))],
        out_shape=[jax.ShapeDtypeStruct((B, T, 256), F32), dup_shape, dup_shape,
                   jax.ShapeDtypeStruct((B, G, T // tm, HEAD_DIM, tm), BF16),
                   jax.ShapeDtypeStruct((B, G, T // WIN_TQ, HEAD_DIM, WIN_TQ), BF16)],
        compiler_params=_cparams(("parallel", "parallel")),
        name="kvproj",
    )(h, g, w, wvt, ksg, kwg, cos, sin)


def _compress_kernel(r_ref, w1_ref, pe_ref, w2_ref, w2t_ref, gain_ref, c_ref, s_ref, kc_ref, vct_ref, *, n_cmp):
    r = r_ref[0, 0, 0].astype(BF16)
    half = r.shape[1]
    a = _dot(r, w1_ref[0, :half, :])
    b = _dot(r, w1_ref[0, half:, :])
    peb = _dot(pe_ref[0], w1_ref[0])[0:1, :]
    rows = r.shape[0]
    hid = a + pltpu.roll(b, rows - 1, 0) + peb
    act = jax.nn.gelu(hid).astype(BF16)
    y = _dot(act, w2_ref[0])
    yk = _pair_rope(_pair_rms(y, gain_ref[...]), c_ref[0], s_ref[0])
    kc_ref[0, 0, 0] = jnp.where(_iota(y.shape, 0) < n_cmp, yk, 0.0).astype(BF16)
    yt = _dot_nt(w2t_ref[0], act)
    vct_ref[0, 0, 0] = jnp.where(_iota(yt.shape, 1) < n_cmp, yt, 0.0).astype(BF16)


def _compress(r, w1, pe, w2, w2t, gain, cos, sin, n_cmp):
    B, _, G, rows, width = r.shape
    per_kv = lambda a: pl.BlockSpec((1,) + a.shape[1:], lambda b, kv, g: (kv,) + (0,) * (a.ndim - 1))
    return pl.pallas_call(
        functools.partial(_compress_kernel, n_cmp=n_cmp),
        grid=(B, 2, G),
        in_specs=[pl.BlockSpec((1, 1, 1, rows, width), lambda b, kv, g: (b, kv, g, 0, 0)),
                  per_kv(w1), per_kv(pe), per_kv(w2), per_kv(w2t),
                  pl.BlockSpec((1, LANES), lambda b, kv, g: (0, 0)),
                  pl.BlockSpec((1, rows, LANES), lambda b, kv, g: (b, 0, 0)),
                  pl.BlockSpec((1, rows, LANES), lambda b, kv, g: (b, 0, 0))],
        out_specs=[pl.BlockSpec((1, 1, 1, rows, LANES), lambda b, kv, g: (b, kv, g, 0, 0)),
                   pl.BlockSpec((1, 1, 1, HEAD_DIM, rows), lambda b, kv, g: (b, kv, g, 0, 0))],
        out_shape=[jax.ShapeDtypeStruct((B, 2, G, rows, LANES), BF16),
                   jax.ShapeDtypeStruct((B, 2, G, HEAD_DIM, rows), BF16)],
        compiler_params=_cparams(("parallel", "parallel", "parallel")),
        name="compress",
    )(r, w1, pe, w2, w2t, gain, cos, sin)


def _nsa_inproj_kernel(h_ref, g_ref, w_ref, wgt_ref, bgt_ref, qg_ref, c_ref, s_ref, q_ref, gate_ref, *, hd):
    xn = _row_rms(h_ref[0], g_ref[...]).astype(BF16)
    cos = c_ref[0]
    sin = s_ref[0]
    for j in range(hd // 256):
        y = _dot(xn, w_ref[:, 256 * j:256 * (j + 1)])
        for hh in range(2):
            blk = _pair_rope(_pair_rms(y[:, LANES * hh:LANES * (hh + 1)], qg_ref[...]), cos, sin) * Q_SCALE
            q_ref[0, :, 256 * j + LANES * hh:256 * j + LANES * (hh + 1)] = blk.astype(BF16)
    gate_ref[0] = jax.nn.sigmoid(_dot_nt(wgt_ref[...], xn) + bgt_ref[...])


def _nsa_inproj(h, g, w, wgt, bgt, qg, cos, sin, tm=ROW_TILE):
    B, T, D = h.shape
    hd = N_HEADS * HEAD_DIM
    vec = lambda n: pl.BlockSpec((1, n), lambda b, t: (0, 0))
    full = lambda a: pl.BlockSpec(a.shape, lambda b, t: (0,) * a.ndim)
    tab = pl.BlockSpec((1, tm, LANES), lambda b, t: (b, t, 0))
    return pl.pallas_call(
        functools.partial(_nsa_inproj_kernel, hd=hd),
        grid=(B, T // tm),
        in_specs=[pl.BlockSpec((1, tm, D), lambda b, t: (b, t, 0)), vec(D), full(w), full(wgt), full(bgt),
                  vec(LANES), tab, tab],
        out_specs=[pl.BlockSpec((1, tm, hd), lambda b, t: (b, t, 0)),
                   pl.BlockSpec((1, wgt.shape[0], tm), lambda b, t: (b, 0, t))],
        out_shape=[jax.ShapeDtypeStruct((B, T, hd), BF16), jax.ShapeDtypeStruct((B, wgt.shape[0], T), F32)],
        compiler_params=_cparams(("parallel", "parallel")),
        name="nsa_inproj",
    )(h, g, w, wgt, bgt, qg, cos, sin)


def _stack_heads(q, dst_ref, tq):
    lo = _iota((tq, LANES), 1) < HEAD_DIM
    zero = jnp.zeros((tq, LANES), q.dtype)
    for j in range(HEADS_PER_GROUP // 2):
        x = q[:, LANES * j:LANES * (j + 1)]
        dst_ref[(2 * j) * tq:(2 * j + 1) * tq, 0:LANES] = jnp.where(lo, x, zero)
        dst_ref[(2 * j + 1) * tq:(2 * j + 2) * tq, 0:LANES] = jnp.where(lo, zero, x)


def _unstack_heads_t(o_t, gate, o_ref, tq, row0=0):
    for j in range(HEADS_PER_GROUP // 2):
        halves = [o_t[:, h * tq:(h + 1) * tq] * gate[h:h + 1, :] for h in (2 * j, 2 * j + 1)]
        o_ref[0, row0:row0 + tq, LANES * j:LANES * (j + 1)] = jnp.concatenate(halves, axis=0).T.astype(o_ref.dtype)


def _cmp_query_mask_features(sub):
    j = np.arange(sub)
    v = (j // CMP_STRIDE) + ((j % CMP_STRIDE) == CMP_STRIDE - 1)
    feat = np.zeros((sub, LANES), np.float32)
    feat[j, v] = NEG
    return jnp.asarray(feat, BF16)


def _nsa_cmp_kernel(q_ref, kc_ref, vct_ref, ovt_ref, gate_ref, qf_ref, oc_ref, sel_ref, qs_sc, *, tq, sub, n_slc, top_k, n_levels):
    qi = pl.program_id(2)
    n_sub = tq // sub
    n_feats = sub // CMP_STRIDE + 1
    qf = qf_ref[...]
    for u in range(n_sub):
        _stack_heads(q_ref[0, u * sub:(u + 1) * sub, :], qs_sc.at[u], sub)
        for h in range(HEADS_PER_GROUP):
            qs_sc[u, h * sub:(h + 1) * sub, LANES:2 * LANES] = qf
    n_rows = kc_ref.shape[3]
    level_rows = n_rows // n_levels
    n_chunks = 2
    cw = HEADS_PER_GROUP * sub // n_chunks
    n_vis = (qi * tq + tq - 1 - (CMP_LEN - 1)) // CMP_STRIDE + 1

    def attend(rows, u):
        base = (qi * tq + u * sub) // CMP_STRIDE - 2
        n_minus_v = _iota((rows, LANES), 0) - _iota((rows, LANES), 1)
        kf = jnp.where(_iota((rows, LANES), 1) < n_feats, jnp.where(n_minus_v > base, 1.0, 0.0), 0.0)
        ka = jnp.concatenate([kc_ref[0, 0, 0, :rows, :], kf.astype(BF16)], axis=1)
        vct = vct_ref[0, 0, 0, :, :rows]
        scores = [_dot_nt(ka, qs_sc[u, c * cw:(c + 1) * cw, :]) for c in range(n_chunks)]
        outs = []
        p_sum = None
        for s in scores:
            mx = jnp.max(s, axis=0, keepdims=True)
            e = jnp.exp2(s - mx)
            inv = jnp.where(mx > 0.5 * NEG, 1.0 / jnp.maximum(jnp.sum(e, axis=0, keepdims=True), 1.0), 0.0)
            p = e * inv
            outs.append(_dot(vct, p.astype(BF16)))
            for h in range(cw // sub):
                term = p[:, h * sub:(h + 1) * sub]
                p_sum = term if p_sum is None else p_sum + term
        _unstack_heads_t(jnp.concatenate(outs, axis=1), gate_ref[0, :, u * sub:(u + 1) * sub], oc_ref, sub, u * sub)
        p_hi = p_sum.astype(BF16)
        p_lo = (p_sum - p_hi.astype(F32)).astype(BF16)
        return _dot(ovt_ref[:, :rows], p_hi) + _dot(ovt_ref[:, :rows], p_lo)

    def select(imp, u, n_blk):
        imp = imp[:n_blk, :]
        jb = _iota(imp.shape, 0)
        qp = qi * tq + u * sub + _iota(imp.shape, 1)
        cur = qp // SLC_LEN
        forced = (jb == 0) | (jb == cur) | (jb == cur - 1)
        score = jnp.where(jb * SLC_LEN <= qp, jnp.where(forced, FORCED_SCORE, imp), NEG)
        live = jb < n_slc
        score = jnp.where(live, score, -jnp.inf)
        for _ in range(top_k):
            mx = jnp.max(score, axis=0, keepdims=True)
            idx = jnp.min(jnp.where(score == mx, jb, LANES), axis=0, keepdims=True)
            score = jnp.where(jb == idx, -jnp.inf, score)
        bias = jnp.where(live, jnp.where(score == -jnp.inf, 0.0, NEG), NEG)
        if n_blk < LANES:
            bias = jnp.concatenate([bias, jnp.full((LANES - n_blk, sub), NEG, F32)], axis=0)
        sel_ref[0, 0, u * sub:(u + 1) * sub, :] = bias.T.astype(BF16)

    def tile(rows):
        n_blk = rows * CMP_STRIDE // SLC_LEN
        n_blk = n_blk if (n_blk >= top_k and n_blk % SUBLANES == 0) else LANES
        imps = [attend(rows, u) for u in range(n_sub)]
        for u in range(n_sub):
            select(imps[u], u, n_blk)

    for level in range(n_levels):
        lo_rows, hi_rows = level * level_rows, (level + 1) * level_rows
        cond = (n_vis <= hi_rows) if level == 0 else ((n_vis > lo_rows) & (n_vis <= hi_rows))
        pl.when(cond)(functools.partial(tile, hi_rows))


def _gate_spec(branch, tq):
    return pl.BlockSpec((1, HEADS_PER_GROUP, tq), lambda b, g, qi: (b, branch * N_KV_GROUPS + g, qi))


def _nsa_cmp(q, kc_all, vct_all, overlap_t, gates_t, n_slc, tq=CMP_TQ):
    B, T, hd = q.shape
    G = N_KV_GROUPS
    rows = kc_all.shape[3]
    gw = hd // G
    top_k = min(SLC_TOPK, n_slc)
    n_levels = 4 if rows % (4 * LANES) == 0 else 1
    return pl.pallas_call(
        functools.partial(_nsa_cmp_kernel, tq=tq, sub=CMP_SUB, n_slc=n_slc, top_k=top_k, n_levels=n_levels),
        grid=(B, G, T // tq),
        in_specs=[pl.BlockSpec((1, tq, gw), lambda b, g, qi: (b, qi, g)),
                  pl.BlockSpec((1, 1, 1, rows, LANES), lambda b, g, qi: (b, 0, g, 0, 0)),
                  pl.BlockSpec((1, 1, 1, HEAD_DIM, rows), lambda b, g, qi: (b, 1, g, 0, 0)),
                  pl.BlockSpec(overlap_t.shape, lambda b, g, qi: (0, 0)), _gate_spec(0, tq),
                  pl.BlockSpec((CMP_SUB, LANES), lambda b, g, qi: (0, 0))],
        out_specs=[pl.BlockSpec((1, tq, gw), lambda b, g, qi: (b, qi, g)),
                   pl.BlockSpec((1, 1, tq, LANES), lambda b, g, qi: (b, g, qi, 0))],
        out_shape=[jax.ShapeDtypeStruct((B, T, hd), BF16), jax.ShapeDtypeStruct((B, G, T, LANES), BF16)],
        scratch_shapes=[pltpu.VMEM((tq // CMP_SUB, HEADS_PER_GROUP * CMP_SUB, 2 * LANES), BF16)],
        compiler_params=_cparams(("parallel", "parallel", "parallel")),
        name="nsa_cmp",
    )(q, kc_all, vct_all, overlap_t, gates_t, _cmp_query_mask_features(CMP_SUB))


def _nsa_slc_kernel(q_ref, sel_ref, k_ref, oh_ref, vt_ref, gate_ref, o_ref, qa_sc, s_sc, mt_sc, m_sc, acc_sc, *, tq, tk, n_chunks):
    qi = pl.program_id(2)
    last = (qi * tq) // tk
    _stack_heads(q_ref[0], qa_sc, tq)
    sel = sel_ref[0, 0]
    for h in range(HEADS_PER_GROUP):
        qa_sc[h * tq:(h + 1) * tq, LANES:2 * LANES] = sel
    _softmax_init(m_sc, acc_sc)
    cw = HEADS_PER_GROUP * tq // n_chunks

    def produce(ki, slot):
        start = pl.multiple_of(ki * tk, tk)
        ka = jnp.concatenate([k_ref[0, 0, pl.ds(start, tk), :], oh_ref[pl.ds(start, tk), :]], axis=1)
        for c in range(n_chunks):
            s = _dot_nt(ka, qa_sc[c * cw:(c + 1) * cw, :])
            s_sc[slot, :, c * cw:(c + 1) * cw] = s
            mt_sc[slot, :, c * cw:(c + 1) * cw] = _column_max(s)

    def consume(ki, slot, diag):
        vt = vt_ref[0, 0, ki]
        for c in range(n_chunks):
            cols = slice(c * cw, (c + 1) * cw)
            s = s_sc[slot, :, cols]
            m_tile = mt_sc[slot, :, cols]
            if diag:
                kpos = ki * tk + _iota(s.shape, 0)
                qpos = qi * tq + (_iota(s.shape, 1) & (tq - 1))
                s = jnp.where(kpos <= qpos, s, NEG)
                m_tile = None
            _softmax_update_t(s, vt, m_sc.at[:, cols], acc_sc.at[:, cols], m_tile)

    _pipelined_key_loop(last, produce, consume)
    _unstack_heads_t(_softmax_result(acc_sc), gate_ref[0], o_ref, tq)


def _nsa_slc(q, sel, ks, vst, gates_t, tq=SLC_TQ, tk=SLC_TK):
    B, T, hd = q.shape
    G = N_KV_GROUPS
    gw = hd // G
    cols = HEADS_PER_GROUP * tq
    assert tk % tq == 0
    onehot = (jnp.arange(T)[:, None] // SLC_LEN == jnp.arange(LANES)[None, :]).astype(BF16)
    return pl.pallas_call(
        functools.partial(_nsa_slc_kernel, tq=tq, tk=tk, n_chunks=4),
        grid=(B, G, T // tq),
        in_specs=[pl.BlockSpec((1, tq, gw), lambda b, g, qi: (b, qi, g)),
                  pl.BlockSpec((1, 1, tq, LANES), lambda b, g, qi: (b, g, qi, 0)),
                  pl.BlockSpec((1, 1, T, LANES), lambda b, g, qi: (b, g, 0, 0)),
                  pl.BlockSpec((T, LANES), lambda b, g, qi: (0, 0)),
                  pl.BlockSpec((1, 1, T // tk, HEAD_DIM, tk), lambda b, g, qi: (b, g, 0, 0, 0)), _gate_spec(1, tq)],
        out_specs=pl.BlockSpec((1, tq, gw), lambda b, g, qi: (b, qi, g)),
        out_shape=jax.ShapeDtypeStruct((B, T, hd), BF16),
        scratch_shapes=[pltpu.VMEM((cols, 2 * LANES), BF16), pltpu.VMEM((2, tk, cols), F32),
                        pltpu.VMEM((2, 1, cols), F32), pltpu.VMEM((1, cols), F32),
                        pltpu.VMEM((ACC_ROWS, cols), F32)],
        compiler_params=_cparams(("parallel", "parallel", "arbitrary")),
        name="nsa_slc",
    )(q, sel, ks, onehot, vst, gates_t)


def _nsa_win_kernel(q_ref, k_ref, vt_ref, gate_ref, o_ref, qs_sc, s_sc, m_sc, acc_sc, *, tq, n_back, n_chunks):
    qi = pl.program_id(2)
    _stack_heads(q_ref[0], qs_sc, tq)
    _softmax_init(m_sc, acc_sc)
    cw = HEADS_PER_GROUP * tq // n_chunks

    def produce(ti, slot):
        start = pl.multiple_of(ti * tq, tq)
        k = k_ref[0, 0, pl.ds(start, tq), :]
        for c in range(n_chunks):
            s_sc[slot, :, c * cw:(c + 1) * cw] = _dot_nt(k, qs_sc[c * cw:(c + 1) * cw, :])

    def consume(ti, slot, kind):
        vt = vt_ref[0, 0, ti]
        for c in range(n_chunks):
            cols = slice(c * cw, (c + 1) * cw)
            s = s_sc[slot, :, cols]
            r = _iota(s.shape, 0)
            q_in_tile = _iota(s.shape, 1) & (tq - 1)
            if kind == "oldest":
                s = jnp.where(r > q_in_tile, s, NEG)
            elif kind == "diag":
                s = jnp.where(r <= q_in_tile, s, NEG)
            _softmax_update_t(s, vt, m_sc.at[:, cols], acc_sc.at[:, cols])

    def kind_of(back):
        return "oldest" if back == n_back else ("diag" if back == 0 else "full")

    for first in range(n_back + 1):
        cond = (qi >= n_back) if first == n_back else (qi == first)

        @pl.when(cond)
        def _(first=first):
            backs = list(range(first, -1, -1))
            produce(qi - backs[0], 0)
            for n, back in enumerate(backs):
                if n + 1 < len(backs):
                    produce(qi - backs[n + 1], (n + 1) % 2)
                consume(qi - back, n % 2, kind_of(back))

    _unstack_heads_t(_softmax_result(acc_sc), gate_ref[0], o_ref, tq)


def _nsa_win(q, kw, vwt, gates_t, tq=WIN_TQ):
    B, T, hd = q.shape
    G = N_KV_GROUPS
    gw = hd // G
    cols = HEADS_PER_GROUP * tq
    return pl.pallas_call(
        functools.partial(_nsa_win_kernel, tq=tq, n_back=WIN // tq, n_chunks=4),
        grid=(B, G, T // tq),
        in_specs=[pl.BlockSpec((1, tq, gw), lambda b, g, qi: (b, qi, g)),
                  pl.BlockSpec((1, 1, T, LANES), lambda b, g, qi: (b, g, 0, 0)),
                  pl.BlockSpec((1, 1, T // tq, HEAD_DIM, tq), lambda b, g, qi: (b, g, 0, 0, 0)), _gate_spec(2, tq)],
        out_specs=pl.BlockSpec((1, tq, gw), lambda b, g, qi: (b, qi, g)),
        out_shape=jax.ShapeDtypeStruct((B, T, hd), BF16),
        scratch_shapes=[pltpu.VMEM((cols, LANES), BF16), pltpu.VMEM((2, tq, cols), F32), pltpu.VMEM((1, cols), F32),
                        pltpu.VMEM((ACC_ROWS, cols), F32)],
        compiler_params=_cparams(("parallel", "parallel", "arbitrary")),
        name="nsa_win",
    )(q, kw, vwt, gates_t)


def _pad_cols(a, n):
    return jnp.pad(a, ((0, 0), (0, n - a.shape[1])))


def _lane_vec(v):
    return jnp.tile(v.astype(F32), 2).reshape(1, LANES)


def _bias_feature_selector():
    sel = np.zeros((N_BIAS_PARTS * LANES, N_HEADS * HEAD_DIM), np.float32)
    for part in range(N_BIAS_PARTS):
        for h in range(N_HEADS):
            sel[part * LANES + h, (h // 2) * LANES + (h % 2) * N_BIAS_PARTS + part] = 1.0
    return jnp.asarray(sel, BF16)


def kernel(x, positions, a_norm, a_w_in, a_b_f, a_q_gain, a_k_gain, a_w_out, kv_norm, kv_w, kc_pe, vc_pe, kc_w1, kc_w2, vc_w1, vc_w2, kc_gain, ks_gain, kw_gain, b_norm, b_w_in, b_b_gate, b_q_gain, b_w_out, f_norm, f_w_up, f_conv_w, f_conv_b, f_w_down):
    B, T, D = x.shape
    hd = N_HEADS * HEAD_DIM
    G = N_KV_GROUPS
    n_a = a_norm.shape[0]
    n_b = b_norm.shape[0]
    depth = n_a + n_b
    n_slc = T // SLC_LEN
    n_cmp = (T - CMP_LEN) // CMP_STRIDE + 1
    assert T % 1024 == 0 and n_slc <= LANES and hd == 1024 and D == 1024

    half = ROT_DIM // 2
    inv = ROPE_THETA ** (-jnp.arange(half, dtype=F32) * (2.0 / ROT_DIM))
    cos_t, sin_t = _rope_tables(positions, inv)
    end_pos = positions[:, CMP_LEN - 1::CMP_STRIDE]
    end_pos = jnp.pad(end_pos, ((0, 0), (0, T // CMP_STRIDE - n_cmp)))
    cos_c, sin_c = _rope_tables(end_pos, inv)

    cs = jnp.arange(T // CMP_STRIDE) * CMP_STRIDE
    ss = jnp.arange(LANES) * SLC_LEN
    overlap_t = (jnp.maximum(jnp.minimum(cs[None, :] + CMP_LEN, ss[:, None] + SLC_LEN)
                             - jnp.maximum(cs[None, :], ss[:, None]), 0).astype(F32) / CMP_LEN).astype(BF16)

    h = x
    kv = None
    for layer in range(depth):
        if layer < n_a:
            i = layer
            w = jnp.concatenate([a_w_in[i][:, :2 * hd], _pad_cols(a_w_in[i][:, 3 * hd:], LANES)], axis=1).astype(BF16)
            wvt = a_w_in[i][:, 2 * hd:3 * hd].T.astype(BF16)
            bf = _pad_cols(a_b_f[i].reshape(1, -1), LANES)
            q, k, vt, cf = _fox_inproj(h, a_norm[i].reshape(1, D), w, wvt, bf, _lane_vec(a_q_gain[i]),
                                       _lane_vec(a_k_gain[i]), _bias_feature_selector())
            mix, w_out = [_fox_attn(q, k, cf, vt)], a_w_out[i]
        else:
            i = layer - n_a
            kc_all, vct_all, ks, kw, vst, vwt = kv
            q, gates_t = _nsa_inproj(h, b_norm[i].reshape(1, D), b_w_in[i][:, :hd].astype(BF16),
                                     b_w_in[i][:, hd:].T.astype(BF16), b_b_gate[i].reshape(-1, 1),
                                     _lane_vec(b_q_gain[i]), cos_t, sin_t)
            o_c, sel = _nsa_cmp(q, kc_all, vct_all, overlap_t, gates_t, n_slc)
            o_s = _nsa_slc(q, sel, ks, vst, gates_t)
            o_w = _nsa_win(q, kw, vwt, gates_t)
            mix, w_out = [o_c, o_s, o_w], b_w_out[i]
        h = _mix_out_conv_ffn(h, mix, w_out.astype(BF16), f_norm[layer].reshape(1, D), f_w_up[layer].astype(BF16),
                              f_conv_w[layer], f_conv_b[layer].reshape(1, -1), f_w_down[layer].astype(BF16))
        if layer == n_a - 1:
            w6 = kv_w.reshape(D, 6, G, HEAD_DIM)
            raw_cols = w6[:, 0:2].reshape(D, 2 * G * HEAD_DIM)
            wk = w6[:, (2, 4)]
            dup_cols = jnp.concatenate([wk, wk], axis=-1).reshape(D, 2 * G * LANES)
            wkv = jnp.concatenate([raw_cols, dup_cols], axis=1).astype(BF16)
            wvt = w6[:, (3, 5)].reshape(D, 2 * G * HEAD_DIM).T.astype(BF16)
            craw, ks, kw, vst, vwt = _kvproj(h, kv_norm.reshape(1, D), wkv, wvt, _lane_vec(ks_gain),
                                             _lane_vec(kw_gain), cos_t, sin_t)
            r = craw.reshape(B, T, 2, G, HEAD_DIM).transpose(0, 2, 3, 1, 4).reshape(B, 2, G, T // CMP_STRIDE, CMP_STRIDE * HEAD_DIM)
            w1 = jnp.stack([kc_w1, vc_w1]).astype(BF16)
            pe = jnp.stack([kc_pe.reshape(1, -1), vc_pe.reshape(1, -1)])
            pe = jnp.pad(pe, ((0, 0), (0, 7), (0, 0))).astype(BF16)
            w2 = jnp.stack([jnp.concatenate([kc_w2, kc_w2], axis=1), jnp.concatenate([vc_w2, vc_w2], axis=1)]).astype(BF16)
            w2t = jnp.stack([kc_w2.T, vc_w2.T]).astype(BF16)
            kc_all, vct_all = _compress(r, w1, pe, w2, w2t, _lane_vec(kc_gain), cos_c, sin_c, n_cmp)
            kv = (kc_all, vct_all, ks, kw, vst, vwt)
    return h
```

```python
import functools

import numpy as np
import jax
import jax.numpy as jnp
from jax import lax
from jax.experimental import pallas as pl
from jax.experimental.pallas import tpu as pltpu

F32 = jnp.float32
BF16 = jnp.bfloat16

LANES = 128
SUBLANES = 8
HEAD_DIM = 64
N_HEADS = 16
N_PAIRS = N_HEADS // 2
N_KV_GROUPS = 2
HEADS_PER_GROUP = N_HEADS // N_KV_GROUPS
ROT_DIM = HEAD_DIM // 4
ROPE_THETA = 500000.0
CMP_LEN = 32
CMP_STRIDE = 16
SLC_LEN = 64
SLC_TOPK = 16
WIN = 512
CONV_W = 3
RMS_EPS = 1e-6
NEG = -1e30
FORCED_SCORE = 1e6
LOG2E = 1.4426950408889634
Q_SCALE = HEAD_DIM ** -0.5 * LOG2E
N_BIAS_PARTS = 3
ONES_ROWS = 16
ACC_ROWS = HEAD_DIM + ONES_ROWS

ROW_TILE = 512
FOX_TQ = 512
FOX_TK = ROW_TILE
SLC_TQ = 256
SLC_TK = ROW_TILE
WIN_TQ = 256
CMP_TQ = 256
CMP_SUB = 128

VMEM_LIMIT = 48 * 1024 * 1024


def _cparams(sem, flags=None):
    return pltpu.CompilerParams(dimension_semantics=sem, vmem_limit_bytes=VMEM_LIMIT, flags=flags)


def _iota(shape, axis):
    return lax.broadcasted_iota(jnp.int32, shape, axis)


def _row_rms(x, g):
    ms = jnp.mean(x * x, axis=-1, keepdims=True)
    return x * lax.rsqrt(ms + RMS_EPS) * g


def _pair_rms(y, gain):
    lo = _iota(y.shape, 1) < HEAD_DIM
    y2 = y * y
    s_lo = jnp.sum(jnp.where(lo, y2, 0.0), axis=-1, keepdims=True)
    s_hi = jnp.sum(jnp.where(lo, 0.0, y2), axis=-1, keepdims=True)
    ms = jnp.where(lo, s_lo, s_hi) * (1.0 / HEAD_DIM)
    return y * lax.rsqrt(ms + RMS_EPS) * gain


def _pair_rope(y, cos, sin):
    lane = _iota(y.shape, 1) & (HEAD_DIM - 1)
    partner = jnp.where(lane < ROT_DIM // 2,
                        pltpu.roll(y, LANES - ROT_DIM // 2, 1),
                        pltpu.roll(y, ROT_DIM // 2, 1))
    return y * cos + partner * sin


def _dot(a, b):
    return jnp.dot(a, b, preferred_element_type=F32)


def _dot_nt(a, b):
    return lax.dot_general(a, b, (((1,), (1,)), ((), ())), preferred_element_type=F32)


def _column_max(s):
    tk, n = s.shape
    return jnp.max(jnp.max(s.reshape(tk // SUBLANES, SUBLANES, n), axis=0), axis=0, keepdims=True)


def _softmax_update_t(s, vt, m_ref, acc_ref, m_tile=None):
    tk, n = s.shape
    m_old = m_ref[...]
    if m_tile is None:
        m_tile = _column_max(s)
    m_new = jnp.maximum(m_old, m_tile)
    alpha = jnp.exp2(m_old - m_new)
    p = jnp.exp2(s - m_new).astype(BF16)
    vta = jnp.concatenate([vt, jnp.ones((ONES_ROWS, tk), vt.dtype)], axis=0)
    acc_ref[...] = alpha * acc_ref[...] + _dot(vta, p)
    m_ref[...] = m_new


def _pipelined_key_loop(n_full, produce, consume, unroll=4):
    assert unroll % 2 == 0

    def run(base, count):
        for i in range(count):
            produce(base + i + 1, (i + 1) % 2)
            consume(base + i, i % 2, False)

    produce(0, 0)

    def body(j, carry):
        run(unroll * j, unroll)
        return carry

    lax.fori_loop(0, n_full // unroll, body, 0)
    base = (n_full // unroll) * unroll
    rem = n_full - base
    step = unroll // 2
    while step >= 2:
        pl.when((rem & step) != 0)(functools.partial(run, base, step))
        base = base + (rem & step)
        step //= 2

    @pl.when((rem & 1) != 0)
    def _():
        run(base, 1)
        consume(base + 1, 1, True)

    @pl.when((rem & 1) == 0)
    def _():
        consume(base, 0, True)


def _softmax_init(m_ref, acc_ref):
    m_ref[...] = jnp.full_like(m_ref, NEG)
    acc_ref[...] = jnp.zeros_like(acc_ref)


def _softmax_result(acc_ref):
    acc = acc_ref[...]
    return acc[:HEAD_DIM, :] * (1.0 / acc[HEAD_DIM:HEAD_DIM + 1, :])


def _rope_table_kernel(pos_ref, inv_ref, c_ref, s_ref):
    ang = pos_ref[0].astype(F32) * inv_ref[...]
    c_ref[0] = jnp.cos(ang)
    s_ref[0] = jnp.sin(ang)


def _rope_tables(pos, inv):
    B, T = pos.shape
    half = inv.shape[0]
    per_row = LANES // half
    rows = T // per_row
    pos_rep = jnp.repeat(pos.reshape(B, rows, per_row), half, axis=-1)
    inv_row = jnp.tile(inv, per_row).reshape(1, LANES)
    spec = pl.BlockSpec((1, rows, LANES), lambda b: (b, 0, 0))
    cos, sin = pl.pallas_call(
        _rope_table_kernel,
        grid=(B,),
        in_specs=[spec, pl.BlockSpec((1, LANES), lambda b: (0, 0))],
        out_specs=[spec, spec],
        out_shape=[jax.ShapeDtypeStruct((B, rows, LANES), F32)] * 2,
        compiler_params=_cparams(("parallel",)),
        name="rope_tables",
    )(pos_rep, inv_row)
    cos = cos.reshape(B, T, half)
    sin = sin.reshape(B, T, half)
    rest = HEAD_DIM - 2 * half
    cos_head = jnp.concatenate([cos, cos, jnp.ones((B, T, rest), F32)], axis=-1)
    sin_head = jnp.concatenate([-sin, sin, jnp.zeros((B, T, rest), F32)], axis=-1)
    return jnp.tile(cos_head, (1, 1, LANES // HEAD_DIM)), jnp.tile(sin_head, (1, 1, LANES // HEAD_DIM))


def _fox_inproj_kernel(x_ref, g_ref, w_ref, wvt_ref, bf_ref, qg_ref, kg_ref, sel_ref,
                       q_ref, k_ref, vt_ref, cf_ref, carry_sc, *, tm, hd):
    ti = pl.program_id(1)
    xn = _row_rms(x_ref[0], g_ref[...]).astype(BF16)
    for j in range(hd // 256):
        for part, (ref, gain, mul) in enumerate(((q_ref, qg_ref, Q_SCALE), (k_ref, kg_ref, 1.0))):
            c0 = part * hd + 256 * j
            y = _dot(xn, w_ref[:, c0:c0 + 256])
            for hh in range(2):
                blk = _pair_rms(y[:, LANES * hh:LANES * (hh + 1)], gain[...]) * mul
                ref[0, :, 256 * j + LANES * hh:256 * j + LANES * (hh + 1)] = blk.astype(BF16)
        yt = _dot_nt(wvt_ref[256 * j:256 * (j + 1), :], xn)
        for hh in range(2):
            vt_ref[0, 2 * j + hh, 0] = yt[LANES * hh:LANES * (hh + 1), :].astype(BF16)
    z = _dot(xn, w_ref[:, 2 * hd:2 * hd + LANES]) + bf_ref[...]
    lf = jnp.minimum(z, 0.0) - jnp.log1p(jnp.exp(-jnp.abs(z)))
    row = _iota(lf.shape, 0)
    sh = 1
    while sh < tm:
        lf = lf + jnp.where(row >= sh, pltpu.roll(lf, sh, 0), 0.0)
        sh *= 2

    @pl.when(ti == 0)
    def _():
        carry_sc[...] = jnp.zeros_like(carry_sc)

    c = lf + carry_sc[0:1, :]
    carry_sc[...] = jnp.broadcast_to(c[tm - 1:tm, :], carry_sc.shape)
    rest = c * (-LOG2E)
    pieces = []
    for _ in range(N_BIAS_PARTS):
        piece = rest.astype(BF16)
        pieces.append(piece)
        rest = rest - piece.astype(F32)
    cf_ref[0] = _dot(jnp.concatenate(pieces, axis=1), sel_ref[...]).astype(BF16)


def _fox_inproj(x, g, w, wvt, bf, qg, kg, sel, tm=ROW_TILE):
    B, T, D = x.shape
    hd = N_HEADS * HEAD_DIM
    act = pl.BlockSpec((1, tm, hd), lambda b, t: (b, t, 0))
    vec = lambda n: pl.BlockSpec((1, n), lambda b, t: (0, 0))
    full = lambda a: pl.BlockSpec(a.shape, lambda b, t: (0,) * a.ndim)
    return pl.pallas_call(
        functools.partial(_fox_inproj_kernel, tm=tm, hd=hd),
        grid=(B, T // tm),
        in_specs=[pl.BlockSpec((1, tm, D), lambda b, t: (b, t, 0)), vec(D), full(w), full(wvt),
                  vec(LANES), vec(LANES), vec(LANES), full(sel)],
        out_specs=[act, act, pl.BlockSpec((1, N_PAIRS, 1, LANES, tm), lambda b, t: (b, 0, t, 0, 0)), act],
        out_shape=[jax.ShapeDtypeStruct((B, T, hd), BF16), jax.ShapeDtypeStruct((B, T, hd), BF16),
                   jax.ShapeDtypeStruct((B, N_PAIRS, T // tm, LANES, tm), BF16),
                   jax.ShapeDtypeStruct((B, T, hd), BF16)],
        scratch_shapes=[pltpu.VMEM((SUBLANES, LANES), F32)],
        compiler_params=_cparams(("arbitrary", "arbitrary")),
        name="fox_inproj",
    )(x, g, w, wvt, bf, qg, kg, sel)


def _fox_attn_kernel(q_ref, k_ref, cf_ref, vt_ref, o_ref, qa_sc, s_sc, mt_sc, m_sc, acc_sc, *, tq, tk, pairs):
    qi = pl.program_id(2)
    lane = _iota((tq, LANES), 1)
    lo = lane < HEAD_DIM
    for hh in range(2 * pairs):
        pp, half = divmod(hh, 2)
        q = q_ref[0, :, LANES * pp:LANES * (pp + 1)]
        zero = jnp.zeros_like(q)
        qa_sc[hh, :, 0:LANES] = jnp.where(lo, q, zero) if half == 0 else jnp.where(lo, zero, q)
        feat = jnp.where(lane < N_BIAS_PARTS * (half + 1), 1.0, 0.0)
        qa_sc[hh, :, LANES:2 * LANES] = jnp.where(lane >= N_BIAS_PARTS * half, feat, 0.0).astype(BF16)
        _softmax_init(m_sc.at[hh], acc_sc.at[hh])

    def produce(ki, slot):
        start = pl.multiple_of(ki * tk, tk)
        for pp in range(pairs):
            cols = slice(LANES * pp, LANES * (pp + 1))
            ka = jnp.concatenate([k_ref[0, pl.ds(start, tk), cols], cf_ref[0, pl.ds(start, tk), cols]], axis=1)
            for hh in (2 * pp, 2 * pp + 1):
                s = _dot_nt(ka, qa_sc[hh])
                s_sc[slot, hh] = s
                mt_sc[slot, hh] = _column_max(s)

    def consume(ki, slot, diag):
        for hh in range(2 * pairs):
            pp, half = divmod(hh, 2)
            s = s_sc[slot, hh]
            m_tile = mt_sc[slot, hh]
            if diag:
                s = jnp.where(_iota(s.shape, 0) <= _iota(s.shape, 1), s, NEG)
                m_tile = None
            vt = vt_ref[0, pp, ki, HEAD_DIM * half:HEAD_DIM * (half + 1), :]
            _softmax_update_t(s, vt, m_sc.at[hh], acc_sc.at[hh], m_tile)

    _pipelined_key_loop(qi, produce, consume)
    for pp in range(pairs):
        o_t = jnp.concatenate([_softmax_result(acc_sc.at[2 * pp + half]) for half in range(2)], axis=0)
        o_ref[0, :, LANES * pp:LANES * (pp + 1)] = o_t.T.astype(BF16)


def _fox_attn(q, k, cf, vt, tq=FOX_TQ, tk=FOX_TK, pairs=2):
    B, T, hd = q.shape
    assert tq == tk
    nk = T // tk
    heads = 2 * pairs
    width = pairs * LANES
    seq = pl.BlockSpec((1, T, width), lambda b, p, qi: (b, 0, p))
    return pl.pallas_call(
        functools.partial(_fox_attn_kernel, tq=tq, tk=tk, pairs=pairs),
        grid=(B, N_PAIRS // pairs, T // tq),
        in_specs=[pl.BlockSpec((1, tq, width), lambda b, p, qi: (b, qi, p)), seq, seq,
                  pl.BlockSpec((1, pairs, nk, LANES, tk), lambda b, p, qi: (b, p, 0, 0, 0))],
        out_specs=pl.BlockSpec((1, tq, width), lambda b, p, qi: (b, qi, p)),
        out_shape=jax.ShapeDtypeStruct((B, T, hd), BF16),
        scratch_shapes=[pltpu.VMEM((heads, tq, 2 * LANES), BF16), pltpu.VMEM((2, heads, tk, tq), F32),
                        pltpu.VMEM((2, heads, 1, tq), F32), pltpu.VMEM((heads, 1, tq), F32),
                        pltpu.VMEM((heads, ACC_ROWS, tq), F32)],
        compiler_params=_cparams(("parallel", "parallel", "arbitrary")),
        name="fox_attn",
    )(q, k, cf, vt)


def _ffn_kernel(*refs, n_mix, tt, tf, d_ff):
    h_ref = refs[0]
    mix_refs = refs[1:1 + n_mix]
    wo_ref, g_ref, wup_ref, cw_ref, cb_ref, wd_ref, out_ref, a_sc, carry_sc = refs[1 + n_mix:]
    ti = pl.program_id(1)

    @pl.when(ti == 0)
    def _():
        carry_sc[...] = jnp.zeros_like(carry_sc)

    o = mix_refs[0][0]
    if n_mix > 1:
        o = o.astype(F32)
        for ref in mix_refs[1:]:
            o = o + ref[0].astype(F32)
        o = o.astype(BF16)
    x = h_ref[0] + _dot(o, wo_ref[...])
    xn = _row_rms(x, g_ref[...]).astype(BF16)
    row8 = _iota((SUBLANES, tf), 0)

    def conv(u, c0):
        prev8 = carry_sc[:, c0:c0 + tf]
        um1 = pltpu.roll(u, 1, 0)
        um2 = pltpu.roll(u, 2, 0)
        top1 = jnp.where(row8 == 0, prev8[7:8, :], um1[0:SUBLANES, :])
        top2 = jnp.where(row8 == 0, prev8[6:7, :], jnp.where(row8 == 1, prev8[7:8, :], um2[0:SUBLANES, :]))
        um1 = jnp.concatenate([top1, um1[SUBLANES:, :]], axis=0)
        um2 = jnp.concatenate([top2, um2[SUBLANES:, :]], axis=0)
        carry_sc[:, c0:c0 + tf] = u[tt - SUBLANES:, :]
        cw = cw_ref[:, c0:c0 + tf]
        return cb_ref[:, c0:c0 + tf] + cw[0:1, :] * um2 + cw[1:2, :] * um1 + cw[2:3, :] * u

    for f in range(d_ff // tf):
        g0 = f * tf
        cg = conv(_dot(xn, wup_ref[:, g0:g0 + tf]), g0)
        cv = conv(_dot(xn, wup_ref[:, d_ff + g0:d_ff + g0 + tf]), d_ff + g0)
        a_sc[:, g0:g0 + tf] = (cg * jax.nn.sigmoid(cg) * cv).astype(BF16)
    out_ref[0] = x + _dot(a_sc[...], wd_ref[...])


def _mix_out_conv_ffn(h, mix, w_out, g, w_up, conv_w, conv_b, w_down, tt=ROW_TILE, tf=256):
    B, T, D = h.shape
    d_ff = w_down.shape[0]
    act = lambda c: pl.BlockSpec((1, tt, c), lambda b, t: (b, t, 0))
    resident = lambda a: pl.BlockSpec(a.shape, lambda b, t: (0,) * a.ndim, pipeline_mode=pl.Buffered(1))
    return pl.pallas_call(
        functools.partial(_ffn_kernel, n_mix=len(mix), tt=tt, tf=tf, d_ff=d_ff),
        grid=(B, T // tt),
        in_specs=[act(D)] + [act(m.shape[-1]) for m in mix]
                 + [resident(w_out), resident(g), resident(w_up), resident(conv_w), resident(conv_b), resident(w_down)],
        out_specs=act(D),
        out_shape=jax.ShapeDtypeStruct((B, T, D), F32),
        scratch_shapes=[pltpu.VMEM((tt, d_ff), BF16), pltpu.VMEM((SUBLANES, 2 * d_ff), F32)],
        compiler_params=_cparams(("arbitrary", "arbitrary")),
        name="conv_ffn",
    )(h, *mix, w_out, g, w_up, conv_w, conv_b, w_down)


def _kvproj_kernel(h_ref, g_ref, w_ref, wvt_ref, ksg_ref, kwg_ref, c_ref, s_ref,
                   craw_ref, ks_ref, kw_ref, vst_ref, vwt_ref, *, tm):
    xn = _row_rms(h_ref[0], g_ref[...]).astype(BF16)
    cos = c_ref[0]
    sin = s_ref[0]
    craw_ref[0] = _dot(xn, w_ref[:, 0:256])
    for idx, (ref, gain) in enumerate(((ks_ref, ksg_ref), (kw_ref, kwg_ref))):
        y = _dot(xn, w_ref[:, 256 * (idx + 1):256 * (idx + 2)])
        for grp in range(N_KV_GROUPS):
            blk = _pair_rope(_pair_rms(y[:, LANES * grp:LANES * (grp + 1)], gain[...]), cos, sin)
            ref[0, grp] = blk.astype(BF16)
    yt = _dot_nt(wvt_ref[...], xn).astype(BF16)
    for grp in range(N_KV_GROUPS):
        vst_ref[0, grp, 0] = yt[HEAD_DIM * grp:HEAD_DIM * (grp + 1), :]
        r0 = HEAD_DIM * (N_KV_GROUPS + grp)
        for c in range(tm // WIN_TQ):
            vwt_ref[0, grp, c] = yt[r0:r0 + HEAD_DIM, WIN_TQ * c:WIN_TQ * (c + 1)]


def _kvproj(h, g, w, wvt, ksg, kwg, cos, sin, tm=ROW_TILE):
    B, T, D = h.shape
    G = N_KV_GROUPS
    vec = lambda n: pl.BlockSpec((1, n), lambda b, t: (0, 0))
    full = lambda a: pl.BlockSpec(a.shape, lambda b, t: (0,) * a.ndim)
    tab = pl.BlockSpec((1, tm, LANES), lambda b, t: (b, t, 0))
    dup = pl.BlockSpec((1, G, tm, LANES), lambda b, t: (b, 0, t, 0))
    dup_shape = jax.ShapeDtypeStruct((B, G, T, LANES), BF16)
    nw = tm // WIN_TQ
    return pl.pallas_call(
        functools.partial(_kvproj_kernel, tm=tm),
        grid=(B, T // tm),
        in_specs=[pl.BlockSpec((1, tm, D), lambda b, t: (b, t, 0)), vec(D), full(w), full(wvt),
                  vec(LANES), vec(LANES), tab, tab],
        out_specs=[pl.BlockSpec((1, tm, 256), lambda b, t: (b, t, 0)), dup, dup,
                   pl.BlockSpec((1, G, 1, HEAD_DIM, tm), lambda b, t: (b, 0, t, 0, 0)),
                   pl.BlockSpec((1, G, nw, HEAD_DIM, WIN_TQ), lambda b, t: (b, 0, t, 0, 0))],
        out_shape=[jax.ShapeDtypeStruct((B, T, 256), F32), dup_shape, dup_shape,
                   jax.ShapeDtypeStruct((B, G, T // tm, HEAD_DIM, tm), BF16),
                   jax.ShapeDtypeStruct((B, G, T // WIN_TQ, HEAD_DIM, WIN_TQ), BF16)],
        compiler_params=_cparams(("parallel", "parallel")),
        name="kvproj",
    )(h, g, w, wvt, ksg, kwg, cos, sin)


def _compress_kernel(r_ref, w1_ref, pe_ref, w2_ref, w2t_ref, gain_ref, c_ref, s_ref, kc_ref, vct_ref, *, n_cmp):
    r = r_ref[0, 0, 0].astype(BF16)
    half = r.shape[1]
    a = _dot(r, w1_ref[0, :half, :])
    b = _dot(r, w1_ref[0, half:, :])
    peb = _dot(pe_ref[0], w1_ref[0])[0:1, :]
    rows = r.shape[0]
    hid = a + pltpu.roll(b, rows - 1, 0) + peb
    act = jax.nn.gelu(hid).astype(BF16)
    y = _dot(act, w2_ref[0])
    yk = _pair_rope(_pair_rms(y, gain_ref[...]), c_ref[0], s_ref[0])
    kc_ref[0, 0, 0] = jnp.where(_iota(y.shape, 0) < n_cmp, yk, 0.0).astype(BF16)
    yt = _dot_nt(w2t_ref[0], act)
    vct_ref[0, 0, 0] = jnp.where(_iota(yt.shape, 1) < n_cmp, yt, 0.0).astype(BF16)


def _compress(r, w1, pe, w2, w2t, gain, cos, sin, n_cmp):
    B, _, G, rows, width = r.shape
    per_kv = lambda a: pl.BlockSpec((1,) + a.shape[1:], lambda b, kv, g: (kv,) + (0,) * (a.ndim - 1))
    return pl.pallas_call(
        functools.partial(_compress_kernel, n_cmp=n_cmp),
        grid=(B, 2, G),
        in_specs=[pl.BlockSpec((1, 1, 1, rows, width), lambda b, kv, g: (b, kv, g, 0, 0)),
                  per_kv(w1), per_kv(pe), per_kv(w2), per_kv(w2t),
                  pl.BlockSpec((1, LANES), lambda b, kv, g: (0, 0)),
                  pl.BlockSpec((1, rows, LANES), lambda b, kv, g: (b, 0, 0)),
                  pl.BlockSpec((1, rows, LANES), lambda b, kv, g: (b, 0, 0))],
        out_specs=[pl.BlockSpec((1, 1, 1, rows, LANES), lambda b, kv, g: (b, kv, g, 0, 0)),
                   pl.BlockSpec((1, 1, 1, HEAD_DIM, rows), lambda b, kv, g: (b, kv, g, 0, 0))],
        out_shape=[jax.ShapeDtypeStruct((B, 2, G, rows, LANES), BF16),
                   jax.ShapeDtypeStruct((B, 2, G, HEAD_DIM, rows), BF16)],
        compiler_params=_cparams(("parallel", "parallel", "parallel")),
        name="compress",
    )(r, w1, pe, w2, w2t, gain, cos, sin)


def _nsa_inproj_kernel(h_ref, g_ref, w_ref, wgt_ref, bgt_ref, qg_ref, c_ref, s_ref, q_ref, gate_ref, *, hd):
    xn = _row_rms(h_ref[0], g_ref[...]).astype(BF16)
    cos = c_ref[0]
    sin = s_ref[0]
    for j in range(hd // 256):
        y = _dot(xn, w_ref[:, 256 * j:256 * (j + 1)])
        for hh in range(2):
            blk = _pair_rope(_pair_rms(y[:, LANES * hh:LANES * (hh + 1)], qg_ref[...]), cos, sin) * Q_SCALE
            q_ref[0, :, 256 * j + LANES * hh:256 * j + LANES * (hh + 1)] = blk.astype(BF16)
    gate_ref[0] = jax.nn.sigmoid(_dot_nt(wgt_ref[...], xn) + bgt_ref[...])


def _nsa_inproj(h, g, w, wgt, bgt, qg, cos, sin, tm=ROW_TILE):
    B, T, D = h.shape
    hd = N_HEADS * HEAD_DIM
    vec = lambda n: pl.BlockSpec((1, n), lambda b, t: (0, 0))
    full = lambda a: pl.BlockSpec(a.shape, lambda b, t: (0,) * a.ndim)
    tab = pl.BlockSpec((1, tm, LANES), lambda b, t: (b, t, 0))
    return pl.pallas_call(
        functools.partial(_nsa_inproj_kernel, hd=hd),
        grid=(B, T // tm),
        in_specs=[pl.BlockSpec((1, tm, D), lambda b, t: (b, t, 0)), vec(D), full(w), full(wgt), full(bgt),
                  vec(LANES), tab, tab],
        out_specs=[pl.BlockSpec((1, tm, hd), lambda b, t: (b, t, 0)),
                   pl.BlockSpec((1, wgt.shape[0], tm), lambda b, t: (b, 0, t))],
        out_shape=[jax.ShapeDtypeStruct((B, T, hd), BF16), jax.ShapeDtypeStruct((B, wgt.shape[0], T), F32)],
        compiler_params=_cparams(("parallel", "parallel")),
        name="nsa_inproj",
    )(h, g, w, wgt, bgt, qg, cos, sin)


def _stack_heads(q, dst_ref, tq):
    lo = _iota((tq, LANES), 1) < HEAD_DIM
    zero = jnp.zeros((tq, LANES), q.dtype)
    for j in range(HEADS_PER_GROUP // 2):
        x = q[:, LANES * j:LANES * (j + 1)]
        dst_ref[(2 * j) * tq:(2 * j + 1) * tq, 0:LANES] = jnp.where(lo, x, zero)
        dst_ref[(2 * j + 1) * tq:(2 * j + 2) * tq, 0:LANES] = jnp.where(lo, zero, x)


def _unstack_heads_t(o_t, gate, o_ref, tq, row0=0):
    for j in range(HEADS_PER_GROUP // 2):
        halves = [o_t[:, h * tq:(h + 1) * tq] * gate[h:h + 1, :] for h in (2 * j, 2 * j + 1)]
        o_ref[0, row0:row0 + tq, LANES * j:LANES * (j + 1)] = jnp.concatenate(halves, axis=0).T.astype(o_ref.dtype)


def _cmp_query_mask_features(sub):
    j = np.arange(sub)
    v = (j // CMP_STRIDE) + ((j % CMP_STRIDE) == CMP_STRIDE - 1)
    feat = np.zeros((sub, LANES), np.float32)
    feat[j, v] = NEG
    return jnp.asarray(feat, BF16)


def _nsa_cmp_kernel(q_ref, kc_ref, vct_ref, ovt_ref, gate_ref, qf_ref, oc_ref, sel_ref, qs_sc, *, tq, sub, n_slc, top_k, n_levels):
    qi = pl.program_id(2)
    n_sub = tq // sub
    n_feats = sub // CMP_STRIDE + 1
    qf = qf_ref[...]
    for u in range(n_sub):
        _stack_heads(q_ref[0, u * sub:(u + 1) * sub, :], qs_sc.at[u], sub)
        for h in range(HEADS_PER_GROUP):
            qs_sc[u, h * sub:(h + 1) * sub, LANES:2 * LANES] = qf
    n_rows = kc_ref.shape[3]
    level_rows = n_rows // n_levels
    n_chunks = 2
    cw = HEADS_PER_GROUP * sub // n_chunks
    n_vis = (qi * tq + tq - 1 - (CMP_LEN - 1)) // CMP_STRIDE + 1

    def attend(rows, u):
        base = (qi * tq + u * sub) // CMP_STRIDE - 2
        n_minus_v = _iota((rows, LANES), 0) - _iota((rows, LANES), 1)
        kf = jnp.where(_iota((rows, LANES), 1) < n_feats, jnp.where(n_minus_v > base, 1.0, 0.0), 0.0)
        ka = jnp.concatenate([kc_ref[0, 0, 0, :rows, :], kf.astype(BF16)], axis=1)
        vct = vct_ref[0, 0, 0, :, :rows]
        scores = [_dot_nt(ka, qs_sc[u, c * cw:(c + 1) * cw, :]) for c in range(n_chunks)]
        outs = []
        p_sum = None
        for s in scores:
            mx = jnp.max(s, axis=0, keepdims=True)
            e = jnp.exp2(s - mx)
            inv = jnp.where(mx > 0.5 * NEG, 1.0 / jnp.maximum(jnp.sum(e, axis=0, keepdims=True), 1.0), 0.0)
            p = e * inv
            outs.append(_dot(vct, p.astype(BF16)))
            for h in range(cw // sub):
                term = p[:, h * sub:(h + 1) * sub]
                p_sum = term if p_sum is None else p_sum + term
        _unstack_heads_t(jnp.concatenate(outs, axis=1), gate_ref[0, :, u * sub:(u + 1) * sub], oc_ref, sub, u * sub)
        p_hi = p_sum.astype(BF16)
        p_lo = (p_sum - p_hi.astype(F32)).astype(BF16)
        return _dot(ovt_ref[:, :rows], p_hi) + _dot(ovt_ref[:, :rows], p_lo)

    def select(imp, u, n_blk):
        imp = imp[:n_blk, :]
        jb = _iota(imp.shape, 0)
        qp = qi * tq + u * sub + _iota(imp.shape, 1)
        cur = qp // SLC_LEN
        forced = (jb == 0) | (jb == cur) | (jb == cur - 1)
        score = jnp.where(jb * SLC_LEN <= qp, jnp.where(forced, FORCED_SCORE, imp), NEG)
        live = jb < n_slc
        score = jnp.where(live, score, -jnp.inf)
        for _ in range(top_k):
            mx = jnp.max(score, axis=0, keepdims=True)
            idx = jnp.min(jnp.where(score == mx, jb, LANES), axis=0, keepdims=True)
            score = jnp.where(jb == idx, -jnp.inf, score)
        bias = jnp.where(live, jnp.where(score == -jnp.inf, 0.0, NEG), NEG)
        if n_blk < LANES:
            bias = jnp.concatenate([bias, jnp.full((LANES - n_blk, sub), NEG, F32)], axis=0)
        sel_ref[0, 0, u * sub:(u + 1) * sub, :] = bias.T.astype(BF16)

    def tile(rows):
        n_blk = rows * CMP_STRIDE // SLC_LEN
        n_blk = n_blk if (n_blk >= top_k and n_blk % SUBLANES == 0) else LANES
        imps = [attend(rows, u) for u in range(n_sub)]
        for u in range(n_sub):
            select(imps[u], u, n_blk)

    for level in range(n_levels):
        lo_rows, hi_rows = level * level_rows, (level + 1) * level_rows
        cond = (n_vis <= hi_rows) if level == 0 else ((n_vis > lo_rows) & (n_vis <= hi_rows))
        pl.when(cond)(functools.partial(tile, hi_rows))


def _gate_spec(branch, tq):
    return pl.BlockSpec((1, HEADS_PER_GROUP, tq), lambda b, g, qi: (b, branch * N_KV_GROUPS + g, qi))


def _nsa_cmp(q, kc_all, vct_all, overlap_t, gates_t, n_slc, tq=CMP_TQ):
    B, T, hd = q.shape
    G = N_KV_GROUPS
    rows = kc_all.shape[3]
    gw = hd // G
    top_k = min(SLC_TOPK, n_slc)
    n_levels = 4 if rows % (4 * LANES) == 0 else 1
    return pl.pallas_call(
        functools.partial(_nsa_cmp_kernel, tq=tq, sub=CMP_SUB, n_slc=n_slc, top_k=top_k, n_levels=n_levels),
        grid=(B, G, T // tq),
        in_specs=[pl.BlockSpec((1, tq, gw), lambda b, g, qi: (b, qi, g)),
                  pl.BlockSpec((1, 1, 1, rows, LANES), lambda b, g, qi: (b, 0, g, 0, 0)),
                  pl.BlockSpec((1, 1, 1, HEAD_DIM, rows), lambda b, g, qi: (b, 1, g, 0, 0)),
                  pl.BlockSpec(overlap_t.shape, lambda b, g, qi: (0, 0)), _gate_spec(0, tq),
                  pl.BlockSpec((CMP_SUB, LANES), lambda b, g, qi: (0, 0))],
        out_specs=[pl.BlockSpec((1, tq, gw), lambda b, g, qi: (b, qi, g)),
                   pl.BlockSpec((1, 1, tq, LANES), lambda b, g, qi: (b, g, qi, 0))],
        out_shape=[jax.ShapeDtypeStruct((B, T, hd), BF16), jax.ShapeDtypeStruct((B, G, T, LANES), BF16)],
        scratch_shapes=[pltpu.VMEM((tq // CMP_SUB, HEADS_PER_GROUP * CMP_SUB, 2 * LANES), BF16)],
        compiler_params=_cparams(("parallel", "parallel", "parallel")),
        name="nsa_cmp",
    )(q, kc_all, vct_all, overlap_t, gates_t, _cmp_query_mask_features(CMP_SUB))


def _nsa_slc_kernel(q_ref, sel_ref, k_ref, oh_ref, vt_ref, gate_ref, o_ref, qa_sc, s_sc, mt_sc, m_sc, acc_sc, *, tq, tk, n_chunks):
    qi = pl.program_id(2)
    last = (qi * tq) // tk
    _stack_heads(q_ref[0], qa_sc, tq)
    sel = sel_ref[0, 0]
    for h in range(HEADS_PER_GROUP):
        qa_sc[h * tq:(h + 1) * tq, LANES:2 * LANES] = sel
    _softmax_init(m_sc, acc_sc)
    cw = HEADS_PER_GROUP * tq // n_chunks

    def produce(ki, slot):
        start = pl.multiple_of(ki * tk, tk)
        ka = jnp.concatenate([k_ref[0, 0, pl.ds(start, tk), :], oh_ref[pl.ds(start, tk), :]], axis=1)
        for c in range(n_chunks):
            s = _dot_nt(ka, qa_sc[c * cw:(c + 1) * cw, :])
            s_sc[slot, :, c * cw:(c + 1) * cw] = s
            mt_sc[slot, :, c * cw:(c + 1) * cw] = _column_max(s)

    def consume(ki, slot, diag):
        vt = vt_ref[0, 0, ki]
        for c in range(n_chunks):
            cols = slice(c * cw, (c + 1) * cw)
            s = s_sc[slot, :, cols]
            m_tile = mt_sc[slot, :, cols]
            if diag:
                kpos = ki * tk + _iota(s.shape, 0)
                qpos = qi * tq + (_iota(s.shape, 1) & (tq - 1))
                s = jnp.where(kpos <= qpos, s, NEG)
                m_tile = None
            _softmax_update_t(s, vt, m_sc.at[:, cols], acc_sc.at[:, cols], m_tile)

    _pipelined_key_loop(last, produce, consume)
    _unstack_heads_t(_softmax_result(acc_sc), gate_ref[0], o_ref, tq)


def _nsa_slc(q, sel, ks, vst, gates_t, tq=SLC_TQ, tk=SLC_TK):
    B, T, hd = q.shape
    G = N_KV_GROUPS
    gw = hd // G
    cols = HEADS_PER_GROUP * tq
    assert tk % tq == 0
    onehot = (jnp.arange(T)[:, None] // SLC_LEN == jnp.arange(LANES)[None, :]).astype(BF16)
    return pl.pallas_call(
        functools.partial(_nsa_slc_kernel, tq=tq, tk=tk, n_chunks=4),
        grid=(B, G, T // tq),
        in_specs=[pl.BlockSpec((1, tq, gw), lambda b, g, qi: (b, qi, g)),
                  pl.BlockSpec((1, 1, tq, LANES), lambda b, g, qi: (b, g, qi, 0)),
                  pl.BlockSpec((1, 1, T, LANES), lambda b, g, qi: (b, g, 0, 0)),
                  pl.BlockSpec((T, LANES), lambda b, g, qi: (0, 0)),
                  pl.BlockSpec((1, 1, T // tk, HEAD_DIM, tk), lambda b, g, qi: (b, g, 0, 0, 0)), _gate_spec(1, tq)],
        out_specs=pl.BlockSpec((1, tq, gw), lambda b, g, qi: (b, qi, g)),
        out_shape=jax.ShapeDtypeStruct((B, T, hd), BF16),
        scratch_shapes=[pltpu.VMEM((cols, 2 * LANES), BF16), pltpu.VMEM((2, tk, cols), F32),
                        pltpu.VMEM((2, 1, cols), F32), pltpu.VMEM((1, cols), F32),
                        pltpu.VMEM((ACC_ROWS, cols), F32)],
        compiler_params=_cparams(("parallel", "parallel", "arbitrary")),
        name="nsa_slc",
    )(q, sel, ks, onehot, vst, gates_t)


def _nsa_win_kernel(q_ref, k_ref, vt_ref, gate_ref, o_ref, qs_sc, s_sc, m_sc, acc_sc, *, tq, n_back, n_chunks):
    qi = pl.program_id(2)
    _stack_heads(q_ref[0], qs_sc, tq)
    _softmax_init(m_sc, acc_sc)
    cw = HEADS_PER_GROUP * tq // n_chunks

    def produce(ti, slot):
        start = pl.multiple_of(ti * tq, tq)
        k = k_ref[0, 0, pl.ds(start, tq), :]
        for c in range(n_chunks):
            s_sc[slot, :, c * cw:(c + 1) * cw] = _dot_nt(k, qs_sc[c * cw:(c + 1) * cw, :])

    def consume(ti, slot, kind):
        vt = vt_ref[0, 0, ti]
        for c in range(n_chunks):
            cols = slice(c * cw, (c + 1) * cw)
            s = s_sc[slot, :, cols]
            r = _iota(s.shape, 0)
            q_in_tile = _iota(s.shape, 1) & (tq - 1)
            if kind == "oldest":
                s = jnp.where(r > q_in_tile, s, NEG)
            elif kind == "diag":
                s = jnp.where(r <= q_in_tile, s, NEG)
            _softmax_update_t(s, vt, m_sc.at[:, cols], acc_sc.at[:, cols])

    def kind_of(back):
        return "oldest" if back == n_back else ("diag" if back == 0 else "full")

    for first in range(n_back + 1):
        cond = (qi >= n_back) if first == n_back else (qi == first)

        @pl.when(cond)
        def _(first=first):
            backs = list(range(first, -1, -1))
            produce(qi - backs[0], 0)
            for n, back in enumerate(backs):
                if n + 1 < len(backs):
                    produce(qi - backs[n + 1], (n + 1) % 2)
                consume(qi - back, n % 2, kind_of(back))

    _unstack_heads_t(_softmax_result(acc_sc), gate_ref[0], o_ref, tq)


def _nsa_win(q, kw, vwt, gates_t, tq=WIN_TQ):
    B, T, hd = q.shape
    G = N_KV_GROUPS
    gw = hd // G
    cols = HEADS_PER_GROUP * tq
    return pl.pallas_call(
        functools.partial(_nsa_win_kernel, tq=tq, n_back=WIN // tq, n_chunks=4),
        grid=(B, G, T // tq),
        in_specs=[pl.BlockSpec((1, tq, gw), lambda b, g, qi: (b, qi, g)),
                  pl.BlockSpec((1, 1, T, LANES), lambda b, g, qi: (b, g, 0, 0)),
                  pl.BlockSpec((1, 1, T // tq, HEAD_DIM, tq), lambda b, g, qi: (b, g, 0, 0, 0)), _gate_spec(2, tq)],
        out_specs=pl.BlockSpec((1, tq, gw), lambda b, g, qi: (b, qi, g)),
        out_shape=jax.ShapeDtypeStruct((B, T, hd), BF16),
        scratch_shapes=[pltpu.VMEM((cols, LANES), BF16), pltpu.VMEM((2, tq, cols), F32), pltpu.VMEM((1, cols), F32),
                        pltpu.VMEM((ACC_ROWS, cols), F32)],
        compiler_params=_cparams(("parallel", "parallel", "arbitrary")),
        name="nsa_win",
    )(q, kw, vwt, gates_t)


def _pad_cols(a, n):
    return jnp.pad(a, ((0, 0), (0, n - a.shape[1])))


def _lane_vec(v):
    return jnp.tile(v.astype(F32), 2).reshape(1, LANES)


def _bias_feature_selector():
    sel = np.zeros((N_BIAS_PARTS * LANES, N_HEADS * HEAD_DIM), np.float32)
    for part in range(N_BIAS_PARTS):
        for h in range(N_HEADS):
            sel[part * LANES + h, (h // 2) * LANES + (h % 2) * N_BIAS_PARTS + part] = 1.0
    return jnp.asarray(sel, BF16)


def kernel(x, positions, a_norm, a_w_in, a_b_f, a_q_gain, a_k_gain, a_w_out, kv_norm, kv_w, kc_pe, vc_pe, kc_w1, kc_w2, vc_w1, vc_w2, kc_gain, ks_gain, kw_gain, b_norm, b_w_in, b_b_gate, b_q_gain, b_w_out, f_norm, f_w_up, f_conv_w, f_conv_b, f_w_down):
    B, T, D = x.shape
    hd = N_HEADS * HEAD_DIM
    G = N_KV_GROUPS
    n_a = a_norm.shape[0]
    n_b = b_norm.shape[0]
    depth = n_a + n_b
    n_slc = T // SLC_LEN
    n_cmp = (T - CMP_LEN) // CMP_STRIDE + 1
    assert T % 1024 == 0 and n_slc <= LANES and hd == 1024 and D == 1024

    half = ROT_DIM // 2
    inv = ROPE_THETA ** (-jnp.arange(half, dtype=F32) * (2.0 / ROT_DIM))
    cos_t, sin_t = _rope_tables(positions, inv)
    end_pos = positions[:, CMP_LEN - 1::CMP_STRIDE]
    end_pos = jnp.pad(end_pos, ((0, 0), (0, T // CMP_STRIDE - n_cmp)))
    cos_c, sin_c = _rope_tables(end_pos, inv)

    cs = jnp.arange(T // CMP_STRIDE) * CMP_STRIDE
    ss = jnp.arange(LANES) * SLC_LEN
    overlap_t = (jnp.maximum(jnp.minimum(cs[None, :] + CMP_LEN, ss[:, None] + SLC_LEN)
                             - jnp.maximum(cs[None, :], ss[:, None]), 0).astype(F32) / CMP_LEN).astype(BF16)

    h = x
    kv = None
    for layer in range(depth):
        if layer < n_a:
            i = layer
            w = jnp.concatenate([a_w_in[i][:, :2 * hd], _pad_cols(a_w_in[i][:, 3 * hd:], LANES)], axis=1).astype(BF16)
            wvt = a_w_in[i][:, 2 * hd:3 * hd].T.astype(BF16)
            bf = _pad_cols(a_b_f[i].reshape(1, -1), LANES)
            q, k, vt, cf = _fox_inproj(h, a_norm[i].reshape(1, D), w, wvt, bf, _lane_vec(a_q_gain[i]),
                                       _lane_vec(a_k_gain[i]), _bias_feature_selector())
            mix, w_out = [_fox_attn(q, k, cf, vt)], a_w_out[i]
        else:
            i = layer - n_a
            kc_all, vct_all, ks, kw, vst, vwt = kv
            q, gates_t = _nsa_inproj(h, b_norm[i].reshape(1, D), b_w_in[i][:, :hd].astype(BF16),
                                     b_w_in[i][:, hd:].T.astype(BF16), b_b_gate[i].reshape(-1, 1),
                                     _lane_vec(b_q_gain[i]), cos_t, sin_t)
            o_c, sel = _nsa_cmp(q, kc_all, vct_all, overlap_t, gates_t, n_slc)
            o_s = _nsa_slc(q, sel, ks, vst, gates_t)
            o_w = _nsa_win(q, kw, vwt, gates_t)
            mix, w_out = [o_c, o_s, o_w], b_w_out[i]
        h = _mix_out_conv_ffn(h, mix, w_out.astype(BF16), f_norm[layer].reshape(1, D), f_w_up[layer].astype(BF16),
                              f_conv_w[layer], f_conv_b[layer].reshape(1, -1), f_w_down[layer].astype(BF16))
        if layer == n_a - 1:
            w6 = kv_w.reshape(D, 6, G, HEAD_DIM)
            raw_cols = w6[:, 0:2].reshape(D, 2 * G * HEAD_DIM)
            wk = w6[:, (2, 4)]
            dup_cols = jnp.concatenate([wk, wk], axis=-1).reshape(D, 2 * G * LANES)
            wkv = jnp.concatenate([raw_cols, dup_cols], axis=1).astype(BF16)
            wvt = w6[:, (3, 5)].reshape(D, 2 * G * HEAD_DIM).T.astype(BF16)
            craw, ks, kw, vst, vwt = _kvproj(h, kv_norm.reshape(1, D), wkv, wvt, _lane_vec(ks_gain),
                                             _lane_vec(kw_gain), cos_t, sin_t)
            r = craw.reshape(B, T, 2, G, HEAD_DIM).transpose(0, 2, 3, 1, 4).reshape(B, 2, G, T // CMP_STRIDE, CMP_STRIDE * HEAD_DIM)
            w1 = jnp.stack([kc_w1, vc_w1]).astype(BF16)
            pe = jnp.stack([kc_pe.reshape(1, -1), vc_pe.reshape(1, -1)])
            pe = jnp.pad(pe, ((0, 0), (0, 7), (0, 0))).astype(BF16)
            w2 = jnp.stack([jnp.concatenate([kc_w2, kc_w2], axis=1), jnp.concatenate([vc_w2, vc_w2], axis=1)]).astype(BF16)
            w2t = jnp.stack([kc_w2.T, vc_w2.T]).astype(BF16)
            kc_all, vct_all = _compress(r, w1, pe, w2, w2t, _lane_vec(kc_gain), cos_c, sin_c, n_cmp)
            kv = (kc_all, vct_all, ks, kw, vst, vwt)
    return h
```

```python
import functools

import numpy as np
import jax
import jax.numpy as jnp
from jax import lax
from jax.experimental import pallas as pl
from jax.experimental.pallas import tpu as pltpu

F32 = jnp.float32
BF16 = jnp.bfloat16

LANES = 128
SUBLANES = 8
MXU_COLS = 256
HEAD_DIM = 64
N_HEADS = 16
N_PAIRS = N_HEADS // 2
N_KV_GROUPS = 2
HEADS_PER_GROUP = N_HEADS // N_KV_GROUPS
ROT_DIM = HEAD_DIM // 4
ROPE_THETA = 500000.0
CMP_LEN = 32
CMP_STRIDE = 16
SLC_LEN = 64
SLC_TOPK = 16
WIN = 512
CONV_W = 3
RMS_EPS = 1e-6
NEG = -1e30
FORCED_SCORE = 1e6
LOG2E = 1.4426950408889634
Q_SCALE = HEAD_DIM ** -0.5 * LOG2E
N_BIAS_PARTS = 3
ONES_ROWS = 16
ACC_ROWS = HEAD_DIM + ONES_ROWS

ROW_TILE = 512
FOX_TQ = 512
FOX_TK = ROW_TILE
SLC_TQ = 256
SLC_TK = ROW_TILE
WIN_TQ = 256
CMP_TQ = 256
CMP_SUB = 128

VMEM_LIMIT = 48 * 1024 * 1024


def _cparams(sem):
    return pltpu.CompilerParams(dimension_semantics=sem, vmem_limit_bytes=VMEM_LIMIT)


def _iota(shape, axis):
    return lax.broadcasted_iota(jnp.int32, shape, axis)


def _row_rms(x, g):
    ms = jnp.mean(x * x, axis=-1, keepdims=True)
    return x * lax.rsqrt(ms + RMS_EPS) * g


def _pair_rms(y, gain):
    lo = _iota(y.shape, 1) < HEAD_DIM
    y2 = y * y
    s_lo = jnp.sum(jnp.where(lo, y2, 0.0), axis=-1, keepdims=True)
    s_hi = jnp.sum(jnp.where(lo, 0.0, y2), axis=-1, keepdims=True)
    ms = jnp.where(lo, s_lo, s_hi) * (1.0 / HEAD_DIM)
    return y * lax.rsqrt(ms + RMS_EPS) * gain


def _pair_rope(y, cos, sin):
    lane = _iota(y.shape, 1) & (HEAD_DIM - 1)
    partner = jnp.where(lane < ROT_DIM // 2,
                        pltpu.roll(y, LANES - ROT_DIM // 2, 1),
                        pltpu.roll(y, ROT_DIM // 2, 1))
    return y * cos + partner * sin


def _dot(a, b):
    return jnp.dot(a, b, preferred_element_type=F32)


def _dot_nt(a, b):
    return lax.dot_general(a, b, (((1,), (1,)), ((), ())), preferred_element_type=F32)


def _column_max(s):
    tk, n = s.shape
    return jnp.max(jnp.max(s.reshape(tk // SUBLANES, SUBLANES, n), axis=0), axis=0, keepdims=True)


def _softmax_update_t(s, vt, m_ref, acc_ref, m_tile=None):
    tk, n = s.shape
    m_old = m_ref[...]
    if m_tile is None:
        m_tile = _column_max(s)
    m_new = jnp.maximum(m_old, m_tile)
    alpha = jnp.exp2(m_old - m_new)
    p = jnp.exp2(s - m_new).astype(BF16)
    vta = jnp.concatenate([vt, jnp.ones((ONES_ROWS, tk), vt.dtype)], axis=0)
    acc_ref[...] = alpha * acc_ref[...] + _dot(vta, p)
    m_ref[...] = m_new


def _pipelined_key_loop(n_full, produce, consume, unroll=4):
    assert unroll % 2 == 0

    def run(base, count):
        for i in range(count):
            produce(base + i + 1, (i + 1) % 2)
            consume(base + i, i % 2, False)

    produce(0, 0)

    def body(j, carry):
        run(unroll * j, unroll)
        return carry

    lax.fori_loop(0, n_full // unroll, body, 0)
    base = (n_full // unroll) * unroll
    rem = n_full - base
    step = unroll // 2
    while step >= 2:
        pl.when((rem & step) != 0)(functools.partial(run, base, step))
        base = base + (rem & step)
        step //= 2

    @pl.when((rem & 1) != 0)
    def _():
        run(base, 1)
        consume(base + 1, 1, True)

    @pl.when((rem & 1) == 0)
    def _():
        consume(base, 0, True)


def _softmax_init(m_ref, acc_ref):
    m_ref[...] = jnp.full_like(m_ref, NEG)
    acc_ref[...] = jnp.zeros_like(acc_ref)


def _softmax_result(acc_ref):
    acc = acc_ref[...]
    return acc[:HEAD_DIM, :] * (1.0 / acc[HEAD_DIM:HEAD_DIM + 1, :])


def _rope_table_kernel(pos_ref, inv_ref, c_ref, s_ref):
    ang = pos_ref[0].astype(F32) * inv_ref[...]
    c_ref[0] = jnp.cos(ang)
    s_ref[0] = jnp.sin(ang)


def _rope_tables(pos, inv):
    B, T = pos.shape
    half = inv.shape[0]
    per_row = LANES // half
    rows = T // per_row
    pos_rep = jnp.repeat(pos.reshape(B, rows, per_row), half, axis=-1)
    inv_row = jnp.tile(inv, per_row).reshape(1, LANES)
    spec = pl.BlockSpec((1, rows, LANES), lambda b: (b, 0, 0))
    cos, sin = pl.pallas_call(
        _rope_table_kernel,
        grid=(B,),
        in_specs=[spec, pl.BlockSpec((1, LANES), lambda b: (0, 0))],
        out_specs=[spec, spec],
        out_shape=[jax.ShapeDtypeStruct((B, rows, LANES), F32)] * 2,
        compiler_params=_cparams(("parallel",)),
        name="rope_tables",
    )(pos_rep, inv_row)
    cos = cos.reshape(B, T, half)
    sin = sin.reshape(B, T, half)
    rest = HEAD_DIM - 2 * half
    cos_head = jnp.concatenate([cos, cos, jnp.ones((B, T, rest), F32)], axis=-1)
    sin_head = jnp.concatenate([-sin, sin, jnp.zeros((B, T, rest), F32)], axis=-1)
    return jnp.tile(cos_head, (1, 1, LANES // HEAD_DIM)), jnp.tile(sin_head, (1, 1, LANES // HEAD_DIM))


def _fox_inproj_kernel(x_ref, g_ref, w_ref, wvt_ref, bf_ref, qg_ref, kg_ref, sel_ref,
                       q_ref, k_ref, vt_ref, cf_ref, carry_sc, *, tm, hd):
    ti = pl.program_id(1)
    xn = _row_rms(x_ref[0], g_ref[...]).astype(BF16)
    for j in range(hd // MXU_COLS):
        for part, (ref, gain, mul) in enumerate(((q_ref, qg_ref, Q_SCALE), (k_ref, kg_ref, 1.0))):
            c0 = part * hd + MXU_COLS * j
            y = _dot(xn, w_ref[:, c0:c0 + MXU_COLS])
            for hh in range(2):
                blk = _pair_rms(y[:, LANES * hh:LANES * (hh + 1)], gain[...]) * mul
                ref[0, :, MXU_COLS * j + LANES * hh:MXU_COLS * j + LANES * (hh + 1)] = blk.astype(BF16)
        yt = _dot_nt(wvt_ref[MXU_COLS * j:MXU_COLS * (j + 1), :], xn)
        for hh in range(2):
            vt_ref[0, 2 * j + hh, 0] = yt[LANES * hh:LANES * (hh + 1), :].astype(BF16)
    z = _dot(xn, w_ref[:, 2 * hd:2 * hd + LANES]) + bf_ref[...]
    lf = jnp.minimum(z, 0.0) - jnp.log1p(jnp.exp(-jnp.abs(z)))
    row = _iota(lf.shape, 0)
    sh = 1
    while sh < tm:
        lf = lf + jnp.where(row >= sh, pltpu.roll(lf, sh, 0), 0.0)
        sh *= 2

    @pl.when(ti == 0)
    def _():
        carry_sc[...] = jnp.zeros_like(carry_sc)

    c = lf + carry_sc[0:1, :]
    carry_sc[...] = jnp.broadcast_to(c[tm - 1:tm, :], carry_sc.shape)
    rest = c * (-LOG2E)
    pieces = []
    for _ in range(N_BIAS_PARTS):
        piece = rest.astype(BF16)
        pieces.append(piece)
        rest = rest - piece.astype(F32)
    cf_ref[0] = _dot(jnp.concatenate(pieces, axis=1), sel_ref[...]).astype(BF16)


def _fox_inproj(x, g, w, wvt, bf, qg, kg, sel, tm=ROW_TILE):
    B, T, D = x.shape
    hd = N_HEADS * HEAD_DIM
    act = pl.BlockSpec((1, tm, hd), lambda b, t: (b, t, 0))
    vec = lambda n: pl.BlockSpec((1, n), lambda b, t: (0, 0))
    full = lambda a: pl.BlockSpec(a.shape, lambda b, t: (0,) * a.ndim)
    return pl.pallas_call(
        functools.partial(_fox_inproj_kernel, tm=tm, hd=hd),
        grid=(B, T // tm),
        in_specs=[pl.BlockSpec((1, tm, D), lambda b, t: (b, t, 0)), vec(D), full(w), full(wvt),
                  vec(LANES), vec(LANES), vec(LANES), full(sel)],
        out_specs=[act, act, pl.BlockSpec((1, N_PAIRS, 1, LANES, tm), lambda b, t: (b, 0, t, 0, 0)), act],
        out_shape=[jax.ShapeDtypeStruct((B, T, hd), BF16), jax.ShapeDtypeStruct((B, T, hd), BF16),
                   jax.ShapeDtypeStruct((B, N_PAIRS, T // tm, LANES, tm), BF16),
                   jax.ShapeDtypeStruct((B, T, hd), BF16)],
        scratch_shapes=[pltpu.VMEM((SUBLANES, LANES), F32)],
        compiler_params=_cparams(("arbitrary", "arbitrary")),
        name="fox_inproj",
    )(x, g, w, wvt, bf, qg, kg, sel)


def _fox_attn_kernel(q_ref, k_ref, cf_ref, vt_ref, o_ref, qa_sc, s_sc, mt_sc, m_sc, acc_sc, *, tq, tk, pairs):
    qi = pl.program_id(2)
    lane = _iota((tq, LANES), 1)
    lo = lane < HEAD_DIM
    for hh in range(2 * pairs):
        pp, half = divmod(hh, 2)
        q = q_ref[0, :, LANES * pp:LANES * (pp + 1)]
        zero = jnp.zeros_like(q)
        qa_sc[hh, :, 0:LANES] = jnp.where(lo, q, zero) if half == 0 else jnp.where(lo, zero, q)
        feat = jnp.where(lane < N_BIAS_PARTS * (half + 1), 1.0, 0.0)
        qa_sc[hh, :, LANES:2 * LANES] = jnp.where(lane >= N_BIAS_PARTS * half, feat, 0.0).astype(BF16)
        _softmax_init(m_sc.at[hh], acc_sc.at[hh])

    def produce(ki, slot):
        start = pl.multiple_of(ki * tk, tk)
        for pp in range(pairs):
            cols = slice(LANES * pp, LANES * (pp + 1))
            ka = jnp.concatenate([k_ref[0, pl.ds(start, tk), cols], cf_ref[0, pl.ds(start, tk), cols]], axis=1)
            for hh in (2 * pp, 2 * pp + 1):
                s = _dot_nt(ka, qa_sc[hh])
                s_sc[slot, hh] = s
                mt_sc[slot, hh] = _column_max(s)

    def consume(ki, slot, diag):
        for hh in range(2 * pairs):
            pp, half = divmod(hh, 2)
            s = s_sc[slot, hh]
            m_tile = mt_sc[slot, hh]
            if diag:
                s = jnp.where(_iota(s.shape, 0) <= _iota(s.shape, 1), s, NEG)
                m_tile = None
            vt = vt_ref[0, pp, ki, HEAD_DIM * half:HEAD_DIM * (half + 1), :]
            _softmax_update_t(s, vt, m_sc.at[hh], acc_sc.at[hh], m_tile)

    _pipelined_key_loop(qi, produce, consume)
    for pp in range(pairs):
        o_t = jnp.concatenate([_softmax_result(acc_sc.at[2 * pp + half]) for half in range(2)], axis=0)
        o_ref[0, :, LANES * pp:LANES * (pp + 1)] = o_t.T.astype(BF16)


def _fox_attn(q, k, cf, vt, tq=FOX_TQ, tk=FOX_TK, pairs=2):
    B, T, hd = q.shape
    assert tq == tk
    nk = T // tk
    heads = 2 * pairs
    width = pairs * LANES
    seq = pl.BlockSpec((1, T, width), lambda b, p, qi: (b, 0, p))
    return pl.pallas_call(
        functools.partial(_fox_attn_kernel, tq=tq, tk=tk, pairs=pairs),
        grid=(B, N_PAIRS // pairs, T // tq),
        in_specs=[pl.BlockSpec((1, tq, width), lambda b, p, qi: (b, qi, p)), seq, seq,
                  pl.BlockSpec((1, pairs, nk, LANES, tk), lambda b, p, qi: (b, p, 0, 0, 0))],
        out_specs=pl.BlockSpec((1, tq, width), lambda b, p, qi: (b, qi, p)),
        out_shape=jax.ShapeDtypeStruct((B, T, hd), BF16),
        scratch_shapes=[pltpu.VMEM((heads, tq, 2 * LANES), BF16), pltpu.VMEM((2, heads, tk, tq), F32),
                        pltpu.VMEM((2, heads, 1, tq), F32), pltpu.VMEM((heads, 1, tq), F32),
                        pltpu.VMEM((heads, ACC_ROWS, tq), F32)],
        compiler_params=_cparams(("parallel", "parallel", "arbitrary")),
        name="fox_attn",
    )(q, k, cf, vt)


def _ffn_kernel(*refs, n_mix, tt, tf, d_ff):
    h_ref = refs[0]
    mix_refs = refs[1:1 + n_mix]
    wo_ref, g_ref, wup_ref, cw_ref, cb_ref, wd_ref, out_ref, a_sc, carry_sc = refs[1 + n_mix:]
    ti = pl.program_id(1)

    @pl.when(ti == 0)
    def _():
        carry_sc[...] = jnp.zeros_like(carry_sc)

    o = mix_refs[0][0]
    if n_mix > 1:
        o = o.astype(F32)
        for ref in mix_refs[1:]:
            o = o + ref[0].astype(F32)
        o = o.astype(BF16)
    x = h_ref[0] + _dot(o, wo_ref[...])
    xn = _row_rms(x, g_ref[...]).astype(BF16)
    row8 = _iota((SUBLANES, tf), 0)

    def conv(u, c0):
        prev8 = carry_sc[:, c0:c0 + tf]
        um1 = pltpu.roll(u, 1, 0)
        um2 = pltpu.roll(u, 2, 0)
        top1 = jnp.where(row8 == 0, prev8[7:8, :], um1[0:SUBLANES, :])
        top2 = jnp.where(row8 == 0, prev8[6:7, :], jnp.where(row8 == 1, prev8[7:8, :], um2[0:SUBLANES, :]))
        um1 = jnp.concatenate([top1, um1[SUBLANES:, :]], axis=0)
        um2 = jnp.concatenate([top2, um2[SUBLANES:, :]], axis=0)
        carry_sc[:, c0:c0 + tf] = u[tt - SUBLANES:, :]
        cw = cw_ref[:, c0:c0 + tf]
        return cb_ref[:, c0:c0 + tf] + cw[0:1, :] * um2 + cw[1:2, :] * um1 + cw[2:3, :] * u

    for f in range(d_ff // tf):
        g0 = f * tf
        cg = conv(_dot(xn, wup_ref[:, g0:g0 + tf]), g0)
        cv = conv(_dot(xn, wup_ref[:, d_ff + g0:d_ff + g0 + tf]), d_ff + g0)
        a_sc[:, g0:g0 + tf] = (cg * jax.nn.sigmoid(cg) * cv).astype(BF16)
    out_ref[0] = x + _dot(a_sc[...], wd_ref[...])


def _mix_out_conv_ffn(h, mix, w_out, g, w_up, conv_w, conv_b, w_down, tt=ROW_TILE, tf=MXU_COLS):
    B, T, D = h.shape
    d_ff = w_down.shape[0]
    act = lambda c: pl.BlockSpec((1, tt, c), lambda b, t: (b, t, 0))
    resident = lambda a: pl.BlockSpec(a.shape, lambda b, t: (0,) * a.ndim, pipeline_mode=pl.Buffered(1))
    return pl.pallas_call(
        functools.partial(_ffn_kernel, n_mix=len(mix), tt=tt, tf=tf, d_ff=d_ff),
        grid=(B, T // tt),
        in_specs=[act(D)] + [act(m.shape[-1]) for m in mix]
                 + [resident(w_out), resident(g), resident(w_up), resident(conv_w), resident(conv_b), resident(w_down)],
        out_specs=act(D),
        out_shape=jax.ShapeDtypeStruct((B, T, D), F32),
        scratch_shapes=[pltpu.VMEM((tt, d_ff), BF16), pltpu.VMEM((SUBLANES, 2 * d_ff), F32)],
        compiler_params=_cparams(("arbitrary", "arbitrary")),
        name="conv_ffn",
    )(h, *mix, w_out, g, w_up, conv_w, conv_b, w_down)


def _kvproj_kernel(h_ref, g_ref, w_ref, wvt_ref, ksg_ref, kwg_ref, c_ref, s_ref,
                   craw_ref, ks_ref, kw_ref, vst_ref, vwt_ref, *, tm):
    xn = _row_rms(h_ref[0], g_ref[...]).astype(BF16)
    cos = c_ref[0]
    sin = s_ref[0]
    raw_cols = craw_ref.shape[-1]
    craw_ref[0] = _dot(xn, w_ref[:, 0:raw_cols])
    for idx, (ref, gain) in enumerate(((ks_ref, ksg_ref), (kw_ref, kwg_ref))):
        c0 = raw_cols + N_KV_GROUPS * LANES * idx
        y = _dot(xn, w_ref[:, c0:c0 + N_KV_GROUPS * LANES])
        for grp in range(N_KV_GROUPS):
            blk = _pair_rope(_pair_rms(y[:, LANES * grp:LANES * (grp + 1)], gain[...]), cos, sin)
            ref[0, grp] = blk.astype(BF16)
    yt = _dot_nt(wvt_ref[...], xn).astype(BF16)
    for grp in range(N_KV_GROUPS):
        vst_ref[0, grp, 0] = yt[HEAD_DIM * grp:HEAD_DIM * (grp + 1), :]
        r0 = HEAD_DIM * (N_KV_GROUPS + grp)
        for c in range(tm // WIN_TQ):
            vwt_ref[0, grp, c] = yt[r0:r0 + HEAD_DIM, WIN_TQ * c:WIN_TQ * (c + 1)]


def _kvproj(h, g, w, wvt, ksg, kwg, cos, sin, tm=ROW_TILE):
    B, T, D = h.shape
    G = N_KV_GROUPS
    vec = lambda n: pl.BlockSpec((1, n), lambda b, t: (0, 0))
    full = lambda a: pl.BlockSpec(a.shape, lambda b, t: (0,) * a.ndim)
    tab = pl.BlockSpec((1, tm, LANES), lambda b, t: (b, t, 0))
    dup = pl.BlockSpec((1, G, tm, LANES), lambda b, t: (b, 0, t, 0))
    dup_shape = jax.ShapeDtypeStruct((B, G, T, LANES), BF16)
    nw = tm // WIN_TQ
    raw_cols = 2 * G * HEAD_DIM
    return pl.pallas_call(
        functools.partial(_kvproj_kernel, tm=tm),
        grid=(B, T // tm),
        in_specs=[pl.BlockSpec((1, tm, D), lambda b, t: (b, t, 0)), vec(D), full(w), full(wvt),
                  vec(LANES), vec(LANES), tab, tab],
        out_specs=[pl.BlockSpec((1, tm, raw_cols), lambda b, t: (b, t, 0)), dup, dup,
                   pl.BlockSpec((1, G, 1, HEAD_DIM, tm), lambda b, t: (b, 0, t, 0, 0)),
                   pl.BlockSpec((1, G, nw, HEAD_DIM, WIN_TQ), lambda b, t: (b, 0, t, 0, 0))],
        out_shape=[jax.ShapeDtypeStruct((B, T, raw_cols), F32), dup_shape, dup_shape,
                   jax.ShapeDtypeStruct((B, G, T // tm, HEAD_DIM, tm), BF16),
                   jax.ShapeDtypeStruct((B, G, T // WIN_TQ, HEAD_DIM, WIN_TQ), BF16)],
        compiler_params=_cparams(("parallel", "parallel")),
        name="kvproj",
    )(h, g, w, wvt, ksg, kwg, cos, sin)


def _compress_kernel(r_ref, w1_ref, pe_ref, w2_ref, w2t_ref, gain_ref, c_ref, s_ref, kc_ref, vct_ref, *, n_cmp):
    r = r_ref[0, 0, 0].astype(BF16)
    half = r.shape[1]
    a = _dot(r, w1_ref[0, :half, :])
    b = _dot(r, w1_ref[0, half:, :])
    peb = _dot(pe_ref[0], w1_ref[0])[0:1, :]
    rows = r.shape[0]
    hid = a + pltpu.roll(b, rows - 1, 0) + peb
    act = jax.nn.gelu(hid).astype(BF16)
    y = _dot(act, w2_ref[0])
    yk = _pair_rope(_pair_rms(y, gain_ref[...]), c_ref[0], s_ref[0])
    kc_ref[0, 0, 0] = jnp.where(_iota(y.shape, 0) < n_cmp, yk, 0.0).astype(BF16)
    yt = _dot_nt(w2t_ref[0], act)
    vct_ref[0, 0, 0] = jnp.where(_iota(yt.shape, 1) < n_cmp, yt, 0.0).astype(BF16)


def _compress(r, w1, pe, w2, w2t, gain, cos, sin, n_cmp):
    B, _, G, rows, width = r.shape
    per_kv = lambda a: pl.BlockSpec((1,) + a.shape[1:], lambda b, kv, g: (kv,) + (0,) * (a.ndim - 1))
    return pl.pallas_call(
        functools.partial(_compress_kernel, n_cmp=n_cmp),
        grid=(B, 2, G),
        in_specs=[pl.BlockSpec((1, 1, 1, rows, width), lambda b, kv, g: (b, kv, g, 0, 0)),
                  per_kv(w1), per_kv(pe), per_kv(w2), per_kv(w2t),
                  pl.BlockSpec((1, LANES), lambda b, kv, g: (0, 0)),
                  pl.BlockSpec((1, rows, LANES), lambda b, kv, g: (b, 0, 0)),
                  pl.BlockSpec((1, rows, LANES), lambda b, kv, g: (b, 0, 0))],
        out_specs=[pl.BlockSpec((1, 1, 1, rows, LANES), lambda b, kv, g: (b, kv, g, 0, 0)),
                   pl.BlockSpec((1, 1, 1, HEAD_DIM, rows), lambda b, kv, g: (b, kv, g, 0, 0))],
        out_shape=[jax.ShapeDtypeStruct((B, 2, G, rows, LANES), BF16),
                   jax.ShapeDtypeStruct((B, 2, G, HEAD_DIM, rows), BF16)],
        compiler_params=_cparams(("parallel", "parallel", "parallel")),
        name="compress",
    )(r, w1, pe, w2, w2t, gain, cos, sin)


def _nsa_inproj_kernel(h_ref, g_ref, w_ref, wgt_ref, bgt_ref, qg_ref, c_ref, s_ref, q_ref, gate_ref, *, hd):
    xn = _row_rms(h_ref[0], g_ref[...]).astype(BF16)
    cos = c_ref[0]
    sin = s_ref[0]
    for j in range(hd // MXU_COLS):
        y = _dot(xn, w_ref[:, MXU_COLS * j:MXU_COLS * (j + 1)])
        for hh in range(2):
            blk = _pair_rope(_pair_rms(y[:, LANES * hh:LANES * (hh + 1)], qg_ref[...]), cos, sin) * Q_SCALE
            q_ref[0, :, MXU_COLS * j + LANES * hh:MXU_COLS * j + LANES * (hh + 1)] = blk.astype(BF16)
    gate_ref[0] = jax.nn.sigmoid(_dot_nt(wgt_ref[...], xn) + bgt_ref[...])


def _nsa_inproj(h, g, w, wgt, bgt, qg, cos, sin, tm=ROW_TILE):
    B, T, D = h.shape
    hd = N_HEADS * HEAD_DIM
    vec = lambda n: pl.BlockSpec((1, n), lambda b, t: (0, 0))
    full = lambda a: pl.BlockSpec(a.shape, lambda b, t: (0,) * a.ndim)
    tab = pl.BlockSpec((1, tm, LANES), lambda b, t: (b, t, 0))
    return pl.pallas_call(
        functools.partial(_nsa_inproj_kernel, hd=hd),
        grid=(B, T // tm),
        in_specs=[pl.BlockSpec((1, tm, D), lambda b, t: (b, t, 0)), vec(D), full(w), full(wgt), full(bgt),
                  vec(LANES), tab, tab],
        out_specs=[pl.BlockSpec((1, tm, hd), lambda b, t: (b, t, 0)),
                   pl.BlockSpec((1, wgt.shape[0], tm), lambda b, t: (b, 0, t))],
        out_shape=[jax.ShapeDtypeStruct((B, T, hd), BF16), jax.ShapeDtypeStruct((B, wgt.shape[0], T), F32)],
        compiler_params=_cparams(("parallel", "parallel")),
        name="nsa_inproj",
    )(h, g, w, wgt, bgt, qg, cos, sin)


def _stack_heads(q, dst_ref, tq):
    lo = _iota((tq, LANES), 1) < HEAD_DIM
    zero = jnp.zeros((tq, LANES), q.dtype)
    for j in range(HEADS_PER_GROUP // 2):
        x = q[:, LANES * j:LANES * (j + 1)]
        dst_ref[(2 * j) * tq:(2 * j + 1) * tq, 0:LANES] = jnp.where(lo, x, zero)
        dst_ref[(2 * j + 1) * tq:(2 * j + 2) * tq, 0:LANES] = jnp.where(lo, zero, x)


def _unstack_heads_t(o_t, gate, o_ref, tq, row0=0):
    for j in range(HEADS_PER_GROUP // 2):
        halves = [o_t[:, h * tq:(h + 1) * tq] * gate[h:h + 1, :] for h in (2 * j, 2 * j + 1)]
        o_ref[0, row0:row0 + tq, LANES * j:LANES * (j + 1)] = jnp.concatenate(halves, axis=0).T.astype(o_ref.dtype)


def _cmp_query_mask_features(sub):
    j = np.arange(sub)
    v = (j // CMP_STRIDE) + ((j % CMP_STRIDE) == CMP_STRIDE - 1)
    feat = np.zeros((sub, LANES), np.float32)
    feat[j, v] = NEG
    return jnp.asarray(feat, BF16)


def _nsa_cmp_kernel(q_ref, kc_ref, vct_ref, ovt_ref, gate_ref, qf_ref, oc_ref, sel_ref, qs_sc, *, tq, sub, n_slc, top_k, n_levels):
    qi = pl.program_id(2)
    n_sub = tq // sub
    n_feats = sub // CMP_STRIDE + 1
    qf = qf_ref[...]
    for u in range(n_sub):
        _stack_heads(q_ref[0, u * sub:(u + 1) * sub, :], qs_sc.at[u], sub)
        for h in range(HEADS_PER_GROUP):
            qs_sc[u, h * sub:(h + 1) * sub, LANES:2 * LANES] = qf
    n_rows = kc_ref.shape[3]
    level_rows = n_rows // n_levels
    n_chunks = 2
    cw = HEADS_PER_GROUP * sub // n_chunks
    n_vis = (qi * tq + tq - 1 - (CMP_LEN - 1)) // CMP_STRIDE + 1

    def attend(rows, u):
        base = (qi * tq + u * sub) // CMP_STRIDE - 2
        n_minus_v = _iota((rows, LANES), 0) - _iota((rows, LANES), 1)
        kf = jnp.where(_iota((rows, LANES), 1) < n_feats, jnp.where(n_minus_v > base, 1.0, 0.0), 0.0)
        ka = jnp.concatenate([kc_ref[0, 0, 0, :rows, :], kf.astype(BF16)], axis=1)
        vct = vct_ref[0, 0, 0, :, :rows]
        scores = [_dot_nt(ka, qs_sc[u, c * cw:(c + 1) * cw, :]) for c in range(n_chunks)]
        outs = []
        p_sum = None
        for s in scores:
            mx = jnp.max(s, axis=0, keepdims=True)
            e = jnp.exp2(s - mx)
            inv = jnp.where(mx > 0.5 * NEG, 1.0 / jnp.maximum(jnp.sum(e, axis=0, keepdims=True), 1.0), 0.0)
            p = e * inv
            outs.append(_dot(vct, p.astype(BF16)))
            for h in range(cw // sub):
                term = p[:, h * sub:(h + 1) * sub]
                p_sum = term if p_sum is None else p_sum + term
        _unstack_heads_t(jnp.concatenate(outs, axis=1), gate_ref[0, :, u * sub:(u + 1) * sub], oc_ref, sub, u * sub)
        p_hi = p_sum.astype(BF16)
        p_lo = (p_sum - p_hi.astype(F32)).astype(BF16)
        return _dot(ovt_ref[:, :rows], p_hi) + _dot(ovt_ref[:, :rows], p_lo)

    def select(imp, u, n_blk):
        imp = imp[:n_blk, :]
        jb = _iota(imp.shape, 0)
        qp = qi * tq + u * sub + _iota(imp.shape, 1)
        cur = qp // SLC_LEN
        forced = (jb == 0) | (jb == cur) | (jb == cur - 1)
        score = jnp.where(jb * SLC_LEN <= qp, jnp.where(forced, FORCED_SCORE, imp), NEG)
        live = jb < n_slc
        score = jnp.where(live, score, -jnp.inf)
        for _ in range(top_k):
            mx = jnp.max(score, axis=0, keepdims=True)
            idx = jnp.min(jnp.where(score == mx, jb, LANES), axis=0, keepdims=True)
            score = jnp.where(jb == idx, -jnp.inf, score)
        bias = jnp.where(live, jnp.where(score == -jnp.inf, 0.0, NEG), NEG)
        if n_blk < LANES:
            bias = jnp.concatenate([bias, jnp.full((LANES - n_blk, sub), NEG, F32)], axis=0)
        sel_ref[0, 0, u * sub:(u + 1) * sub, :] = bias.T.astype(BF16)

    def tile(rows):
        n_blk = rows * CMP_STRIDE // SLC_LEN
        n_blk = n_blk if (n_blk >= top_k and n_blk % SUBLANES == 0) else LANES
        imps = [attend(rows, u) for u in range(n_sub)]
        for u in range(n_sub):
            select(imps[u], u, n_blk)

    for level in range(n_levels):
        lo_rows, hi_rows = level * level_rows, (level + 1) * level_rows
        cond = (n_vis <= hi_rows) if level == 0 else ((n_vis > lo_rows) & (n_vis <= hi_rows))
        pl.when(cond)(functools.partial(tile, hi_rows))


def _gate_spec(branch, tq):
    return pl.BlockSpec((1, HEADS_PER_GROUP, tq), lambda b, g, qi: (b, branch * N_KV_GROUPS + g, qi))


def _nsa_cmp(q, kc_all, vct_all, overlap_t, gates_t, n_slc, tq=CMP_TQ):
    B, T, hd = q.shape
    G = N_KV_GROUPS
    rows = kc_all.shape[3]
    gw = hd // G
    top_k = min(SLC_TOPK, n_slc)
    assert tq % CMP_SUB == 0 and CMP_SUB % SLC_LEN == 0 and CMP_SUB % LANES == 0
    n_levels = 4 if rows % (4 * LANES) == 0 else 1
    return pl.pallas_call(
        functools.partial(_nsa_cmp_kernel, tq=tq, sub=CMP_SUB, n_slc=n_slc, top_k=top_k, n_levels=n_levels),
        grid=(B, G, T // tq),
        in_specs=[pl.BlockSpec((1, tq, gw), lambda b, g, qi: (b, qi, g)),
                  pl.BlockSpec((1, 1, 1, rows, LANES), lambda b, g, qi: (b, 0, g, 0, 0)),
                  pl.BlockSpec((1, 1, 1, HEAD_DIM, rows), lambda b, g, qi: (b, 1, g, 0, 0)),
                  pl.BlockSpec(overlap_t.shape, lambda b, g, qi: (0, 0)), _gate_spec(0, tq),
                  pl.BlockSpec((CMP_SUB, LANES), lambda b, g, qi: (0, 0))],
        out_specs=[pl.BlockSpec((1, tq, gw), lambda b, g, qi: (b, qi, g)),
                   pl.BlockSpec((1, 1, tq, LANES), lambda b, g, qi: (b, g, qi, 0))],
        out_shape=[jax.ShapeDtypeStruct((B, T, hd), BF16), jax.ShapeDtypeStruct((B, G, T, LANES), BF16)],
        scratch_shapes=[pltpu.VMEM((tq // CMP_SUB, HEADS_PER_GROUP * CMP_SUB, 2 * LANES), BF16)],
        compiler_params=_cparams(("parallel", "parallel", "parallel")),
        name="nsa_cmp",
    )(q, kc_all, vct_all, overlap_t, gates_t, _cmp_query_mask_features(CMP_SUB))


def _nsa_slc_kernel(q_ref, sel_ref, k_ref, oh_ref, vt_ref, gate_ref, o_ref, qa_sc, s_sc, mt_sc, m_sc, acc_sc, *, tq, tk, n_chunks):
    qi = pl.program_id(2)
    last = (qi * tq) // tk
    _stack_heads(q_ref[0], qa_sc, tq)
    sel = sel_ref[0, 0]
    for h in range(HEADS_PER_GROUP):
        qa_sc[h * tq:(h + 1) * tq, LANES:2 * LANES] = sel
    _softmax_init(m_sc, acc_sc)
    cw = HEADS_PER_GROUP * tq // n_chunks

    def produce(ki, slot):
        start = pl.multiple_of(ki * tk, tk)
        ka = jnp.concatenate([k_ref[0, 0, pl.ds(start, tk), :], oh_ref[pl.ds(start, tk), :]], axis=1)
        for c in range(n_chunks):
            s = _dot_nt(ka, qa_sc[c * cw:(c + 1) * cw, :])
            s_sc[slot, :, c * cw:(c + 1) * cw] = s
            mt_sc[slot, :, c * cw:(c + 1) * cw] = _column_max(s)

    def consume(ki, slot, diag):
        vt = vt_ref[0, 0, ki]
        for c in range(n_chunks):
            cols = slice(c * cw, (c + 1) * cw)
            s = s_sc[slot, :, cols]
            m_tile = mt_sc[slot, :, cols]
            if diag:
                kpos = ki * tk + _iota(s.shape, 0)
                qpos = qi * tq + (_iota(s.shape, 1) & (tq - 1))
                s = jnp.where(kpos <= qpos, s, NEG)
                m_tile = None
            _softmax_update_t(s, vt, m_sc.at[:, cols], acc_sc.at[:, cols], m_tile)

    _pipelined_key_loop(last, produce, consume)
    _unstack_heads_t(_softmax_result(acc_sc), gate_ref[0], o_ref, tq)


def _nsa_slc(q, sel, ks, vst, gates_t, tq=SLC_TQ, tk=SLC_TK):
    B, T, hd = q.shape
    G = N_KV_GROUPS
    gw = hd // G
    cols = HEADS_PER_GROUP * tq
    assert tk % tq == 0
    onehot = (jnp.arange(T)[:, None] // SLC_LEN == jnp.arange(LANES)[None, :]).astype(BF16)
    return pl.pallas_call(
        functools.partial(_nsa_slc_kernel, tq=tq, tk=tk, n_chunks=4),
        grid=(B, G, T // tq),
        in_specs=[pl.BlockSpec((1, tq, gw), lambda b, g, qi: (b, qi, g)),
                  pl.BlockSpec((1, 1, tq, LANES), lambda b, g, qi: (b, g, qi, 0)),
                  pl.BlockSpec((1, 1, T, LANES), lambda b, g, qi: (b, g, 0, 0)),
                  pl.BlockSpec((T, LANES), lambda b, g, qi: (0, 0)),
                  pl.BlockSpec((1, 1, T // tk, HEAD_DIM, tk), lambda b, g, qi: (b, g, 0, 0, 0)), _gate_spec(1, tq)],
        out_specs=pl.BlockSpec((1, tq, gw), lambda b, g, qi: (b, qi, g)),
        out_shape=jax.ShapeDtypeStruct((B, T, hd), BF16),
        scratch_shapes=[pltpu.VMEM((cols, 2 * LANES), BF16), pltpu.VMEM((2, tk, cols), F32),
                        pltpu.VMEM((2, 1, cols), F32), pltpu.VMEM((1, cols), F32),
                        pltpu.VMEM((ACC_ROWS, cols), F32)],
        compiler_params=_cparams(("parallel", "parallel", "arbitrary")),
        name="nsa_slc",
    )(q, sel, ks, onehot, vst, gates_t)


def _nsa_win_kernel(q_ref, k_ref, vt_ref, gate_ref, o_ref, qs_sc, s_sc, m_sc, acc_sc, *, tq, n_back, n_chunks):
    qi = pl.program_id(2)
    _stack_heads(q_ref[0], qs_sc, tq)
    _softmax_init(m_sc, acc_sc)
    cw = HEADS_PER_GROUP * tq // n_chunks

    def produce(ti, slot):
        start = pl.multiple_of(ti * tq, tq)
        k = k_ref[0, 0, pl.ds(start, tq), :]
        for c in range(n_chunks):
            s_sc[slot, :, c * cw:(c + 1) * cw] = _dot_nt(k, qs_sc[c * cw:(c + 1) * cw, :])

    def consume(ti, slot, kind):
        vt = vt_ref[0, 0, ti]
        for c in range(n_chunks):
            cols = slice(c * cw, (c + 1) * cw)
            s = s_sc[slot, :, cols]
            r = _iota(s.shape, 0)
            q_in_tile = _iota(s.shape, 1) & (tq - 1)
            if kind == "oldest":
                s = jnp.where(r > q_in_tile, s, NEG)
            elif kind == "diag":
                s = jnp.where(r <= q_in_tile, s, NEG)
            _softmax_update_t(s, vt, m_sc.at[:, cols], acc_sc.at[:, cols])

    def kind_of(back):
        return "oldest" if back == n_back else ("diag" if back == 0 else "full")

    for first in range(n_back + 1):
        cond = (qi >= n_back) if first == n_back else (qi == first)

        @pl.when(cond)
        def _(first=first):
            backs = list(range(first, -1, -1))
            produce(qi - backs[0], 0)
            for n, back in enumerate(backs):
                if n + 1 < len(backs):
                    produce(qi - backs[n + 1], (n + 1) % 2)
                consume(qi - back, n % 2, kind_of(back))

    _unstack_heads_t(_softmax_result(acc_sc), gate_ref[0], o_ref, tq)


def _nsa_win(q, kw, vwt, gates_t, tq=WIN_TQ):
    B, T, hd = q.shape
    G = N_KV_GROUPS
    gw = hd // G
    cols = HEADS_PER_GROUP * tq
    return pl.pallas_call(
        functools.partial(_nsa_win_kernel, tq=tq, n_back=WIN // tq, n_chunks=4),
        grid=(B, G, T // tq),
        in_specs=[pl.BlockSpec((1, tq, gw), lambda b, g, qi: (b, qi, g)),
                  pl.BlockSpec((1, 1, T, LANES), lambda b, g, qi: (b, g, 0, 0)),
                  pl.BlockSpec((1, 1, T // tq, HEAD_DIM, tq), lambda b, g, qi: (b, g, 0, 0, 0)), _gate_spec(2, tq)],
        out_specs=pl.BlockSpec((1, tq, gw), lambda b, g, qi: (b, qi, g)),
        out_shape=jax.ShapeDtypeStruct((B, T, hd), BF16),
        scratch_shapes=[pltpu.VMEM((cols, LANES), BF16), pltpu.VMEM((2, tq, cols), F32), pltpu.VMEM((1, cols), F32),
                        pltpu.VMEM((ACC_ROWS, cols), F32)],
        compiler_params=_cparams(("parallel", "parallel", "arbitrary")),
        name="nsa_win",
    )(q, kw, vwt, gates_t)


def _pad_cols(a, n):
    return jnp.pad(a, ((0, 0), (0, n - a.shape[1])))


def _lane_vec(v):
    return jnp.tile(v.astype(F32), 2).reshape(1, LANES)


def _bias_feature_selector():
    sel = np.zeros((N_BIAS_PARTS * LANES, N_HEADS * HEAD_DIM), np.float32)
    for part in range(N_BIAS_PARTS):
        for h in range(N_HEADS):
            sel[part * LANES + h, (h // 2) * LANES + (h % 2) * N_BIAS_PARTS + part] = 1.0
    return jnp.asarray(sel, BF16)


def kernel(x, positions, a_norm, a_w_in, a_b_f, a_q_gain, a_k_gain, a_w_out, kv_norm, kv_w, kc_pe, vc_pe, kc_w1, kc_w2, vc_w1, vc_w2, kc_gain, ks_gain, kw_gain, b_norm, b_w_in, b_b_gate, b_q_gain, b_w_out, f_norm, f_w_up, f_conv_w, f_conv_b, f_w_down):
    B, T, D = x.shape
    hd = N_HEADS * HEAD_DIM
    G = N_KV_GROUPS
    n_a = a_norm.shape[0]
    n_b = b_norm.shape[0]
    depth = n_a + n_b
    n_slc = T // SLC_LEN
    n_cmp = (T - CMP_LEN) // CMP_STRIDE + 1
    assert T % 1024 == 0 and n_slc <= LANES and hd == 1024 and D == 1024

    half = ROT_DIM // 2
    inv = ROPE_THETA ** (-jnp.arange(half, dtype=F32) * (2.0 / ROT_DIM))
    cos_t, sin_t = _rope_tables(positions, inv)
    end_pos = positions[:, CMP_LEN - 1::CMP_STRIDE]
    end_pos = jnp.pad(end_pos, ((0, 0), (0, T // CMP_STRIDE - n_cmp)))
    cos_c, sin_c = _rope_tables(end_pos, inv)

    cs = jnp.arange(T // CMP_STRIDE) * CMP_STRIDE
    ss = jnp.arange(LANES) * SLC_LEN
    overlap_t = (jnp.maximum(jnp.minimum(cs[None, :] + CMP_LEN, ss[:, None] + SLC_LEN)
                             - jnp.maximum(cs[None, :], ss[:, None]), 0).astype(F32) / CMP_LEN).astype(BF16)

    h = x
    kv = None
    for layer in range(depth):
        if layer < n_a:
            i = layer
            w = jnp.concatenate([a_w_in[i][:, :2 * hd], _pad_cols(a_w_in[i][:, 3 * hd:], LANES)], axis=1).astype(BF16)
            wvt = a_w_in[i][:, 2 * hd:3 * hd].T.astype(BF16)
            bf = _pad_cols(a_b_f[i].reshape(1, -1), LANES)
            q, k, vt, cf = _fox_inproj(h, a_norm[i].reshape(1, D), w, wvt, bf, _lane_vec(a_q_gain[i]),
                                       _lane_vec(a_k_gain[i]), _bias_feature_selector())
            mix, w_out = [_fox_attn(q, k, cf, vt)], a_w_out[i]
        else:
            i = layer - n_a
            kc_all, vct_all, ks, kw, vst, vwt = kv
            q, gates_t = _nsa_inproj(h, b_norm[i].reshape(1, D), b_w_in[i][:, :hd].astype(BF16),
                                     b_w_in[i][:, hd:].T.astype(BF16), b_b_gate[i].reshape(-1, 1),
                                     _lane_vec(b_q_gain[i]), cos_t, sin_t)
            o_c, sel = _nsa_cmp(q, kc_all, vct_all, overlap_t, gates_t, n_slc)
            o_s = _nsa_slc(q, sel, ks, vst, gates_t)
            o_w = _nsa_win(q, kw, vwt, gates_t)
            mix, w_out = [o_c, o_s, o_w], b_w_out[i]
        h = _mix_out_conv_ffn(h, mix, w_out.astype(BF16), f_norm[layer].reshape(1, D), f_w_up[layer].astype(BF16),
                              f_conv_w[layer], f_conv_b[layer].reshape(1, -1), f_w_down[layer].astype(BF16))
        if layer == n_a - 1:
            w6 = kv_w.reshape(D, 6, G, HEAD_DIM)
            raw_cols = w6[:, 0:2].reshape(D, 2 * G * HEAD_DIM)
            wk = w6[:, (2, 4)]
            dup_cols = jnp.concatenate([wk, wk], axis=-1).reshape(D, 2 * G * LANES)
            wkv = jnp.concatenate([raw_cols, dup_cols], axis=1).astype(BF16)
            wvt = w6[:, (3, 5)].reshape(D, 2 * G * HEAD_DIM).T.astype(BF16)
            craw, ks, kw, vst, vwt = _kvproj(h, kv_norm.reshape(1, D), wkv, wvt, _lane_vec(ks_gain),
                                             _lane_vec(kw_gain), cos_t, sin_t)
            r = craw.reshape(B, T, 2, G, HEAD_DIM).transpose(0, 2, 3, 1, 4).reshape(B, 2, G, T // CMP_STRIDE, CMP_STRIDE * HEAD_DIM)
            w1 = jnp.stack([kc_w1, vc_w1]).astype(BF16)
            pe = jnp.stack([kc_pe.reshape(1, -1), vc_pe.reshape(1, -1)])
            pe = jnp.pad(pe, ((0, 0), (0, 7), (0, 0))).astype(BF16)
            w2 = jnp.stack([jnp.concatenate([kc_w2, kc_w2], axis=1), jnp.concatenate([vc_w2, vc_w2], axis=1)]).astype(BF16)
            w2t = jnp.stack([kc_w2.T, vc_w2.T]).astype(BF16)
            kc_all, vct_all = _compress(r, w1, pe, w2, w2t, _lane_vec(kc_gain), cos_c, sin_c, n_cmp)
            kv = (kc_all, vct_all, ks, kw, vst, vwt)
    return h
```

```python
import functools

import numpy as np
import jax
import jax.numpy as jnp
from jax import lax
from jax.experimental import pallas as pl
from jax.experimental.pallas import tpu as pltpu

F32 = jnp.float32
BF16 = jnp.bfloat16

LANES = 128
SUBLANES = 8
MXU_COLS = 256
HEAD_DIM = 64
N_HEADS = 16
N_PAIRS = N_HEADS // 2
N_KV_GROUPS = 2
HEADS_PER_GROUP = N_HEADS // N_KV_GROUPS
ROT_DIM = HEAD_DIM // 4
ROPE_THETA = 500000.0
CMP_LEN = 32
CMP_STRIDE = 16
SLC_LEN = 64
SLC_TOPK = 16
WIN = 512
CONV_W = 3
RMS_EPS = 1e-6
NEG = -1e30
FORCED_SCORE = 1e6
LOG2E = 1.4426950408889634
Q_SCALE = HEAD_DIM ** -0.5 * LOG2E
N_BIAS_PARTS = 3
ONES_ROWS = 16
ACC_ROWS = HEAD_DIM + ONES_ROWS

ROW_TILE = 512
FOX_TQ = 512
FOX_TK = ROW_TILE
SLC_TQ = 256
SLC_TK = ROW_TILE
WIN_TQ = 256
CMP_TQ = 512
CMP_SUB = 128

VMEM_LIMIT = 48 * 1024 * 1024


def _cparams(sem):
    return pltpu.CompilerParams(dimension_semantics=sem, vmem_limit_bytes=VMEM_LIMIT)


def _iota(shape, axis):
    return lax.broadcasted_iota(jnp.int32, shape, axis)


def _row_rms(x, g):
    ms = jnp.mean(x * x, axis=-1, keepdims=True)
    return x * lax.rsqrt(ms + RMS_EPS) * g


def _pair_rms(y, gain):
    lo = _iota(y.shape, 1) < HEAD_DIM
    y2 = y * y
    s_lo = jnp.sum(jnp.where(lo, y2, 0.0), axis=-1, keepdims=True)
    s_hi = jnp.sum(jnp.where(lo, 0.0, y2), axis=-1, keepdims=True)
    ms = jnp.where(lo, s_lo, s_hi) * (1.0 / HEAD_DIM)
    return y * lax.rsqrt(ms + RMS_EPS) * gain


def _pair_rope(y, cos, sin):
    lane = _iota(y.shape, 1) & (HEAD_DIM - 1)
    partner = jnp.where(lane < ROT_DIM // 2,
                        pltpu.roll(y, LANES - ROT_DIM // 2, 1),
                        pltpu.roll(y, ROT_DIM // 2, 1))
    return y * cos + partner * sin


def _dot(a, b):
    return jnp.dot(a, b, preferred_element_type=F32)


def _dot_nt(a, b):
    return lax.dot_general(a, b, (((1,), (1,)), ((), ())), preferred_element_type=F32)


def _column_max(s):
    tk, n = s.shape
    return jnp.max(jnp.max(s.reshape(tk // SUBLANES, SUBLANES, n), axis=0), axis=0, keepdims=True)


def _softmax_update_t(s, vt, m_ref, acc_ref, m_tile=None):
    tk, n = s.shape
    m_old = m_ref[...]
    if m_tile is None:
        m_tile = _column_max(s)
    m_new = jnp.maximum(m_old, m_tile)
    alpha = jnp.exp2(m_old - m_new)
    p = jnp.exp2(s - m_new).astype(BF16)
    vta = jnp.concatenate([vt, jnp.ones((ONES_ROWS, tk), vt.dtype)], axis=0)
    acc_ref[...] = alpha * acc_ref[...] + _dot(vta, p)
    m_ref[...] = m_new


def _pipelined_key_loop(n_full, produce, consume, unroll=4):
    assert unroll % 2 == 0

    def run(base, count):
        for i in range(count):
            produce(base + i + 1, (i + 1) % 2)
            consume(base + i, i % 2, False)

    produce(0, 0)

    def body(j, carry):
        run(unroll * j, unroll)
        return carry

    lax.fori_loop(0, n_full // unroll, body, 0)
    base = (n_full // unroll) * unroll
    rem = n_full - base
    step = unroll // 2
    while step >= 2:
        pl.when((rem & step) != 0)(functools.partial(run, base, step))
        base = base + (rem & step)
        step //= 2

    @pl.when((rem & 1) != 0)
    def _():
        run(base, 1)
        consume(base + 1, 1, True)

    @pl.when((rem & 1) == 0)
    def _():
        consume(base, 0, True)


def _softmax_init(m_ref, acc_ref):
    m_ref[...] = jnp.full_like(m_ref, NEG)
    acc_ref[...] = jnp.zeros_like(acc_ref)


def _softmax_result(acc_ref):
    acc = acc_ref[...]
    return acc[:HEAD_DIM, :] * (1.0 / acc[HEAD_DIM:HEAD_DIM + 1, :])


def _rope_table_kernel(pos_ref, inv_ref, c_ref, s_ref):
    ang = pos_ref[0].astype(F32) * inv_ref[...]
    c_ref[0] = jnp.cos(ang)
    s_ref[0] = jnp.sin(ang)


def _rope_tables(pos, inv):
    B, T = pos.shape
    half = inv.shape[0]
    per_row = LANES // half
    rows = T // per_row
    pos_rep = jnp.repeat(pos.reshape(B, rows, per_row), half, axis=-1)
    inv_row = jnp.tile(inv, per_row).reshape(1, LANES)
    spec = pl.BlockSpec((1, rows, LANES), lambda b: (b, 0, 0))
    cos, sin = pl.pallas_call(
        _rope_table_kernel,
        grid=(B,),
        in_specs=[spec, pl.BlockSpec((1, LANES), lambda b: (0, 0))],
        out_specs=[spec, spec],
        out_shape=[jax.ShapeDtypeStruct((B, rows, LANES), F32)] * 2,
        compiler_params=_cparams(("parallel",)),
        name="rope_tables",
    )(pos_rep, inv_row)
    cos = cos.reshape(B, T, half)
    sin = sin.reshape(B, T, half)
    rest = HEAD_DIM - 2 * half
    cos_head = jnp.concatenate([cos, cos, jnp.ones((B, T, rest), F32)], axis=-1)
    sin_head = jnp.concatenate([-sin, sin, jnp.zeros((B, T, rest), F32)], axis=-1)
    return jnp.tile(cos_head, (1, 1, LANES // HEAD_DIM)), jnp.tile(sin_head, (1, 1, LANES // HEAD_DIM))


def _fox_inproj_kernel(x_ref, g_ref, w_ref, wvt_ref, bf_ref, qg_ref, kg_ref, sel_ref,
                       q_ref, k_ref, vt_ref, cf_ref, carry_sc, *, tm, hd):
    ti = pl.program_id(1)
    xn = _row_rms(x_ref[0], g_ref[...]).astype(BF16)
    for j in range(hd // MXU_COLS):
        for part, (ref, gain, mul) in enumerate(((q_ref, qg_ref, Q_SCALE), (k_ref, kg_ref, 1.0))):
            c0 = part * hd + MXU_COLS * j
            y = _dot(xn, w_ref[:, c0:c0 + MXU_COLS])
            for hh in range(2):
                blk = _pair_rms(y[:, LANES * hh:LANES * (hh + 1)], gain[...]) * mul
                ref[0, :, MXU_COLS * j + LANES * hh:MXU_COLS * j + LANES * (hh + 1)] = blk.astype(BF16)
        yt = _dot_nt(wvt_ref[MXU_COLS * j:MXU_COLS * (j + 1), :], xn)
        for hh in range(2):
            vt_ref[0, 2 * j + hh, 0] = yt[LANES * hh:LANES * (hh + 1), :].astype(BF16)
    z = _dot(xn, w_ref[:, 2 * hd:2 * hd + LANES]) + bf_ref[...]
    lf = jnp.minimum(z, 0.0) - jnp.log1p(jnp.exp(-jnp.abs(z)))
    row = _iota(lf.shape, 0)
    sh = 1
    while sh < tm:
        lf = lf + jnp.where(row >= sh, pltpu.roll(lf, sh, 0), 0.0)
        sh *= 2

    @pl.when(ti == 0)
    def _():
        carry_sc[...] = jnp.zeros_like(carry_sc)

    c = lf + carry_sc[0:1, :]
    carry_sc[...] = jnp.broadcast_to(c[tm - 1:tm, :], carry_sc.shape)
    rest = c * (-LOG2E)
    pieces = []
    for _ in range(N_BIAS_PARTS):
        piece = rest.astype(BF16)
        pieces.append(piece)
        rest = rest - piece.astype(F32)
    cf_ref[0] = _dot(jnp.concatenate(pieces, axis=1), sel_ref[...]).astype(BF16)


def _fox_inproj(x, g, w, wvt, bf, qg, kg, sel, tm=ROW_TILE):
    B, T, D = x.shape
    hd = N_HEADS * HEAD_DIM
    act = pl.BlockSpec((1, tm, hd), lambda b, t: (b, t, 0))
    vec = lambda n: pl.BlockSpec((1, n), lambda b, t: (0, 0))
    full = lambda a: pl.BlockSpec(a.shape, lambda b, t: (0,) * a.ndim)
    return pl.pallas_call(
        functools.partial(_fox_inproj_kernel, tm=tm, hd=hd),
        grid=(B, T // tm),
        in_specs=[pl.BlockSpec((1, tm, D), lambda b, t: (b, t, 0)), vec(D), full(w), full(wvt),
                  vec(LANES), vec(LANES), vec(LANES), full(sel)],
        out_specs=[act, act, pl.BlockSpec((1, N_PAIRS, 1, LANES, tm), lambda b, t: (b, 0, t, 0, 0)), act],
        out_shape=[jax.ShapeDtypeStruct((B, T, hd), BF16), jax.ShapeDtypeStruct((B, T, hd), BF16),
                   jax.ShapeDtypeStruct((B, N_PAIRS, T // tm, LANES, tm), BF16),
                   jax.ShapeDtypeStruct((B, T, hd), BF16)],
        scratch_shapes=[pltpu.VMEM((SUBLANES, LANES), F32)],
        compiler_params=_cparams(("arbitrary", "arbitrary")),
        name="fox_inproj",
    )(x, g, w, wvt, bf, qg, kg, sel)


def _fox_attn_kernel(q_ref, k_ref, cf_ref, vt_ref, o_ref, qa_sc, s_sc, mt_sc, m_sc, acc_sc, *, tq, tk, pairs):
    qi = pl.program_id(2)
    lane = _iota((tq, LANES), 1)
    lo = lane < HEAD_DIM
    for hh in range(2 * pairs):
        pp, half = divmod(hh, 2)
        q = q_ref[0, :, LANES * pp:LANES * (pp + 1)]
        zero = jnp.zeros_like(q)
        qa_sc[hh, :, 0:LANES] = jnp.where(lo, q, zero) if half == 0 else jnp.where(lo, zero, q)
        feat = jnp.where(lane < N_BIAS_PARTS * (half + 1), 1.0, 0.0)
        qa_sc[hh, :, LANES:2 * LANES] = jnp.where(lane >= N_BIAS_PARTS * half, feat, 0.0).astype(BF16)
        _softmax_init(m_sc.at[hh], acc_sc.at[hh])

    def produce(ki, slot):
        start = pl.multiple_of(ki * tk, tk)
        for pp in range(pairs):
            cols = slice(LANES * pp, LANES * (pp + 1))
            ka = jnp.concatenate([k_ref[0, pl.ds(start, tk), cols], cf_ref[0, pl.ds(start, tk), cols]], axis=1)
            for hh in (2 * pp, 2 * pp + 1):
                s = _dot_nt(ka, qa_sc[hh])
                s_sc[slot, hh] = s
                mt_sc[slot, hh] = _column_max(s)

    def consume(ki, slot, diag):
        for hh in range(2 * pairs):
            pp, half = divmod(hh, 2)
            s = s_sc[slot, hh]
            m_tile = mt_sc[slot, hh]
            if diag:
                s = jnp.where(_iota(s.shape, 0) <= _iota(s.shape, 1), s, NEG)
                m_tile = None
            vt = vt_ref[0, pp, ki, HEAD_DIM * half:HEAD_DIM * (half + 1), :]
            _softmax_update_t(s, vt, m_sc.at[hh], acc_sc.at[hh], m_tile)

    _pipelined_key_loop(qi, produce, consume)
    for pp in range(pairs):
        o_t = jnp.concatenate([_softmax_result(acc_sc.at[2 * pp + half]) for half in range(2)], axis=0)
        o_ref[0, :, LANES * pp:LANES * (pp + 1)] = o_t.T.astype(BF16)


def _fox_attn(q, k, cf, vt, tq=FOX_TQ, tk=FOX_TK, pairs=2):
    B, T, hd = q.shape
    assert tq == tk
    nk = T // tk
    heads = 2 * pairs
    width = pairs * LANES
    seq = pl.BlockSpec((1, T, width), lambda b, p, qi: (b, 0, p))
    return pl.pallas_call(
        functools.partial(_fox_attn_kernel, tq=tq, tk=tk, pairs=pairs),
        grid=(B, N_PAIRS // pairs, T // tq),
        in_specs=[pl.BlockSpec((1, tq, width), lambda b, p, qi: (b, qi, p)), seq, seq,
                  pl.BlockSpec((1, pairs, nk, LANES, tk), lambda b, p, qi: (b, p, 0, 0, 0))],
        out_specs=pl.BlockSpec((1, tq, width), lambda b, p, qi: (b, qi, p)),
        out_shape=jax.ShapeDtypeStruct((B, T, hd), BF16),
        scratch_shapes=[pltpu.VMEM((heads, tq, 2 * LANES), BF16), pltpu.VMEM((2, heads, tk, tq), F32),
                        pltpu.VMEM((2, heads, 1, tq), F32), pltpu.VMEM((heads, 1, tq), F32),
                        pltpu.VMEM((heads, ACC_ROWS, tq), F32)],
        compiler_params=_cparams(("parallel", "parallel", "arbitrary")),
        name="fox_attn",
    )(q, k, cf, vt)


def _ffn_kernel(*refs, n_mix, tt, tf, d_ff):
    h_ref = refs[0]
    mix_refs = refs[1:1 + n_mix]
    wo_ref, g_ref, wup_ref, cw_ref, cb_ref, wd_ref, out_ref, a_sc, carry_sc = refs[1 + n_mix:]
    ti = pl.program_id(1)

    @pl.when(ti == 0)
    def _():
        carry_sc[...] = jnp.zeros_like(carry_sc)

    o = mix_refs[0][0]
    if n_mix > 1:
        o = o.astype(F32)
        for ref in mix_refs[1:]:
            o = o + ref[0].astype(F32)
        o = o.astype(BF16)
    x = h_ref[0] + _dot(o, wo_ref[...])
    xn = _row_rms(x, g_ref[...]).astype(BF16)
    row8 = _iota((SUBLANES, tf), 0)

    def conv(u, c0):
        prev8 = carry_sc[:, c0:c0 + tf]
        um1 = pltpu.roll(u, 1, 0)
        um2 = pltpu.roll(u, 2, 0)
        top1 = jnp.where(row8 == 0, prev8[7:8, :], um1[0:SUBLANES, :])
        top2 = jnp.where(row8 == 0, prev8[6:7, :], jnp.where(row8 == 1, prev8[7:8, :], um2[0:SUBLANES, :]))
        um1 = jnp.concatenate([top1, um1[SUBLANES:, :]], axis=0)
        um2 = jnp.concatenate([top2, um2[SUBLANES:, :]], axis=0)
        carry_sc[:, c0:c0 + tf] = u[tt - SUBLANES:, :]
        cw = cw_ref[:, c0:c0 + tf]
        return cb_ref[:, c0:c0 + tf] + cw[0:1, :] * um2 + cw[1:2, :] * um1 + cw[2:3, :] * u

    for f in range(d_ff // tf):
        g0 = f * tf
        cg = conv(_dot(xn, wup_ref[:, g0:g0 + tf]), g0)
        cv = conv(_dot(xn, wup_ref[:, d_ff + g0:d_ff + g0 + tf]), d_ff + g0)
        a_sc[:, g0:g0 + tf] = (cg * jax.nn.sigmoid(cg) * cv).astype(BF16)
    out_ref[0] = x + _dot(a_sc[...], wd_ref[...])


def _mix_out_conv_ffn(h, mix, w_out, g, w_up, conv_w, conv_b, w_down, tt=ROW_TILE, tf=MXU_COLS):
    B, T, D = h.shape
    d_ff = w_down.shape[0]
    act = lambda c: pl.BlockSpec((1, tt, c), lambda b, t: (b, t, 0))
    resident = lambda a: pl.BlockSpec(a.shape, lambda b, t: (0,) * a.ndim, pipeline_mode=pl.Buffered(1))
    return pl.pallas_call(
        functools.partial(_ffn_kernel, n_mix=len(mix), tt=tt, tf=tf, d_ff=d_ff),
        grid=(B, T // tt),
        in_specs=[act(D)] + [act(m.shape[-1]) for m in mix]
                 + [resident(w_out), resident(g), resident(w_up), resident(conv_w), resident(conv_b), resident(w_down)],
        out_specs=act(D),
        out_shape=jax.ShapeDtypeStruct((B, T, D), F32),
        scratch_shapes=[pltpu.VMEM((tt, d_ff), BF16), pltpu.VMEM((SUBLANES, 2 * d_ff), F32)],
        compiler_params=_cparams(("arbitrary", "arbitrary")),
        name="conv_ffn",
    )(h, *mix, w_out, g, w_up, conv_w, conv_b, w_down)


def _kvproj_kernel(h_ref, g_ref, w_ref, wvt_ref, ksg_ref, kwg_ref, c_ref, s_ref,
                   craw_ref, ks_ref, kw_ref, vst_ref, vwt_ref, *, tm):
    xn = _row_rms(h_ref[0], g_ref[...]).astype(BF16)
    cos = c_ref[0]
    sin = s_ref[0]
    raw_cols = craw_ref.shape[-1]
    craw_ref[0] = _dot(xn, w_ref[:, 0:raw_cols])
    for idx, (ref, gain) in enumerate(((ks_ref, ksg_ref), (kw_ref, kwg_ref))):
        c0 = raw_cols + N_KV_GROUPS * LANES * idx
        y = _dot(xn, w_ref[:, c0:c0 + N_KV_GROUPS * LANES])
        for grp in range(N_KV_GROUPS):
            blk = _pair_rope(_pair_rms(y[:, LANES * grp:LANES * (grp + 1)], gain[...]), cos, sin)
            ref[0, grp] = blk.astype(BF16)
    yt = _dot_nt(wvt_ref[...], xn).astype(BF16)
    for grp in range(N_KV_GROUPS):
        vst_ref[0, grp, 0] = yt[HEAD_DIM * grp:HEAD_DIM * (grp + 1), :]
        r0 = HEAD_DIM * (N_KV_GROUPS + grp)
        for c in range(tm // WIN_TQ):
            vwt_ref[0, grp, c] = yt[r0:r0 + HEAD_DIM, WIN_TQ * c:WIN_TQ * (c + 1)]


def _kvproj(h, g, w, wvt, ksg, kwg, cos, sin, tm=ROW_TILE):
    B, T, D = h.shape
    G = N_KV_GROUPS
    vec = lambda n: pl.BlockSpec((1, n), lambda b, t: (0, 0))
    full = lambda a: pl.BlockSpec(a.shape, lambda b, t: (0,) * a.ndim)
    tab = pl.BlockSpec((1, tm, LANES), lambda b, t: (b, t, 0))
    dup = pl.BlockSpec((1, G, tm, LANES), lambda b, t: (b, 0, t, 0))
    dup_shape = jax.ShapeDtypeStruct((B, G, T, LANES), BF16)
    nw = tm // WIN_TQ
    raw_cols = 2 * G * HEAD_DIM
    return pl.pallas_call(
        functools.partial(_kvproj_kernel, tm=tm),
        grid=(B, T // tm),
        in_specs=[pl.BlockSpec((1, tm, D), lambda b, t: (b, t, 0)), vec(D), full(w), full(wvt),
                  vec(LANES), vec(LANES), tab, tab],
        out_specs=[pl.BlockSpec((1, tm, raw_cols), lambda b, t: (b, t, 0)), dup, dup,
                   pl.BlockSpec((1, G, 1, HEAD_DIM, tm), lambda b, t: (b, 0, t, 0, 0)),
                   pl.BlockSpec((1, G, nw, HEAD_DIM, WIN_TQ), lambda b, t: (b, 0, t, 0, 0))],
        out_shape=[jax.ShapeDtypeStruct((B, T, raw_cols), F32), dup_shape, dup_shape,
                   jax.ShapeDtypeStruct((B, G, T // tm, HEAD_DIM, tm), BF16),
                   jax.ShapeDtypeStruct((B, G, T // WIN_TQ, HEAD_DIM, WIN_TQ), BF16)],
        compiler_params=_cparams(("parallel", "parallel")),
        name="kvproj",
    )(h, g, w, wvt, ksg, kwg, cos, sin)


def _compress_kernel(r_ref, w1_ref, pe_ref, w2_ref, w2t_ref, gain_ref, c_ref, s_ref, kc_ref, vct_ref, *, n_cmp):
    r = r_ref[0, 0, 0].astype(BF16)
    half = r.shape[1]
    a = _dot(r, w1_ref[0, :half, :])
    b = _dot(r, w1_ref[0, half:, :])
    peb = _dot(pe_ref[0], w1_ref[0])[0:1, :]
    rows = r.shape[0]
    hid = a + pltpu.roll(b, rows - 1, 0) + peb
    act = jax.nn.gelu(hid).astype(BF16)
    y = _dot(act, w2_ref[0])
    yk = _pair_rope(_pair_rms(y, gain_ref[...]), c_ref[0], s_ref[0])
    kc_ref[0, 0, 0] = jnp.where(_iota(y.shape, 0) < n_cmp, yk, 0.0).astype(BF16)
    yt = _dot_nt(w2t_ref[0], act)
    vct_ref[0, 0, 0] = jnp.where(_iota(yt.shape, 1) < n_cmp, yt, 0.0).astype(BF16)


def _compress(r, w1, pe, w2, w2t, gain, cos, sin, n_cmp):
    B, _, G, rows, width = r.shape
    per_kv = lambda a: pl.BlockSpec((1,) + a.shape[1:], lambda b, kv, g: (kv,) + (0,) * (a.ndim - 1))
    return pl.pallas_call(
        functools.partial(_compress_kernel, n_cmp=n_cmp),
        grid=(B, 2, G),
        in_specs=[pl.BlockSpec((1, 1, 1, rows, width), lambda b, kv, g: (b, kv, g, 0, 0)),
                  per_kv(w1), per_kv(pe), per_kv(w2), per_kv(w2t),
                  pl.BlockSpec((1, LANES), lambda b, kv, g: (0, 0)),
                  pl.BlockSpec((1, rows, LANES), lambda b, kv, g: (b, 0, 0)),
                  pl.BlockSpec((1, rows, LANES), lambda b, kv, g: (b, 0, 0))],
        out_specs=[pl.BlockSpec((1, 1, 1, rows, LANES), lambda b, kv, g: (b, kv, g, 0, 0)),
                   pl.BlockSpec((1, 1, 1, HEAD_DIM, rows), lambda b, kv, g: (b, kv, g, 0, 0))],
        out_shape=[jax.ShapeDtypeStruct((B, 2, G, rows, LANES), BF16),
                   jax.ShapeDtypeStruct((B, 2, G, HEAD_DIM, rows), BF16)],
        compiler_params=_cparams(("parallel", "parallel", "parallel")),
        name="compress",
    )(r, w1, pe, w2, w2t, gain, cos, sin)


def _nsa_inproj_kernel(h_ref, g_ref, w_ref, wgt_ref, bgt_ref, qg_ref, c_ref, s_ref, q_ref, gate_ref, *, hd):
    xn = _row_rms(h_ref[0], g_ref[...]).astype(BF16)
    cos = c_ref[0]
    sin = s_ref[0]
    for j in range(hd // MXU_COLS):
        y = _dot(xn, w_ref[:, MXU_COLS * j:MXU_COLS * (j + 1)])
        for hh in range(2):
            blk = _pair_rope(_pair_rms(y[:, LANES * hh:LANES * (hh + 1)], qg_ref[...]), cos, sin) * Q_SCALE
            q_ref[0, :, MXU_COLS * j + LANES * hh:MXU_COLS * j + LANES * (hh + 1)] = blk.astype(BF16)
    gate_ref[0] = jax.nn.sigmoid(_dot_nt(wgt_ref[...], xn) + bgt_ref[...])


def _nsa_inproj(h, g, w, wgt, bgt, qg, cos, sin, tm=ROW_TILE):
    B, T, D = h.shape
    hd = N_HEADS * HEAD_DIM
    vec = lambda n: pl.BlockSpec((1, n), lambda b, t: (0, 0))
    full = lambda a: pl.BlockSpec(a.shape, lambda b, t: (0,) * a.ndim)
    tab = pl.BlockSpec((1, tm, LANES), lambda b, t: (b, t, 0))
    return pl.pallas_call(
        functools.partial(_nsa_inproj_kernel, hd=hd),
        grid=(B, T // tm),
        in_specs=[pl.BlockSpec((1, tm, D), lambda b, t: (b, t, 0)), vec(D), full(w), full(wgt), full(bgt),
                  vec(LANES), tab, tab],
        out_specs=[pl.BlockSpec((1, tm, hd), lambda b, t: (b, t, 0)),
                   pl.BlockSpec((1, wgt.shape[0], tm), lambda b, t: (b, 0, t))],
        out_shape=[jax.ShapeDtypeStruct((B, T, hd), BF16), jax.ShapeDtypeStruct((B, wgt.shape[0], T), F32)],
        compiler_params=_cparams(("parallel", "parallel")),
        name="nsa_inproj",
    )(h, g, w, wgt, bgt, qg, cos, sin)


def _stack_heads(q, dst_ref, tq):
    lo = _iota((tq, LANES), 1) < HEAD_DIM
    zero = jnp.zeros((tq, LANES), q.dtype)
    for j in range(HEADS_PER_GROUP // 2):
        x = q[:, LANES * j:LANES * (j + 1)]
        dst_ref[(2 * j) * tq:(2 * j + 1) * tq, 0:LANES] = jnp.where(lo, x, zero)
        dst_ref[(2 * j + 1) * tq:(2 * j + 2) * tq, 0:LANES] = jnp.where(lo, zero, x)


def _unstack_heads_t(o_t, gate, o_ref, tq, row0=0):
    for j in range(HEADS_PER_GROUP // 2):
        halves = [o_t[:, h * tq:(h + 1) * tq] * gate[h:h + 1, :] for h in (2 * j, 2 * j + 1)]
        o_ref[0, row0:row0 + tq, LANES * j:LANES * (j + 1)] = jnp.concatenate(halves, axis=0).T.astype(o_ref.dtype)


def _cmp_query_mask_features(sub):
    j = np.arange(sub)
    v = (j // CMP_STRIDE) + ((j % CMP_STRIDE) == CMP_STRIDE - 1)
    feat = np.zeros((sub, LANES), np.float32)
    feat[j, v] = NEG
    return jnp.asarray(feat, BF16)


def _nsa_cmp_kernel(q_ref, kc_ref, vct_ref, ovt_ref, gate_ref, qf_ref, oc_ref, sel_ref, qs_sc, *, tq, sub, n_slc, top_k, n_levels):
    qi = pl.program_id(2)
    n_sub = tq // sub
    n_feats = sub // CMP_STRIDE + 1
    qf = qf_ref[...]
    for u in range(n_sub):
        _stack_heads(q_ref[0, u * sub:(u + 1) * sub, :], qs_sc.at[u], sub)
        for h in range(HEADS_PER_GROUP):
            qs_sc[u, h * sub:(h + 1) * sub, LANES:2 * LANES] = qf
    n_rows = kc_ref.shape[3]
    level_rows = n_rows // n_levels
    n_chunks = 2
    cw = HEADS_PER_GROUP * sub // n_chunks
    n_vis = (qi * tq + tq - 1 - (CMP_LEN - 1)) // CMP_STRIDE + 1

    def attend(rows, u):
        base = (qi * tq + u * sub) // CMP_STRIDE - 2
        n_minus_v = _iota((rows, LANES), 0) - _iota((rows, LANES), 1)
        kf = jnp.where(_iota((rows, LANES), 1) < n_feats, jnp.where(n_minus_v > base, 1.0, 0.0), 0.0)
        ka = jnp.concatenate([kc_ref[0, 0, 0, :rows, :], kf.astype(BF16)], axis=1)
        vct = vct_ref[0, 0, 0, :, :rows]
        scores = [_dot_nt(ka, qs_sc[u, c * cw:(c + 1) * cw, :]) for c in range(n_chunks)]
        outs = []
        p_sum = None
        for s in scores:
            mx = jnp.max(s, axis=0, keepdims=True)
            e = jnp.exp2(s - mx)
            inv = jnp.where(mx > 0.5 * NEG, 1.0 / jnp.maximum(jnp.sum(e, axis=0, keepdims=True), 1.0), 0.0)
            p = e * inv
            outs.append(_dot(vct, p.astype(BF16)))
            for h in range(cw // sub):
                term = p[:, h * sub:(h + 1) * sub]
                p_sum = term if p_sum is None else p_sum + term
        _unstack_heads_t(jnp.concatenate(outs, axis=1), gate_ref[0, :, u * sub:(u + 1) * sub], oc_ref, sub, u * sub)
        p_hi = p_sum.astype(BF16)
        p_lo = (p_sum - p_hi.astype(F32)).astype(BF16)
        return _dot(ovt_ref[:, :rows], p_hi) + _dot(ovt_ref[:, :rows], p_lo)

    def select(imp, u, n_blk):
        imp = imp[:n_blk, :]
        jb = _iota(imp.shape, 0)
        qp = qi * tq + u * sub + _iota(imp.shape, 1)
        cur = qp // SLC_LEN
        forced = (jb == 0) | (jb == cur) | (jb == cur - 1)
        score = jnp.where(jb * SLC_LEN <= qp, jnp.where(forced, FORCED_SCORE, imp), NEG)
        live = jb < n_slc
        score = jnp.where(live, score, -jnp.inf)
        for _ in range(top_k):
            mx = jnp.max(score, axis=0, keepdims=True)
            idx = jnp.min(jnp.where(score == mx, jb, LANES), axis=0, keepdims=True)
            score = jnp.where(jb == idx, -jnp.inf, score)
        bias = jnp.where(live, jnp.where(score == -jnp.inf, 0.0, NEG), NEG)
        if n_blk < LANES:
            bias = jnp.concatenate([bias, jnp.full((LANES - n_blk, sub), NEG, F32)], axis=0)
        sel_ref[0, 0, u * sub:(u + 1) * sub, :] = bias.T.astype(BF16)

    def tile(rows):
        n_blk = rows * CMP_STRIDE // SLC_LEN
        n_blk = n_blk if (n_blk >= top_k and n_blk % SUBLANES == 0) else LANES
        imps = [attend(rows, u) for u in range(n_sub)]
        for u in range(n_sub):
            select(imps[u], u, n_blk)

    for level in range(n_levels):
        lo_rows, hi_rows = level * level_rows, (level + 1) * level_rows
        cond = (n_vis <= hi_rows) if level == 0 else ((n_vis > lo_rows) & (n_vis <= hi_rows))
        pl.when(cond)(functools.partial(tile, hi_rows))


def _gate_spec(branch, tq):
    return pl.BlockSpec((1, HEADS_PER_GROUP, tq), lambda b, g, qi: (b, branch * N_KV_GROUPS + g, qi))


def _nsa_cmp(q, kc_all, vct_all, overlap_t, gates_t, n_slc, tq=CMP_TQ):
    B, T, hd = q.shape
    G = N_KV_GROUPS
    rows = kc_all.shape[3]
    gw = hd // G
    top_k = min(SLC_TOPK, n_slc)
    assert tq % CMP_SUB == 0 and CMP_SUB % SLC_LEN == 0 and CMP_SUB % LANES == 0
    n_levels = 4 if rows % (4 * LANES) == 0 else 1
    return pl.pallas_call(
        functools.partial(_nsa_cmp_kernel, tq=tq, sub=CMP_SUB, n_slc=n_slc, top_k=top_k, n_levels=n_levels),
        grid=(B, G, T // tq),
        in_specs=[pl.BlockSpec((1, tq, gw), lambda b, g, qi: (b, qi, g)),
                  pl.BlockSpec((1, 1, 1, rows, LANES), lambda b, g, qi: (b, 0, g, 0, 0)),
                  pl.BlockSpec((1, 1, 1, HEAD_DIM, rows), lambda b, g, qi: (b, 1, g, 0, 0)),
                  pl.BlockSpec(overlap_t.shape, lambda b, g, qi: (0, 0)), _gate_spec(0, tq),
                  pl.BlockSpec((CMP_SUB, LANES), lambda b, g, qi: (0, 0))],
        out_specs=[pl.BlockSpec((1, tq, gw), lambda b, g, qi: (b, qi, g)),
                   pl.BlockSpec((1, 1, tq, LANES), lambda b, g, qi: (b, g, qi, 0))],
        out_shape=[jax.ShapeDtypeStruct((B, T, hd), BF16), jax.ShapeDtypeStruct((B, G, T, LANES), BF16)],
        scratch_shapes=[pltpu.VMEM((tq // CMP_SUB, HEADS_PER_GROUP * CMP_SUB, 2 * LANES), BF16)],
        compiler_params=_cparams(("parallel", "parallel", "parallel")),
        name="nsa_cmp",
    )(q, kc_all, vct_all, overlap_t, gates_t, _cmp_query_mask_features(CMP_SUB))


def _nsa_slc_kernel(q_ref, sel_ref, k_ref, oh_ref, vt_ref, gate_ref, o_ref, qa_sc, s_sc, mt_sc, m_sc, acc_sc, *, tq, tk, n_chunks):
    qi = pl.program_id(2)
    last = (qi * tq) // tk
    _stack_heads(q_ref[0], qa_sc, tq)
    sel = sel_ref[0, 0]
    for h in range(HEADS_PER_GROUP):
        qa_sc[h * tq:(h + 1) * tq, LANES:2 * LANES] = sel
    _softmax_init(m_sc, acc_sc)
    cw = HEADS_PER_GROUP * tq // n_chunks

    def produce(ki, slot):
        start = pl.multiple_of(ki * tk, tk)
        ka = jnp.concatenate([k_ref[0, 0, pl.ds(start, tk), :], oh_ref[pl.ds(start, tk), :]], axis=1)
        for c in range(n_chunks):
            s = _dot_nt(ka, qa_sc[c * cw:(c + 1) * cw, :])
            s_sc[slot, :, c * cw:(c + 1) * cw] = s
            mt_sc[slot, :, c * cw:(c + 1) * cw] = _column_max(s)

    def consume(ki, slot, diag):
        vt = vt_ref[0, 0, ki]
        for c in range(n_chunks):
            cols = slice(c * cw, (c + 1) * cw)
            s = s_sc[slot, :, cols]
            m_tile = mt_sc[slot, :, cols]
            if diag:
                kpos = ki * tk + _iota(s.shape, 0)
                qpos = qi * tq + (_iota(s.shape, 1) & (tq - 1))
                s = jnp.where(kpos <= qpos, s, NEG)
                m_tile = None
            _softmax_update_t(s, vt, m_sc.at[:, cols], acc_sc.at[:, cols], m_tile)

    _pipelined_key_loop(last, produce, consume)
    _unstack_heads_t(_softmax_result(acc_sc), gate_ref[0], o_ref, tq)


def _nsa_slc(q, sel, ks, vst, gates_t, tq=SLC_TQ, tk=SLC_TK):
    B, T, hd = q.shape
    G = N_KV_GROUPS
    gw = hd // G
    cols = HEADS_PER_GROUP * tq
    assert tk % tq == 0
    onehot = (jnp.arange(T)[:, None] // SLC_LEN == jnp.arange(LANES)[None, :]).astype(BF16)
    return pl.pallas_call(
        functools.partial(_nsa_slc_kernel, tq=tq, tk=tk, n_chunks=4),
        grid=(B, G, T // tq),
        in_specs=[pl.BlockSpec((1, tq, gw), lambda b, g, qi: (b, qi, g)),
                  pl.BlockSpec((1, 1, tq, LANES), lambda b, g, qi: (b, g, qi, 0)),
                  pl.BlockSpec((1, 1, T, LANES), lambda b, g, qi: (b, g, 0, 0)),
                  pl.BlockSpec((T, LANES), lambda b, g, qi: (0, 0)),
                  pl.BlockSpec((1, 1, T // tk, HEAD_DIM, tk), lambda b, g, qi: (b, g, 0, 0, 0)), _gate_spec(1, tq)],
        out_specs=pl.BlockSpec((1, tq, gw), lambda b, g, qi: (b, qi, g)),
        out_shape=jax.ShapeDtypeStruct((B, T, hd), BF16),
        scratch_shapes=[pltpu.VMEM((cols, 2 * LANES), BF16), pltpu.VMEM((2, tk, cols), F32),
                        pltpu.VMEM((2, 1, cols), F32), pltpu.VMEM((1, cols), F32),
                        pltpu.VMEM((ACC_ROWS, cols), F32)],
        compiler_params=_cparams(("parallel", "parallel", "arbitrary")),
        name="nsa_slc",
    )(q, sel, ks, onehot, vst, gates_t)


def _nsa_win_kernel(q_ref, k_ref, vt_ref, gate_ref, o_ref, qs_sc, s_sc, m_sc, acc_sc, *, tq, n_back, n_chunks):
    qi = pl.program_id(2)
    _stack_heads(q_ref[0], qs_sc, tq)
    _softmax_init(m_sc, acc_sc)
    cw = HEADS_PER_GROUP * tq // n_chunks

    def produce(ti, slot):
        start = pl.multiple_of(ti * tq, tq)
        k = k_ref[0, 0, pl.ds(start, tq), :]
        for c in range(n_chunks):
            s_sc[slot, :, c * cw:(c + 1) * cw] = _dot_nt(k, qs_sc[c * cw:(c + 1) * cw, :])

    def consume(ti, slot, kind):
        vt = vt_ref[0, 0, ti]
        for c in range(n_chunks):
            cols = slice(c * cw, (c + 1) * cw)
            s = s_sc[slot, :, cols]
            r = _iota(s.shape, 0)
            q_in_tile = _iota(s.shape, 1) & (tq - 1)
            if kind == "oldest":
                s = jnp.where(r > q_in_tile, s, NEG)
            elif kind == "diag":
                s = jnp.where(r <= q_in_tile, s, NEG)
            _softmax_update_t(s, vt, m_sc.at[:, cols], acc_sc.at[:, cols])

    def kind_of(back):
        return "oldest" if back == n_back else ("diag" if back == 0 else "full")

    for first in range(n_back + 1):
        cond = (qi >= n_back) if first == n_back else (qi == first)

        @pl.when(cond)
        def _(first=first):
            backs = list(range(first, -1, -1))
            produce(qi - backs[0], 0)
            for n, back in enumerate(backs):
                if n + 1 < len(backs):
                    produce(qi - backs[n + 1], (n + 1) % 2)
                consume(qi - back, n % 2, kind_of(back))

    _unstack_heads_t(_softmax_result(acc_sc), gate_ref[0], o_ref, tq)


def _nsa_win(q, kw, vwt, gates_t, tq=WIN_TQ):
    B, T, hd = q.shape
    G = N_KV_GROUPS
    gw = hd // G
    cols = HEADS_PER_GROUP * tq
    return pl.pallas_call(
        functools.partial(_nsa_win_kernel, tq=tq, n_back=WIN // tq, n_chunks=4),
        grid=(B, G, T // tq),
        in_specs=[pl.BlockSpec((1, tq, gw), lambda b, g, qi: (b, qi, g)),
                  pl.BlockSpec((1, 1, T, LANES), lambda b, g, qi: (b, g, 0, 0)),
                  pl.BlockSpec((1, 1, T // tq, HEAD_DIM, tq), lambda b, g, qi: (b, g, 0, 0, 0)), _gate_spec(2, tq)],
        out_specs=pl.BlockSpec((1, tq, gw), lambda b, g, qi: (b, qi, g)),
        out_shape=jax.ShapeDtypeStruct((B, T, hd), BF16),
        scratch_shapes=[pltpu.VMEM((cols, LANES), BF16), pltpu.VMEM((2, tq, cols), F32), pltpu.VMEM((1, cols), F32),
                        pltpu.VMEM((ACC_ROWS, cols), F32)],
        compiler_params=_cparams(("parallel", "parallel", "arbitrary")),
        name="nsa_win",
    )(q, kw, vwt, gates_t)


def _pad_cols(a, n):
    return jnp.pad(a, ((0, 0), (0, n - a.shape[1])))


def _lane_vec(v):
    return jnp.tile(v.astype(F32), 2).reshape(1, LANES)


def _bias_feature_selector():
    sel = np.zeros((N_BIAS_PARTS * LANES, N_HEADS * HEAD_DIM), np.float32)
    for part in range(N_BIAS_PARTS):
        for h in range(N_HEADS):
            sel[part * LANES + h, (h // 2) * LANES + (h % 2) * N_BIAS_PARTS + part] = 1.0
    return jnp.asarray(sel, BF16)


def kernel(x, positions, a_norm, a_w_in, a_b_f, a_q_gain, a_k_gain, a_w_out, kv_norm, kv_w, kc_pe, vc_pe, kc_w1, kc_w2, vc_w1, vc_w2, kc_gain, ks_gain, kw_gain, b_norm, b_w_in, b_b_gate, b_q_gain, b_w_out, f_norm, f_w_up, f_conv_w, f_conv_b, f_w_down):
    B, T, D = x.shape
    hd = N_HEADS * HEAD_DIM
    G = N_KV_GROUPS
    n_a = a_norm.shape[0]
    n_b = b_norm.shape[0]
    depth = n_a + n_b
    n_slc = T // SLC_LEN
    n_cmp = (T - CMP_LEN) // CMP_STRIDE + 1
    assert T % 1024 == 0 and n_slc <= LANES and hd == 1024 and D == 1024

    half = ROT_DIM // 2
    inv = ROPE_THETA ** (-jnp.arange(half, dtype=F32) * (2.0 / ROT_DIM))
    cos_t, sin_t = _rope_tables(positions, inv)
    end_pos = positions[:, CMP_LEN - 1::CMP_STRIDE]
    end_pos = jnp.pad(end_pos, ((0, 0), (0, T // CMP_STRIDE - n_cmp)))
    cos_c, sin_c = _rope_tables(end_pos, inv)

    cs = jnp.arange(T // CMP_STRIDE) * CMP_STRIDE
    ss = jnp.arange(LANES) * SLC_LEN
    overlap_t = (jnp.maximum(jnp.minimum(cs[None, :] + CMP_LEN, ss[:, None] + SLC_LEN)
                             - jnp.maximum(cs[None, :], ss[:, None]), 0).astype(F32) / CMP_LEN).astype(BF16)

    h = x
    kv = None
    for layer in range(depth):
        if layer < n_a:
            i = layer
            w = jnp.concatenate([a_w_in[i][:, :2 * hd], _pad_cols(a_w_in[i][:, 3 * hd:], LANES)], axis=1).astype(BF16)
            wvt = a_w_in[i][:, 2 * hd:3 * hd].T.astype(BF16)
            bf = _pad_cols(a_b_f[i].reshape(1, -1), LANES)
            q, k, vt, cf = _fox_inproj(h, a_norm[i].reshape(1, D), w, wvt, bf, _lane_vec(a_q_gain[i]),
                                       _lane_vec(a_k_gain[i]), _bias_feature_selector())
            mix, w_out = [_fox_attn(q, k, cf, vt)], a_w_out[i]
        else:
            i = layer - n_a
            kc_all, vct_all, ks, kw, vst, vwt = kv
            q, gates_t = _nsa_inproj(h, b_norm[i].reshape(1, D), b_w_in[i][:, :hd].astype(BF16),
                                     b_w_in[i][:, hd:].T.astype(BF16), b_b_gate[i].reshape(-1, 1),
                                     _lane_vec(b_q_gain[i]), cos_t, sin_t)
            o_c, sel = _nsa_cmp(q, kc_all, vct_all, overlap_t, gates_t, n_slc)
            o_s = _nsa_slc(q, sel, ks, vst, gates_t)
            o_w = _nsa_win(q, kw, vwt, gates_t)
            mix, w_out = [o_c, o_s, o_w], b_w_out[i]
        h = _mix_out_conv_ffn(h, mix, w_out.astype(BF16), f_norm[layer].reshape(1, D), f_w_up[layer].astype(BF16),
                              f_conv_w[layer], f_conv_b[layer].reshape(1, -1), f_w_down[layer].astype(BF16))
        if layer == n_a - 1:
            w6 = kv_w.reshape(D, 6, G, HEAD_DIM)
            raw_cols = w6[:, 0:2].reshape(D, 2 * G * HEAD_DIM)
            wk = w6[:, (2, 4)]
            dup_cols = jnp.concatenate([wk, wk], axis=-1).reshape(D, 2 * G * LANES)
            wkv = jnp.concatenate([raw_cols, dup_cols], axis=1).astype(BF16)
            wvt = w6[:, (3, 5)].reshape(D, 2 * G * HEAD_DIM).T.astype(BF16)
            craw, ks, kw, vst, vwt = _kvproj(h, kv_norm.reshape(1, D), wkv, wvt, _lane_vec(ks_gain),
                                             _lane_vec(kw_gain), cos_t, sin_t)
            r = craw.reshape(B, T, 2, G, HEAD_DIM).transpose(0, 2, 3, 1, 4).reshape(B, 2, G, T // CMP_STRIDE, CMP_STRIDE * HEAD_DIM)
            w1 = jnp.stack([kc_w1, vc_w1]).astype(BF16)
            pe = jnp.stack([kc_pe.reshape(1, -1), vc_pe.reshape(1, -1)])
            pe = jnp.pad(pe, ((0, 0), (0, 7), (0, 0))).astype(BF16)
            w2 = jnp.stack([jnp.concatenate([kc_w2, kc_w2], axis=1), jnp.concatenate([vc_w2, vc_w2], axis=1)]).astype(BF16)
            w2t = jnp.stack([kc_w2.T, vc_w2.T]).astype(BF16)
            kc_all, vct_all = _compress(r, w1, pe, w2, w2t, _lane_vec(kc_gain), cos_c, sin_c, n_cmp)
            kv = (kc_all, vct_all, ks, kw, vst, vwt)
    return h
```

```python
import functools

import numpy as np
import jax
import jax.numpy as jnp
from jax import lax
from jax.experimental import pallas as pl
from jax.experimental.pallas import tpu as pltpu

F32 = jnp.float32
BF16 = jnp.bfloat16

LANES = 128
SUBLANES = 8
MXU_COLS = 256
HEAD_DIM = 64
N_HEADS = 16
N_PAIRS = N_HEADS // 2
N_KV_GROUPS = 2
HEADS_PER_GROUP = N_HEADS // N_KV_GROUPS
ROT_DIM = HEAD_DIM // 4
ROPE_THETA = 500000.0
CMP_LEN = 32
CMP_STRIDE = 16
SLC_LEN = 64
SLC_TOPK = 16
WIN = 512
CONV_W = 3
RMS_EPS = 1e-6
NEG = -1e30
FORCED_SCORE = 1e6
LOG2E = 1.4426950408889634
Q_SCALE = HEAD_DIM ** -0.5 * LOG2E
N_BIAS_PARTS = 3
ONES_ROWS = 16
ACC_ROWS = HEAD_DIM + ONES_ROWS

ROW_TILE = 512
FOX_TQ = 512
FOX_TK = ROW_TILE
SLC_TQ = 256
SLC_TK = ROW_TILE
WIN_TQ = 256
CMP_TQ = 512
CMP_SUB = 128

VMEM_LIMIT = 48 * 1024 * 1024


def _cparams(sem):
    return pltpu.CompilerParams(dimension_semantics=sem, vmem_limit_bytes=VMEM_LIMIT)


def _iota(shape, axis):
    return lax.broadcasted_iota(jnp.int32, shape, axis)


def _row_rms(x, g):
    ms = jnp.mean(x * x, axis=-1, keepdims=True)
    return x * lax.rsqrt(ms + RMS_EPS) * g


def _pair_rms(y, gain):
    lo = _iota(y.shape, 1) < HEAD_DIM
    y2 = y * y
    s_lo = jnp.sum(jnp.where(lo, y2, 0.0), axis=-1, keepdims=True)
    s_hi = jnp.sum(jnp.where(lo, 0.0, y2), axis=-1, keepdims=True)
    ms = jnp.where(lo, s_lo, s_hi) * (1.0 / HEAD_DIM)
    return y * lax.rsqrt(ms + RMS_EPS) * gain


def _pair_rope(y, cos, sin):
    lane = _iota(y.shape, 1) & (HEAD_DIM - 1)
    partner = jnp.where(lane < ROT_DIM // 2,
                        pltpu.roll(y, LANES - ROT_DIM // 2, 1),
                        pltpu.roll(y, ROT_DIM // 2, 1))
    return y * cos + partner * sin


def _dot(a, b):
    return jnp.dot(a, b, preferred_element_type=F32)


def _dot_nt(a, b):
    return lax.dot_general(a, b, (((1,), (1,)), ((), ())), preferred_element_type=F32)


def _column_max(s):
    tk, n = s.shape
    return jnp.max(jnp.max(s.reshape(tk // SUBLANES, SUBLANES, n), axis=0), axis=0, keepdims=True)


def _softmax_update_t(s, vt, m_ref, acc_ref, m_tile=None):
    tk, n = s.shape
    m_old = m_ref[...]
    if m_tile is None:
        m_tile = _column_max(s)
    m_new = jnp.maximum(m_old, m_tile)
    alpha = jnp.exp2(m_old - m_new)
    p = jnp.exp2(s - m_new).astype(BF16)
    vta = jnp.concatenate([vt, jnp.ones((ONES_ROWS, tk), vt.dtype)], axis=0)
    acc_ref[...] = alpha * acc_ref[...] + _dot(vta, p)
    m_ref[...] = m_new


def _pipelined_key_loop(n_full, produce, consume, unroll=4):
    assert unroll % 2 == 0

    def run(base, count):
        for i in range(count):
            produce(base + i + 1, (i + 1) % 2)
            consume(base + i, i % 2, False)

    produce(0, 0)

    def body(j, carry):
        run(unroll * j, unroll)
        return carry

    lax.fori_loop(0, n_full // unroll, body, 0)
    base = (n_full // unroll) * unroll
    rem = n_full - base
    step = unroll // 2
    while step >= 2:
        pl.when((rem & step) != 0)(functools.partial(run, base, step))
        base = base + (rem & step)
        step //= 2

    @pl.when((rem & 1) != 0)
    def _():
        run(base, 1)
        consume(base + 1, 1, True)

    @pl.when((rem & 1) == 0)
    def _():
        consume(base, 0, True)


def _softmax_init(m_ref, acc_ref):
    m_ref[...] = jnp.full_like(m_ref, NEG)
    acc_ref[...] = jnp.zeros_like(acc_ref)


def _softmax_result(acc_ref):
    acc = acc_ref[...]
    return acc[:HEAD_DIM, :] * (1.0 / acc[HEAD_DIM:HEAD_DIM + 1, :])


def _rope_table_kernel(pos_ref, inv_ref, c_ref, s_ref):
    ang = pos_ref[0].astype(F32) * inv_ref[...]
    c_ref[0] = jnp.cos(ang)
    s_ref[0] = jnp.sin(ang)


def _rope_tables(pos, inv):
    B, T = pos.shape
    half = inv.shape[0]
    per_row = LANES // half
    rows = T // per_row
    pos_rep = jnp.repeat(pos.reshape(B, rows, per_row), half, axis=-1)
    inv_row = jnp.tile(inv, per_row).reshape(1, LANES)
    spec = pl.BlockSpec((1, rows, LANES), lambda b: (b, 0, 0))
    cos, sin = pl.pallas_call(
        _rope_table_kernel,
        grid=(B,),
        in_specs=[spec, pl.BlockSpec((1, LANES), lambda b: (0, 0))],
        out_specs=[spec, spec],
        out_shape=[jax.ShapeDtypeStruct((B, rows, LANES), F32)] * 2,
        compiler_params=_cparams(("parallel",)),
        name="rope_tables",
    )(pos_rep, inv_row)
    cos = cos.reshape(B, T, half)
    sin = sin.reshape(B, T, half)
    rest = HEAD_DIM - 2 * half
    cos_head = jnp.concatenate([cos, cos, jnp.ones((B, T, rest), F32)], axis=-1)
    sin_head = jnp.concatenate([-sin, sin, jnp.zeros((B, T, rest), F32)], axis=-1)
    return jnp.tile(cos_head, (1, 1, LANES // HEAD_DIM)), jnp.tile(sin_head, (1, 1, LANES // HEAD_DIM))


def _fox_inproj_kernel(x_ref, g_ref, w_ref, wvt_ref, bf_ref, qg_ref, kg_ref,
                       q_ref, k_ref, vt_ref, cf_ref, carry_sc, *, tm, hd):
    ti = pl.program_id(1)
    xn = _row_rms(x_ref[0], g_ref[...]).astype(BF16)
    for j in range(hd // MXU_COLS):
        for part, (ref, gain, mul) in enumerate(((q_ref, qg_ref, Q_SCALE), (k_ref, kg_ref, 1.0))):
            c0 = part * hd + MXU_COLS * j
            y = _dot(xn, w_ref[:, c0:c0 + MXU_COLS])
            for hh in range(2):
                blk = _pair_rms(y[:, LANES * hh:LANES * (hh + 1)], gain[...]) * mul
                ref[0, :, MXU_COLS * j + LANES * hh:MXU_COLS * j + LANES * (hh + 1)] = blk.astype(BF16)
        yt = _dot_nt(wvt_ref[MXU_COLS * j:MXU_COLS * (j + 1), :], xn)
        for hh in range(2):
            vt_ref[0, 2 * j + hh, 0] = yt[LANES * hh:LANES * (hh + 1), :].astype(BF16)
    z = _dot(xn, w_ref[:, 2 * hd:2 * hd + LANES]) + bf_ref[...]
    lf = jnp.minimum(z, 0.0) - jnp.log1p(jnp.exp(-jnp.abs(z)))
    row = _iota(lf.shape, 0)
    sh = 1
    while sh < tm:
        lf = lf + jnp.where(row >= sh, pltpu.roll(lf, sh, 0), 0.0)
        sh *= 2

    @pl.when(ti == 0)
    def _():
        carry_sc[...] = jnp.zeros_like(carry_sc)

    c = lf + carry_sc[0:1, :]
    carry_sc[...] = jnp.broadcast_to(c[tm - 1:tm, :], carry_sc.shape)
    rest = c * (-LOG2E)
    lane = _iota(c.shape, 1)
    feats = jnp.zeros_like(c)
    for part in range(N_BIAS_PARTS):
        piece = rest.astype(BF16).astype(F32)
        rest = rest - piece
        placed = piece if part == 0 else pltpu.roll(piece, N_HEADS * part, 1)
        feats = jnp.where((lane >= N_HEADS * part) & (lane < N_HEADS * (part + 1)), placed, feats)
    cf_ref[0] = feats.astype(BF16)


def _fox_inproj(x, g, w, wvt, bf, qg, kg, tm=ROW_TILE):
    B, T, D = x.shape
    hd = N_HEADS * HEAD_DIM
    assert N_BIAS_PARTS * N_HEADS <= LANES
    act = pl.BlockSpec((1, tm, hd), lambda b, t: (b, t, 0))
    vec = lambda n: pl.BlockSpec((1, n), lambda b, t: (0, 0))
    full = lambda a: pl.BlockSpec(a.shape, lambda b, t: (0,) * a.ndim)
    return pl.pallas_call(
        functools.partial(_fox_inproj_kernel, tm=tm, hd=hd),
        grid=(B, T // tm),
        in_specs=[pl.BlockSpec((1, tm, D), lambda b, t: (b, t, 0)), vec(D), full(w), full(wvt),
                  vec(LANES), vec(LANES), vec(LANES)],
        out_specs=[act, act, pl.BlockSpec((1, N_PAIRS, 1, LANES, tm), lambda b, t: (b, 0, t, 0, 0)),
                   pl.BlockSpec((1, tm, LANES), lambda b, t: (b, t, 0))],
        out_shape=[jax.ShapeDtypeStruct((B, T, hd), BF16), jax.ShapeDtypeStruct((B, T, hd), BF16),
                   jax.ShapeDtypeStruct((B, N_PAIRS, T // tm, LANES, tm), BF16),
                   jax.ShapeDtypeStruct((B, T, LANES), BF16)],
        scratch_shapes=[pltpu.VMEM((SUBLANES, LANES), F32)],
        compiler_params=_cparams(("arbitrary", "arbitrary")),
        name="fox_inproj",
    )(x, g, w, wvt, bf, qg, kg)


def _fox_attn_kernel(q_ref, k_ref, cf_ref, vt_ref, o_ref, qa_sc, s_sc, mt_sc, m_sc, acc_sc, *, tq, tk, pairs):
    qi = pl.program_id(2)
    lane = _iota((tq, LANES), 1)
    lo = lane < HEAD_DIM
    for hh in range(2 * pairs):
        pp, half = divmod(hh, 2)
        q = q_ref[0, :, LANES * pp:LANES * (pp + 1)]
        zero = jnp.zeros_like(q)
        qa_sc[hh, :, 0:LANES] = jnp.where(lo, q, zero) if half == 0 else jnp.where(lo, zero, q)
        head = 2 * (pl.program_id(1) * pairs + pp) + half
        feat = jnp.where((lane & (N_HEADS - 1)) == head, 1.0, 0.0)
        qa_sc[hh, :, LANES:2 * LANES] = jnp.where(lane < N_BIAS_PARTS * N_HEADS, feat, 0.0).astype(BF16)
        _softmax_init(m_sc.at[hh], acc_sc.at[hh])

    def produce(ki, slot):
        start = pl.multiple_of(ki * tk, tk)
        for pp in range(pairs):
            cols = slice(LANES * pp, LANES * (pp + 1))
            ka = jnp.concatenate([k_ref[0, pl.ds(start, tk), cols], cf_ref[0, pl.ds(start, tk), :]], axis=1)
            for hh in (2 * pp, 2 * pp + 1):
                s = _dot_nt(ka, qa_sc[hh])
                s_sc[slot, hh] = s
                mt_sc[slot, hh] = _column_max(s)

    def consume(ki, slot, diag):
        for hh in range(2 * pairs):
            pp, half = divmod(hh, 2)
            s = s_sc[slot, hh]
            m_tile = mt_sc[slot, hh]
            if diag:
                s = jnp.where(_iota(s.shape, 0) <= _iota(s.shape, 1), s, NEG)
                m_tile = None
            vt = vt_ref[0, pp, ki, HEAD_DIM * half:HEAD_DIM * (half + 1), :]
            _softmax_update_t(s, vt, m_sc.at[hh], acc_sc.at[hh], m_tile)

    _pipelined_key_loop(qi, produce, consume)
    for pp in range(pairs):
        o_t = jnp.concatenate([_softmax_result(acc_sc.at[2 * pp + half]) for half in range(2)], axis=0)
        o_ref[0, :, LANES * pp:LANES * (pp + 1)] = o_t.T.astype(BF16)


def _fox_attn(q, k, cf, vt, tq=FOX_TQ, tk=FOX_TK, pairs=2):
    B, T, hd = q.shape
    assert tq == tk
    nk = T // tk
    heads = 2 * pairs
    width = pairs * LANES
    seq = pl.BlockSpec((1, T, width), lambda b, p, qi: (b, 0, p))
    return pl.pallas_call(
        functools.partial(_fox_attn_kernel, tq=tq, tk=tk, pairs=pairs),
        grid=(B, N_PAIRS // pairs, T // tq),
        in_specs=[pl.BlockSpec((1, tq, width), lambda b, p, qi: (b, qi, p)), seq,
                  pl.BlockSpec((1, T, LANES), lambda b, p, qi: (b, 0, 0)),
                  pl.BlockSpec((1, pairs, nk, LANES, tk), lambda b, p, qi: (b, p, 0, 0, 0))],
        out_specs=pl.BlockSpec((1, tq, width), lambda b, p, qi: (b, qi, p)),
        out_shape=jax.ShapeDtypeStruct((B, T, hd), BF16),
        scratch_shapes=[pltpu.VMEM((heads, tq, 2 * LANES), BF16), pltpu.VMEM((2, heads, tk, tq), F32),
                        pltpu.VMEM((2, heads, 1, tq), F32), pltpu.VMEM((heads, 1, tq), F32),
                        pltpu.VMEM((heads, ACC_ROWS, tq), F32)],
        compiler_params=_cparams(("parallel", "parallel", "arbitrary")),
        name="fox_attn",
    )(q, k, cf, vt)


def _ffn_kernel(*refs, n_mix, tt, tf, d_ff):
    h_ref = refs[0]
    mix_refs = refs[1:1 + n_mix]
    wo_ref, g_ref, wup_ref, cw_ref, cb_ref, wd_ref, out_ref, a_sc, carry_sc = refs[1 + n_mix:]
    ti = pl.program_id(1)

    @pl.when(ti == 0)
    def _():
        carry_sc[...] = jnp.zeros_like(carry_sc)

    o = mix_refs[0][0]
    if n_mix > 1:
        o = o.astype(F32)
        for ref in mix_refs[1:]:
            o = o + ref[0].astype(F32)
        o = o.astype(BF16)
    x = h_ref[0] + _dot(o, wo_ref[...])
    xn = _row_rms(x, g_ref[...]).astype(BF16)
    row8 = _iota((SUBLANES, tf), 0)

    def conv(u, c0):
        prev8 = carry_sc[:, c0:c0 + tf]
        um1 = pltpu.roll(u, 1, 0)
        um2 = pltpu.roll(u, 2, 0)
        top1 = jnp.where(row8 == 0, prev8[7:8, :], um1[0:SUBLANES, :])
        top2 = jnp.where(row8 == 0, prev8[6:7, :], jnp.where(row8 == 1, prev8[7:8, :], um2[0:SUBLANES, :]))
        um1 = jnp.concatenate([top1, um1[SUBLANES:, :]], axis=0)
        um2 = jnp.concatenate([top2, um2[SUBLANES:, :]], axis=0)
        carry_sc[:, c0:c0 + tf] = u[tt - SUBLANES:, :]
        cw = cw_ref[:, c0:c0 + tf]
        return cb_ref[:, c0:c0 + tf] + cw[0:1, :] * um2 + cw[1:2, :] * um1 + cw[2:3, :] * u

    for f in range(d_ff // tf):
        g0 = f * tf
        cg = conv(_dot(xn, wup_ref[:, g0:g0 + tf]), g0)
        cv = conv(_dot(xn, wup_ref[:, d_ff + g0:d_ff + g0 + tf]), d_ff + g0)
        a_sc[:, g0:g0 + tf] = (cg * jax.nn.sigmoid(cg) * cv).astype(BF16)
    out_ref[0] = x + _dot(a_sc[...], wd_ref[...])


def _mix_out_conv_ffn(h, mix, w_out, g, w_up, conv_w, conv_b, w_down, tt=ROW_TILE, tf=MXU_COLS):
    B, T, D = h.shape
    d_ff = w_down.shape[0]
    act = lambda c: pl.BlockSpec((1, tt, c), lambda b, t: (b, t, 0))
    resident = lambda a: pl.BlockSpec(a.shape, lambda b, t: (0,) * a.ndim, pipeline_mode=pl.Buffered(1))
    return pl.pallas_call(
        functools.partial(_ffn_kernel, n_mix=len(mix), tt=tt, tf=tf, d_ff=d_ff),
        grid=(B, T // tt),
        in_specs=[act(D)] + [act(m.shape[-1]) for m in mix]
                 + [resident(w_out), resident(g), resident(w_up), resident(conv_w), resident(conv_b), resident(w_down)],
        out_specs=act(D),
        out_shape=jax.ShapeDtypeStruct((B, T, D), F32),
        scratch_shapes=[pltpu.VMEM((tt, d_ff), BF16), pltpu.VMEM((SUBLANES, 2 * d_ff), F32)],
        compiler_params=_cparams(("arbitrary", "arbitrary")),
        name="conv_ffn",
    )(h, *mix, w_out, g, w_up, conv_w, conv_b, w_down)


def _kvproj_kernel(h_ref, g_ref, w_ref, wvt_ref, ksg_ref, kwg_ref, c_ref, s_ref,
                   craw_ref, ks_ref, kw_ref, vst_ref, vwt_ref, *, tm):
    xn = _row_rms(h_ref[0], g_ref[...]).astype(BF16)
    cos = c_ref[0]
    sin = s_ref[0]
    raw_cols = craw_ref.shape[-1]
    craw_ref[0] = _dot(xn, w_ref[:, 0:raw_cols])
    for idx, (ref, gain) in enumerate(((ks_ref, ksg_ref), (kw_ref, kwg_ref))):
        c0 = raw_cols + N_KV_GROUPS * LANES * idx
        y = _dot(xn, w_ref[:, c0:c0 + N_KV_GROUPS * LANES])
        for grp in range(N_KV_GROUPS):
            blk = _pair_rope(_pair_rms(y[:, LANES * grp:LANES * (grp + 1)], gain[...]), cos, sin)
            ref[0, grp] = blk.astype(BF16)
    yt = _dot_nt(wvt_ref[...], xn).astype(BF16)
    for grp in range(N_KV_GROUPS):
        vst_ref[0, grp, 0] = yt[HEAD_DIM * grp:HEAD_DIM * (grp + 1), :]
        r0 = HEAD_DIM * (N_KV_GROUPS + grp)
        for c in range(tm // WIN_TQ):
            vwt_ref[0, grp, c] = yt[r0:r0 + HEAD_DIM, WIN_TQ * c:WIN_TQ * (c + 1)]


def _kvproj(h, g, w, wvt, ksg, kwg, cos, sin, tm=ROW_TILE):
    B, T, D = h.shape
    G = N_KV_GROUPS
    vec = lambda n: pl.BlockSpec((1, n), lambda b, t: (0, 0))
    full = lambda a: pl.BlockSpec(a.shape, lambda b, t: (0,) * a.ndim)
    tab = pl.BlockSpec((1, tm, LANES), lambda b, t: (b, t, 0))
    dup = pl.BlockSpec((1, G, tm, LANES), lambda b, t: (b, 0, t, 0))
    dup_shape = jax.ShapeDtypeStruct((B, G, T, LANES), BF16)
    nw = tm // WIN_TQ
    raw_cols = 2 * G * HEAD_DIM
    return pl.pallas_call(
        functools.partial(_kvproj_kernel, tm=tm),
        grid=(B, T // tm),
        in_specs=[pl.BlockSpec((1, tm, D), lambda b, t: (b, t, 0)), vec(D), full(w), full(wvt),
                  vec(LANES), vec(LANES), tab, tab],
        out_specs=[pl.BlockSpec((1, tm, raw_cols), lambda b, t: (b, t, 0)), dup, dup,
                   pl.BlockSpec((1, G, 1, HEAD_DIM, tm), lambda b, t: (b, 0, t, 0, 0)),
                   pl.BlockSpec((1, G, nw, HEAD_DIM, WIN_TQ), lambda b, t: (b, 0, t, 0, 0))],
        out_shape=[jax.ShapeDtypeStruct((B, T, raw_cols), F32), dup_shape, dup_shape,
                   jax.ShapeDtypeStruct((B, G, T // tm, HEAD_DIM, tm), BF16),
                   jax.ShapeDtypeStruct((B, G, T // WIN_TQ, HEAD_DIM, WIN_TQ), BF16)],
        compiler_params=_cparams(("parallel", "parallel")),
        name="kvproj",
    )(h, g, w, wvt, ksg, kwg, cos, sin)


def _compress_kernel(r_ref, w1_ref, pe_ref, w2_ref, w2t_ref, gain_ref, c_ref, s_ref, kc_ref, vct_ref, *, n_cmp):
    r = r_ref[0, 0, 0].astype(BF16)
    half = r.shape[1]
    a = _dot(r, w1_ref[0, :half, :])
    b = _dot(r, w1_ref[0, half:, :])
    peb = _dot(pe_ref[0], w1_ref[0])[0:1, :]
    rows = r.shape[0]
    hid = a + pltpu.roll(b, rows - 1, 0) + peb
    act = jax.nn.gelu(hid).astype(BF16)
    y = _dot(act, w2_ref[0])
    yk = _pair_rope(_pair_rms(y, gain_ref[...]), c_ref[0], s_ref[0])
    kc_ref[0, 0, 0] = jnp.where(_iota(y.shape, 0) < n_cmp, yk, 0.0).astype(BF16)
    yt = _dot_nt(w2t_ref[0], act)
    vct_ref[0, 0, 0] = jnp.where(_iota(yt.shape, 1) < n_cmp, yt, 0.0).astype(BF16)


def _compress(r, w1, pe, w2, w2t, gain, cos, sin, n_cmp):
    B, _, G, rows, width = r.shape
    per_kv = lambda a: pl.BlockSpec((1,) + a.shape[1:], lambda b, kv, g: (kv,) + (0,) * (a.ndim - 1))
    return pl.pallas_call(
        functools.partial(_compress_kernel, n_cmp=n_cmp),
        grid=(B, 2, G),
        in_specs=[pl.BlockSpec((1, 1, 1, rows, width), lambda b, kv, g: (b, kv, g, 0, 0)),
                  per_kv(w1), per_kv(pe), per_kv(w2), per_kv(w2t),
                  pl.BlockSpec((1, LANES), lambda b, kv, g: (0, 0)),
                  pl.BlockSpec((1, rows, LANES), lambda b, kv, g: (b, 0, 0)),
                  pl.BlockSpec((1, rows, LANES), lambda b, kv, g: (b, 0, 0))],
        out_specs=[pl.BlockSpec((1, 1, 1, rows, LANES), lambda b, kv, g: (b, kv, g, 0, 0)),
                   pl.BlockSpec((1, 1, 1, HEAD_DIM, rows), lambda b, kv, g: (b, kv, g, 0, 0))],
        out_shape=[jax.ShapeDtypeStruct((B, 2, G, rows, LANES), BF16),
                   jax.ShapeDtypeStruct((B, 2, G, HEAD_DIM, rows), BF16)],
        compiler_params=_cparams(("parallel", "parallel", "parallel")),
        name="compress",
    )(r, w1, pe, w2, w2t, gain, cos, sin)


def _nsa_inproj_kernel(h_ref, g_ref, w_ref, wgt_ref, bgt_ref, qg_ref, c_ref, s_ref, q_ref, gate_ref, *, hd):
    xn = _row_rms(h_ref[0], g_ref[...]).astype(BF16)
    cos = c_ref[0]
    sin = s_ref[0]
    for j in range(hd // MXU_COLS):
        y = _dot(xn, w_ref[:, MXU_COLS * j:MXU_COLS * (j + 1)])
        for hh in range(2):
            blk = _pair_rope(_pair_rms(y[:, LANES * hh:LANES * (hh + 1)], qg_ref[...]), cos, sin) * Q_SCALE
            q_ref[0, :, MXU_COLS * j + LANES * hh:MXU_COLS * j + LANES * (hh + 1)] = blk.astype(BF16)
    gate_ref[0] = jax.nn.sigmoid(_dot_nt(wgt_ref[...], xn) + bgt_ref[...])


def _nsa_inproj(h, g, w, wgt, bgt, qg, cos, sin, tm=ROW_TILE):
    B, T, D = h.shape
    hd = N_HEADS * HEAD_DIM
    vec = lambda n: pl.BlockSpec((1, n), lambda b, t: (0, 0))
    full = lambda a: pl.BlockSpec(a.shape, lambda b, t: (0,) * a.ndim)
    tab = pl.BlockSpec((1, tm, LANES), lambda b, t: (b, t, 0))
    return pl.pallas_call(
        functools.partial(_nsa_inproj_kernel, hd=hd),
        grid=(B, T // tm),
        in_specs=[pl.BlockSpec((1, tm, D), lambda b, t: (b, t, 0)), vec(D), full(w), full(wgt), full(bgt),
                  vec(LANES), tab, tab],
        out_specs=[pl.BlockSpec((1, tm, hd), lambda b, t: (b, t, 0)),
                   pl.BlockSpec((1, wgt.shape[0], tm), lambda b, t: (b, 0, t))],
        out_shape=[jax.ShapeDtypeStruct((B, T, hd), BF16), jax.ShapeDtypeStruct((B, wgt.shape[0], T), F32)],
        compiler_params=_cparams(("parallel", "parallel")),
        name="nsa_inproj",
    )(h, g, w, wgt, bgt, qg, cos, sin)


def _stack_heads(q, dst_ref, tq):
    lo = _iota((tq, LANES), 1) < HEAD_DIM
    zero = jnp.zeros((tq, LANES), q.dtype)
    for j in range(HEADS_PER_GROUP // 2):
        x = q[:, LANES * j:LANES * (j + 1)]
        dst_ref[(2 * j) * tq:(2 * j + 1) * tq, 0:LANES] = jnp.where(lo, x, zero)
        dst_ref[(2 * j + 1) * tq:(2 * j + 2) * tq, 0:LANES] = jnp.where(lo, zero, x)


def _unstack_heads_t(o_t, gate, o_ref, tq, row0=0):
    for j in range(HEADS_PER_GROUP // 2):
        halves = [o_t[:, h * tq:(h + 1) * tq] * gate[h:h + 1, :] for h in (2 * j, 2 * j + 1)]
        o_ref[0, row0:row0 + tq, LANES * j:LANES * (j + 1)] = jnp.concatenate(halves, axis=0).T.astype(o_ref.dtype)


def _cmp_query_mask_features(sub):
    j = np.arange(sub)
    v = (j // CMP_STRIDE) + ((j % CMP_STRIDE) == CMP_STRIDE - 1)
    feat = np.zeros((sub, LANES), np.float32)
    feat[j, v] = NEG
    return jnp.asarray(feat, BF16)


def _nsa_cmp_kernel(q_ref, kc_ref, vct_ref, ovt_ref, gate_ref, qf_ref, oc_ref, sel_ref, qs_sc, *, tq, sub, n_slc, top_k, n_levels):
    qi = pl.program_id(2)
    n_sub = tq // sub
    n_feats = sub // CMP_STRIDE + 1
    qf = qf_ref[...]
    for u in range(n_sub):
        _stack_heads(q_ref[0, u * sub:(u + 1) * sub, :], qs_sc.at[u], sub)
        for h in range(HEADS_PER_GROUP):
            qs_sc[u, h * sub:(h + 1) * sub, LANES:2 * LANES] = qf
    n_rows = kc_ref.shape[3]
    level_rows = n_rows // n_levels
    n_chunks = 2
    cw = HEADS_PER_GROUP * sub // n_chunks
    n_vis = (qi * tq + tq - 1 - (CMP_LEN - 1)) // CMP_STRIDE + 1

    def attend(rows, u):
        base = (qi * tq + u * sub) // CMP_STRIDE - 2
        n_minus_v = _iota((rows, LANES), 0) - _iota((rows, LANES), 1)
        kf = jnp.where(_iota((rows, LANES), 1) < n_feats, jnp.where(n_minus_v > base, 1.0, 0.0), 0.0)
        ka = jnp.concatenate([kc_ref[0, 0, 0, :rows, :], kf.astype(BF16)], axis=1)
        vct = vct_ref[0, 0, 0, :, :rows]
        scores = [_dot_nt(ka, qs_sc[u, c * cw:(c + 1) * cw, :]) for c in range(n_chunks)]
        outs = []
        p_sum = None
        for s in scores:
            mx = jnp.max(s, axis=0, keepdims=True)
            e = jnp.exp2(s - mx)
            inv = jnp.where(mx > 0.5 * NEG, 1.0 / jnp.maximum(jnp.sum(e, axis=0, keepdims=True), 1.0), 0.0)
            p = e * inv
            outs.append(_dot(vct, p.astype(BF16)))
            for h in range(cw // sub):
                term = p[:, h * sub:(h + 1) * sub]
                p_sum = term if p_sum is None else p_sum + term
        _unstack_heads_t(jnp.concatenate(outs, axis=1), gate_ref[0, :, u * sub:(u + 1) * sub], oc_ref, sub, u * sub)
        p_hi = p_sum.astype(BF16)
        p_lo = (p_sum - p_hi.astype(F32)).astype(BF16)
        return _dot(ovt_ref[:, :rows], p_hi) + _dot(ovt_ref[:, :rows], p_lo)

    def select(imp, u, n_blk):
        imp = imp[:n_blk, :]
        jb = _iota(imp.shape, 0)
        qp = qi * tq + u * sub + _iota(imp.shape, 1)
        cur = qp // SLC_LEN
        forced = (jb == 0) | (jb == cur) | (jb == cur - 1)
        score = jnp.where(jb * SLC_LEN <= qp, jnp.where(forced, FORCED_SCORE, imp), NEG)
        live = jb < n_slc
        score = jnp.where(live, score, -jnp.inf)
        for _ in range(top_k):
            mx = jnp.max(score, axis=0, keepdims=True)
            idx = jnp.min(jnp.where(score == mx, jb, LANES), axis=0, keepdims=True)
            score = jnp.where(jb == idx, -jnp.inf, score)
        bias = jnp.where(live, jnp.where(score == -jnp.inf, 0.0, NEG), NEG)
        if n_blk < LANES:
            bias = jnp.concatenate([bias, jnp.full((LANES - n_blk, sub), NEG, F32)], axis=0)
        sel_ref[0, 0, u * sub:(u + 1) * sub, :] = bias.T.astype(BF16)

    def tile(rows):
        n_blk = rows * CMP_STRIDE // SLC_LEN
        n_blk = n_blk if (n_blk >= top_k and n_blk % SUBLANES == 0) else LANES
        imps = [attend(rows, u) for u in range(n_sub)]
        for u in range(n_sub):
            select(imps[u], u, n_blk)

    for level in range(n_levels):
        lo_rows, hi_rows = level * level_rows, (level + 1) * level_rows
        cond = (n_vis <= hi_rows) if level == 0 else ((n_vis > lo_rows) & (n_vis <= hi_rows))
        pl.when(cond)(functools.partial(tile, hi_rows))


def _gate_spec(branch, tq):
    return pl.BlockSpec((1, HEADS_PER_GROUP, tq), lambda b, g, qi: (b, branch * N_KV_GROUPS + g, qi))


def _nsa_cmp(q, kc_all, vct_all, overlap_t, gates_t, n_slc, tq=CMP_TQ):
    B, T, hd = q.shape
    G = N_KV_GROUPS
    rows = kc_all.shape[3]
    gw = hd // G
    top_k = min(SLC_TOPK, n_slc)
    assert tq % CMP_SUB == 0 and CMP_SUB % SLC_LEN == 0 and CMP_SUB % LANES == 0
    n_levels = 4 if rows % (4 * LANES) == 0 else 1
    return pl.pallas_call(
        functools.partial(_nsa_cmp_kernel, tq=tq, sub=CMP_SUB, n_slc=n_slc, top_k=top_k, n_levels=n_levels),
        grid=(B, G, T // tq),
        in_specs=[pl.BlockSpec((1, tq, gw), lambda b, g, qi: (b, qi, g)),
                  pl.BlockSpec((1, 1, 1, rows, LANES), lambda b, g, qi: (b, 0, g, 0, 0)),
                  pl.BlockSpec((1, 1, 1, HEAD_DIM, rows), lambda b, g, qi: (b, 1, g, 0, 0)),
                  pl.BlockSpec(overlap_t.shape, lambda b, g, qi: (0, 0)), _gate_spec(0, tq),
                  pl.BlockSpec((CMP_SUB, LANES), lambda b, g, qi: (0, 0))],
        out_specs=[pl.BlockSpec((1, tq, gw), lambda b, g, qi: (b, qi, g)),
                   pl.BlockSpec((1, 1, tq, LANES), lambda b, g, qi: (b, g, qi, 0))],
        out_shape=[jax.ShapeDtypeStruct((B, T, hd), BF16), jax.ShapeDtypeStruct((B, G, T, LANES), BF16)],
        scratch_shapes=[pltpu.VMEM((tq // CMP_SUB, HEADS_PER_GROUP * CMP_SUB, 2 * LANES), BF16)],
        compiler_params=_cparams(("parallel", "parallel", "parallel")),
        name="nsa_cmp",
    )(q, kc_all, vct_all, overlap_t, gates_t, _cmp_query_mask_features(CMP_SUB))


def _nsa_slc_kernel(q_ref, sel_ref, k_ref, oh_ref, vt_ref, gate_ref, o_ref, qa_sc, s_sc, mt_sc, m_sc, acc_sc, *, tq, tk, n_chunks):
    qi = pl.program_id(2)
    last = (qi * tq) // tk
    _stack_heads(q_ref[0], qa_sc, tq)
    sel = sel_ref[0, 0]
    for h in range(HEADS_PER_GROUP):
        qa_sc[h * tq:(h + 1) * tq, LANES:2 * LANES] = sel
    _softmax_init(m_sc, acc_sc)
    cw = HEADS_PER_GROUP * tq // n_chunks

    def produce(ki, slot):
        start = pl.multiple_of(ki * tk, tk)
        ka = jnp.concatenate([k_ref[0, 0, pl.ds(start, tk), :], oh_ref[pl.ds(start, tk), :]], axis=1)
        for c in range(n_chunks):
            s = _dot_nt(ka, qa_sc[c * cw:(c + 1) * cw, :])
            s_sc[slot, :, c * cw:(c + 1) * cw] = s
            mt_sc[slot, :, c * cw:(c + 1) * cw] = _column_max(s)

    def consume(ki, slot, diag):
        vt = vt_ref[0, 0, ki]
        for c in range(n_chunks):
            cols = slice(c * cw, (c + 1) * cw)
            s = s_sc[slot, :, cols]
            m_tile = mt_sc[slot, :, cols]
            if diag:
                kpos = ki * tk + _iota(s.shape, 0)
                qpos = qi * tq + (_iota(s.shape, 1) & (tq - 1))
                s = jnp.where(kpos <= qpos, s, NEG)
                m_tile = None
            _softmax_update_t(s, vt, m_sc.at[:, cols], acc_sc.at[:, cols], m_tile)

    _pipelined_key_loop(last, produce, consume)
    _unstack_heads_t(_softmax_result(acc_sc), gate_ref[0], o_ref, tq)


def _nsa_slc(q, sel, ks, vst, gates_t, tq=SLC_TQ, tk=SLC_TK):
    B, T, hd = q.shape
    G = N_KV_GROUPS
    gw = hd // G
    cols = HEADS_PER_GROUP * tq
    assert tk % tq == 0
    onehot = (jnp.arange(T)[:, None] // SLC_LEN == jnp.arange(LANES)[None, :]).astype(BF16)
    return pl.pallas_call(
        functools.partial(_nsa_slc_kernel, tq=tq, tk=tk, n_chunks=4),
        grid=(B, G, T // tq),
        in_specs=[pl.BlockSpec((1, tq, gw), lambda b, g, qi: (b, qi, g)),
                  pl.BlockSpec((1, 1, tq, LANES), lambda b, g, qi: (b, g, qi, 0)),
                  pl.BlockSpec((1, 1, T, LANES), lambda b, g, qi: (b, g, 0, 0)),
                  pl.BlockSpec((T, LANES), lambda b, g, qi: (0, 0)),
                  pl.BlockSpec((1, 1, T // tk, HEAD_DIM, tk), lambda b, g, qi: (b, g, 0, 0, 0)), _gate_spec(1, tq)],
        out_specs=pl.BlockSpec((1, tq, gw), lambda b, g, qi: (b, qi, g)),
        out_shape=jax.ShapeDtypeStruct((B, T, hd), BF16),
        scratch_shapes=[pltpu.VMEM((cols, 2 * LANES), BF16), pltpu.VMEM((2, tk, cols), F32),
                        pltpu.VMEM((2, 1, cols), F32), pltpu.VMEM((1, cols), F32),
                        pltpu.VMEM((ACC_ROWS, cols), F32)],
        compiler_params=_cparams(("parallel", "parallel", "arbitrary")),
        name="nsa_slc",
    )(q, sel, ks, onehot, vst, gates_t)


def _nsa_win_kernel(q_ref, k_ref, vt_ref, gate_ref, o_ref, qs_sc, s_sc, m_sc, acc_sc, *, tq, n_back, n_chunks):
    qi = pl.program_id(2)
    _stack_heads(q_ref[0], qs_sc, tq)
    _softmax_init(m_sc, acc_sc)
    cw = HEADS_PER_GROUP * tq // n_chunks

    def produce(ti, slot):
        start = pl.multiple_of(ti * tq, tq)
        k = k_ref[0, 0, pl.ds(start, tq), :]
        for c in range(n_chunks):
            s_sc[slot, :, c * cw:(c + 1) * cw] = _dot_nt(k, qs_sc[c * cw:(c + 1) * cw, :])

    def consume(ti, slot, kind):
        vt = vt_ref[0, 0, ti]
        for c in range(n_chunks):
            cols = slice(c * cw, (c + 1) * cw)
            s = s_sc[slot, :, cols]
            r = _iota(s.shape, 0)
            q_in_tile = _iota(s.shape, 1) & (tq - 1)
            if kind == "oldest":
                s = jnp.where(r > q_in_tile, s, NEG)
            elif kind == "diag":
                s = jnp.where(r <= q_in_tile, s, NEG)
            _softmax_update_t(s, vt, m_sc.at[:, cols], acc_sc.at[:, cols])

    def kind_of(back):
        return "oldest" if back == n_back else ("diag" if back == 0 else "full")

    for first in range(n_back + 1):
        cond = (qi >= n_back) if first == n_back else (qi == first)

        @pl.when(cond)
        def _(first=first):
            backs = list(range(first, -1, -1))
            produce(qi - backs[0], 0)
            for n, back in enumerate(backs):
                if n + 1 < len(backs):
                    produce(qi - backs[n + 1], (n + 1) % 2)
                consume(qi - back, n % 2, kind_of(back))

    _unstack_heads_t(_softmax_result(acc_sc), gate_ref[0], o_ref, tq)


def _nsa_win(q, kw, vwt, gates_t, tq=WIN_TQ):
    B, T, hd = q.shape
    G = N_KV_GROUPS
    gw = hd // G
    cols = HEADS_PER_GROUP * tq
    return pl.pallas_call(
        functools.partial(_nsa_win_kernel, tq=tq, n_back=WIN // tq, n_chunks=4),
        grid=(B, G, T // tq),
        in_specs=[pl.BlockSpec((1, tq, gw), lambda b, g, qi: (b, qi, g)),
                  pl.BlockSpec((1, 1, T, LANES), lambda b, g, qi: (b, g, 0, 0)),
                  pl.BlockSpec((1, 1, T // tq, HEAD_DIM, tq), lambda b, g, qi: (b, g, 0, 0, 0)), _gate_spec(2, tq)],
        out_specs=pl.BlockSpec((1, tq, gw), lambda b, g, qi: (b, qi, g)),
        out_shape=jax.ShapeDtypeStruct((B, T, hd), BF16),
        scratch_shapes=[pltpu.VMEM((cols, LANES), BF16), pltpu.VMEM((2, tq, cols), F32), pltpu.VMEM((1, cols), F32),
                        pltpu.VMEM((ACC_ROWS, cols), F32)],
        compiler_params=_cparams(("parallel", "parallel", "arbitrary")),
        name="nsa_win",
    )(q, kw, vwt, gates_t)


def _pad_cols(a, n):
    return jnp.pad(a, ((0, 0), (0, n - a.shape[1])))


def _lane_vec(v):
    return jnp.tile(v.astype(F32), 2).reshape(1, LANES)


def kernel(x, positions, a_norm, a_w_in, a_b_f, a_q_gain, a_k_gain, a_w_out, kv_norm, kv_w, kc_pe, vc_pe, kc_w1, kc_w2, vc_w1, vc_w2, kc_gain, ks_gain, kw_gain, b_norm, b_w_in, b_b_gate, b_q_gain, b_w_out, f_norm, f_w_up, f_conv_w, f_conv_b, f_w_down):
    B, T, D = x.shape
    hd = N_HEADS * HEAD_DIM
    G = N_KV_GROUPS
    n_a = a_norm.shape[0]
    n_b = b_norm.shape[0]
    depth = n_a + n_b
    n_slc = T // SLC_LEN
    n_cmp = (T - CMP_LEN) // CMP_STRIDE + 1
    assert T % 1024 == 0 and n_slc <= LANES and hd == 1024 and D == 1024

    half = ROT_DIM // 2
    inv = ROPE_THETA ** (-jnp.arange(half, dtype=F32) * (2.0 / ROT_DIM))
    cos_t, sin_t = _rope_tables(positions, inv)
    end_pos = positions[:, CMP_LEN - 1::CMP_STRIDE]
    end_pos = jnp.pad(end_pos, ((0, 0), (0, T // CMP_STRIDE - n_cmp)))
    cos_c, sin_c = _rope_tables(end_pos, inv)

    cs = jnp.arange(T // CMP_STRIDE) * CMP_STRIDE
    ss = jnp.arange(LANES) * SLC_LEN
    overlap_t = (jnp.maximum(jnp.minimum(cs[None, :] + CMP_LEN, ss[:, None] + SLC_LEN)
                             - jnp.maximum(cs[None, :], ss[:, None]), 0).astype(F32) / CMP_LEN).astype(BF16)

    h = x
    kv = None
    for layer in range(depth):
        if layer < n_a:
            i = layer
            w = jnp.concatenate([a_w_in[i][:, :2 * hd], _pad_cols(a_w_in[i][:, 3 * hd:], LANES)], axis=1).astype(BF16)
            wvt = a_w_in[i][:, 2 * hd:3 * hd].T.astype(BF16)
            bf = _pad_cols(a_b_f[i].reshape(1, -1), LANES)
            q, k, vt, cf = _fox_inproj(h, a_norm[i].reshape(1, D), w, wvt, bf, _lane_vec(a_q_gain[i]),
                                       _lane_vec(a_k_gain[i]))
            mix, w_out = [_fox_attn(q, k, cf, vt)], a_w_out[i]
        else:
            i = layer - n_a
            kc_all, vct_all, ks, kw, vst, vwt = kv
            q, gates_t = _nsa_inproj(h, b_norm[i].reshape(1, D), b_w_in[i][:, :hd].astype(BF16),
                                     b_w_in[i][:, hd:].T.astype(BF16), b_b_gate[i].reshape(-1, 1),
                                     _lane_vec(b_q_gain[i]), cos_t, sin_t)
            o_c, sel = _nsa_cmp(q, kc_all, vct_all, overlap_t, gates_t, n_slc)
            o_s = _nsa_slc(q, sel, ks, vst, gates_t)
            o_w = _nsa_win(q, kw, vwt, gates_t)
            mix, w_out = [o_c, o_s, o_w], b_w_out[i]
        h = _mix_out_conv_ffn(h, mix, w_out.astype(BF16), f_norm[layer].reshape(1, D), f_w_up[layer].astype(BF16),
                              f_conv_w[layer], f_conv_b[layer].reshape(1, -1), f_w_down[layer].astype(BF16))
        if layer == n_a - 1:
            w6 = kv_w.reshape(D, 6, G, HEAD_DIM)
            raw_cols = w6[:, 0:2].reshape(D, 2 * G * HEAD_DIM)
            wk = w6[:, (2, 4)]
            dup_cols = jnp.concatenate([wk, wk], axis=-1).reshape(D, 2 * G * LANES)
            wkv = jnp.concatenate([raw_cols, dup_cols], axis=1).astype(BF16)
            wvt = w6[:, (3, 5)].reshape(D, 2 * G * HEAD_DIM).T.astype(BF16)
            craw, ks, kw, vst, vwt = _kvproj(h, kv_norm.reshape(1, D), wkv, wvt, _lane_vec(ks_gain),
                                             _lane_vec(kw_gain), cos_t, sin_t)
            r = craw.reshape(B, T, 2, G, HEAD_DIM).transpose(0, 2, 3, 1, 4).reshape(B, 2, G, T // CMP_STRIDE, CMP_STRIDE * HEAD_DIM)
            w1 = jnp.stack([kc_w1, vc_w1]).astype(BF16)
            pe = jnp.stack([kc_pe.reshape(1, -1), vc_pe.reshape(1, -1)])
            pe = jnp.pad(pe, ((0, 0), (0, 7), (0, 0))).astype(BF16)
            w2 = jnp.stack([jnp.concatenate([kc_w2, kc_w2], axis=1), jnp.concatenate([vc_w2, vc_w2], axis=1)]).astype(BF16)
            w2t = jnp.stack([kc_w2.T, vc_w2.T]).astype(BF16)
            kc_all, vct_all = _compress(r, w1, pe, w2, w2t, _lane_vec(kc_gain), cos_c, sin_c, n_cmp)
            kv = (kc_all, vct_all, ks, kw, vst, vwt)
    return h
```

```python
import functools

import numpy as np
import jax
import jax.numpy as jnp
from jax import lax
from jax.experimental import pallas as pl
from jax.experimental.pallas import tpu as pltpu

F32 = jnp.float32
BF16 = jnp.bfloat16

LANES = 128
SUBLANES = 8
MXU_COLS = 256
HEAD_DIM = 64
N_HEADS = 16
N_PAIRS = N_HEADS // 2
N_KV_GROUPS = 2
HEADS_PER_GROUP = N_HEADS // N_KV_GROUPS
ROT_DIM = HEAD_DIM // 4
ROPE_THETA = 500000.0
CMP_LEN = 32
CMP_STRIDE = 16
SLC_LEN = 64
SLC_TOPK = 16
WIN = 512
CONV_W = 3
RMS_EPS = 1e-6
NEG = -1e30
FORCED_SCORE = 1e6
LOG2E = 1.4426950408889634
Q_SCALE = HEAD_DIM ** -0.5 * LOG2E
N_BIAS_PARTS = 3
ONES_ROWS = 16
ACC_ROWS = HEAD_DIM + ONES_ROWS

ROW_TILE = 512
FOX_TQ = 512
FOX_TK = ROW_TILE
SLC_TQ = 256
SLC_TK = ROW_TILE
WIN_TQ = 256
CMP_TQ = 512
CMP_SUB = 128

VMEM_LIMIT = 48 * 1024 * 1024


def _cparams(sem):
    return pltpu.CompilerParams(dimension_semantics=sem, vmem_limit_bytes=VMEM_LIMIT)


def _iota(shape, axis):
    return lax.broadcasted_iota(jnp.int32, shape, axis)


def _row_rms(x, g):
    ms = jnp.mean(x * x, axis=-1, keepdims=True)
    return x * lax.rsqrt(ms + RMS_EPS) * g


def _pair_rms(y, gain):
    lo = _iota(y.shape, 1) < HEAD_DIM
    y2 = y * y
    s_lo = jnp.sum(jnp.where(lo, y2, 0.0), axis=-1, keepdims=True)
    s_hi = jnp.sum(jnp.where(lo, 0.0, y2), axis=-1, keepdims=True)
    ms = jnp.where(lo, s_lo, s_hi) * (1.0 / HEAD_DIM)
    return y * lax.rsqrt(ms + RMS_EPS) * gain


def _pair_rope(y, cos, sin):
    lane = _iota(y.shape, 1) & (HEAD_DIM - 1)
    partner = jnp.where(lane < ROT_DIM // 2,
                        pltpu.roll(y, LANES - ROT_DIM // 2, 1),
                        pltpu.roll(y, ROT_DIM // 2, 1))
    return y * cos + partner * sin


def _dot(a, b):
    return jnp.dot(a, b, preferred_element_type=F32)


def _dot_nt(a, b):
    return lax.dot_general(a, b, (((1,), (1,)), ((), ())), preferred_element_type=F32)


def _column_max(s):
    tk, n = s.shape
    return jnp.max(jnp.max(s.reshape(tk // SUBLANES, SUBLANES, n), axis=0), axis=0, keepdims=True)


def _softmax_update_t(s, vt, m_ref, acc_ref, m_tile=None):
    tk, n = s.shape
    m_old = m_ref[...]
    if m_tile is None:
        m_tile = _column_max(s)
    m_new = jnp.maximum(m_old, m_tile)
    alpha = jnp.exp2(m_old - m_new)
    p = jnp.exp2(s - m_new).astype(BF16)
    vta = jnp.concatenate([vt, jnp.ones((ONES_ROWS, tk), vt.dtype)], axis=0)
    acc_ref[...] = alpha * acc_ref[...] + _dot(vta, p)
    m_ref[...] = m_new


def _pipelined_key_loop(n_full, produce, consume, unroll=4):
    assert unroll % 2 == 0

    def run(base, count):
        for i in range(count):
            produce(base + i + 1, (i + 1) % 2)
            consume(base + i, i % 2, False)

    produce(0, 0)

    def body(j, carry):
        run(unroll * j, unroll)
        return carry

    lax.fori_loop(0, n_full // unroll, body, 0)
    base = (n_full // unroll) * unroll
    rem = n_full - base
    step = unroll // 2
    while step >= 2:
        pl.when((rem & step) != 0)(functools.partial(run, base, step))
        base = base + (rem & step)
        step //= 2

    @pl.when((rem & 1) != 0)
    def _():
        run(base, 1)
        consume(base + 1, 1, True)

    @pl.when((rem & 1) == 0)
    def _():
        consume(base, 0, True)


def _softmax_init(m_ref, acc_ref):
    m_ref[...] = jnp.full_like(m_ref, NEG)
    acc_ref[...] = jnp.zeros_like(acc_ref)


def _softmax_result(acc_ref):
    acc = acc_ref[...]
    return acc[:HEAD_DIM, :] * (1.0 / acc[HEAD_DIM:HEAD_DIM + 1, :])


def _rope_table_kernel(pos_ref, inv_ref, c_ref, s_ref):
    ang = pos_ref[0].astype(F32) * inv_ref[...]
    c_ref[0] = jnp.cos(ang)
    s_ref[0] = jnp.sin(ang)


def _rope_tables(pos, inv):
    B, T = pos.shape
    half = inv.shape[0]
    per_row = LANES // half
    rows = T // per_row
    pos_rep = jnp.repeat(pos.reshape(B, rows, per_row), half, axis=-1)
    inv_row = jnp.tile(inv, per_row).reshape(1, LANES)
    spec = pl.BlockSpec((1, rows, LANES), lambda b: (b, 0, 0))
    cos, sin = pl.pallas_call(
        _rope_table_kernel,
        grid=(B,),
        in_specs=[spec, pl.BlockSpec((1, LANES), lambda b: (0, 0))],
        out_specs=[spec, spec],
        out_shape=[jax.ShapeDtypeStruct((B, rows, LANES), F32)] * 2,
        compiler_params=_cparams(("parallel",)),
        name="rope_tables",
    )(pos_rep, inv_row)
    cos = cos.reshape(B, T, half)
    sin = sin.reshape(B, T, half)
    rest = HEAD_DIM - 2 * half
    cos_head = jnp.concatenate([cos, cos, jnp.ones((B, T, rest), F32)], axis=-1)
    sin_head = jnp.concatenate([-sin, sin, jnp.zeros((B, T, rest), F32)], axis=-1)
    return jnp.tile(cos_head, (1, 1, LANES // HEAD_DIM)), jnp.tile(sin_head, (1, 1, LANES // HEAD_DIM))


def _fox_inproj_kernel(x_ref, g_ref, w_ref, wvt_ref, bf_ref, qg_ref, kg_ref,
                       q_ref, k_ref, vt_ref, cf_ref, carry_sc, *, tm, hd):
    ti = pl.program_id(1)
    xn = _row_rms(x_ref[0], g_ref[...]).astype(BF16)
    for j in range(hd // MXU_COLS):
        for part, (ref, gain, mul) in enumerate(((q_ref, qg_ref, Q_SCALE), (k_ref, kg_ref, 1.0))):
            c0 = part * hd + MXU_COLS * j
            y = _dot(xn, w_ref[:, c0:c0 + MXU_COLS])
            for hh in range(2):
                blk = _pair_rms(y[:, LANES * hh:LANES * (hh + 1)], gain[...]) * mul
                ref[0, :, MXU_COLS * j + LANES * hh:MXU_COLS * j + LANES * (hh + 1)] = blk.astype(BF16)
        yt = _dot_nt(wvt_ref[MXU_COLS * j:MXU_COLS * (j + 1), :], xn)
        for hh in range(2):
            vt_ref[0, 2 * j + hh, 0] = yt[LANES * hh:LANES * (hh + 1), :].astype(BF16)
    z = _dot(xn, w_ref[:, 2 * hd:2 * hd + LANES]) + bf_ref[...]
    lf = jnp.minimum(z, 0.0) - jnp.log1p(jnp.exp(-jnp.abs(z)))
    row = _iota(lf.shape, 0)
    sh = 1
    while sh < tm:
        lf = lf + jnp.where(row >= sh, pltpu.roll(lf, sh, 0), 0.0)
        sh *= 2

    @pl.when(ti == 0)
    def _():
        carry_sc[...] = jnp.zeros_like(carry_sc)

    c = lf + carry_sc[0:1, :]
    carry_sc[...] = jnp.broadcast_to(c[tm - 1:tm, :], carry_sc.shape)
    rest = c * (-LOG2E)
    lane = _iota(c.shape, 1)
    feats = jnp.zeros_like(c)
    for part in range(N_BIAS_PARTS):
        piece = rest.astype(BF16).astype(F32)
        rest = rest - piece
        placed = piece if part == 0 else pltpu.roll(piece, N_HEADS * part, 1)
        feats = jnp.where((lane >= N_HEADS * part) & (lane < N_HEADS * (part + 1)), placed, feats)
    cf_ref[0] = feats.astype(BF16)


def _fox_inproj(x, g, w, wvt, bf, qg, kg, tm=ROW_TILE):
    B, T, D = x.shape
    hd = N_HEADS * HEAD_DIM
    assert N_BIAS_PARTS * N_HEADS <= LANES
    act = pl.BlockSpec((1, tm, hd), lambda b, t: (b, t, 0))
    vec = lambda n: pl.BlockSpec((1, n), lambda b, t: (0, 0))
    full = lambda a: pl.BlockSpec(a.shape, lambda b, t: (0,) * a.ndim)
    return pl.pallas_call(
        functools.partial(_fox_inproj_kernel, tm=tm, hd=hd),
        grid=(B, T // tm),
        in_specs=[pl.BlockSpec((1, tm, D), lambda b, t: (b, t, 0)), vec(D), full(w), full(wvt),
                  vec(LANES), vec(LANES), vec(LANES)],
        out_specs=[act, act, pl.BlockSpec((1, N_PAIRS, 1, LANES, tm), lambda b, t: (b, 0, t, 0, 0)),
                   pl.BlockSpec((1, tm, LANES), lambda b, t: (b, t, 0))],
        out_shape=[jax.ShapeDtypeStruct((B, T, hd), BF16), jax.ShapeDtypeStruct((B, T, hd), BF16),
                   jax.ShapeDtypeStruct((B, N_PAIRS, T // tm, LANES, tm), BF16),
                   jax.ShapeDtypeStruct((B, T, LANES), BF16)],
        scratch_shapes=[pltpu.VMEM((SUBLANES, LANES), F32)],
        compiler_params=_cparams(("arbitrary", "arbitrary")),
        name="fox_inproj",
    )(x, g, w, wvt, bf, qg, kg)


def _fox_attn_kernel(q_ref, k_ref, cf_ref, vt_ref, o_ref, qa_sc, s_sc, mt_sc, m_sc, acc_sc, *, tq, tk, pairs):
    qi = pl.program_id(2)
    lane = _iota((tq, LANES), 1)
    lo = lane < HEAD_DIM
    for hh in range(2 * pairs):
        pp, half = divmod(hh, 2)
        q = q_ref[0, :, LANES * pp:LANES * (pp + 1)]
        zero = jnp.zeros_like(q)
        qa_sc[hh, :, 0:LANES] = jnp.where(lo, q, zero) if half == 0 else jnp.where(lo, zero, q)
        head = 2 * (pl.program_id(1) * pairs + pp) + half
        feat = jnp.where((lane & (N_HEADS - 1)) == head, 1.0, 0.0)
        qa_sc[hh, :, LANES:2 * LANES] = jnp.where(lane < N_BIAS_PARTS * N_HEADS, feat, 0.0).astype(BF16)
        _softmax_init(m_sc.at[hh], acc_sc.at[hh])

    def produce(ki, slot):
        start = pl.multiple_of(ki * tk, tk)
        for pp in range(pairs):
            cols = slice(LANES * pp, LANES * (pp + 1))
            ka = jnp.concatenate([k_ref[0, pl.ds(start, tk), cols], cf_ref[0, pl.ds(start, tk), :]], axis=1)
            for hh in (2 * pp, 2 * pp + 1):
                s = _dot_nt(ka, qa_sc[hh])
                s_sc[slot, hh] = s
                mt_sc[slot, hh] = _column_max(s)

    def consume(ki, slot, diag):
        for hh in range(2 * pairs):
            pp, half = divmod(hh, 2)
            s = s_sc[slot, hh]
            m_tile = mt_sc[slot, hh]
            if diag:
                s = jnp.where(_iota(s.shape, 0) <= _iota(s.shape, 1), s, NEG)
                m_tile = None
            vt = vt_ref[0, pp, ki, HEAD_DIM * half:HEAD_DIM * (half + 1), :]
            _softmax_update_t(s, vt, m_sc.at[hh], acc_sc.at[hh], m_tile)

    _pipelined_key_loop(qi, produce, consume)
    for pp in range(pairs):
        o_t = jnp.concatenate([_softmax_result(acc_sc.at[2 * pp + half]) for half in range(2)], axis=0)
        o_ref[0, :, LANES * pp:LANES * (pp + 1)] = o_t.T.astype(BF16)


def _fox_attn(q, k, cf, vt, tq=FOX_TQ, tk=FOX_TK, pairs=2):
    B, T, hd = q.shape
    assert tq == tk
    nk = T // tk
    heads = 2 * pairs
    width = pairs * LANES
    seq = pl.BlockSpec((1, T, width), lambda b, p, qi: (b, 0, p))
    return pl.pallas_call(
        functools.partial(_fox_attn_kernel, tq=tq, tk=tk, pairs=pairs),
        grid=(B, N_PAIRS // pairs, T // tq),
        in_specs=[pl.BlockSpec((1, tq, width), lambda b, p, qi: (b, qi, p)), seq,
                  pl.BlockSpec((1, T, LANES), lambda b, p, qi: (b, 0, 0)),
                  pl.BlockSpec((1, pairs, nk, LANES, tk), lambda b, p, qi: (b, p, 0, 0, 0))],
        out_specs=pl.BlockSpec((1, tq, width), lambda b, p, qi: (b, qi, p)),
        out_shape=jax.ShapeDtypeStruct((B, T, hd), BF16),
        scratch_shapes=[pltpu.VMEM((heads, tq, 2 * LANES), BF16), pltpu.VMEM((2, heads, tk, tq), F32),
                        pltpu.VMEM((2, heads, 1, tq), F32), pltpu.VMEM((heads, 1, tq), F32),
                        pltpu.VMEM((heads, ACC_ROWS, tq), F32)],
        compiler_params=_cparams(("parallel", "parallel", "arbitrary")),
        name="fox_attn",
    )(q, k, cf, vt)


def _ffn_kernel(*refs, n_mix, tt, tf, d_ff):
    h_ref = refs[0]
    mix_refs = refs[1:1 + n_mix]
    wo_ref, g_ref, wup_ref, cw_ref, cb_ref, wd_ref, out_ref, a_sc, carry_sc = refs[1 + n_mix:]
    ti = pl.program_id(1)

    @pl.when(ti == 0)
    def _():
        carry_sc[...] = jnp.zeros_like(carry_sc)

    o = mix_refs[0][0]
    if n_mix > 1:
        o = o.astype(F32)
        for ref in mix_refs[1:]:
            o = o + ref[0].astype(F32)
        o = o.astype(BF16)
    x = h_ref[0] + _dot(o, wo_ref[...])
    xn = _row_rms(x, g_ref[...]).astype(BF16)
    row8 = _iota((SUBLANES, tf), 0)

    def conv(u, c0):
        prev8 = carry_sc[:, c0:c0 + tf]
        um1 = pltpu.roll(u, 1, 0)
        um2 = pltpu.roll(u, 2, 0)
        top1 = jnp.where(row8 == 0, prev8[7:8, :], um1[0:SUBLANES, :])
        top2 = jnp.where(row8 == 0, prev8[6:7, :], jnp.where(row8 == 1, prev8[7:8, :], um2[0:SUBLANES, :]))
        um1 = jnp.concatenate([top1, um1[SUBLANES:, :]], axis=0)
        um2 = jnp.concatenate([top2, um2[SUBLANES:, :]], axis=0)
        carry_sc[:, c0:c0 + tf] = u[tt - SUBLANES:, :]
        cw = cw_ref[:, c0:c0 + tf]
        return cb_ref[:, c0:c0 + tf] + cw[0:1, :] * um2 + cw[1:2, :] * um1 + cw[2:3, :] * u

    for f in range(d_ff // tf):
        g0 = f * tf
        cg = conv(_dot(xn, wup_ref[:, g0:g0 + tf]), g0)
        cv = conv(_dot(xn, wup_ref[:, d_ff + g0:d_ff + g0 + tf]), d_ff + g0)
        a_sc[:, g0:g0 + tf] = (cg * jax.nn.sigmoid(cg) * cv).astype(BF16)
    out_ref[0] = x + _dot(a_sc[...], wd_ref[...])


def _mix_out_conv_ffn(h, mix, w_out, g, w_up, conv_w, conv_b, w_down, tt=ROW_TILE, tf=MXU_COLS):
    B, T, D = h.shape
    d_ff = w_down.shape[0]
    assert conv_w.shape[0] == CONV_W and d_ff % tf == 0
    act = lambda c: pl.BlockSpec((1, tt, c), lambda b, t: (b, t, 0))
    resident = lambda a: pl.BlockSpec(a.shape, lambda b, t: (0,) * a.ndim, pipeline_mode=pl.Buffered(1))
    return pl.pallas_call(
        functools.partial(_ffn_kernel, n_mix=len(mix), tt=tt, tf=tf, d_ff=d_ff),
        grid=(B, T // tt),
        in_specs=[act(D)] + [act(m.shape[-1]) for m in mix]
                 + [resident(w_out), resident(g), resident(w_up), resident(conv_w), resident(conv_b), resident(w_down)],
        out_specs=act(D),
        out_shape=jax.ShapeDtypeStruct((B, T, D), F32),
        scratch_shapes=[pltpu.VMEM((tt, d_ff), BF16), pltpu.VMEM((SUBLANES, 2 * d_ff), F32)],
        compiler_params=_cparams(("arbitrary", "arbitrary")),
        name="conv_ffn",
    )(h, *mix, w_out, g, w_up, conv_w, conv_b, w_down)


def _kvproj_kernel(h_ref, g_ref, w_ref, wvt_ref, ksg_ref, kwg_ref, c_ref, s_ref,
                   craw_ref, ks_ref, kw_ref, vst_ref, vwt_ref, *, tm):
    xn = _row_rms(h_ref[0], g_ref[...]).astype(BF16)
    cos = c_ref[0]
    sin = s_ref[0]
    raw_cols = craw_ref.shape[-1]
    craw_ref[0] = _dot(xn, w_ref[:, 0:raw_cols])
    for idx, (ref, gain) in enumerate(((ks_ref, ksg_ref), (kw_ref, kwg_ref))):
        c0 = raw_cols + N_KV_GROUPS * LANES * idx
        y = _dot(xn, w_ref[:, c0:c0 + N_KV_GROUPS * LANES])
        for grp in range(N_KV_GROUPS):
            blk = _pair_rope(_pair_rms(y[:, LANES * grp:LANES * (grp + 1)], gain[...]), cos, sin)
            ref[0, grp] = blk.astype(BF16)
    yt = _dot_nt(wvt_ref[...], xn).astype(BF16)
    for grp in range(N_KV_GROUPS):
        vst_ref[0, grp, 0] = yt[HEAD_DIM * grp:HEAD_DIM * (grp + 1), :]
        r0 = HEAD_DIM * (N_KV_GROUPS + grp)
        for c in range(tm // WIN_TQ):
            vwt_ref[0, grp, c] = yt[r0:r0 + HEAD_DIM, WIN_TQ * c:WIN_TQ * (c + 1)]


def _kvproj(h, g, w, wvt, ksg, kwg, cos, sin, tm=ROW_TILE):
    B, T, D = h.shape
    G = N_KV_GROUPS
    vec = lambda n: pl.BlockSpec((1, n), lambda b, t: (0, 0))
    full = lambda a: pl.BlockSpec(a.shape, lambda b, t: (0,) * a.ndim)
    tab = pl.BlockSpec((1, tm, LANES), lambda b, t: (b, t, 0))
    dup = pl.BlockSpec((1, G, tm, LANES), lambda b, t: (b, 0, t, 0))
    dup_shape = jax.ShapeDtypeStruct((B, G, T, LANES), BF16)
    nw = tm // WIN_TQ
    raw_cols = 2 * G * HEAD_DIM
    return pl.pallas_call(
        functools.partial(_kvproj_kernel, tm=tm),
        grid=(B, T // tm),
        in_specs=[pl.BlockSpec((1, tm, D), lambda b, t: (b, t, 0)), vec(D), full(w), full(wvt),
                  vec(LANES), vec(LANES), tab, tab],
        out_specs=[pl.BlockSpec((1, tm, raw_cols), lambda b, t: (b, t, 0)), dup, dup,
                   pl.BlockSpec((1, G, 1, HEAD_DIM, tm), lambda b, t: (b, 0, t, 0, 0)),
                   pl.BlockSpec((1, G, nw, HEAD_DIM, WIN_TQ), lambda b, t: (b, 0, t, 0, 0))],
        out_shape=[jax.ShapeDtypeStruct((B, T, raw_cols), F32), dup_shape, dup_shape,
                   jax.ShapeDtypeStruct((B, G, T // tm, HEAD_DIM, tm), BF16),
                   jax.ShapeDtypeStruct((B, G, T // WIN_TQ, HEAD_DIM, WIN_TQ), BF16)],
        compiler_params=_cparams(("parallel", "parallel")),
        name="kvproj",
    )(h, g, w, wvt, ksg, kwg, cos, sin)


def _compress_kernel(r_ref, w1_ref, pe_ref, w2_ref, w2t_ref, gain_ref, c_ref, s_ref, kc_ref, vct_ref, *, n_cmp):
    r = r_ref[0, 0, 0].astype(BF16)
    half = r.shape[1]
    a = _dot(r, w1_ref[0, :half, :])
    b = _dot(r, w1_ref[0, half:, :])
    peb = _dot(pe_ref[0], w1_ref[0])[0:1, :]
    rows = r.shape[0]
    hid = a + pltpu.roll(b, rows - 1, 0) + peb
    act = jax.nn.gelu(hid).astype(BF16)
    y = _dot(act, w2_ref[0])
    yk = _pair_rope(_pair_rms(y, gain_ref[...]), c_ref[0], s_ref[0])
    kc_ref[0, 0, 0] = jnp.where(_iota(y.shape, 0) < n_cmp, yk, 0.0).astype(BF16)
    yt = _dot_nt(w2t_ref[0], act)
    vct_ref[0, 0, 0] = jnp.where(_iota(yt.shape, 1) < n_cmp, yt, 0.0).astype(BF16)


def _compress(r, w1, pe, w2, w2t, gain, cos, sin, n_cmp):
    B, _, G, rows, width = r.shape
    per_kv = lambda a: pl.BlockSpec((1,) + a.shape[1:], lambda b, kv, g: (kv,) + (0,) * (a.ndim - 1))
    return pl.pallas_call(
        functools.partial(_compress_kernel, n_cmp=n_cmp),
        grid=(B, 2, G),
        in_specs=[pl.BlockSpec((1, 1, 1, rows, width), lambda b, kv, g: (b, kv, g, 0, 0)),
                  per_kv(w1), per_kv(pe), per_kv(w2), per_kv(w2t),
                  pl.BlockSpec((1, LANES), lambda b, kv, g: (0, 0)),
                  pl.BlockSpec((1, rows, LANES), lambda b, kv, g: (b, 0, 0)),
                  pl.BlockSpec((1, rows, LANES), lambda b, kv, g: (b, 0, 0))],
        out_specs=[pl.BlockSpec((1, 1, 1, rows, LANES), lambda b, kv, g: (b, kv, g, 0, 0)),
                   pl.BlockSpec((1, 1, 1, HEAD_DIM, rows), lambda b, kv, g: (b, kv, g, 0, 0))],
        out_shape=[jax.ShapeDtypeStruct((B, 2, G, rows, LANES), BF16),
                   jax.ShapeDtypeStruct((B, 2, G, HEAD_DIM, rows), BF16)],
        compiler_params=_cparams(("parallel", "parallel", "parallel")),
        name="compress",
    )(r, w1, pe, w2, w2t, gain, cos, sin)


def _nsa_inproj_kernel(h_ref, g_ref, w_ref, wgt_ref, bgt_ref, qg_ref, c_ref, s_ref, q_ref, gate_ref, *, hd):
    xn = _row_rms(h_ref[0], g_ref[...]).astype(BF16)
    cos = c_ref[0]
    sin = s_ref[0]
    for j in range(hd // MXU_COLS):
        y = _dot(xn, w_ref[:, MXU_COLS * j:MXU_COLS * (j + 1)])
        for hh in range(2):
            blk = _pair_rope(_pair_rms(y[:, LANES * hh:LANES * (hh + 1)], qg_ref[...]), cos, sin) * Q_SCALE
            q_ref[0, :, MXU_COLS * j + LANES * hh:MXU_COLS * j + LANES * (hh + 1)] = blk.astype(BF16)
    gate_ref[0] = jax.nn.sigmoid(_dot_nt(wgt_ref[...], xn) + bgt_ref[...])


def _nsa_inproj(h, g, w, wgt, bgt, qg, cos, sin, tm=ROW_TILE):
    B, T, D = h.shape
    hd = N_HEADS * HEAD_DIM
    vec = lambda n: pl.BlockSpec((1, n), lambda b, t: (0, 0))
    full = lambda a: pl.BlockSpec(a.shape, lambda b, t: (0,) * a.ndim)
    tab = pl.BlockSpec((1, tm, LANES), lambda b, t: (b, t, 0))
    return pl.pallas_call(
        functools.partial(_nsa_inproj_kernel, hd=hd),
        grid=(B, T // tm),
        in_specs=[pl.BlockSpec((1, tm, D), lambda b, t: (b, t, 0)), vec(D), full(w), full(wgt), full(bgt),
                  vec(LANES), tab, tab],
        out_specs=[pl.BlockSpec((1, tm, hd), lambda b, t: (b, t, 0)),
                   pl.BlockSpec((1, wgt.shape[0], tm), lambda b, t: (b, 0, t))],
        out_shape=[jax.ShapeDtypeStruct((B, T, hd), BF16), jax.ShapeDtypeStruct((B, wgt.shape[0], T), F32)],
        compiler_params=_cparams(("parallel", "parallel")),
        name="nsa_inproj",
    )(h, g, w, wgt, bgt, qg, cos, sin)


def _stack_heads(q, dst_ref, tq):
    lo = _iota((tq, LANES), 1) < HEAD_DIM
    zero = jnp.zeros((tq, LANES), q.dtype)
    for j in range(HEADS_PER_GROUP // 2):
        x = q[:, LANES * j:LANES * (j + 1)]
        dst_ref[(2 * j) * tq:(2 * j + 1) * tq, 0:LANES] = jnp.where(lo, x, zero)
        dst_ref[(2 * j + 1) * tq:(2 * j + 2) * tq, 0:LANES] = jnp.where(lo, zero, x)


def _unstack_heads_t(o_t, gate, o_ref, tq, row0=0):
    for j in range(HEADS_PER_GROUP // 2):
        halves = [o_t[:, h * tq:(h + 1) * tq] * gate[h:h + 1, :] for h in (2 * j, 2 * j + 1)]
        o_ref[0, row0:row0 + tq, LANES * j:LANES * (j + 1)] = jnp.concatenate(halves, axis=0).T.astype(o_ref.dtype)


def _cmp_query_mask_features(sub):
    j = np.arange(sub)
    v = (j // CMP_STRIDE) + ((j % CMP_STRIDE) == CMP_STRIDE - 1)
    feat = np.zeros((sub, LANES), np.float32)
    feat[j, v] = NEG
    return jnp.asarray(feat, BF16)


def _nsa_cmp_kernel(q_ref, kc_ref, vct_ref, ovt_ref, gate_ref, qf_ref, oc_ref, sel_ref, qs_sc, *, tq, sub, n_slc, top_k, n_levels):
    qi = pl.program_id(2)
    n_sub = tq // sub
    n_feats = sub // CMP_STRIDE + 1
    qf = qf_ref[...]
    for u in range(n_sub):
        _stack_heads(q_ref[0, u * sub:(u + 1) * sub, :], qs_sc.at[u], sub)
        for h in range(HEADS_PER_GROUP):
            qs_sc[u, h * sub:(h + 1) * sub, LANES:2 * LANES] = qf
    n_rows = kc_ref.shape[3]
    level_rows = n_rows // n_levels
    n_chunks = 2
    cw = HEADS_PER_GROUP * sub // n_chunks
    n_vis = (qi * tq + tq - 1 - (CMP_LEN - 1)) // CMP_STRIDE + 1

    def attend(rows, u):
        base = (qi * tq + u * sub) // CMP_STRIDE - 2
        n_minus_v = _iota((rows, LANES), 0) - _iota((rows, LANES), 1)
        kf = jnp.where(_iota((rows, LANES), 1) < n_feats, jnp.where(n_minus_v > base, 1.0, 0.0), 0.0)
        ka = jnp.concatenate([kc_ref[0, 0, 0, :rows, :], kf.astype(BF16)], axis=1)
        vct = vct_ref[0, 0, 0, :, :rows]
        scores = [_dot_nt(ka, qs_sc[u, c * cw:(c + 1) * cw, :]) for c in range(n_chunks)]
        outs = []
        p_sum = None
        for s in scores:
            mx = jnp.max(s, axis=0, keepdims=True)
            e = jnp.exp2(s - mx)
            inv = jnp.where(mx > 0.5 * NEG, 1.0 / jnp.maximum(jnp.sum(e, axis=0, keepdims=True), 1.0), 0.0)
            p = e * inv
            outs.append(_dot(vct, p.astype(BF16)))
            for h in range(cw // sub):
                term = p[:, h * sub:(h + 1) * sub]
                p_sum = term if p_sum is None else p_sum + term
        _unstack_heads_t(jnp.concatenate(outs, axis=1), gate_ref[0, :, u * sub:(u + 1) * sub], oc_ref, sub, u * sub)
        p_hi = p_sum.astype(BF16)
        p_lo = (p_sum - p_hi.astype(F32)).astype(BF16)
        return _dot(ovt_ref[:, :rows], p_hi) + _dot(ovt_ref[:, :rows], p_lo)

    def select(imp, u, n_blk):
        imp = imp[:n_blk, :]
        jb = _iota(imp.shape, 0)
        qp = qi * tq + u * sub + _iota(imp.shape, 1)
        cur = qp // SLC_LEN
        forced = (jb == 0) | (jb == cur) | (jb == cur - 1)
        score = jnp.where(jb * SLC_LEN <= qp, jnp.where(forced, FORCED_SCORE, imp), NEG)
        live = jb < n_slc
        score = jnp.where(live, score, -jnp.inf)
        for _ in range(top_k):
            mx = jnp.max(score, axis=0, keepdims=True)
            idx = jnp.min(jnp.where(score == mx, jb, LANES), axis=0, keepdims=True)
            score = jnp.where(jb == idx, -jnp.inf, score)
        bias = jnp.where(live, jnp.where(score == -jnp.inf, 0.0, NEG), NEG)
        if n_blk < LANES:
            bias = jnp.concatenate([bias, jnp.full((LANES - n_blk, sub), NEG, F32)], axis=0)
        sel_ref[0, 0, u * sub:(u + 1) * sub, :] = bias.T.astype(BF16)

    def tile(rows):
        n_blk = rows * CMP_STRIDE // SLC_LEN
        n_blk = n_blk if (n_blk >= top_k and n_blk % SUBLANES == 0) else LANES
        imps = [attend(rows, u) for u in range(n_sub)]
        for u in range(n_sub):
            select(imps[u], u, n_blk)

    for level in range(n_levels):
        lo_rows, hi_rows = level * level_rows, (level + 1) * level_rows
        cond = (n_vis <= hi_rows) if level == 0 else ((n_vis > lo_rows) & (n_vis <= hi_rows))
        pl.when(cond)(functools.partial(tile, hi_rows))


def _gate_spec(branch, tq):
    return pl.BlockSpec((1, HEADS_PER_GROUP, tq), lambda b, g, qi: (b, branch * N_KV_GROUPS + g, qi))


def _nsa_cmp(q, kc_all, vct_all, overlap_t, gates_t, n_slc, tq=CMP_TQ):
    B, T, hd = q.shape
    G = N_KV_GROUPS
    rows = kc_all.shape[3]
    gw = hd // G
    top_k = min(SLC_TOPK, n_slc)
    assert tq % CMP_SUB == 0 and CMP_SUB % SLC_LEN == 0 and CMP_SUB % LANES == 0
    n_levels = 4 if rows % (4 * LANES) == 0 else 1
    return pl.pallas_call(
        functools.partial(_nsa_cmp_kernel, tq=tq, sub=CMP_SUB, n_slc=n_slc, top_k=top_k, n_levels=n_levels),
        grid=(B, G, T // tq),
        in_specs=[pl.BlockSpec((1, tq, gw), lambda b, g, qi: (b, qi, g)),
                  pl.BlockSpec((1, 1, 1, rows, LANES), lambda b, g, qi: (b, 0, g, 0, 0)),
                  pl.BlockSpec((1, 1, 1, HEAD_DIM, rows), lambda b, g, qi: (b, 1, g, 0, 0)),
                  pl.BlockSpec(overlap_t.shape, lambda b, g, qi: (0, 0)), _gate_spec(0, tq),
                  pl.BlockSpec((CMP_SUB, LANES), lambda b, g, qi: (0, 0))],
        out_specs=[pl.BlockSpec((1, tq, gw), lambda b, g, qi: (b, qi, g)),
                   pl.BlockSpec((1, 1, tq, LANES), lambda b, g, qi: (b, g, qi, 0))],
        out_shape=[jax.ShapeDtypeStruct((B, T, hd), BF16), jax.ShapeDtypeStruct((B, G, T, LANES), BF16)],
        scratch_shapes=[pltpu.VMEM((tq // CMP_SUB, HEADS_PER_GROUP * CMP_SUB, 2 * LANES), BF16)],
        compiler_params=_cparams(("parallel", "parallel", "parallel")),
        name="nsa_cmp",
    )(q, kc_all, vct_all, overlap_t, gates_t, _cmp_query_mask_features(CMP_SUB))


def _nsa_slc_kernel(q_ref, sel_ref, k_ref, oh_ref, vt_ref, gate_ref, o_ref, qa_sc, s_sc, mt_sc, m_sc, acc_sc, *, tq, tk, n_chunks):
    qi = pl.program_id(2)
    last = (qi * tq) // tk
    _stack_heads(q_ref[0], qa_sc, tq)
    sel = sel_ref[0, 0]
    for h in range(HEADS_PER_GROUP):
        qa_sc[h * tq:(h + 1) * tq, LANES:2 * LANES] = sel
    _softmax_init(m_sc, acc_sc)
    cw = HEADS_PER_GROUP * tq // n_chunks

    def produce(ki, slot):
        start = pl.multiple_of(ki * tk, tk)
        ka = jnp.concatenate([k_ref[0, 0, pl.ds(start, tk), :], oh_ref[pl.ds(start, tk), :]], axis=1)
        for c in range(n_chunks):
            s = _dot_nt(ka, qa_sc[c * cw:(c + 1) * cw, :])
            s_sc[slot, :, c * cw:(c + 1) * cw] = s
            mt_sc[slot, :, c * cw:(c + 1) * cw] = _column_max(s)

    def consume(ki, slot, diag):
        vt = vt_ref[0, 0, ki]
        for c in range(n_chunks):
            cols = slice(c * cw, (c + 1) * cw)
            s = s_sc[slot, :, cols]
            m_tile = mt_sc[slot, :, cols]
            if diag:
                kpos = ki * tk + _iota(s.shape, 0)
                qpos = qi * tq + (_iota(s.shape, 1) & (tq - 1))
                s = jnp.where(kpos <= qpos, s, NEG)
                m_tile = None
            _softmax_update_t(s, vt, m_sc.at[:, cols], acc_sc.at[:, cols], m_tile)

    _pipelined_key_loop(last, produce, consume)
    _unstack_heads_t(_softmax_result(acc_sc), gate_ref[0], o_ref, tq)


def _nsa_slc(q, sel, ks, vst, gates_t, tq=SLC_TQ, tk=SLC_TK):
    B, T, hd = q.shape
    G = N_KV_GROUPS
    gw = hd // G
    cols = HEADS_PER_GROUP * tq
    assert tk % tq == 0
    onehot = (jnp.arange(T)[:, None] // SLC_LEN == jnp.arange(LANES)[None, :]).astype(BF16)
    return pl.pallas_call(
        functools.partial(_nsa_slc_kernel, tq=tq, tk=tk, n_chunks=4),
        grid=(B, G, T // tq),
        in_specs=[pl.BlockSpec((1, tq, gw), lambda b, g, qi: (b, qi, g)),
                  pl.BlockSpec((1, 1, tq, LANES), lambda b, g, qi: (b, g, qi, 0)),
                  pl.BlockSpec((1, 1, T, LANES), lambda b, g, qi: (b, g, 0, 0)),
                  pl.BlockSpec((T, LANES), lambda b, g, qi: (0, 0)),
                  pl.BlockSpec((1, 1, T // tk, HEAD_DIM, tk), lambda b, g, qi: (b, g, 0, 0, 0)), _gate_spec(1, tq)],
        out_specs=pl.BlockSpec((1, tq, gw), lambda b, g, qi: (b, qi, g)),
        out_shape=jax.ShapeDtypeStruct((B, T, hd), BF16),
        scratch_shapes=[pltpu.VMEM((cols, 2 * LANES), BF16), pltpu.VMEM((2, tk, cols), F32),
                        pltpu.VMEM((2, 1, cols), F32), pltpu.VMEM((1, cols), F32),
                        pltpu.VMEM((ACC_ROWS, cols), F32)],
        compiler_params=_cparams(("parallel", "parallel", "arbitrary")),
        name="nsa_slc",
    )(q, sel, ks, onehot, vst, gates_t)


def _nsa_win_kernel(q_ref, k_ref, vt_ref, gate_ref, o_ref, qs_sc, s_sc, m_sc, acc_sc, *, tq, n_back, n_chunks):
    qi = pl.program_id(2)
    _stack_heads(q_ref[0], qs_sc, tq)
    _softmax_init(m_sc, acc_sc)
    cw = HEADS_PER_GROUP * tq // n_chunks

    def produce(ti, slot):
        start = pl.multiple_of(ti * tq, tq)
        k = k_ref[0, 0, pl.ds(start, tq), :]
        for c in range(n_chunks):
            s_sc[slot, :, c * cw:(c + 1) * cw] = _dot_nt(k, qs_sc[c * cw:(c + 1) * cw, :])

    def consume(ti, slot, kind):
        vt = vt_ref[0, 0, ti]
        for c in range(n_chunks):
            cols = slice(c * cw, (c + 1) * cw)
            s = s_sc[slot, :, cols]
            r = _iota(s.shape, 0)
            q_in_tile = _iota(s.shape, 1) & (tq - 1)
            if kind == "oldest":
                s = jnp.where(r > q_in_tile, s, NEG)
            elif kind == "diag":
                s = jnp.where(r <= q_in_tile, s, NEG)
            _softmax_update_t(s, vt, m_sc.at[:, cols], acc_sc.at[:, cols])

    def kind_of(back):
        return "oldest" if back == n_back else ("diag" if back == 0 else "full")

    for first in range(n_back + 1):
        cond = (qi >= n_back) if first == n_back else (qi == first)

        @pl.when(cond)
        def _(first=first):
            backs = list(range(first, -1, -1))
            produce(qi - backs[0], 0)
            for n, back in enumerate(backs):
                if n + 1 < len(backs):
                    produce(qi - backs[n + 1], (n + 1) % 2)
                consume(qi - back, n % 2, kind_of(back))

    _unstack_heads_t(_softmax_result(acc_sc), gate_ref[0], o_ref, tq)


def _nsa_win(q, kw, vwt, gates_t, tq=WIN_TQ):
    B, T, hd = q.shape
    G = N_KV_GROUPS
    gw = hd // G
    cols = HEADS_PER_GROUP * tq
    return pl.pallas_call(
        functools.partial(_nsa_win_kernel, tq=tq, n_back=WIN // tq, n_chunks=4),
        grid=(B, G, T // tq),
        in_specs=[pl.BlockSpec((1, tq, gw), lambda b, g, qi: (b, qi, g)),
                  pl.BlockSpec((1, 1, T, LANES), lambda b, g, qi: (b, g, 0, 0)),
                  pl.BlockSpec((1, 1, T // tq, HEAD_DIM, tq), lambda b, g, qi: (b, g, 0, 0, 0)), _gate_spec(2, tq)],
        out_specs=pl.BlockSpec((1, tq, gw), lambda b, g, qi: (b, qi, g)),
        out_shape=jax.ShapeDtypeStruct((B, T, hd), BF16),
        scratch_shapes=[pltpu.VMEM((cols, LANES), BF16), pltpu.VMEM((2, tq, cols), F32), pltpu.VMEM((1, cols), F32),
                        pltpu.VMEM((ACC_ROWS, cols), F32)],
        compiler_params=_cparams(("parallel", "parallel", "arbitrary")),
        name="nsa_win",
    )(q, kw, vwt, gates_t)


def _pad_cols(a, n):
    return jnp.pad(a, ((0, 0), (0, n - a.shape[1])))


def _lane_vec(v):
    return jnp.tile(v.astype(F32), 2).reshape(1, LANES)


def kernel(x, positions, a_norm, a_w_in, a_b_f, a_q_gain, a_k_gain, a_w_out, kv_norm, kv_w, kc_pe, vc_pe, kc_w1, kc_w2, vc_w1, vc_w2, kc_gain, ks_gain, kw_gain, b_norm, b_w_in, b_b_gate, b_q_gain, b_w_out, f_norm, f_w_up, f_conv_w, f_conv_b, f_w_down):
    B, T, D = x.shape
    hd = N_HEADS * HEAD_DIM
    G = N_KV_GROUPS
    n_a = a_norm.shape[0]
    n_b = b_norm.shape[0]
    depth = n_a + n_b
    n_slc = T // SLC_LEN
    n_cmp = (T - CMP_LEN) // CMP_STRIDE + 1
    assert T % 1024 == 0 and n_slc <= LANES and hd == 1024 and D == 1024

    half = ROT_DIM // 2
    inv = ROPE_THETA ** (-jnp.arange(half, dtype=F32) * (2.0 / ROT_DIM))
    cos_t, sin_t = _rope_tables(positions, inv)
    end_pos = positions[:, CMP_LEN - 1::CMP_STRIDE]
    end_pos = jnp.pad(end_pos, ((0, 0), (0, T // CMP_STRIDE - n_cmp)))
    cos_c, sin_c = _rope_tables(end_pos, inv)

    cs = jnp.arange(T // CMP_STRIDE) * CMP_STRIDE
    ss = jnp.arange(LANES) * SLC_LEN
    overlap_t = (jnp.maximum(jnp.minimum(cs[None, :] + CMP_LEN, ss[:, None] + SLC_LEN)
                             - jnp.maximum(cs[None, :], ss[:, None]), 0).astype(F32) / CMP_LEN).astype(BF16)

    h = x
    kv = None
    for layer in range(depth):
        if layer < n_a:
            i = layer
            w = jnp.concatenate([a_w_in[i][:, :2 * hd], _pad_cols(a_w_in[i][:, 3 * hd:], LANES)], axis=1).astype(BF16)
            wvt = a_w_in[i][:, 2 * hd:3 * hd].T.astype(BF16)
            bf = _pad_cols(a_b_f[i].reshape(1, -1), LANES)
            q, k, vt, cf = _fox_inproj(h, a_norm[i].reshape(1, D), w, wvt, bf, _lane_vec(a_q_gain[i]),
                                       _lane_vec(a_k_gain[i]))
            mix, w_out = [_fox_attn(q, k, cf, vt)], a_w_out[i]
        else:
            i = layer - n_a
            kc_all, vct_all, ks, kw, vst, vwt = kv
            q, gates_t = _nsa_inproj(h, b_norm[i].reshape(1, D), b_w_in[i][:, :hd].astype(BF16),
                                     b_w_in[i][:, hd:].T.astype(BF16), b_b_gate[i].reshape(-1, 1),
                                     _lane_vec(b_q_gain[i]), cos_t, sin_t)
            o_c, sel = _nsa_cmp(q, kc_all, vct_all, overlap_t, gates_t, n_slc)
            o_s = _nsa_slc(q, sel, ks, vst, gates_t)
            o_w = _nsa_win(q, kw, vwt, gates_t)
            mix, w_out = [o_c, o_s, o_w], b_w_out[i]
        h = _mix_out_conv_ffn(h, mix, w_out.astype(BF16), f_norm[layer].reshape(1, D), f_w_up[layer].astype(BF16),
                              f_conv_w[layer], f_conv_b[layer].reshape(1, -1), f_w_down[layer].astype(BF16))
        if layer == n_a - 1:
            w6 = kv_w.reshape(D, 6, G, HEAD_DIM)
            raw_cols = w6[:, 0:2].reshape(D, 2 * G * HEAD_DIM)
            wk = w6[:, (2, 4)]
            dup_cols = jnp.concatenate([wk, wk], axis=-1).reshape(D, 2 * G * LANES)
            wkv = jnp.concatenate([raw_cols, dup_cols], axis=1).astype(BF16)
            wvt = w6[:, (3, 5)].reshape(D, 2 * G * HEAD_DIM).T.astype(BF16)
            craw, ks, kw, vst, vwt = _kvproj(h, kv_norm.reshape(1, D), wkv, wvt, _lane_vec(ks_gain),
                                             _lane_vec(kw_gain), cos_t, sin_t)
            r = craw.reshape(B, T, 2, G, HEAD_DIM).transpose(0, 2, 3, 1, 4).reshape(B, 2, G, T // CMP_STRIDE, CMP_STRIDE * HEAD_DIM)
            w1 = jnp.stack([kc_w1, vc_w1]).astype(BF16)
            pe = jnp.stack([kc_pe.reshape(1, -1), vc_pe.reshape(1, -1)])
            pe = jnp.pad(pe, ((0, 0), (0, 7), (0, 0))).astype(BF16)
            w2 = jnp.stack([jnp.concatenate([kc_w2, kc_w2], axis=1), jnp.concatenate([vc_w2, vc_w2], axis=1)]).astype(BF16)
            w2t = jnp.stack([kc_w2.T, vc_w2.T]).astype(BF16)
            kc_all, vct_all = _compress(r, w1, pe, w2, w2t, _lane_vec(kc_gain), cos_c, sin_c, n_cmp)
            kv = (kc_all, vct_all, ks, kw, vst, vwt)
    return h
```

```python
import functools

import numpy as np
import jax
import jax.numpy as jnp
from jax import lax
from jax.experimental import pallas as pl
from jax.experimental.pallas import tpu as pltpu

F32 = jnp.float32
BF16 = jnp.bfloat16

LANES = 128
SUBLANES = 8
MXU_COLS = 256
HEAD_DIM = 64
N_HEADS = 16
N_PAIRS = N_HEADS // 2
N_KV_GROUPS = 2
HEADS_PER_GROUP = N_HEADS // N_KV_GROUPS
ROT_DIM = HEAD_DIM // 4
ROPE_THETA = 500000.0
CMP_LEN = 32
CMP_STRIDE = 16
SLC_LEN = 64
SLC_TOPK = 16
WIN = 512
CONV_W = 3
RMS_EPS = 1e-6
NEG = -1e30
FORCED_SCORE = 1e6
LOG2E = 1.4426950408889634
Q_SCALE = HEAD_DIM ** -0.5 * LOG2E
N_BIAS_PARTS = 3
ONES_ROWS = 16
ACC_ROWS = HEAD_DIM + ONES_ROWS

ROW_TILE = 512
FOX_TQ = 512
FOX_TK = ROW_TILE
SLC_TQ = 256
SLC_TK = ROW_TILE
WIN_TQ = 256
CMP_TQ = 512
CMP_SUB = 128

VMEM_LIMIT = 48 * 1024 * 1024


def _cparams(sem):
    return pltpu.CompilerParams(dimension_semantics=sem, vmem_limit_bytes=VMEM_LIMIT)


def _iota(shape, axis):
    return lax.broadcasted_iota(jnp.int32, shape, axis)


def _row_rms(x, g):
    ms = jnp.mean(x * x, axis=-1, keepdims=True)
    return x * lax.rsqrt(ms + RMS_EPS) * g


def _pair_rms(y, gain):
    lo = _iota(y.shape, 1) < HEAD_DIM
    y2 = y * y
    s_lo = jnp.sum(jnp.where(lo, y2, 0.0), axis=-1, keepdims=True)
    s_hi = jnp.sum(jnp.where(lo, 0.0, y2), axis=-1, keepdims=True)
    ms = jnp.where(lo, s_lo, s_hi) * (1.0 / HEAD_DIM)
    return y * lax.rsqrt(ms + RMS_EPS) * gain


def _pair_rope(y, cos, sin):
    lane = _iota(y.shape, 1) & (HEAD_DIM - 1)
    partner = jnp.where(lane < ROT_DIM // 2,
                        pltpu.roll(y, LANES - ROT_DIM // 2, 1),
                        pltpu.roll(y, ROT_DIM // 2, 1))
    return y * cos + partner * sin


def _dot(a, b):
    return jnp.dot(a, b, preferred_element_type=F32)


def _dot_nt(a, b):
    return lax.dot_general(a, b, (((1,), (1,)), ((), ())), preferred_element_type=F32)


def _column_max(s):
    tk, n = s.shape
    return jnp.max(jnp.max(s.reshape(tk // SUBLANES, SUBLANES, n), axis=0), axis=0, keepdims=True)


def _softmax_update_t(s, vt, m_ref, acc_ref, m_tile=None):
    tk, n = s.shape
    m_old = m_ref[...]
    if m_tile is None:
        m_tile = _column_max(s)
    m_new = jnp.maximum(m_old, m_tile)
    alpha = jnp.exp2(m_old - m_new)
    p = jnp.exp2(s - m_new).astype(BF16)
    vta = jnp.concatenate([vt, jnp.ones((ONES_ROWS, tk), vt.dtype)], axis=0)
    acc_ref[...] = alpha * acc_ref[...] + _dot(vta, p)
    m_ref[...] = m_new


def _pipelined_key_loop(n_full, produce, consume, unroll=4):
    assert unroll % 2 == 0

    def run(base, count):
        for i in range(count):
            produce(base + i + 1, (i + 1) % 2)
            consume(base + i, i % 2, False)

    produce(0, 0)

    def body(j, carry):
        run(unroll * j, unroll)
        return carry

    lax.fori_loop(0, n_full // unroll, body, 0)
    base = (n_full // unroll) * unroll
    rem = n_full - base
    step = unroll // 2
    while step >= 2:
        pl.when((rem & step) != 0)(functools.partial(run, base, step))
        base = base + (rem & step)
        step //= 2

    @pl.when((rem & 1) != 0)
    def _():
        run(base, 1)
        consume(base + 1, 1, True)

    @pl.when((rem & 1) == 0)
    def _():
        consume(base, 0, True)


def _softmax_init(m_ref, acc_ref):
    m_ref[...] = jnp.full_like(m_ref, NEG)
    acc_ref[...] = jnp.zeros_like(acc_ref)


def _softmax_result(acc_ref):
    acc = acc_ref[...]
    return acc[:HEAD_DIM, :] * (1.0 / acc[HEAD_DIM:HEAD_DIM + 1, :])


def _rope_table_kernel(pos_ref, inv_ref, c_ref, s_ref):
    ang = pos_ref[0].astype(F32) * inv_ref[...]
    c_ref[0] = jnp.cos(ang)
    s_ref[0] = jnp.sin(ang)


def _rope_tables(pos, inv):
    B, T = pos.shape
    half = inv.shape[0]
    per_row = LANES // half
    rows = T // per_row
    pos_rep = jnp.repeat(pos.reshape(B, rows, per_row), half, axis=-1)
    inv_row = jnp.tile(inv, per_row).reshape(1, LANES)
    spec = pl.BlockSpec((1, rows, LANES), lambda b: (b, 0, 0))
    cos, sin = pl.pallas_call(
        _rope_table_kernel,
        grid=(B,),
        in_specs=[spec, pl.BlockSpec((1, LANES), lambda b: (0, 0))],
        out_specs=[spec, spec],
        out_shape=[jax.ShapeDtypeStruct((B, rows, LANES), F32)] * 2,
        compiler_params=_cparams(("parallel",)),
        name="rope_tables",
    )(pos_rep, inv_row)
    cos = cos.reshape(B, T, half)
    sin = sin.reshape(B, T, half)
    rest = HEAD_DIM - 2 * half
    cos_head = jnp.concatenate([cos, cos, jnp.ones((B, T, rest), F32)], axis=-1)
    sin_head = jnp.concatenate([-sin, sin, jnp.zeros((B, T, rest), F32)], axis=-1)
    return jnp.tile(cos_head, (1, 1, LANES // HEAD_DIM)), jnp.tile(sin_head, (1, 1, LANES // HEAD_DIM))


def _fox_inproj_kernel(x_ref, g_ref, w_ref, wvt_ref, bf_ref, qg_ref, kg_ref,
                       q_ref, k_ref, vt_ref, cf_ref, carry_sc, *, tm, hd):
    ti = pl.program_id(1)
    xn = _row_rms(x_ref[0], g_ref[...]).astype(BF16)
    for j in range(hd // MXU_COLS):
        for part, (ref, gain, mul) in enumerate(((q_ref, qg_ref, Q_SCALE), (k_ref, kg_ref, 1.0))):
            c0 = part * hd + MXU_COLS * j
            y = _dot(xn, w_ref[:, c0:c0 + MXU_COLS])
            for hh in range(2):
                blk = _pair_rms(y[:, LANES * hh:LANES * (hh + 1)], gain[...]) * mul
                ref[0, :, MXU_COLS * j + LANES * hh:MXU_COLS * j + LANES * (hh + 1)] = blk.astype(BF16)
        yt = _dot_nt(wvt_ref[MXU_COLS * j:MXU_COLS * (j + 1), :], xn)
        for hh in range(2):
            vt_ref[0, 2 * j + hh, 0] = yt[LANES * hh:LANES * (hh + 1), :].astype(BF16)
    z = _dot(xn, w_ref[:, 2 * hd:2 * hd + LANES]) + bf_ref[...]
    lf = jnp.minimum(z, 0.0) - jnp.log1p(jnp.exp(-jnp.abs(z)))
    row = _iota(lf.shape, 0)
    sh = 1
    while sh < tm:
        lf = lf + jnp.where(row >= sh, pltpu.roll(lf, sh, 0), 0.0)
        sh *= 2

    @pl.when(ti == 0)
    def _():
        carry_sc[...] = jnp.zeros_like(carry_sc)

    c = lf + carry_sc[0:1, :]
    carry_sc[...] = jnp.broadcast_to(c[tm - 1:tm, :], carry_sc.shape)
    rest = c * (-LOG2E)
    lane = _iota(c.shape, 1)
    feats = jnp.zeros_like(c)
    for part in range(N_BIAS_PARTS):
        piece = rest.astype(BF16).astype(F32)
        rest = rest - piece
        placed = piece if part == 0 else pltpu.roll(piece, N_HEADS * part, 1)
        feats = jnp.where((lane >= N_HEADS * part) & (lane < N_HEADS * (part + 1)), placed, feats)
    cf_ref[0] = feats.astype(BF16)


def _fox_inproj(x, g, w, wvt, bf, qg, kg, tm=ROW_TILE):
    B, T, D = x.shape
    hd = N_HEADS * HEAD_DIM
    assert N_BIAS_PARTS * N_HEADS <= LANES
    act = pl.BlockSpec((1, tm, hd), lambda b, t: (b, t, 0))
    vec = lambda n: pl.BlockSpec((1, n), lambda b, t: (0, 0))
    full = lambda a: pl.BlockSpec(a.shape, lambda b, t: (0,) * a.ndim)
    return pl.pallas_call(
        functools.partial(_fox_inproj_kernel, tm=tm, hd=hd),
        grid=(B, T // tm),
        in_specs=[pl.BlockSpec((1, tm, D), lambda b, t: (b, t, 0)), vec(D), full(w), full(wvt),
                  vec(LANES), vec(LANES), vec(LANES)],
        out_specs=[act, act, pl.BlockSpec((1, N_PAIRS, 1, LANES, tm), lambda b, t: (b, 0, t, 0, 0)),
                   pl.BlockSpec((1, tm, LANES), lambda b, t: (b, t, 0))],
        out_shape=[jax.ShapeDtypeStruct((B, T, hd), BF16), jax.ShapeDtypeStruct((B, T, hd), BF16),
                   jax.ShapeDtypeStruct((B, N_PAIRS, T // tm, LANES, tm), BF16),
                   jax.ShapeDtypeStruct((B, T, LANES), BF16)],
        scratch_shapes=[pltpu.VMEM((SUBLANES, LANES), F32)],
        compiler_params=_cparams(("arbitrary", "arbitrary")),
        name="fox_inproj",
    )(x, g, w, wvt, bf, qg, kg)


def _fox_attn_kernel(q_ref, k_ref, cf_ref, vt_ref, o_ref, qa_sc, s_sc, mt_sc, m_sc, acc_sc, *, tq, tk, pairs):
    qi = pl.program_id(2)
    lane = _iota((tq, LANES), 1)
    lo = lane < HEAD_DIM
    for hh in range(2 * pairs):
        pp, half = divmod(hh, 2)
        q = q_ref[0, :, LANES * pp:LANES * (pp + 1)]
        zero = jnp.zeros_like(q)
        qa_sc[hh, :, 0:LANES] = jnp.where(lo, q, zero) if half == 0 else jnp.where(lo, zero, q)
        head = 2 * (pl.program_id(1) * pairs + pp) + half
        feat = jnp.where((lane & (N_HEADS - 1)) == head, 1.0, 0.0)
        qa_sc[hh, :, LANES:2 * LANES] = jnp.where(lane < N_BIAS_PARTS * N_HEADS, feat, 0.0).astype(BF16)
        _softmax_init(m_sc.at[hh], acc_sc.at[hh])

    def produce(ki, slot):
        start = pl.multiple_of(ki * tk, tk)
        for pp in range(pairs):
            cols = slice(LANES * pp, LANES * (pp + 1))
            ka = jnp.concatenate([k_ref[0, pl.ds(start, tk), cols], cf_ref[0, pl.ds(start, tk), :]], axis=1)
            for hh in (2 * pp, 2 * pp + 1):
                s = _dot_nt(ka, qa_sc[hh])
                s_sc[slot, hh] = s
                mt_sc[slot, hh] = _column_max(s)

    def consume(ki, slot, diag):
        for hh in range(2 * pairs):
            pp, half = divmod(hh, 2)
            s = s_sc[slot, hh]
            m_tile = mt_sc[slot, hh]
            if diag:
                s = jnp.where(_iota(s.shape, 0) <= _iota(s.shape, 1), s, NEG)
                m_tile = None
            vt = vt_ref[0, pp, ki, HEAD_DIM * half:HEAD_DIM * (half + 1), :]
            _softmax_update_t(s, vt, m_sc.at[hh], acc_sc.at[hh], m_tile)

    _pipelined_key_loop(qi, produce, consume)
    for pp in range(pairs):
        o_t = jnp.concatenate([_softmax_result(acc_sc.at[2 * pp + half]) for half in range(2)], axis=0)
        o_ref[0, :, LANES * pp:LANES * (pp + 1)] = o_t.T.astype(BF16)


def _fox_attn(q, k, cf, vt, tq=FOX_TQ, tk=FOX_TK, pairs=2):
    B, T, hd = q.shape
    assert tq == tk
    nk = T // tk
    heads = 2 * pairs
    width = pairs * LANES
    seq = pl.BlockSpec((1, T, width), lambda b, p, qi: (b, 0, p))
    return pl.pallas_call(
        functools.partial(_fox_attn_kernel, tq=tq, tk=tk, pairs=pairs),
        grid=(B, N_PAIRS // pairs, T // tq),
        in_specs=[pl.BlockSpec((1, tq, width), lambda b, p, qi: (b, qi, p)), seq,
                  pl.BlockSpec((1, T, LANES), lambda b, p, qi: (b, 0, 0)),
                  pl.BlockSpec((1, pairs, nk, LANES, tk), lambda b, p, qi: (b, p, 0, 0, 0))],
        out_specs=pl.BlockSpec((1, tq, width), lambda b, p, qi: (b, qi, p)),
        out_shape=jax.ShapeDtypeStruct((B, T, hd), BF16),
        scratch_shapes=[pltpu.VMEM((heads, tq, 2 * LANES), BF16), pltpu.VMEM((2, heads, tk, tq), F32),
                        pltpu.VMEM((2, heads, 1, tq), F32), pltpu.VMEM((heads, 1, tq), F32),
                        pltpu.VMEM((heads, ACC_ROWS, tq), F32)],
        compiler_params=_cparams(("parallel", "parallel", "arbitrary")),
        name="fox_attn",
    )(q, k, cf, vt)


def _ffn_kernel(*refs, n_mix, tt, tf, d_ff):
    h_ref = refs[0]
    mix_refs = refs[1:1 + n_mix]
    wo_ref, g_ref, wup_ref, cw_ref, cb_ref, wd_ref, out_ref, a_sc, carry_sc = refs[1 + n_mix:]
    ti = pl.program_id(1)

    @pl.when(ti == 0)
    def _():
        carry_sc[...] = jnp.zeros_like(carry_sc)

    o = mix_refs[0][0]
    if n_mix > 1:
        o = o.astype(F32)
        for ref in mix_refs[1:]:
            o = o + ref[0].astype(F32)
        o = o.astype(BF16)
    x = h_ref[0] + _dot(o, wo_ref[...])
    xn = _row_rms(x, g_ref[...]).astype(BF16)
    row8 = _iota((SUBLANES, tf), 0)

    def conv(u, c0):
        prev8 = carry_sc[:, c0:c0 + tf]
        um1 = pltpu.roll(u, 1, 0)
        um2 = pltpu.roll(u, 2, 0)
        top1 = jnp.where(row8 == 0, prev8[7:8, :], um1[0:SUBLANES, :])
        top2 = jnp.where(row8 == 0, prev8[6:7, :], jnp.where(row8 == 1, prev8[7:8, :], um2[0:SUBLANES, :]))
        um1 = jnp.concatenate([top1, um1[SUBLANES:, :]], axis=0)
        um2 = jnp.concatenate([top2, um2[SUBLANES:, :]], axis=0)
        carry_sc[:, c0:c0 + tf] = u[tt - SUBLANES:, :]
        cw = cw_ref[:, c0:c0 + tf]
        return cb_ref[:, c0:c0 + tf] + cw[0:1, :] * um2 + cw[1:2, :] * um1 + cw[2:3, :] * u

    for f in range(d_ff // tf):
        g0 = f * tf
        cg = conv(_dot(xn, wup_ref[:, g0:g0 + tf]), g0)
        cv = conv(_dot(xn, wup_ref[:, d_ff + g0:d_ff + g0 + tf]), d_ff + g0)
        a_sc[:, g0:g0 + tf] = (cg * jax.nn.sigmoid(cg) * cv).astype(BF16)
    out_ref[0] = x + _dot(a_sc[...], wd_ref[...])


def _mix_out_conv_ffn(h, mix, w_out, g, w_up, conv_w, conv_b, w_down, tt=ROW_TILE, tf=MXU_COLS):
    B, T, D = h.shape
    d_ff = w_down.shape[0]
    act = lambda c: pl.BlockSpec((1, tt, c), lambda b, t: (b, t, 0))
    resident = lambda a: pl.BlockSpec(a.shape, lambda b, t: (0,) * a.ndim, pipeline_mode=pl.Buffered(1))
    return pl.pallas_call(
        functools.partial(_ffn_kernel, n_mix=len(mix), tt=tt, tf=tf, d_ff=d_ff),
        grid=(B, T // tt),
        in_specs=[act(D)] + [act(m.shape[-1]) for m in mix]
                 + [resident(w_out), resident(g), resident(w_up), resident(conv_w), resident(conv_b), resident(w_down)],
        out_specs=act(D),
        out_shape=jax.ShapeDtypeStruct((B, T, D), F32),
        scratch_shapes=[pltpu.VMEM((tt, d_ff), BF16), pltpu.VMEM((SUBLANES, 2 * d_ff), F32)],
        compiler_params=_cparams(("arbitrary", "arbitrary")),
        name="conv_ffn",
    )(h, *mix, w_out, g, w_up, conv_w, conv_b, w_down)


def _kvproj_kernel(h_ref, g_ref, w_ref, wvt_ref, ksg_ref, kwg_ref, c_ref, s_ref,
                   craw_ref, ks_ref, kw_ref, vst_ref, vwt_ref, *, tm):
    xn = _row_rms(h_ref[0], g_ref[...]).astype(BF16)
    cos = c_ref[0]
    sin = s_ref[0]
    raw_cols = craw_ref.shape[-1]
    craw_ref[0] = _dot(xn, w_ref[:, 0:raw_cols])
    for idx, (ref, gain) in enumerate(((ks_ref, ksg_ref), (kw_ref, kwg_ref))):
        c0 = raw_cols + N_KV_GROUPS * LANES * idx
        y = _dot(xn, w_ref[:, c0:c0 + N_KV_GROUPS * LANES])
        for grp in range(N_KV_GROUPS):
            blk = _pair_rope(_pair_rms(y[:, LANES * grp:LANES * (grp + 1)], gain[...]), cos, sin)
            ref[0, grp] = blk.astype(BF16)
    yt = _dot_nt(wvt_ref[...], xn).astype(BF16)
    for grp in range(N_KV_GROUPS):
        vst_ref[0, grp, 0] = yt[HEAD_DIM * grp:HEAD_DIM * (grp + 1), :]
        r0 = HEAD_DIM * (N_KV_GROUPS + grp)
        for c in range(tm // WIN_TQ):
            vwt_ref[0, grp, c] = yt[r0:r0 + HEAD_DIM, WIN_TQ * c:WIN_TQ * (c + 1)]


def _kvproj(h, g, w, wvt, ksg, kwg, cos, sin, tm=ROW_TILE):
    B, T, D = h.shape
    G = N_KV_GROUPS
    vec = lambda n: pl.BlockSpec((1, n), lambda b, t: (0, 0))
    full = lambda a: pl.BlockSpec(a.shape, lambda b, t: (0,) * a.ndim)
    tab = pl.BlockSpec((1, tm, LANES), lambda b, t: (b, t, 0))
    dup = pl.BlockSpec((1, G, tm, LANES), lambda b, t: (b, 0, t, 0))
    dup_shape = jax.ShapeDtypeStruct((B, G, T, LANES), BF16)
    nw = tm // WIN_TQ
    raw_cols = 2 * G * HEAD_DIM
    return pl.pallas_call(
        functools.partial(_kvproj_kernel, tm=tm),
        grid=(B, T // tm),
        in_specs=[pl.BlockSpec((1, tm, D), lambda b, t: (b, t, 0)), vec(D), full(w), full(wvt),
                  vec(LANES), vec(LANES), tab, tab],
        out_specs=[pl.BlockSpec((1, tm, raw_cols), lambda b, t: (b, t, 0)), dup, dup,
                   pl.BlockSpec((1, G, 1, HEAD_DIM, tm), lambda b, t: (b, 0, t, 0, 0)),
                   pl.BlockSpec((1, G, nw, HEAD_DIM, WIN_TQ), lambda b, t: (b, 0, t, 0, 0))],
        out_shape=[jax.ShapeDtypeStruct((B, T, raw_cols), F32), dup_shape, dup_shape,
                   jax.ShapeDtypeStruct((B, G, T // tm, HEAD_DIM, tm), BF16),
                   jax.ShapeDtypeStruct((B, G, T // WIN_TQ, HEAD_DIM, WIN_TQ), BF16)],
        compiler_params=_cparams(("parallel", "parallel")),
        name="kvproj",
    )(h, g, w, wvt, ksg, kwg, cos, sin)


def _compress_kernel(r_ref, w1_ref, pe_ref, w2_ref, w2t_ref, gain_ref, c_ref, s_ref, kc_ref, vct_ref, *, n_cmp):
    r = r_ref[0, 0, 0].astype(BF16)
    half = r.shape[1]
    a = _dot(r, w1_ref[0, :half, :])
    b = _dot(r, w1_ref[0, half:, :])
    peb = _dot(pe_ref[0], w1_ref[0])[0:1, :]
    rows = r.shape[0]
    hid = a + pltpu.roll(b, rows - 1, 0) + peb
    act = jax.nn.gelu(hid).astype(BF16)
    y = _dot(act, w2_ref[0])
    yk = _pair_rope(_pair_rms(y, gain_ref[...]), c_ref[0], s_ref[0])
    kc_ref[0, 0, 0] = jnp.where(_iota(y.shape, 0) < n_cmp, yk, 0.0).astype(BF16)
    yt = _dot_nt(w2t_ref[0], act)
    vct_ref[0, 0, 0] = jnp.where(_iota(yt.shape, 1) < n_cmp, yt, 0.0).astype(BF16)


def _compress(r, w1, pe, w2, w2t, gain, cos, sin, n_cmp):
    B, _, G, rows, width = r.shape
    per_kv = lambda a: pl.BlockSpec((1,) + a.shape[1:], lambda b, kv, g: (kv,) + (0,) * (a.ndim - 1))
    return pl.pallas_call(
        functools.partial(_compress_kernel, n_cmp=n_cmp),
        grid=(B, 2, G),
        in_specs=[pl.BlockSpec((1, 1, 1, rows, width), lambda b, kv, g: (b, kv, g, 0, 0)),
                  per_kv(w1), per_kv(pe), per_kv(w2), per_kv(w2t),
                  pl.BlockSpec((1, LANES), lambda b, kv, g: (0, 0)),
                  pl.BlockSpec((1, rows, LANES), lambda b, kv, g: (b, 0, 0)),
                  pl.BlockSpec((1, rows, LANES), lambda b, kv, g: (b, 0, 0))],
        out_specs=[pl.BlockSpec((1, 1, 1, rows, LANES), lambda b, kv, g: (b, kv, g, 0, 0)),
                   pl.BlockSpec((1, 1, 1, HEAD_DIM, rows), lambda b, kv, g: (b, kv, g, 0, 0))],
        out_shape=[jax.ShapeDtypeStruct((B, 2, G, rows, LANES), BF16),
                   jax.ShapeDtypeStruct((B, 2, G, HEAD_DIM, rows), BF16)],
        compiler_params=_cparams(("parallel", "parallel", "parallel")),
        name="compress",
    )(r, w1, pe, w2, w2t, gain, cos, sin)


def _nsa_inproj_kernel(h_ref, g_ref, w_ref, wgt_ref, bgt_ref, qg_ref, c_ref, s_ref, q_ref, gate_ref, *, hd):
    xn = _row_rms(h_ref[0], g_ref[...]).astype(BF16)
    cos = c_ref[0]
    sin = s_ref[0]
    for j in range(hd // MXU_COLS):
        y = _dot(xn, w_ref[:, MXU_COLS * j:MXU_COLS * (j + 1)])
        for hh in range(2):
            blk = _pair_rope(_pair_rms(y[:, LANES * hh:LANES * (hh + 1)], qg_ref[...]), cos, sin) * Q_SCALE
            q_ref[0, :, MXU_COLS * j + LANES * hh:MXU_COLS * j + LANES * (hh + 1)] = blk.astype(BF16)
    gate_ref[0] = jax.nn.sigmoid(_dot_nt(wgt_ref[...], xn) + bgt_ref[...])


def _nsa_inproj(h, g, w, wgt, bgt, qg, cos, sin, tm=ROW_TILE):
    B, T, D = h.shape
    hd = N_HEADS * HEAD_DIM
    vec = lambda n: pl.BlockSpec((1, n), lambda b, t: (0, 0))
    full = lambda a: pl.BlockSpec(a.shape, lambda b, t: (0,) * a.ndim)
    tab = pl.BlockSpec((1, tm, LANES), lambda b, t: (b, t, 0))
    return pl.pallas_call(
        functools.partial(_nsa_inproj_kernel, hd=hd),
        grid=(B, T // tm),
        in_specs=[pl.BlockSpec((1, tm, D), lambda b, t: (b, t, 0)), vec(D), full(w), full(wgt), full(bgt),
                  vec(LANES), tab, tab],
        out_specs=[pl.BlockSpec((1, tm, hd), lambda b, t: (b, t, 0)),
                   pl.BlockSpec((1, wgt.shape[0], tm), lambda b, t: (b, 0, t))],
        out_shape=[jax.ShapeDtypeStruct((B, T, hd), BF16), jax.ShapeDtypeStruct((B, wgt.shape[0], T), F32)],
        compiler_params=_cparams(("parallel", "parallel")),
        name="nsa_inproj",
    )(h, g, w, wgt, bgt, qg, cos, sin)


def _stack_heads(q, dst_ref, tq):
    lo = _iota((tq, LANES), 1) < HEAD_DIM
    zero = jnp.zeros((tq, LANES), q.dtype)
    for j in range(HEADS_PER_GROUP // 2):
        x = q[:, LANES * j:LANES * (j + 1)]
        dst_ref[(2 * j) * tq:(2 * j + 1) * tq, 0:LANES] = jnp.where(lo, x, zero)
        dst_ref[(2 * j + 1) * tq:(2 * j + 2) * tq, 0:LANES] = jnp.where(lo, zero, x)


def _unstack_heads_t(o_t, gate, o_ref, tq, row0=0, col0=0):
    for j in range(HEADS_PER_GROUP // 2):
        halves = [o_t[:, h * tq:(h + 1) * tq] * gate[h:h + 1, :] for h in (2 * j, 2 * j + 1)]
        lanes = slice(col0 + LANES * j, col0 + LANES * (j + 1))
        o_ref[0, row0:row0 + tq, lanes] = jnp.concatenate(halves, axis=0).T.astype(o_ref.dtype)


def _cmp_query_mask_features(sub):
    j = np.arange(sub)
    v = (j // CMP_STRIDE) + ((j % CMP_STRIDE) == CMP_STRIDE - 1)
    feat = np.zeros((sub, LANES), np.float32)
    feat[j, v] = NEG
    return jnp.asarray(feat, BF16)


def _nsa_cmp_kernel(q_ref, kc_ref, vct_ref, ovt_ref, gate_ref, qf_ref, oc_ref, sel_ref, qs_sc, *, tq, sub, n_slc, top_k, n_levels):
    qi = pl.program_id(2)
    n_sub = tq // sub
    n_feats = sub // CMP_STRIDE + 1
    qf = qf_ref[...]
    for u in range(n_sub):
        _stack_heads(q_ref[0, u * sub:(u + 1) * sub, :], qs_sc.at[u], sub)
        for h in range(HEADS_PER_GROUP):
            qs_sc[u, h * sub:(h + 1) * sub, LANES:2 * LANES] = qf
    n_rows = kc_ref.shape[3]
    level_rows = n_rows // n_levels
    n_chunks = 2
    cw = HEADS_PER_GROUP * sub // n_chunks
    n_vis = (qi * tq + tq - 1 - (CMP_LEN - 1)) // CMP_STRIDE + 1

    def attend(rows, u):
        base = (qi * tq + u * sub) // CMP_STRIDE - 2
        n_minus_v = _iota((rows, LANES), 0) - _iota((rows, LANES), 1)
        kf = jnp.where(_iota((rows, LANES), 1) < n_feats, jnp.where(n_minus_v > base, 1.0, 0.0), 0.0)
        ka = jnp.concatenate([kc_ref[0, 0, 0, :rows, :], kf.astype(BF16)], axis=1)
        vct = vct_ref[0, 0, 0, :, :rows]
        scores = [_dot_nt(ka, qs_sc[u, c * cw:(c + 1) * cw, :]) for c in range(n_chunks)]
        outs = []
        p_sum = None
        for s in scores:
            mx = jnp.max(s, axis=0, keepdims=True)
            e = jnp.exp2(s - mx)
            inv = jnp.where(mx > 0.5 * NEG, 1.0 / jnp.maximum(jnp.sum(e, axis=0, keepdims=True), 1.0), 0.0)
            p = e * inv
            outs.append(_dot(vct, p.astype(BF16)))
            for h in range(cw // sub):
                term = p[:, h * sub:(h + 1) * sub]
                p_sum = term if p_sum is None else p_sum + term
        _unstack_heads_t(jnp.concatenate(outs, axis=1), gate_ref[0, :, u * sub:(u + 1) * sub], oc_ref, sub, u * sub)
        p_hi = p_sum.astype(BF16)
        p_lo = (p_sum - p_hi.astype(F32)).astype(BF16)
        return _dot(ovt_ref[:, :rows], p_hi) + _dot(ovt_ref[:, :rows], p_lo)

    def select(imp, u, n_blk):
        imp = imp[:n_blk, :]
        jb = _iota(imp.shape, 0)
        qp = qi * tq + u * sub + _iota(imp.shape, 1)
        cur = qp // SLC_LEN
        forced = (jb == 0) | (jb == cur) | (jb == cur - 1)
        score = jnp.where(jb * SLC_LEN <= qp, jnp.where(forced, FORCED_SCORE, imp), NEG)
        live = jb < n_slc
        score = jnp.where(live, score, -jnp.inf)
        for _ in range(top_k):
            mx = jnp.max(score, axis=0, keepdims=True)
            idx = jnp.min(jnp.where(score == mx, jb, LANES), axis=0, keepdims=True)
            score = jnp.where(jb == idx, -jnp.inf, score)
        bias = jnp.where(live, jnp.where(score == -jnp.inf, 0.0, NEG), NEG)
        if n_blk < LANES:
            bias = jnp.concatenate([bias, jnp.full((LANES - n_blk, sub), NEG, F32)], axis=0)
        sel_ref[0, 0, u * sub:(u + 1) * sub, :] = bias.T.astype(BF16)

    def tile(rows):
        n_blk = rows * CMP_STRIDE // SLC_LEN
        n_blk = n_blk if (n_blk >= top_k and n_blk % SUBLANES == 0) else LANES
        imps = [attend(rows, u) for u in range(n_sub)]
        for u in range(n_sub):
            select(imps[u], u, n_blk)

    for level in range(n_levels):
        lo_rows, hi_rows = level * level_rows, (level + 1) * level_rows
        cond = (n_vis <= hi_rows) if level == 0 else ((n_vis > lo_rows) & (n_vis <= hi_rows))
        pl.when(cond)(functools.partial(tile, hi_rows))


def _gate_spec(branch, tq):
    return pl.BlockSpec((1, HEADS_PER_GROUP, tq), lambda b, g, qi: (b, branch * N_KV_GROUPS + g, qi))


def _nsa_cmp(q, kc_all, vct_all, overlap_t, gates_t, n_slc, tq=CMP_TQ):
    B, T, hd = q.shape
    G = N_KV_GROUPS
    rows = kc_all.shape[3]
    gw = hd // G
    top_k = min(SLC_TOPK, n_slc)
    assert tq % CMP_SUB == 0 and CMP_SUB % SLC_LEN == 0 and CMP_SUB % LANES == 0
    n_levels = 4 if rows % (4 * LANES) == 0 else 1
    return pl.pallas_call(
        functools.partial(_nsa_cmp_kernel, tq=tq, sub=CMP_SUB, n_slc=n_slc, top_k=top_k, n_levels=n_levels),
        grid=(B, G, T // tq),
        in_specs=[pl.BlockSpec((1, tq, gw), lambda b, g, qi: (b, qi, g)),
                  pl.BlockSpec((1, 1, 1, rows, LANES), lambda b, g, qi: (b, 0, g, 0, 0)),
                  pl.BlockSpec((1, 1, 1, HEAD_DIM, rows), lambda b, g, qi: (b, 1, g, 0, 0)),
                  pl.BlockSpec(overlap_t.shape, lambda b, g, qi: (0, 0)), _gate_spec(0, tq),
                  pl.BlockSpec((CMP_SUB, LANES), lambda b, g, qi: (0, 0))],
        out_specs=[pl.BlockSpec((1, tq, gw), lambda b, g, qi: (b, qi, g)),
                   pl.BlockSpec((1, 1, tq, LANES), lambda b, g, qi: (b, g, qi, 0))],
        out_shape=[jax.ShapeDtypeStruct((B, T, hd), BF16), jax.ShapeDtypeStruct((B, G, T, LANES), BF16)],
        scratch_shapes=[pltpu.VMEM((tq // CMP_SUB, HEADS_PER_GROUP * CMP_SUB, 2 * LANES), BF16)],
        compiler_params=_cparams(("parallel", "parallel", "parallel")),
        name="nsa_cmp",
    )(q, kc_all, vct_all, overlap_t, gates_t, _cmp_query_mask_features(CMP_SUB))


def _nsa_slc_kernel(q_ref, sel_ref, k_ref, oh_ref, vt_ref, gate_ref, o_ref, qa_sc, s_sc, mt_sc, m_sc, acc_sc, *, tq, tk, n_chunks):
    qi = pl.program_id(2)
    last = (qi * tq) // tk
    _stack_heads(q_ref[0], qa_sc, tq)
    sel = sel_ref[0, 0]
    for h in range(HEADS_PER_GROUP):
        qa_sc[h * tq:(h + 1) * tq, LANES:2 * LANES] = sel
    _softmax_init(m_sc, acc_sc)
    cw = HEADS_PER_GROUP * tq // n_chunks

    def produce(ki, slot):
        start = pl.multiple_of(ki * tk, tk)
        ka = jnp.concatenate([k_ref[0, 0, pl.ds(start, tk), :], oh_ref[pl.ds(start, tk), :]], axis=1)
        for c in range(n_chunks):
            s = _dot_nt(ka, qa_sc[c * cw:(c + 1) * cw, :])
            s_sc[slot, :, c * cw:(c + 1) * cw] = s
            mt_sc[slot, :, c * cw:(c + 1) * cw] = _column_max(s)

    def consume(ki, slot, diag):
        vt = vt_ref[0, 0, ki]
        for c in range(n_chunks):
            cols = slice(c * cw, (c + 1) * cw)
            s = s_sc[slot, :, cols]
            m_tile = mt_sc[slot, :, cols]
            if diag:
                kpos = ki * tk + _iota(s.shape, 0)
                qpos = qi * tq + (_iota(s.shape, 1) & (tq - 1))
                s = jnp.where(kpos <= qpos, s, NEG)
                m_tile = None
            _softmax_update_t(s, vt, m_sc.at[:, cols], acc_sc.at[:, cols], m_tile)

    _pipelined_key_loop(last, produce, consume)
    _unstack_heads_t(_softmax_result(acc_sc), gate_ref[0], o_ref, tq)


def _nsa_slc(q, sel, ks, vst, gates_t, tq=SLC_TQ, tk=SLC_TK):
    B, T, hd = q.shape
    G = N_KV_GROUPS
    gw = hd // G
    cols = HEADS_PER_GROUP * tq
    assert tk % tq == 0
    onehot = (jnp.arange(T)[:, None] // SLC_LEN == jnp.arange(LANES)[None, :]).astype(BF16)
    return pl.pallas_call(
        functools.partial(_nsa_slc_kernel, tq=tq, tk=tk, n_chunks=4),
        grid=(B, G, T // tq),
        in_specs=[pl.BlockSpec((1, tq, gw), lambda b, g, qi: (b, qi, g)),
                  pl.BlockSpec((1, 1, tq, LANES), lambda b, g, qi: (b, g, qi, 0)),
                  pl.BlockSpec((1, 1, T, LANES), lambda b, g, qi: (b, g, 0, 0)),
                  pl.BlockSpec((T, LANES), lambda b, g, qi: (0, 0)),
                  pl.BlockSpec((1, 1, T // tk, HEAD_DIM, tk), lambda b, g, qi: (b, g, 0, 0, 0)), _gate_spec(1, tq)],
        out_specs=pl.BlockSpec((1, tq, gw), lambda b, g, qi: (b, qi, g)),
        out_shape=jax.ShapeDtypeStruct((B, T, hd), BF16),
        scratch_shapes=[pltpu.VMEM((cols, 2 * LANES), BF16), pltpu.VMEM((2, tk, cols), F32),
                        pltpu.VMEM((2, 1, cols), F32), pltpu.VMEM((1, cols), F32),
                        pltpu.VMEM((ACC_ROWS, cols), F32)],
        compiler_params=_cparams(("parallel", "parallel", "arbitrary")),
        name="nsa_slc",
    )(q, sel, ks, onehot, vst, gates_t)


def _nsa_win_kernel(q_ref, k_ref, vt_ref, gate_ref, o_ref, qs_sc, s_sc, m_sc, acc_sc, *, tq, n_back, n_chunks):
    qi = pl.program_id(1)
    gw = HEADS_PER_GROUP * HEAD_DIM
    for g in range(N_KV_GROUPS):
        _stack_heads(q_ref[0, :, gw * g:gw * (g + 1)], qs_sc.at[g], tq)
    _softmax_init(m_sc, acc_sc)
    cw = HEADS_PER_GROUP * tq // n_chunks

    def produce(ti, slot):
        start = pl.multiple_of(ti * tq, tq)
        for g in range(N_KV_GROUPS):
            k = k_ref[0, g, pl.ds(start, tq), :]
            for c in range(n_chunks):
                s_sc[slot, g, :, c * cw:(c + 1) * cw] = _dot_nt(k, qs_sc[g, c * cw:(c + 1) * cw, :])

    def consume(ti, slot, kind):
        for g in range(N_KV_GROUPS):
            vt = vt_ref[0, g, ti]
            for c in range(n_chunks):
                cols = slice(c * cw, (c + 1) * cw)
                s = s_sc[slot, g, :, cols]
                r = _iota(s.shape, 0)
                q_in_tile = _iota(s.shape, 1) & (tq - 1)
                if kind == "oldest":
                    s = jnp.where(r > q_in_tile, s, NEG)
                elif kind == "diag":
                    s = jnp.where(r <= q_in_tile, s, NEG)
                _softmax_update_t(s, vt, m_sc.at[g, :, cols], acc_sc.at[g, :, cols])

    def kind_of(back):
        return "oldest" if back == n_back else ("diag" if back == 0 else "full")

    for first in range(n_back + 1):
        cond = (qi >= n_back) if first == n_back else (qi == first)

        @pl.when(cond)
        def _(first=first):
            backs = list(range(first, -1, -1))
            produce(qi - backs[0], 0)
            for n, back in enumerate(backs):
                if n + 1 < len(backs):
                    produce(qi - backs[n + 1], (n + 1) % 2)
                consume(qi - back, n % 2, kind_of(back))

    for g in range(N_KV_GROUPS):
        gate = gate_ref[0, HEADS_PER_GROUP * g:HEADS_PER_GROUP * (g + 1), :]
        _unstack_heads_t(_softmax_result(acc_sc.at[g]), gate, o_ref, tq, col0=gw * g)


def _nsa_win(q, kw, vwt, gates_t, tq=WIN_TQ):
    B, T, hd = q.shape
    G = N_KV_GROUPS
    cols = HEADS_PER_GROUP * tq
    return pl.pallas_call(
        functools.partial(_nsa_win_kernel, tq=tq, n_back=WIN // tq, n_chunks=4),
        grid=(B, T // tq),
        in_specs=[pl.BlockSpec((1, tq, hd), lambda b, qi: (b, qi, 0)),
                  pl.BlockSpec((1, G, T, LANES), lambda b, qi: (b, 0, 0, 0)),
                  pl.BlockSpec((1, G, T // tq, HEAD_DIM, tq), lambda b, qi: (b, 0, 0, 0, 0)),
                  pl.BlockSpec((1, N_HEADS, tq), lambda b, qi: (b, 2, qi))],
        out_specs=pl.BlockSpec((1, tq, hd), lambda b, qi: (b, qi, 0)),
        out_shape=jax.ShapeDtypeStruct((B, T, hd), BF16),
        scratch_shapes=[pltpu.VMEM((G, cols, LANES), BF16), pltpu.VMEM((2, G, tq, cols), F32),
                        pltpu.VMEM((G, 1, cols), F32), pltpu.VMEM((G, ACC_ROWS, cols), F32)],
        compiler_params=_cparams(("parallel", "arbitrary")),
        name="nsa_win",
    )(q, kw, vwt, gates_t)


def _pad_cols(a, n):
    return jnp.pad(a, ((0, 0), (0, n - a.shape[1])))


def _lane_vec(v):
    return jnp.tile(v.astype(F32), 2).reshape(1, LANES)


def kernel(x, positions, a_norm, a_w_in, a_b_f, a_q_gain, a_k_gain, a_w_out, kv_norm, kv_w, kc_pe, vc_pe, kc_w1, kc_w2, vc_w1, vc_w2, kc_gain, ks_gain, kw_gain, b_norm, b_w_in, b_b_gate, b_q_gain, b_w_out, f_norm, f_w_up, f_conv_w, f_conv_b, f_w_down):
    B, T, D = x.shape
    hd = N_HEADS * HEAD_DIM
    G = N_KV_GROUPS
    n_a = a_norm.shape[0]
    n_b = b_norm.shape[0]
    depth = n_a + n_b
    n_slc = T // SLC_LEN
    n_cmp = (T - CMP_LEN) // CMP_STRIDE + 1
    assert T % 1024 == 0 and n_slc <= LANES and hd == 1024 and D == 1024

    half = ROT_DIM // 2
    inv = ROPE_THETA ** (-jnp.arange(half, dtype=F32) * (2.0 / ROT_DIM))
    cos_t, sin_t = _rope_tables(positions, inv)
    end_pos = positions[:, CMP_LEN - 1::CMP_STRIDE]
    end_pos = jnp.pad(end_pos, ((0, 0), (0, T // CMP_STRIDE - n_cmp)))
    cos_c, sin_c = _rope_tables(end_pos, inv)

    cs = jnp.arange(T // CMP_STRIDE) * CMP_STRIDE
    ss = jnp.arange(LANES) * SLC_LEN
    overlap_t = (jnp.maximum(jnp.minimum(cs[None, :] + CMP_LEN, ss[:, None] + SLC_LEN)
                             - jnp.maximum(cs[None, :], ss[:, None]), 0).astype(F32) / CMP_LEN).astype(BF16)

    h = x
    kv = None
    for layer in range(depth):
        if layer < n_a:
            i = layer
            w = jnp.concatenate([a_w_in[i][:, :2 * hd], _pad_cols(a_w_in[i][:, 3 * hd:], LANES)], axis=1).astype(BF16)
            wvt = a_w_in[i][:, 2 * hd:3 * hd].T.astype(BF16)
            bf = _pad_cols(a_b_f[i].reshape(1, -1), LANES)
            q, k, vt, cf = _fox_inproj(h, a_norm[i].reshape(1, D), w, wvt, bf, _lane_vec(a_q_gain[i]),
                                       _lane_vec(a_k_gain[i]))
            mix, w_out = [_fox_attn(q, k, cf, vt)], a_w_out[i]
        else:
            i = layer - n_a
            kc_all, vct_all, ks, kw, vst, vwt = kv
            q, gates_t = _nsa_inproj(h, b_norm[i].reshape(1, D), b_w_in[i][:, :hd].astype(BF16),
                                     b_w_in[i][:, hd:].T.astype(BF16), b_b_gate[i].reshape(-1, 1),
                                     _lane_vec(b_q_gain[i]), cos_t, sin_t)
            o_c, sel = _nsa_cmp(q, kc_all, vct_all, overlap_t, gates_t, n_slc)
            o_s = _nsa_slc(q, sel, ks, vst, gates_t)
            o_w = _nsa_win(q, kw, vwt, gates_t)
            mix, w_out = [o_c, o_s, o_w], b_w_out[i]
        h = _mix_out_conv_ffn(h, mix, w_out.astype(BF16), f_norm[layer].reshape(1, D), f_w_up[layer].astype(BF16),
                              f_conv_w[layer], f_conv_b[layer].reshape(1, -1), f_w_down[layer].astype(BF16))
        if layer == n_a - 1:
            w6 = kv_w.reshape(D, 6, G, HEAD_DIM)
            raw_cols = w6[:, 0:2].reshape(D, 2 * G * HEAD_DIM)
            wk = w6[:, (2, 4)]
            dup_cols = jnp.concatenate([wk, wk], axis=-1).reshape(D, 2 * G * LANES)
            wkv = jnp.concatenate([raw_cols, dup_cols], axis=1).astype(BF16)
            wvt = w6[:, (3, 5)].reshape(D, 2 * G * HEAD_DIM).T.astype(BF16)
            craw, ks, kw, vst, vwt = _kvproj(h, kv_norm.reshape(1, D), wkv, wvt, _lane_vec(ks_gain),
                                             _lane_vec(kw_gain), cos_t, sin_t)
            r = craw.reshape(B, T, 2, G, HEAD_DIM).transpose(0, 2, 3, 1, 4).reshape(B, 2, G, T // CMP_STRIDE, CMP_STRIDE * HEAD_DIM)
            w1 = jnp.stack([kc_w1, vc_w1]).astype(BF16)
            pe = jnp.stack([kc_pe.reshape(1, -1), vc_pe.reshape(1, -1)])
            pe = jnp.pad(pe, ((0, 0), (0, 7), (0, 0))).astype(BF16)
            w2 = jnp.stack([jnp.concatenate([kc_w2, kc_w2], axis=1), jnp.concatenate([vc_w2, vc_w2], axis=1)]).astype(BF16)
            w2t = jnp.stack([kc_w2.T, vc_w2.T]).astype(BF16)
            kc_all, vct_all = _compress(r, w1, pe, w2, w2t, _lane_vec(kc_gain), cos_c, sin_c, n_cmp)
            kv = (kc_all, vct_all, ks, kw, vst, vwt)
    return h
```

```python
import functools

import numpy as np
import jax
import jax.numpy as jnp
from jax import lax
from jax.experimental import pallas as pl
from jax.experimental.pallas import tpu as pltpu

F32 = jnp.float32
BF16 = jnp.bfloat16

LANES = 128
SUBLANES = 8
MXU_COLS = 256
HEAD_DIM = 64
N_HEADS = 16
N_PAIRS = N_HEADS // 2
N_KV_GROUPS = 2
HEADS_PER_GROUP = N_HEADS // N_KV_GROUPS
ROT_DIM = HEAD_DIM // 4
ROPE_THETA = 500000.0
CMP_LEN = 32
CMP_STRIDE = 16
SLC_LEN = 64
SLC_TOPK = 16
WIN = 512
CONV_W = 3
RMS_EPS = 1e-6
NEG = -1e30
FORCED_SCORE = 1e6
LOG2E = 1.4426950408889634
Q_SCALE = HEAD_DIM ** -0.5 * LOG2E
N_BIAS_PARTS = 3
ONES_ROWS = 16
ACC_ROWS = HEAD_DIM + ONES_ROWS

ROW_TILE = 512
FOX_TQ = 512
FOX_TK = ROW_TILE
SLC_TQ = 256
SLC_TK = ROW_TILE
WIN_TQ = 256
CMP_TQ = 512
CMP_SUB = 128

VMEM_LIMIT = 48 * 1024 * 1024


def _cparams(sem):
    return pltpu.CompilerParams(dimension_semantics=sem, vmem_limit_bytes=VMEM_LIMIT)


def _iota(shape, axis):
    return lax.broadcasted_iota(jnp.int32, shape, axis)


def _row_rms(x, g):
    ms = jnp.mean(x * x, axis=-1, keepdims=True)
    return x * lax.rsqrt(ms + RMS_EPS) * g


def _pair_rms(y, gain):
    lo = _iota(y.shape, 1) < HEAD_DIM
    y2 = y * y
    s_lo = jnp.sum(jnp.where(lo, y2, 0.0), axis=-1, keepdims=True)
    s_hi = jnp.sum(jnp.where(lo, 0.0, y2), axis=-1, keepdims=True)
    ms = jnp.where(lo, s_lo, s_hi) * (1.0 / HEAD_DIM)
    return y * lax.rsqrt(ms + RMS_EPS) * gain


def _pair_rope(y, cos, sin):
    lane = _iota(y.shape, 1) & (HEAD_DIM - 1)
    partner = jnp.where(lane < ROT_DIM // 2,
                        pltpu.roll(y, LANES - ROT_DIM // 2, 1),
                        pltpu.roll(y, ROT_DIM // 2, 1))
    return y * cos + partner * sin


def _dot(a, b):
    return jnp.dot(a, b, preferred_element_type=F32)


def _dot_nt(a, b):
    return lax.dot_general(a, b, (((1,), (1,)), ((), ())), preferred_element_type=F32)


def _column_max(s):
    tk, n = s.shape
    return jnp.max(jnp.max(s.reshape(tk // SUBLANES, SUBLANES, n), axis=0), axis=0, keepdims=True)


def _softmax_update_t(s, vt, m_ref, acc_ref, m_tile=None):
    tk, n = s.shape
    m_old = m_ref[...]
    if m_tile is None:
        m_tile = _column_max(s)
    m_new = jnp.maximum(m_old, m_tile)
    alpha = jnp.exp2(m_old - m_new)
    p = jnp.exp2(s - m_new).astype(BF16)
    vta = jnp.concatenate([vt, jnp.ones((ONES_ROWS, tk), vt.dtype)], axis=0)
    acc_ref[...] = alpha * acc_ref[...] + _dot(vta, p)
    m_ref[...] = m_new


def _pipelined_key_loop(n_full, produce, consume, unroll=4):
    assert unroll % 2 == 0

    def run(base, count):
        for i in range(count):
            produce(base + i + 1, (i + 1) % 2)
            consume(base + i, i % 2, False)

    produce(0, 0)

    def body(j, carry):
        run(unroll * j, unroll)
        return carry

    lax.fori_loop(0, n_full // unroll, body, 0)
    base = (n_full // unroll) * unroll
    rem = n_full - base
    step = unroll // 2
    while step >= 2:
        pl.when((rem & step) != 0)(functools.partial(run, base, step))
        base = base + (rem & step)
        step //= 2

    @pl.when((rem & 1) != 0)
    def _():
        run(base, 1)
        consume(base + 1, 1, True)

    @pl.when((rem & 1) == 0)
    def _():
        consume(base, 0, True)


def _softmax_init(m_ref, acc_ref):
    m_ref[...] = jnp.full_like(m_ref, NEG)
    acc_ref[...] = jnp.zeros_like(acc_ref)


def _softmax_result(acc_ref):
    acc = acc_ref[...]
    return acc[:HEAD_DIM, :] * (1.0 / acc[HEAD_DIM:HEAD_DIM + 1, :])


def _rope_table_kernel(pos_ref, inv_ref, c_ref, s_ref):
    ang = pos_ref[0].astype(F32) * inv_ref[...]
    c_ref[0] = jnp.cos(ang)
    s_ref[0] = jnp.sin(ang)


def _rope_tables(pos, inv):
    B, T = pos.shape
    half = inv.shape[0]
    per_row = LANES // half
    rows = T // per_row
    pos_rep = jnp.repeat(pos.reshape(B, rows, per_row), half, axis=-1)
    inv_row = jnp.tile(inv, per_row).reshape(1, LANES)
    spec = pl.BlockSpec((1, rows, LANES), lambda b: (b, 0, 0))
    cos, sin = pl.pallas_call(
        _rope_table_kernel,
        grid=(B,),
        in_specs=[spec, pl.BlockSpec((1, LANES), lambda b: (0, 0))],
        out_specs=[spec, spec],
        out_shape=[jax.ShapeDtypeStruct((B, rows, LANES), F32)] * 2,
        compiler_params=_cparams(("parallel",)),
        name="rope_tables",
    )(pos_rep, inv_row)
    cos = cos.reshape(B, T, half)
    sin = sin.reshape(B, T, half)
    rest = HEAD_DIM - 2 * half
    cos_head = jnp.concatenate([cos, cos, jnp.ones((B, T, rest), F32)], axis=-1)
    sin_head = jnp.concatenate([-sin, sin, jnp.zeros((B, T, rest), F32)], axis=-1)
    return jnp.tile(cos_head, (1, 1, LANES // HEAD_DIM)), jnp.tile(sin_head, (1, 1, LANES // HEAD_DIM))


def _fox_inproj_kernel(x_ref, g_ref, w_ref, wvt_ref, bf_ref, qg_ref, kg_ref,
                       q_ref, k_ref, vt_ref, cf_ref, carry_sc, *, tm, hd):
    ti = pl.program_id(1)
    xn = _row_rms(x_ref[0], g_ref[...]).astype(BF16)
    for j in range(hd // MXU_COLS):
        for part, (ref, gain, mul) in enumerate(((q_ref, qg_ref, Q_SCALE), (k_ref, kg_ref, 1.0))):
            c0 = part * hd + MXU_COLS * j
            y = _dot(xn, w_ref[:, c0:c0 + MXU_COLS])
            for hh in range(2):
                blk = _pair_rms(y[:, LANES * hh:LANES * (hh + 1)], gain[...]) * mul
                ref[0, :, MXU_COLS * j + LANES * hh:MXU_COLS * j + LANES * (hh + 1)] = blk.astype(BF16)
        yt = _dot_nt(wvt_ref[MXU_COLS * j:MXU_COLS * (j + 1), :], xn)
        for hh in range(2):
            vt_ref[0, 2 * j + hh, 0] = yt[LANES * hh:LANES * (hh + 1), :].astype(BF16)
    z = _dot(xn, w_ref[:, 2 * hd:2 * hd + LANES]) + bf_ref[...]
    lf = jnp.minimum(z, 0.0) - jnp.log1p(jnp.exp(-jnp.abs(z)))
    row = _iota(lf.shape, 0)
    sh = 1
    while sh < tm:
        lf = lf + jnp.where(row >= sh, pltpu.roll(lf, sh, 0), 0.0)
        sh *= 2

    @pl.when(ti == 0)
    def _():
        carry_sc[...] = jnp.zeros_like(carry_sc)

    c = lf + carry_sc[0:1, :]
    carry_sc[...] = jnp.broadcast_to(c[tm - 1:tm, :], carry_sc.shape)
    rest = c * (-LOG2E)
    lane = _iota(c.shape, 1)
    feats = jnp.zeros_like(c)
    for part in range(N_BIAS_PARTS):
        piece = rest.astype(BF16).astype(F32)
        rest = rest - piece
        placed = piece if part == 0 else pltpu.roll(piece, N_HEADS * part, 1)
        feats = jnp.where((lane >= N_HEADS * part) & (lane < N_HEADS * (part + 1)), placed, feats)
    cf_ref[0] = feats.astype(BF16)


def _fox_inproj(x, g, w, wvt, bf, qg, kg, tm=ROW_TILE):
    B, T, D = x.shape
    hd = N_HEADS * HEAD_DIM
    assert N_BIAS_PARTS * N_HEADS <= LANES
    act = pl.BlockSpec((1, tm, hd), lambda b, t: (b, t, 0))
    vec = lambda n: pl.BlockSpec((1, n), lambda b, t: (0, 0))
    full = lambda a: pl.BlockSpec(a.shape, lambda b, t: (0,) * a.ndim)
    return pl.pallas_call(
        functools.partial(_fox_inproj_kernel, tm=tm, hd=hd),
        grid=(B, T // tm),
        in_specs=[pl.BlockSpec((1, tm, D), lambda b, t: (b, t, 0)), vec(D), full(w), full(wvt),
                  vec(LANES), vec(LANES), vec(LANES)],
        out_specs=[act, act, pl.BlockSpec((1, N_PAIRS, 1, LANES, tm), lambda b, t: (b, 0, t, 0, 0)),
                   pl.BlockSpec((1, tm, LANES), lambda b, t: (b, t, 0))],
        out_shape=[jax.ShapeDtypeStruct((B, T, hd), BF16), jax.ShapeDtypeStruct((B, T, hd), BF16),
                   jax.ShapeDtypeStruct((B, N_PAIRS, T // tm, LANES, tm), BF16),
                   jax.ShapeDtypeStruct((B, T, LANES), BF16)],
        scratch_shapes=[pltpu.VMEM((SUBLANES, LANES), F32)],
        compiler_params=_cparams(("arbitrary", "arbitrary")),
        name="fox_inproj",
    )(x, g, w, wvt, bf, qg, kg)


def _fox_attn_kernel(q_ref, k_ref, cf_ref, vt_ref, o_ref, qa_sc, s_sc, mt_sc, m_sc, acc_sc, *, tq, tk, pairs):
    qi = pl.program_id(2)
    lane = _iota((tq, LANES), 1)
    lo = lane < HEAD_DIM
    for hh in range(2 * pairs):
        pp, half = divmod(hh, 2)
        q = q_ref[0, :, LANES * pp:LANES * (pp + 1)]
        zero = jnp.zeros_like(q)
        qa_sc[hh, :, 0:LANES] = jnp.where(lo, q, zero) if half == 0 else jnp.where(lo, zero, q)
        head = 2 * (pl.program_id(1) * pairs + pp) + half
        feat = jnp.where((lane & (N_HEADS - 1)) == head, 1.0, 0.0)
        qa_sc[hh, :, LANES:2 * LANES] = jnp.where(lane < N_BIAS_PARTS * N_HEADS, feat, 0.0).astype(BF16)
        _softmax_init(m_sc.at[hh], acc_sc.at[hh])

    def produce(ki, slot):
        start = pl.multiple_of(ki * tk, tk)
        for pp in range(pairs):
            cols = slice(LANES * pp, LANES * (pp + 1))
            ka = jnp.concatenate([k_ref[0, pl.ds(start, tk), cols], cf_ref[0, pl.ds(start, tk), :]], axis=1)
            for hh in (2 * pp, 2 * pp + 1):
                s = _dot_nt(ka, qa_sc[hh])
                s_sc[slot, hh] = s
                mt_sc[slot, hh] = _column_max(s)

    def consume(ki, slot, diag):
        for hh in range(2 * pairs):
            pp, half = divmod(hh, 2)
            vt = vt_ref[0, pp, ki, HEAD_DIM * half:HEAD_DIM * (half + 1), :]
            if diag:
                hk = tk // 2
                top = s_sc[slot, hh, 0:hk, :]
                top = jnp.where(_iota(top.shape, 0) <= _iota(top.shape, 1), top, NEG)
                _softmax_update_t(top, vt[:, 0:hk], m_sc.at[hh], acc_sc.at[hh])
                bot = s_sc[slot, hh, hk:tk, hk:tq]
                bot = jnp.where(_iota(bot.shape, 0) <= _iota(bot.shape, 1), bot, NEG)
                _softmax_update_t(bot, vt[:, hk:tk], m_sc.at[hh, :, hk:tq], acc_sc.at[hh, :, hk:tq])
            else:
                _softmax_update_t(s_sc[slot, hh], vt, m_sc.at[hh], acc_sc.at[hh], mt_sc[slot, hh])

    _pipelined_key_loop(qi, produce, consume)
    for pp in range(pairs):
        o_t = jnp.concatenate([_softmax_result(acc_sc.at[2 * pp + half]) for half in range(2)], axis=0)
        o_ref[0, :, LANES * pp:LANES * (pp + 1)] = o_t.T.astype(BF16)


def _fox_attn(q, k, cf, vt, tq=FOX_TQ, tk=FOX_TK, pairs=2):
    B, T, hd = q.shape
    assert tq == tk
    nk = T // tk
    heads = 2 * pairs
    width = pairs * LANES
    seq = pl.BlockSpec((1, T, width), lambda b, p, qi: (b, 0, p))
    return pl.pallas_call(
        functools.partial(_fox_attn_kernel, tq=tq, tk=tk, pairs=pairs),
        grid=(B, N_PAIRS // pairs, T // tq),
        in_specs=[pl.BlockSpec((1, tq, width), lambda b, p, qi: (b, qi, p)), seq,
                  pl.BlockSpec((1, T, LANES), lambda b, p, qi: (b, 0, 0)),
                  pl.BlockSpec((1, pairs, nk, LANES, tk), lambda b, p, qi: (b, p, 0, 0, 0))],
        out_specs=pl.BlockSpec((1, tq, width), lambda b, p, qi: (b, qi, p)),
        out_shape=jax.ShapeDtypeStruct((B, T, hd), BF16),
        scratch_shapes=[pltpu.VMEM((heads, tq, 2 * LANES), BF16), pltpu.VMEM((2, heads, tk, tq), F32),
                        pltpu.VMEM((2, heads, 1, tq), F32), pltpu.VMEM((heads, 1, tq), F32),
                        pltpu.VMEM((heads, ACC_ROWS, tq), F32)],
        compiler_params=_cparams(("parallel", "parallel", "arbitrary")),
        name="fox_attn",
    )(q, k, cf, vt)


def _ffn_kernel(*refs, n_mix, tt, tf, d_ff):
    h_ref = refs[0]
    mix_refs = refs[1:1 + n_mix]
    wo_ref, g_ref, wup_ref, cw_ref, cb_ref, wd_ref, out_ref, a_sc, carry_sc = refs[1 + n_mix:]
    ti = pl.program_id(1)

    @pl.when(ti == 0)
    def _():
        carry_sc[...] = jnp.zeros_like(carry_sc)

    o = mix_refs[0][0]
    if n_mix > 1:
        o = o.astype(F32)
        for ref in mix_refs[1:]:
            o = o + ref[0].astype(F32)
        o = o.astype(BF16)
    x = h_ref[0] + _dot(o, wo_ref[...])
    xn = _row_rms(x, g_ref[...]).astype(BF16)
    row8 = _iota((SUBLANES, tf), 0)

    def conv(u, c0):
        prev8 = carry_sc[:, c0:c0 + tf]
        um1 = pltpu.roll(u, 1, 0)
        um2 = pltpu.roll(u, 2, 0)
        top1 = jnp.where(row8 == 0, prev8[7:8, :], um1[0:SUBLANES, :])
        top2 = jnp.where(row8 == 0, prev8[6:7, :], jnp.where(row8 == 1, prev8[7:8, :], um2[0:SUBLANES, :]))
        um1 = jnp.concatenate([top1, um1[SUBLANES:, :]], axis=0)
        um2 = jnp.concatenate([top2, um2[SUBLANES:, :]], axis=0)
        carry_sc[:, c0:c0 + tf] = u[tt - SUBLANES:, :]
        cw = cw_ref[:, c0:c0 + tf]
        return cb_ref[:, c0:c0 + tf] + cw[0:1, :] * um2 + cw[1:2, :] * um1 + cw[2:3, :] * u

    for f in range(d_ff // tf):
        g0 = f * tf
        cg = conv(_dot(xn, wup_ref[:, g0:g0 + tf]), g0)
        cv = conv(_dot(xn, wup_ref[:, d_ff + g0:d_ff + g0 + tf]), d_ff + g0)
        a_sc[:, g0:g0 + tf] = (cg * jax.nn.sigmoid(cg) * cv).astype(BF16)
    out_ref[0] = x + _dot(a_sc[...], wd_ref[...])


def _mix_out_conv_ffn(h, mix, w_out, g, w_up, conv_w, conv_b, w_down, tt=ROW_TILE, tf=MXU_COLS):
    B, T, D = h.shape
    d_ff = w_down.shape[0]
    act = lambda c: pl.BlockSpec((1, tt, c), lambda b, t: (b, t, 0))
    resident = lambda a: pl.BlockSpec(a.shape, lambda b, t: (0,) * a.ndim, pipeline_mode=pl.Buffered(1))
    return pl.pallas_call(
        functools.partial(_ffn_kernel, n_mix=len(mix), tt=tt, tf=tf, d_ff=d_ff),
        grid=(B, T // tt),
        in_specs=[act(D)] + [act(m.shape[-1]) for m in mix]
                 + [resident(w_out), resident(g), resident(w_up), resident(conv_w), resident(conv_b), resident(w_down)],
        out_specs=act(D),
        out_shape=jax.ShapeDtypeStruct((B, T, D), F32),
        scratch_shapes=[pltpu.VMEM((tt, d_ff), BF16), pltpu.VMEM((SUBLANES, 2 * d_ff), F32)],
        compiler_params=_cparams(("arbitrary", "arbitrary")),
        name="conv_ffn",
    )(h, *mix, w_out, g, w_up, conv_w, conv_b, w_down)


def _kvproj_kernel(h_ref, g_ref, w_ref, wvt_ref, ksg_ref, kwg_ref, c_ref, s_ref,
                   craw_ref, ks_ref, kw_ref, vst_ref, vwt_ref, *, tm):
    xn = _row_rms(h_ref[0], g_ref[...]).astype(BF16)
    cos = c_ref[0]
    sin = s_ref[0]
    raw_cols = craw_ref.shape[-1]
    craw_ref[0] = _dot(xn, w_ref[:, 0:raw_cols])
    for idx, (ref, gain) in enumerate(((ks_ref, ksg_ref), (kw_ref, kwg_ref))):
        c0 = raw_cols + N_KV_GROUPS * LANES * idx
        y = _dot(xn, w_ref[:, c0:c0 + N_KV_GROUPS * LANES])
        for grp in range(N_KV_GROUPS):
            blk = _pair_rope(_pair_rms(y[:, LANES * grp:LANES * (grp + 1)], gain[...]), cos, sin)
            ref[0, grp] = blk.astype(BF16)
    yt = _dot_nt(wvt_ref[...], xn).astype(BF16)
    for grp in range(N_KV_GROUPS):
        vst_ref[0, grp, 0] = yt[HEAD_DIM * grp:HEAD_DIM * (grp + 1), :]
        r0 = HEAD_DIM * (N_KV_GROUPS + grp)
        for c in range(tm // WIN_TQ):
            vwt_ref[0, grp, c] = yt[r0:r0 + HEAD_DIM, WIN_TQ * c:WIN_TQ * (c + 1)]


def _kvproj(h, g, w, wvt, ksg, kwg, cos, sin, tm=ROW_TILE):
    B, T, D = h.shape
    G = N_KV_GROUPS
    vec = lambda n: pl.BlockSpec((1, n), lambda b, t: (0, 0))
    full = lambda a: pl.BlockSpec(a.shape, lambda b, t: (0,) * a.ndim)
    tab = pl.BlockSpec((1, tm, LANES), lambda b, t: (b, t, 0))
    dup = pl.BlockSpec((1, G, tm, LANES), lambda b, t: (b, 0, t, 0))
    dup_shape = jax.ShapeDtypeStruct((B, G, T, LANES), BF16)
    nw = tm // WIN_TQ
    raw_cols = 2 * G * HEAD_DIM
    return pl.pallas_call(
        functools.partial(_kvproj_kernel, tm=tm),
        grid=(B, T // tm),
        in_specs=[pl.BlockSpec((1, tm, D), lambda b, t: (b, t, 0)), vec(D), full(w), full(wvt),
                  vec(LANES), vec(LANES), tab, tab],
        out_specs=[pl.BlockSpec((1, tm, raw_cols), lambda b, t: (b, t, 0)), dup, dup,
                   pl.BlockSpec((1, G, 1, HEAD_DIM, tm), lambda b, t: (b, 0, t, 0, 0)),
                   pl.BlockSpec((1, G, nw, HEAD_DIM, WIN_TQ), lambda b, t: (b, 0, t, 0, 0))],
        out_shape=[jax.ShapeDtypeStruct((B, T, raw_cols), F32), dup_shape, dup_shape,
                   jax.ShapeDtypeStruct((B, G, T // tm, HEAD_DIM, tm), BF16),
                   jax.ShapeDtypeStruct((B, G, T // WIN_TQ, HEAD_DIM, WIN_TQ), BF16)],
        compiler_params=_cparams(("parallel", "parallel")),
        name="kvproj",
    )(h, g, w, wvt, ksg, kwg, cos, sin)


def _compress_kernel(r_ref, w1_ref, pe_ref, w2_ref, w2t_ref, gain_ref, c_ref, s_ref, kc_ref, vct_ref, *, n_cmp):
    r = r_ref[0, 0, 0].astype(BF16)
    half = r.shape[1]
    a = _dot(r, w1_ref[0, :half, :])
    b = _dot(r, w1_ref[0, half:, :])
    peb = _dot(pe_ref[0], w1_ref[0])[0:1, :]
    rows = r.shape[0]
    hid = a + pltpu.roll(b, rows - 1, 0) + peb
    act = jax.nn.gelu(hid).astype(BF16)
    y = _dot(act, w2_ref[0])
    yk = _pair_rope(_pair_rms(y, gain_ref[...]), c_ref[0], s_ref[0])
    kc_ref[0, 0, 0] = jnp.where(_iota(y.shape, 0) < n_cmp, yk, 0.0).astype(BF16)
    yt = _dot_nt(w2t_ref[0], act)
    vct_ref[0, 0, 0] = jnp.where(_iota(yt.shape, 1) < n_cmp, yt, 0.0).astype(BF16)


def _compress(r, w1, pe, w2, w2t, gain, cos, sin, n_cmp):
    B, _, G, rows, width = r.shape
    per_kv = lambda a: pl.BlockSpec((1,) + a.shape[1:], lambda b, kv, g: (kv,) + (0,) * (a.ndim - 1))
    return pl.pallas_call(
        functools.partial(_compress_kernel, n_cmp=n_cmp),
        grid=(B, 2, G),
        in_specs=[pl.BlockSpec((1, 1, 1, rows, width), lambda b, kv, g: (b, kv, g, 0, 0)),
                  per_kv(w1), per_kv(pe), per_kv(w2), per_kv(w2t),
                  pl.BlockSpec((1, LANES), lambda b, kv, g: (0, 0)),
                  pl.BlockSpec((1, rows, LANES), lambda b, kv, g: (b, 0, 0)),
                  pl.BlockSpec((1, rows, LANES), lambda b, kv, g: (b, 0, 0))],
        out_specs=[pl.BlockSpec((1, 1, 1, rows, LANES), lambda b, kv, g: (b, kv, g, 0, 0)),
                   pl.BlockSpec((1, 1, 1, HEAD_DIM, rows), lambda b, kv, g: (b, kv, g, 0, 0))],
        out_shape=[jax.ShapeDtypeStruct((B, 2, G, rows, LANES), BF16),
                   jax.ShapeDtypeStruct((B, 2, G, HEAD_DIM, rows), BF16)],
        compiler_params=_cparams(("parallel", "parallel", "parallel")),
        name="compress",
    )(r, w1, pe, w2, w2t, gain, cos, sin)


def _nsa_inproj_kernel(h_ref, g_ref, w_ref, wgt_ref, bgt_ref, qg_ref, c_ref, s_ref, q_ref, gate_ref, *, hd):
    xn = _row_rms(h_ref[0], g_ref[...]).astype(BF16)
    cos = c_ref[0]
    sin = s_ref[0]
    for j in range(hd // MXU_COLS):
        y = _dot(xn, w_ref[:, MXU_COLS * j:MXU_COLS * (j + 1)])
        for hh in range(2):
            blk = _pair_rope(_pair_rms(y[:, LANES * hh:LANES * (hh + 1)], qg_ref[...]), cos, sin) * Q_SCALE
            q_ref[0, :, MXU_COLS * j + LANES * hh:MXU_COLS * j + LANES * (hh + 1)] = blk.astype(BF16)
    gate_ref[0] = jax.nn.sigmoid(_dot_nt(wgt_ref[...], xn) + bgt_ref[...])


def _nsa_inproj(h, g, w, wgt, bgt, qg, cos, sin, tm=ROW_TILE):
    B, T, D = h.shape
    hd = N_HEADS * HEAD_DIM
    vec = lambda n: pl.BlockSpec((1, n), lambda b, t: (0, 0))
    full = lambda a: pl.BlockSpec(a.shape, lambda b, t: (0,) * a.ndim)
    tab = pl.BlockSpec((1, tm, LANES), lambda b, t: (b, t, 0))
    return pl.pallas_call(
        functools.partial(_nsa_inproj_kernel, hd=hd),
        grid=(B, T // tm),
        in_specs=[pl.BlockSpec((1, tm, D), lambda b, t: (b, t, 0)), vec(D), full(w), full(wgt), full(bgt),
                  vec(LANES), tab, tab],
        out_specs=[pl.BlockSpec((1, tm, hd), lambda b, t: (b, t, 0)),
                   pl.BlockSpec((1, wgt.shape[0], tm), lambda b, t: (b, 0, t))],
        out_shape=[jax.ShapeDtypeStruct((B, T, hd), BF16), jax.ShapeDtypeStruct((B, wgt.shape[0], T), F32)],
        compiler_params=_cparams(("parallel", "parallel")),
        name="nsa_inproj",
    )(h, g, w, wgt, bgt, qg, cos, sin)


def _stack_heads(q, dst_ref, tq):
    lo = _iota((tq, LANES), 1) < HEAD_DIM
    zero = jnp.zeros((tq, LANES), q.dtype)
    for j in range(HEADS_PER_GROUP // 2):
        x = q[:, LANES * j:LANES * (j + 1)]
        dst_ref[(2 * j) * tq:(2 * j + 1) * tq, 0:LANES] = jnp.where(lo, x, zero)
        dst_ref[(2 * j + 1) * tq:(2 * j + 2) * tq, 0:LANES] = jnp.where(lo, zero, x)


def _unstack_heads_t(o_t, gate, o_ref, tq, row0=0, col0=0):
    for j in range(HEADS_PER_GROUP // 2):
        halves = [o_t[:, h * tq:(h + 1) * tq] * gate[h:h + 1, :] for h in (2 * j, 2 * j + 1)]
        lanes = slice(col0 + LANES * j, col0 + LANES * (j + 1))
        o_ref[0, row0:row0 + tq, lanes] = jnp.concatenate(halves, axis=0).T.astype(o_ref.dtype)


def _cmp_query_mask_features(sub):
    j = np.arange(sub)
    v = (j // CMP_STRIDE) + ((j % CMP_STRIDE) == CMP_STRIDE - 1)
    feat = np.zeros((sub, LANES), np.float32)
    feat[j, v] = NEG
    return jnp.asarray(feat, BF16)


def _nsa_cmp_kernel(q_ref, kc_ref, vct_ref, ovt_ref, gate_ref, qf_ref, oc_ref, sel_ref, qs_sc, *, tq, sub, n_slc, top_k, n_levels):
    qi = pl.program_id(2)
    n_sub = tq // sub
    n_feats = sub // CMP_STRIDE + 1
    qf = qf_ref[...]
    for u in range(n_sub):
        _stack_heads(q_ref[0, u * sub:(u + 1) * sub, :], qs_sc.at[u], sub)
        for h in range(HEADS_PER_GROUP):
            qs_sc[u, h * sub:(h + 1) * sub, LANES:2 * LANES] = qf
    n_rows = kc_ref.shape[3]
    level_rows = n_rows // n_levels
    n_chunks = 2
    cw = HEADS_PER_GROUP * sub // n_chunks
    n_vis = (qi * tq + tq - 1 - (CMP_LEN - 1)) // CMP_STRIDE + 1

    def attend(rows, u):
        base = (qi * tq + u * sub) // CMP_STRIDE - 2
        n_minus_v = _iota((rows, LANES), 0) - _iota((rows, LANES), 1)
        kf = jnp.where(_iota((rows, LANES), 1) < n_feats, jnp.where(n_minus_v > base, 1.0, 0.0), 0.0)
        ka = jnp.concatenate([kc_ref[0, 0, 0, :rows, :], kf.astype(BF16)], axis=1)
        vct = vct_ref[0, 0, 0, :, :rows]
        scores = [_dot_nt(ka, qs_sc[u, c * cw:(c + 1) * cw, :]) for c in range(n_chunks)]
        outs = []
        p_sum = None
        for s in scores:
            mx = jnp.max(s, axis=0, keepdims=True)
            e = jnp.exp2(s - mx)
            inv = jnp.where(mx > 0.5 * NEG, 1.0 / jnp.maximum(jnp.sum(e, axis=0, keepdims=True), 1.0), 0.0)
            p = e * inv
            outs.append(_dot(vct, p.astype(BF16)))
            for h in range(cw // sub):
                term = p[:, h * sub:(h + 1) * sub]
                p_sum = term if p_sum is None else p_sum + term
        _unstack_heads_t(jnp.concatenate(outs, axis=1), gate_ref[0, :, u * sub:(u + 1) * sub], oc_ref, sub, u * sub)
        p_hi = p_sum.astype(BF16)
        p_lo = (p_sum - p_hi.astype(F32)).astype(BF16)
        return _dot(ovt_ref[:, :rows], p_hi) + _dot(ovt_ref[:, :rows], p_lo)

    def select(imp, u, n_blk):
        imp = imp[:n_blk, :]
        jb = _iota(imp.shape, 0)
        qp = qi * tq + u * sub + _iota(imp.shape, 1)
        cur = qp // SLC_LEN
        forced = (jb == 0) | (jb == cur) | (jb == cur - 1)
        score = jnp.where(jb * SLC_LEN <= qp, jnp.where(forced, FORCED_SCORE, imp), NEG)
        live = jb < n_slc
        score = jnp.where(live, score, -jnp.inf)
        for _ in range(top_k):
            mx = jnp.max(score, axis=0, keepdims=True)
            idx = jnp.min(jnp.where(score == mx, jb, LANES), axis=0, keepdims=True)
            score = jnp.where(jb == idx, -jnp.inf, score)
        bias = jnp.where(live, jnp.where(score == -jnp.inf, 0.0, NEG), NEG)
        if n_blk < LANES:
            bias = jnp.concatenate([bias, jnp.full((LANES - n_blk, sub), NEG, F32)], axis=0)
        sel_ref[0, 0, u * sub:(u + 1) * sub, :] = bias.T.astype(BF16)

    def tile(rows):
        n_blk = rows * CMP_STRIDE // SLC_LEN
        n_blk = n_blk if (n_blk >= top_k and n_blk % SUBLANES == 0) else LANES
        imps = [attend(rows, u) for u in range(n_sub)]
        for u in range(n_sub):
            select(imps[u], u, n_blk)

    for level in range(n_levels):
        lo_rows, hi_rows = level * level_rows, (level + 1) * level_rows
        cond = (n_vis <= hi_rows) if level == 0 else ((n_vis > lo_rows) & (n_vis <= hi_rows))
        pl.when(cond)(functools.partial(tile, hi_rows))


def _gate_spec(branch, tq):
    return pl.BlockSpec((1, HEADS_PER_GROUP, tq), lambda b, g, qi: (b, branch * N_KV_GROUPS + g, qi))


def _nsa_cmp(q, kc_all, vct_all, overlap_t, gates_t, n_slc, tq=CMP_TQ):
    B, T, hd = q.shape
    G = N_KV_GROUPS
    rows = kc_all.shape[3]
    gw = hd // G
    top_k = min(SLC_TOPK, n_slc)
    assert tq % CMP_SUB == 0 and CMP_SUB % SLC_LEN == 0 and CMP_SUB % LANES == 0
    n_levels = 4 if rows % (4 * LANES) == 0 else 1
    return pl.pallas_call(
        functools.partial(_nsa_cmp_kernel, tq=tq, sub=CMP_SUB, n_slc=n_slc, top_k=top_k, n_levels=n_levels),
        grid=(B, G, T // tq),
        in_specs=[pl.BlockSpec((1, tq, gw), lambda b, g, qi: (b, qi, g)),
                  pl.BlockSpec((1, 1, 1, rows, LANES), lambda b, g, qi: (b, 0, g, 0, 0)),
                  pl.BlockSpec((1, 1, 1, HEAD_DIM, rows), lambda b, g, qi: (b, 1, g, 0, 0)),
                  pl.BlockSpec(overlap_t.shape, lambda b, g, qi: (0, 0)), _gate_spec(0, tq),
                  pl.BlockSpec((CMP_SUB, LANES), lambda b, g, qi: (0, 0))],
        out_specs=[pl.BlockSpec((1, tq, gw), lambda b, g, qi: (b, qi, g)),
                   pl.BlockSpec((1, 1, tq, LANES), lambda b, g, qi: (b, g, qi, 0))],
        out_shape=[jax.ShapeDtypeStruct((B, T, hd), BF16), jax.ShapeDtypeStruct((B, G, T, LANES), BF16)],
        scratch_shapes=[pltpu.VMEM((tq // CMP_SUB, HEADS_PER_GROUP * CMP_SUB, 2 * LANES), BF16)],
        compiler_params=_cparams(("parallel", "parallel", "parallel")),
        name="nsa_cmp",
    )(q, kc_all, vct_all, overlap_t, gates_t, _cmp_query_mask_features(CMP_SUB))


def _nsa_slc_kernel(q_ref, sel_ref, k_ref, oh_ref, vt_ref, gate_ref, o_ref, qa_sc, s_sc, mt_sc, m_sc, acc_sc, *, tq, tk, n_chunks):
    qi = pl.program_id(2)
    last = (qi * tq) // tk
    _stack_heads(q_ref[0], qa_sc, tq)
    sel = sel_ref[0, 0]
    for h in range(HEADS_PER_GROUP):
        qa_sc[h * tq:(h + 1) * tq, LANES:2 * LANES] = sel
    _softmax_init(m_sc, acc_sc)
    cw = HEADS_PER_GROUP * tq // n_chunks

    def produce(ki, slot):
        start = pl.multiple_of(ki * tk, tk)
        ka = jnp.concatenate([k_ref[0, 0, pl.ds(start, tk), :], oh_ref[pl.ds(start, tk), :]], axis=1)
        for c in range(n_chunks):
            s = _dot_nt(ka, qa_sc[c * cw:(c + 1) * cw, :])
            s_sc[slot, :, c * cw:(c + 1) * cw] = s
            mt_sc[slot, :, c * cw:(c + 1) * cw] = _column_max(s)

    def consume(ki, slot, diag):
        vt = vt_ref[0, 0, ki]
        for c in range(n_chunks):
            cols = slice(c * cw, (c + 1) * cw)
            s = s_sc[slot, :, cols]
            m_tile = mt_sc[slot, :, cols]
            if diag:
                kpos = ki * tk + _iota(s.shape, 0)
                qpos = qi * tq + (_iota(s.shape, 1) & (tq - 1))
                s = jnp.where(kpos <= qpos, s, NEG)
                m_tile = None
            _softmax_update_t(s, vt, m_sc.at[:, cols], acc_sc.at[:, cols], m_tile)

    _pipelined_key_loop(last, produce, consume)
    _unstack_heads_t(_softmax_result(acc_sc), gate_ref[0], o_ref, tq)


def _nsa_slc(q, sel, ks, vst, gates_t, tq=SLC_TQ, tk=SLC_TK):
    B, T, hd = q.shape
    G = N_KV_GROUPS
    gw = hd // G
    cols = HEADS_PER_GROUP * tq
    assert tk % tq == 0
    onehot = (jnp.arange(T)[:, None] // SLC_LEN == jnp.arange(LANES)[None, :]).astype(BF16)
    return pl.pallas_call(
        functools.partial(_nsa_slc_kernel, tq=tq, tk=tk, n_chunks=4),
        grid=(B, G, T // tq),
        in_specs=[pl.BlockSpec((1, tq, gw), lambda b, g, qi: (b, qi, g)),
                  pl.BlockSpec((1, 1, tq, LANES), lambda b, g, qi: (b, g, qi, 0)),
                  pl.BlockSpec((1, 1, T, LANES), lambda b, g, qi: (b, g, 0, 0)),
                  pl.BlockSpec((T, LANES), lambda b, g, qi: (0, 0)),
                  pl.BlockSpec((1, 1, T // tk, HEAD_DIM, tk), lambda b, g, qi: (b, g, 0, 0, 0)), _gate_spec(1, tq)],
        out_specs=pl.BlockSpec((1, tq, gw), lambda b, g, qi: (b, qi, g)),
        out_shape=jax.ShapeDtypeStruct((B, T, hd), BF16),
        scratch_shapes=[pltpu.VMEM((cols, 2 * LANES), BF16), pltpu.VMEM((2, tk, cols), F32),
                        pltpu.VMEM((2, 1, cols), F32), pltpu.VMEM((1, cols), F32),
                        pltpu.VMEM((ACC_ROWS, cols), F32)],
        compiler_params=_cparams(("parallel", "parallel", "arbitrary")),
        name="nsa_slc",
    )(q, sel, ks, onehot, vst, gates_t)


def _nsa_win_kernel(q_ref, k_ref, vt_ref, gate_ref, o_ref, qs_sc, s_sc, m_sc, acc_sc, *, tq, n_back, n_chunks):
    qi = pl.program_id(1)
    gw = HEADS_PER_GROUP * HEAD_DIM
    for g in range(N_KV_GROUPS):
        _stack_heads(q_ref[0, :, gw * g:gw * (g + 1)], qs_sc.at[g], tq)
    _softmax_init(m_sc, acc_sc)
    cw = HEADS_PER_GROUP * tq // n_chunks

    def produce(ti, slot):
        start = pl.multiple_of(ti * tq, tq)
        for g in range(N_KV_GROUPS):
            k = k_ref[0, g, pl.ds(start, tq), :]
            for c in range(n_chunks):
                s_sc[slot, g, :, c * cw:(c + 1) * cw] = _dot_nt(k, qs_sc[g, c * cw:(c + 1) * cw, :])

    def consume(ti, slot, kind):
        for g in range(N_KV_GROUPS):
            vt = vt_ref[0, g, ti]
            for c in range(n_chunks):
                cols = slice(c * cw, (c + 1) * cw)
                s = s_sc[slot, g, :, cols]
                r = _iota(s.shape, 0)
                q_in_tile = _iota(s.shape, 1) & (tq - 1)
                if kind == "oldest":
                    s = jnp.where(r > q_in_tile, s, NEG)
                elif kind == "diag":
                    s = jnp.where(r <= q_in_tile, s, NEG)
                _softmax_update_t(s, vt, m_sc.at[g, :, cols], acc_sc.at[g, :, cols])

    def kind_of(back):
        return "oldest" if back == n_back else ("diag" if back == 0 else "full")

    for first in range(n_back + 1):
        cond = (qi >= n_back) if first == n_back else (qi == first)

        @pl.when(cond)
        def _(first=first):
            backs = list(range(first, -1, -1))
            produce(qi - backs[0], 0)
            for n, back in enumerate(backs):
                if n + 1 < len(backs):
                    produce(qi - backs[n + 1], (n + 1) % 2)
                consume(qi - back, n % 2, kind_of(back))

    for g in range(N_KV_GROUPS):
        gate = gate_ref[0, HEADS_PER_GROUP * g:HEADS_PER_GROUP * (g + 1), :]
        _unstack_heads_t(_softmax_result(acc_sc.at[g]), gate, o_ref, tq, col0=gw * g)


def _nsa_win(q, kw, vwt, gates_t, tq=WIN_TQ):
    B, T, hd = q.shape
    G = N_KV_GROUPS
    cols = HEADS_PER_GROUP * tq
    return pl.pallas_call(
        functools.partial(_nsa_win_kernel, tq=tq, n_back=WIN // tq, n_chunks=4),
        grid=(B, T // tq),
        in_specs=[pl.BlockSpec((1, tq, hd), lambda b, qi: (b, qi, 0)),
                  pl.BlockSpec((1, G, T, LANES), lambda b, qi: (b, 0, 0, 0)),
                  pl.BlockSpec((1, G, T // tq, HEAD_DIM, tq), lambda b, qi: (b, 0, 0, 0, 0)),
                  pl.BlockSpec((1, N_HEADS, tq), lambda b, qi: (b, 2, qi))],
        out_specs=pl.BlockSpec((1, tq, hd), lambda b, qi: (b, qi, 0)),
        out_shape=jax.ShapeDtypeStruct((B, T, hd), BF16),
        scratch_shapes=[pltpu.VMEM((G, cols, LANES), BF16), pltpu.VMEM((2, G, tq, cols), F32),
                        pltpu.VMEM((G, 1, cols), F32), pltpu.VMEM((G, ACC_ROWS, cols), F32)],
        compiler_params=_cparams(("parallel", "arbitrary")),
        name="nsa_win",
    )(q, kw, vwt, gates_t)


def _pad_cols(a, n):
    return jnp.pad(a, ((0, 0), (0, n - a.shape[1])))


def _lane_vec(v):
    return jnp.tile(v.astype(F32), 2).reshape(1, LANES)


def kernel(x, positions, a_norm, a_w_in, a_b_f, a_q_gain, a_k_gain, a_w_out, kv_norm, kv_w, kc_pe, vc_pe, kc_w1, kc_w2, vc_w1, vc_w2, kc_gain, ks_gain, kw_gain, b_norm, b_w_in, b_b_gate, b_q_gain, b_w_out, f_norm, f_w_up, f_conv_w, f_conv_b, f_w_down):
    B, T, D = x.shape
    hd = N_HEADS * HEAD_DIM
    G = N_KV_GROUPS
    n_a = a_norm.shape[0]
    n_b = b_norm.shape[0]
    depth = n_a + n_b
    n_slc = T // SLC_LEN
    n_cmp = (T - CMP_LEN) // CMP_STRIDE + 1
    assert T % 1024 == 0 and n_slc <= LANES and hd == 1024 and D == 1024

    half = ROT_DIM // 2
    inv = ROPE_THETA ** (-jnp.arange(half, dtype=F32) * (2.0 / ROT_DIM))
    cos_t, sin_t = _rope_tables(positions, inv)
    end_pos = positions[:, CMP_LEN - 1::CMP_STRIDE]
    end_pos = jnp.pad(end_pos, ((0, 0), (0, T // CMP_STRIDE - n_cmp)))
    cos_c, sin_c = _rope_tables(end_pos, inv)

    cs = jnp.arange(T // CMP_STRIDE) * CMP_STRIDE
    ss = jnp.arange(LANES) * SLC_LEN
    overlap_t = (jnp.maximum(jnp.minimum(cs[None, :] + CMP_LEN, ss[:, None] + SLC_LEN)
                             - jnp.maximum(cs[None, :], ss[:, None]), 0).astype(F32) / CMP_LEN).astype(BF16)

    h = x
    kv = None
    for layer in range(depth):
        if layer < n_a:
            i = layer
            w = jnp.concatenate([a_w_in[i][:, :2 * hd], _pad_cols(a_w_in[i][:, 3 * hd:], LANES)], axis=1).astype(BF16)
            wvt = a_w_in[i][:, 2 * hd:3 * hd].T.astype(BF16)
            bf = _pad_cols(a_b_f[i].reshape(1, -1), LANES)
            q, k, vt, cf = _fox_inproj(h, a_norm[i].reshape(1, D), w, wvt, bf, _lane_vec(a_q_gain[i]),
                                       _lane_vec(a_k_gain[i]))
            mix, w_out = [_fox_attn(q, k, cf, vt)], a_w_out[i]
        else:
            i = layer - n_a
            kc_all, vct_all, ks, kw, vst, vwt = kv
            q, gates_t = _nsa_inproj(h, b_norm[i].reshape(1, D), b_w_in[i][:, :hd].astype(BF16),
                                     b_w_in[i][:, hd:].T.astype(BF16), b_b_gate[i].reshape(-1, 1),
                                     _lane_vec(b_q_gain[i]), cos_t, sin_t)
            o_c, sel = _nsa_cmp(q, kc_all, vct_all, overlap_t, gates_t, n_slc)
            o_s = _nsa_slc(q, sel, ks, vst, gates_t)
            o_w = _nsa_win(q, kw, vwt, gates_t)
            mix, w_out = [o_c, o_s, o_w], b_w_out[i]
        h = _mix_out_conv_ffn(h, mix, w_out.astype(BF16), f_norm[layer].reshape(1, D), f_w_up[layer].astype(BF16),
                              f_conv_w[layer], f_conv_b[layer].reshape(1, -1), f_w_down[layer].astype(BF16))
        if layer == n_a - 1:
            w6 = kv_w.reshape(D, 6, G, HEAD_DIM)
            raw_cols = w6[:, 0:2].reshape(D, 2 * G * HEAD_DIM)
            wk = w6[:, (2, 4)]
            dup_cols = jnp.concatenate([wk, wk], axis=-1).reshape(D, 2 * G * LANES)
            wkv = jnp.concatenate([raw_cols, dup_cols], axis=1).astype(BF16)
            wvt = w6[:, (3, 5)].reshape(D, 2 * G * HEAD_DIM).T.astype(BF16)
            craw, ks, kw, vst, vwt = _kvproj(h, kv_norm.reshape(1, D), wkv, wvt, _lane_vec(ks_gain),
                                             _lane_vec(kw_gain), cos_t, sin_t)
            r = craw.reshape(B, T, 2, G, HEAD_DIM).transpose(0, 2, 3, 1, 4).reshape(B, 2, G, T // CMP_STRIDE, CMP_STRIDE * HEAD_DIM)
            w1 = jnp.stack([kc_w1, vc_w1]).astype(BF16)
            pe = jnp.stack([kc_pe.reshape(1, -1), vc_pe.reshape(1, -1)])
            pe = jnp.pad(pe, ((0, 0), (0, 7), (0, 0))).astype(BF16)
            w2 = jnp.stack([jnp.concatenate([kc_w2, kc_w2], axis=1), jnp.concatenate([vc_w2, vc_w2], axis=1)]).astype(BF16)
            w2t = jnp.stack([kc_w2.T, vc_w2.T]).astype(BF16)
            kc_all, vct_all = _compress(r, w1, pe, w2, w2t, _lane_vec(kc_gain), cos_c, sin_c, n_cmp)
            kv = (kc_all, vct_all, ks, kw, vst, vwt)
    return h
```
